```python
import jax
import jax.numpy as jnp
from jax import lax
import numpy as np

D_MODEL = 1024
BATCH = 8
SEQ = 2048
DEPTH = 4
DEC_BATCH = 128
DEC_SEQ = 8
PAST_LEN = 8192
PAGE_SIZE = 128

GROUP_WIDTH = D_MODEL // 4
MIX_WIDTH = 4 * GROUP_WIDTH
HEAD_DIM = 64
A_HEADS = GROUP_WIDTH // HEAD_DIM
A_KV_HEADS = A_HEADS // 2
A_REP = A_HEADS // A_KV_HEADS
WINDOW = 128
ROPE_THETA = 500000.0
ROT_DIM = HEAD_DIM // 4
SSM_GC = 16
SSM_GROUPS = GROUP_WIDTH // SSM_GC
SSM_STATE = 64
GDN_DK = 64
GDN_DV = 64
GDN_HEADS = GROUP_WIDTH // GDN_DV
GDN_CONV = 4
GDN_CONV_DIM = GDN_HEADS * (2 * GDN_DK + GDN_DV)
GDN_CHUNK = 64
GLA_HEADS = 4
GLA_DV = GROUP_WIDTH // GLA_HEADS
GLA_DK = GLA_DV // 2
GLA_RANK = 16
GLA_TAU = 16.0
GLA_CHUNK = 16
N_EXPERTS = 32
TOP_K = 4
D_FF = D_MODEL
SWIGLU_LIMIT = 7.0
SWIGLU_ALPHA = 1.702
MOE_BLOCK = 128
DN_ALPHA = (2 * DEPTH) ** 0.25
DN_BETA = (8 * DEPTH) ** -0.25
LN_EPS = 1e-5
RMS_EPS = 1e-6
IN_SPLITS = (A_HEADS * HEAD_DIM, A_KV_HEADS * HEAD_DIM, A_KV_HEADS * HEAD_DIM,
             GROUP_WIDTH,
             GDN_CONV_DIM, GDN_HEADS * GDN_DV, GDN_HEADS, GDN_HEADS,
             GLA_HEADS * GLA_DK, GLA_HEADS * GLA_DK, GLA_HEADS * GLA_DV, GLA_HEADS * GLA_DV, GLA_RANK)
N_IN = sum(IN_SPLITS)
IN_OFFSETS = tuple(int(o) for o in np.cumsum(IN_SPLITS)[:-1])

kernel_name = 'hybrid_swa_s5_gdn_gla_moe_step'


def layer_norm(x, g, b):
    xf = x.astype(jnp.float32)
    xc = xf - xf.mean(-1, keepdims=True)
    var = (xc * xc).mean(-1, keepdims=True)
    return (xc * lax.rsqrt(var + LN_EPS) * g.astype(jnp.float32) + b.astype(jnp.float32)).astype(x.dtype)


def rms_norm(x, w):
    xf = x.astype(jnp.float32)
    return xf * lax.rsqrt((xf * xf).mean(-1, keepdims=True) + RMS_EPS) * w.astype(jnp.float32)


def l2_normalize(x):
    return x * lax.rsqrt((x * x).sum(-1, keepdims=True) + 1e-6)


def rope_partial(x, pos):
    half = ROT_DIM // 2
    inv_freq = ROPE_THETA ** (-jnp.arange(half, dtype=jnp.float32) / half)
    ang = pos.astype(jnp.float32)[:, None] * inv_freq[None, :]
    cos = jnp.cos(ang)[:, None, :]
    sin = jnp.sin(ang)[:, None, :]
    xf = x.astype(jnp.float32)
    x1 = xf[..., :half]
    x2 = xf[..., half:ROT_DIM]
    out = jnp.concatenate([x1 * cos - x2 * sin, x2 * cos + x1 * sin, xf[..., ROT_DIM:]], axis=-1)
    return out.astype(x.dtype)


def window_mask(q_pos, k_pos):
    d = q_pos - k_pos
    return (d >= 0) & (d <= WINDOW) & (k_pos >= 0)


def sink_attention(q, k, v, mask, sinks):
    s = jnp.einsum('...qgrd,...kgd->...grqk', q.astype(jnp.float32), k.astype(jnp.float32)) * (HEAD_DIM ** -0.5)
    s = jnp.where(mask, s, -jnp.inf)
    sink = jnp.broadcast_to(sinks.astype(jnp.float32).reshape(A_KV_HEADS, A_REP, 1, 1), s.shape[:-1] + (1,))
    p = jax.nn.softmax(jnp.concatenate([s, sink], axis=-1), axis=-1)[..., :-1]
    return jnp.einsum('...grqk,...kgd->...qgrd', p, v.astype(jnp.float32)).astype(v.dtype)


def swa_banded(q, k, v, sinks):
    bsz, L = q.shape[:2]
    nb = L // WINDOW
    qb = q.reshape(bsz, nb, WINDOW, A_KV_HEADS, A_REP, HEAD_DIM)

    def band(t):
        tp = jnp.pad(t, ((0, 0), (WINDOW, 0), (0, 0), (0, 0)))[:, :L]
        return jnp.concatenate([tp.reshape(bsz, nb, WINDOW, A_KV_HEADS, HEAD_DIM),
                                t.reshape(bsz, nb, WINDOW, A_KV_HEADS, HEAD_DIM)], axis=2)

    blk = jnp.arange(nb, dtype=jnp.int32)[:, None] * WINDOW
    q_pos = blk + jnp.arange(WINDOW, dtype=jnp.int32)[None, :]
    k_pos = blk - WINDOW + jnp.arange(2 * WINDOW, dtype=jnp.int32)[None, :]
    mask = window_mask(q_pos[:, :, None], k_pos[:, None, :])
    o = sink_attention(qb, band(k), band(v), mask[:, None, None], sinks)
    return o.reshape(bsz, L, A_HEADS * HEAD_DIM)


def swa_cached(q, k, v, k_buf, v_buf, pos, sinks):
    bsz, L = q.shape[:2]
    cw = k_buf.shape[1]
    kk = jnp.concatenate([k_buf.astype(k.dtype), k], axis=1)
    vv = jnp.concatenate([v_buf.astype(v.dtype), v], axis=1)
    k_pos = pos[0] - cw + jnp.arange(cw + L, dtype=jnp.int32)
    mask = window_mask(pos[:, None], k_pos[None, :])
    o = sink_attention(q.reshape(bsz, L, A_KV_HEADS, A_REP, HEAD_DIM), kk, vv, mask, sinks)
    return o.reshape(bsz, L, A_HEADS * HEAD_DIM), kk[:, L:], vv[:, L:]


def _ssm_combine(e1, e2):
    a1, b1 = e1
    a2, b2 = e2
    return a1 * a2, a2 * b1 + b2


def s5_mixer(u, h0_re, h0_im, a_re, a_im, b_re, b_im, c_re, c_im, d_skip, log_dt, glu_w, glu_b):
    f = jnp.float32
    bsz, L, _ = u.shape
    uf = u.astype(f).reshape(bsz, L, SSM_GROUPS, SSM_GC)
    lam = lax.complex(a_re.astype(f), a_im.astype(f))
    delta = jnp.exp(log_dt.astype(f))[:, None]
    a_bar = jnp.exp(lam * delta)
    b_bar = ((a_bar - 1.0) / lam)[..., None] * lax.complex(b_re.astype(f), b_im.astype(f))
    c_mat = lax.complex(c_re.astype(f), c_im.astype(f))
    bu = jnp.einsum('gpc,blgc->blgp', b_bar, uf.astype(jnp.complex64))
    a_seq = jnp.broadcast_to(a_bar, bu.shape)
    a_cum, hs = lax.associative_scan(_ssm_combine, (a_seq, bu), axis=1)
    hs = hs + a_cum * lax.complex(h0_re.astype(f), h0_im.astype(f))[:, None]
    y = jnp.einsum('gcp,blgp->blgc', c_mat, hs).real + d_skip.astype(f) * uf
    y = jax.nn.gelu(y.reshape(bsz, L, GROUP_WIDTH))
    z = y @ glu_w.astype(f) + glu_b.astype(f)
    out = z[..., :GROUP_WIDTH] * jax.nn.sigmoid(z[..., GROUP_WIDTH:])
    h_last = hs[:, -1]
    return out.astype(u.dtype), h_last.real, h_last.imag


def _to_chunks(t, c):
    pad = (-t.shape[1]) % c
    t = jnp.pad(t, [(0, 0), (0, pad)] + [(0, 0)] * (t.ndim - 2))
    n = t.shape[1] // c
    t = t.reshape((t.shape[0], n, c) + t.shape[2:])
    return jnp.moveaxis(t, 3, 1)


def _from_chunks(o, L):
    n, bsz, h, c, e = o.shape
    o = jnp.moveaxis(o, 0, 2).reshape(bsz, h, n * c, e)[:, :, :L]
    return jnp.moveaxis(o, 1, 2)


def gated_delta_chunked(q, k, v, g, beta, s0):
    L = q.shape[1]
    c = min(GDN_CHUNK, L)
    dv = v.shape[-1]
    q, k, v, g, beta = (_to_chunks(t, c) for t in (q, k, v, g, beta))
    g = jnp.cumsum(g, axis=-1)
    causal = jnp.tril(jnp.ones((c, c), bool))
    strict = jnp.tril(jnp.ones((c, c), bool), -1)
    decay = jnp.exp(jnp.where(causal, g[..., :, None] - g[..., None, :], -jnp.inf))
    kb = k * beta[..., None]
    m = jnp.einsum('bhnid,bhnjd->bhnij', kb, k) * decay
    a_mat = jnp.where(strict, m, 0.0) + jnp.eye(c, dtype=jnp.float32)
    rhs = jnp.concatenate([v * beta[..., None], kb * jnp.exp(g)[..., None]], axis=-1)
    sol = lax.linalg.triangular_solve(a_mat, rhs, left_side=True, lower=True, unit_diagonal=True)
    u, w = sol[..., :dv], sol[..., dv:]
    qk = jnp.einsum('bhnid,bhnjd->bhnij', q, k) * decay

    def step(s, xs):
        q_c, k_c, u_c, w_c, g_c, qk_c = xs
        v_new = u_c - jnp.einsum('bhcd,bhde->bhce', w_c, s)
        o = jnp.einsum('bhcd,bhde->bhce', q_c * jnp.exp(g_c)[..., None], s) + jnp.einsum('bhij,bhje->bhie', qk_c, v_new)
        g_last = g_c[..., -1]
        s = s * jnp.exp(g_last)[..., None, None] + jnp.einsum(
            'bhcd,bhce->bhde', k_c * jnp.exp(g_last[..., None] - g_c)[..., None], v_new)
        return s, o

    xs = tuple(jnp.moveaxis(t, 2, 0) for t in (q, k, u, w, g, qk))
    s_fin, o = lax.scan(step, s0, xs)
    return _from_chunks(o, L), s_fin


def gdn_mixer(x_qkv, z, b, a, conv_buf, s0, conv_w, a_log, dt_bias, norm_w):
    bsz, L, _ = x_qkv.shape
    f = jnp.float32
    xp = jnp.concatenate([conv_buf.astype(x_qkv.dtype), x_qkv], axis=1)
    conv = xp[:, 0:L] * conv_w[0]
    for i in range(1, GDN_CONV):
        conv = conv + xp[:, i:i + L] * conv_w[i]
    qkv = jax.nn.silu(conv.astype(f))
    new_buf = xp[:, L:]
    q, k, v = jnp.split(qkv, [GDN_HEADS * GDN_DK, 2 * GDN_HEADS * GDN_DK], axis=-1)
    q = l2_normalize(q.reshape(bsz, L, GDN_HEADS, GDN_DK)) * (GDN_DK ** -0.5)
    k = l2_normalize(k.reshape(bsz, L, GDN_HEADS, GDN_DK))
    v = v.reshape(bsz, L, GDN_HEADS, GDN_DV)
    beta = jax.nn.sigmoid(b.astype(f))
    g = -jnp.exp(a_log.astype(f)) * jax.nn.softplus(a.astype(f) + dt_bias.astype(f))
    o, s = gated_delta_chunked(q, k, v, g, beta, s0.astype(f))
    o = rms_norm(o, norm_w) * jax.nn.silu(z.astype(f).reshape(bsz, L, GDN_HEADS, GDN_DV))
    return o.reshape(bsz, L, GDN_HEADS * GDN_DV).astype(x_qkv.dtype), new_buf, s


def gla_chunked(q, k, v, log_a, s0):
    L = q.shape[1]
    c = min(GLA_CHUNK, L)
    q, k, v, log_a = (_to_chunks(t, c) for t in (q, k, v, log_a))
    bcum = jnp.cumsum(log_a, axis=-2)
    causal = jnp.tril(jnp.ones((c, c), bool))
    dec = jnp.exp(jnp.where(causal[..., None], bcum[..., :, None, :] - bcum[..., None, :, :], -jnp.inf))
    att = jnp.einsum('bhnid,bhnjd,bhnijd->bhnij', q, k, dec)
    o_intra = jnp.einsum('bhnij,bhnje->bhnie', att, v)

    def step(s, xs):
        q_c, k_c, v_c, b_c, oi_c = xs
        b_last = b_c[..., -1:, :]
        o = jnp.einsum('bhcd,bhde->bhce', q_c * jnp.exp(b_c), s) + oi_c
        s = s * jnp.exp(b_last)[..., 0, :, None] + jnp.einsum('bhcd,bhce->bhde', k_c * jnp.exp(b_last - b_c), v_c)
        return s, o

    xs = tuple(jnp.moveaxis(t, 2, 0) for t in (q, k, v, bcum, o_intra))
    s_fin, o = lax.scan(step, s0, xs)
    return _from_chunks(o, L), s_fin


def gla_mixer(q, k, v, r, g_lr, s0, gate_w, gate_b, norm_w):
    bsz, L, _ = q.shape
    f = jnp.float32
    qh = q.astype(f).reshape(bsz, L, GLA_HEADS, GLA_DK) * (GLA_DK ** -0.5)
    kh = k.astype(f).reshape(bsz, L, GLA_HEADS, GLA_DK)
    vh = v.astype(f).reshape(bsz, L, GLA_HEADS, GLA_DV)
    log_a = jax.nn.log_sigmoid(g_lr.astype(f) @ gate_w.astype(f) + gate_b.astype(f)) / GLA_TAU
    o, s = gla_chunked(qh, kh, vh, log_a.reshape(bsz, L, GLA_HEADS, GLA_DK), s0.astype(f))
    o = rms_norm(o, norm_w) * jax.nn.silu(r.astype(f).reshape(bsz, L, GLA_HEADS, GLA_DV))
    return o.reshape(bsz, L, GLA_HEADS * GLA_DV).astype(q.dtype), s


def mixer_block(x, pos, is_prompt, k_buf, v_buf, ssm_re, ssm_im, conv_buf, gdn_s, gla_s, lp):
    bsz, L, _ = x.shape
    dt = x.dtype
    h = x @ lp['w_in']
    aq, ak, av, su, gqkv, gz, gb, ga, lq, lk, lv, lr, lg = jnp.split(h, IN_OFFSETS, axis=-1)
    q = rope_partial(aq.reshape(bsz, L, A_HEADS, HEAD_DIM), pos)
    k = rope_partial(ak.reshape(bsz, L, A_KV_HEADS, HEAD_DIM), pos)
    v = av.reshape(bsz, L, A_KV_HEADS, HEAD_DIM)
    if is_prompt:
        o_a = swa_banded(q, k, v, lp['sinks'])
        cw = min(WINDOW, PAST_LEN)
        new_k, new_v = k[:, L - cw:], v[:, L - cw:]
    else:
        o_a, new_k, new_v = swa_cached(q, k, v, k_buf, v_buf, pos, lp['sinks'])
    o_b, h_re, h_im = s5_mixer(su, ssm_re, ssm_im, lp['ssm_a_re'], lp['ssm_a_im'], lp['ssm_b_re'], lp['ssm_b_im'],
                               lp['ssm_c_re'], lp['ssm_c_im'], lp['ssm_d'], lp['ssm_log_dt'], lp['ssm_glu_w'], lp['ssm_glu_b'])
    o_c, new_conv, new_gdn = gdn_mixer(gqkv, gz, gb, ga, conv_buf, gdn_s, lp['gdn_conv_w'], lp['gdn_a_log'],
                                       lp['gdn_dt_bias'], lp['gdn_norm_w'])
    o_d, new_gla = gla_mixer(lq, lk, lv, lr, lg, gla_s, lp['gla_gate_w'], lp['gla_gate_b'], lp['gla_norm_w'])
    out = jnp.concatenate([o_a.astype(dt), o_b.astype(dt), o_c.astype(dt), o_d.astype(dt)], axis=-1) @ lp['w_out']
    states = (new_k.astype(dt), new_v.astype(dt), h_re.astype(dt), h_im.astype(dt),
              new_conv.astype(dt), new_gdn.astype(dt), new_gla.astype(dt))
    return out, states


def expert_ffn(xb, w_gu, b_gu, w_d, b_d):
    gu = xb @ w_gu + b_gu
    gate = jnp.minimum(gu[:, 0::2], SWIGLU_LIMIT)
    up = jnp.clip(gu[:, 1::2], -SWIGLU_LIMIT, SWIGLU_LIMIT)
    glu = gate * jax.nn.sigmoid(gate * SWIGLU_ALPHA)
    return (glu * (up + 1.0)) @ w_d + b_d


def moe_ffn(x, router_w, router_b, w_gu, b_gu, w_d, b_d):
    T, D = x.shape
    logits = (x @ router_w + router_b).astype(jnp.float32)
    top_v, top_e = lax.top_k(logits, TOP_K)
    gates = jax.nn.softmax(top_v, axis=-1)
    tk = T * TOP_K
    flat_e = top_e.reshape(tk)
    order = jnp.argsort(flat_e)
    sorted_e = flat_e[order]
    tok = (order // TOP_K).astype(jnp.int32)
    counts = jnp.bincount(flat_e, length=N_EXPERTS)
    row_start = jnp.cumsum(counts) - counts
    nblk_e = (counts + MOE_BLOCK - 1) // MOE_BLOCK
    blk_end = jnp.cumsum(nblk_e)
    blk_start = blk_end - nblk_e
    n_blocks = -(-tk // MOE_BLOCK) + N_EXPERTS
    dest = blk_start[sorted_e] * MOE_BLOCK + (jnp.arange(tk, dtype=jnp.int32) - row_start[sorted_e])
    buf_tok = jnp.zeros((n_blocks * MOE_BLOCK,), jnp.int32).at[dest].set(tok)
    blk_expert = jnp.minimum(jnp.searchsorted(blk_end, jnp.arange(n_blocks, dtype=jnp.int32), side='right'),
                             N_EXPERTS - 1)
    xb = x[buf_tok].reshape(n_blocks, MOE_BLOCK, D)
    yb = lax.map(lambda args: expert_ffn(args[0], w_gu[args[1]], b_gu[args[1]], w_d[args[1]], b_d[args[1]]),
                 (xb, blk_expert)).reshape(n_blocks * MOE_BLOCK, D)
    contrib = yb[dest] * gates.reshape(tk)[order][:, None].astype(yb.dtype)
    return jax.ops.segment_sum(contrib, tok, num_segments=T)


def setup_inputs(seed: int = 0) -> dict:
    key = jax.random.key(seed)
    ks = iter(jax.random.split(key, 64))
    f = jnp.float32

    def nrm(shape, scale):
        return jax.random.normal(next(ks), shape, f) * scale

    def unif(shape, lo, hi):
        return jax.random.uniform(next(ks), shape, f, lo, hi)

    cw = min(WINDOW, PAST_LEN)
    dt_g = jnp.exp(unif((DEPTH, GDN_HEADS), float(np.log(1e-3)), float(np.log(1e-1))))
    return {
        'x_prompt': nrm((BATCH, SEQ, D_MODEL), 1.0),
        'x_sample': nrm((DEC_BATCH, DEC_SEQ, D_MODEL), 1.0),
        'cache_swa_k': nrm((DEPTH, DEC_BATCH, cw, A_KV_HEADS, HEAD_DIM), 1.0),
        'cache_swa_v': nrm((DEPTH, DEC_BATCH, cw, A_KV_HEADS, HEAD_DIM), 1.0),
        'state_ssm_re': nrm((DEPTH, DEC_BATCH, SSM_GROUPS, SSM_STATE), 0.1),
        'state_ssm_im': nrm((DEPTH, DEC_BATCH, SSM_GROUPS, SSM_STATE), 0.1),
        'state_gdn_conv': nrm((DEPTH, DEC_BATCH, GDN_CONV - 1, GDN_CONV_DIM), 1.0),
        'state_gdn': nrm((DEPTH, DEC_BATCH, GDN_HEADS, GDN_DK, GDN_DV), 0.1),
        'state_gla': nrm((DEPTH, DEC_BATCH, GLA_HEADS, GLA_DK, GLA_DV), 1.0),
        'w_in': nrm((DEPTH, D_MODEL, N_IN), D_MODEL ** -0.5),
        'w_out': nrm((DEPTH, MIX_WIDTH, D_MODEL), MIX_WIDTH ** -0.5 * DN_BETA),
        'attn_sinks': nrm((DEPTH, A_HEADS), 0.5),
        'ssm_a_re': -0.5 + nrm((DEPTH, SSM_GROUPS, SSM_STATE), 0.01),
        'ssm_a_im': jnp.pi * jnp.arange(SSM_STATE, dtype=f) + nrm((DEPTH, SSM_GROUPS, SSM_STATE), 0.01),
        'ssm_b_re': nrm((DEPTH, SSM_GROUPS, SSM_STATE, SSM_GC), (2 * SSM_GC) ** -0.5),
        'ssm_b_im': nrm((DEPTH, SSM_GROUPS, SSM_STATE, SSM_GC), (2 * SSM_GC) ** -0.5),
        'ssm_c_re': nrm((DEPTH, SSM_GROUPS, SSM_GC, SSM_STATE), SSM_STATE ** -0.5),
        'ssm_c_im': nrm((DEPTH, SSM_GROUPS, SSM_GC, SSM_STATE), SSM_STATE ** -0.5),
        'ssm_d': nrm((DEPTH, SSM_GROUPS, SSM_GC), 1.0),
        'ssm_log_dt': unif((DEPTH, SSM_GROUPS), float(np.log(1e-3)), float(np.log(1e-1))),
        'ssm_glu_w': nrm((DEPTH, GROUP_WIDTH, 2 * GROUP_WIDTH), GROUP_WIDTH ** -0.5),
        'ssm_glu_b': nrm((DEPTH, 2 * GROUP_WIDTH), 0.01),
        'gdn_conv_w': nrm((DEPTH, GDN_CONV, GDN_CONV_DIM), GDN_CONV ** -0.5),
        'gdn_a_log': jnp.log(unif((DEPTH, GDN_HEADS), 1.0, 16.0)),
        'gdn_dt_bias': dt_g + jnp.log(-jnp.expm1(-dt_g)),
        'gdn_norm_w': 1.0 + nrm((DEPTH, GDN_DV), 0.01),
        'gla_gate_w': nrm((DEPTH, GLA_RANK, GLA_HEADS * GLA_DK), GLA_RANK ** -0.5),
        'gla_gate_b': nrm((DEPTH, GLA_HEADS * GLA_DK), 0.1),
        'gla_norm_w': 1.0 + nrm((DEPTH, GLA_DV), 0.01),
        'ln1_g': 1.0 + nrm((DEPTH, D_MODEL), 0.01),
        'ln1_b': nrm((DEPTH, D_MODEL), 0.01),
        'ln2_g': 1.0 + nrm((DEPTH, D_MODEL), 0.01),
        'ln2_b': nrm((DEPTH, D_MODEL), 0.01),
        'router_w': nrm((DEPTH, D_MODEL, N_EXPERTS), D_MODEL ** -0.5),
        'router_b': nrm((DEPTH, N_EXPERTS), 0.01),
        'moe_w_gate_up': nrm((DEPTH, N_EXPERTS, D_MODEL, 2 * D_FF), D_MODEL ** -0.5),
        'moe_b_gate_up': nrm((DEPTH, N_EXPERTS, 2 * D_FF), 0.01),
        'moe_w_down': nrm((DEPTH, N_EXPERTS, D_FF, D_MODEL), D_FF ** -0.5 * DN_BETA),
        'moe_b_down': nrm((DEPTH, N_EXPERTS, D_MODEL), 0.01),
    }


def reference(x_prompt, x_sample, cache_swa_k, cache_swa_v, state_ssm_re, state_ssm_im, state_gdn_conv,
              state_gdn, state_gla, w_in, w_out, attn_sinks, ssm_a_re, ssm_a_im, ssm_b_re, ssm_b_im,
              ssm_c_re, ssm_c_im, ssm_d, ssm_log_dt, ssm_glu_w, ssm_glu_b, gdn_conv_w, gdn_a_log,
              gdn_dt_bias, gdn_norm_w, gla_gate_w, gla_gate_b, gla_norm_w, ln1_g, ln1_b, ln2_g, ln2_b,
              router_w, router_b, moe_w_gate_up, moe_b_gate_up, moe_w_down, moe_b_down):
    dt = x_prompt.dtype
    bp, lp_len, _ = x_prompt.shape
    bs, ls, _ = x_sample.shape
    pos_p = jnp.arange(lp_len, dtype=jnp.int32)
    pos_s = PAST_LEN + jnp.arange(ls, dtype=jnp.int32)
    z_ssm = jnp.zeros((bp, SSM_GROUPS, SSM_STATE), dt)
    z_conv = jnp.zeros((bp, GDN_CONV - 1, GDN_CONV_DIM), dt)
    z_gdn = jnp.zeros((bp, GDN_HEADS, GDN_DK, GDN_DV), dt)
    z_gla = jnp.zeros((bp, GLA_HEADS, GLA_DK, GLA_DV), dt)
    xp, xs = x_prompt, x_sample
    new_p = [[] for _ in range(7)]
    new_s = [[] for _ in range(7)]
    n_p = bp * lp_len
    for l in range(DEPTH):
        lp = {'w_in': w_in[l], 'w_out': w_out[l], 'sinks': attn_sinks[l],
              'ssm_a_re': ssm_a_re[l], 'ssm_a_im': ssm_a_im[l], 'ssm_b_re': ssm_b_re[l], 'ssm_b_im': ssm_b_im[l],
              'ssm_c_re': ssm_c_re[l], 'ssm_c_im': ssm_c_im[l], 'ssm_d': ssm_d[l], 'ssm_log_dt': ssm_log_dt[l],
              'ssm_glu_w': ssm_glu_w[l], 'ssm_glu_b': ssm_glu_b[l], 'gdn_conv_w': gdn_conv_w[l],
              'gdn_a_log': gdn_a_log[l], 'gdn_dt_bias': gdn_dt_bias[l], 'gdn_norm_w': gdn_norm_w[l],
              'gla_gate_w': gla_gate_w[l], 'gla_gate_b': gla_gate_b[l], 'gla_norm_w': gla_norm_w[l]}
        mp, st_p = mixer_block(xp, pos_p, True, None, None, z_ssm, z_ssm, z_conv, z_gdn, z_gla, lp)
        ms, st_s = mixer_block(xs, pos_s, False, cache_swa_k[l], cache_swa_v[l], state_ssm_re[l], state_ssm_im[l],
                               state_gdn_conv[l], state_gdn[l], state_gla[l], lp)
        xp = layer_norm(DN_ALPHA * xp + mp, ln1_g[l], ln1_b[l])
        xs = layer_norm(DN_ALPHA * xs + ms, ln1_g[l], ln1_b[l])
        for i in range(7):
            new_p[i].append(st_p[i])
            new_s[i].append(st_s[i])
        flat = jnp.concatenate([xp.reshape(n_p, D_MODEL), xs.reshape(bs * ls, D_MODEL)], axis=0)
        ffn = moe_ffn(flat, router_w[l], router_b[l], moe_w_gate_up[l], moe_b_gate_up[l], moe_w_down[l], moe_b_down[l])
        flat = layer_norm(DN_ALPHA * flat + ffn, ln2_g[l], ln2_b[l])
        xp = flat[:n_p].reshape(bp, lp_len, D_MODEL)
        xs = flat[n_p:].reshape(bs, ls, D_MODEL)
    p_swa_k, p_swa_v, p_ssm_re, p_ssm_im, p_gdn_conv, p_gdn, p_gla = [jnp.stack(t, axis=0) for t in new_p]
    s_swa_k, s_swa_v, s_ssm_re, s_ssm_im, s_gdn_conv, s_gdn, s_gla = [jnp.stack(t, axis=0) for t in new_s]
    return (xp, xs, p_swa_k, p_swa_v, p_ssm_re, p_ssm_im, p_gdn_conv, p_gdn, p_gla,
            s_swa_k, s_swa_v, s_ssm_re, s_ssm_im, s_gdn_conv, s_gdn, s_gla)
```

```python
import functools

import numpy as np
import jax
import jax.numpy as jnp
from jax import lax
from jax.experimental import pallas as pl
from jax.experimental.pallas import tpu as pltpu

F32 = jnp.float32
BF16 = jnp.bfloat16
I32 = jnp.int32
HI = lax.Precision.HIGHEST

D_MODEL = 1024
DEPTH = 4
PAST_LEN = 8192
GROUP_WIDTH = 256
HEAD_DIM = 64
A_HEADS = 4
A_KV_HEADS = 2
WINDOW = 128
ROPE_THETA = 500000.0
ROT_DIM = 16
SSM_GC = 16
SSM_GROUPS = 16
SSM_STATE = 64
SSM_W = SSM_GROUPS * SSM_STATE
GDN_HEADS = 4
GDN_DK = 64
GDN_DV = 64
GDN_CONV = 4
GDN_CONV_DIM = 768
GDN_CHUNK = 64
GLA_HEADS = 4
GLA_DK = 32
GLA_DV = 64
GLA_RANK = 16
GLA_TAU = 16.0
GLA_CHUNK = 16
N_EXPERTS = 32
TOP_K = 4
D_FF = 1024
SWIGLU_LIMIT = 7.0
SWIGLU_ALPHA = 1.702
DN_ALPHA = (2 * DEPTH) ** 0.25
LN_EPS = 1e-5
RMS_EPS = 1e-6
N_IN = 2584

C_AQ, C_AK, C_AV, C_SU, C_GQKV, C_GZ = 0, 256, 384, 512, 768, 1536
C_LQ, C_LK, C_LV, C_LR, C_SM = 1792, 1920, 2048, 2304, 2560
NH = 2688
SM_GB, SM_GA, SM_LG = 0, 4, 8

LANE = 128
ROW_TILE = 512
MOE_TILE = 512
VMEM_LIMIT = 56 * 1024 * 1024


def _cp(*sem):
    return pltpu.CompilerParams(dimension_semantics=sem, vmem_limit_bytes=VMEM_LIMIT)


def _dot(a, b, precision=None):
    return jnp.dot(a, b, preferred_element_type=F32, precision=precision)


def _dot_nt(a, b, precision=None):
    return lax.dot_general(a, b, (((1,), (1,)), ((), ())), preferred_element_type=F32, precision=precision)


def _dot_tn(a, b, precision=None):
    return lax.dot_general(a, b, (((0,), (0,)), ((), ())), preferred_element_type=F32, precision=precision)


def _bf(x):
    return x.astype(BF16)


def _iota(shape, dim):
    return lax.broadcasted_iota(I32, shape, dim)


def _shr(idx, size):
    return lax.shift_right_logical(idx, int(size).bit_length() - 1)


def _sigmoid(x):
    return 1.0 / (1.0 + jnp.exp(-x))


def _silu(x):
    return x * _sigmoid(x)


def _softplus(x):
    return jnp.maximum(x, 0.0) + jnp.log(1.0 + jnp.exp(-jnp.abs(x)))


def _log_sigmoid(x):
    return -_softplus(-x)


def _gelu_tanh(x):
    return 0.5 * x * (1.0 + jnp.tanh(0.7978845608028654 * (x + 0.044715 * x * x * x)))


def _layer_norm(y, g, b):
    mu = jnp.mean(y, axis=-1, keepdims=True)
    yc = y - mu
    var = jnp.mean(yc * yc, axis=-1, keepdims=True)
    return yc * lax.rsqrt(var + LN_EPS) * g + b


def _in_proj_body(x_ref, w_ref, o_ref):
    xb = _bf(x_ref[...])
    for c0 in range(0, NH, 512):
        c1 = min(c0 + 512, NH)
        o_ref[:, c0:c1] = _dot(xb, w_ref[:, c0:c1])


def in_proj(x, w):
    t = x.shape[0]
    return pl.pallas_call(
        _in_proj_body,
        grid=(t // ROW_TILE,),
        in_specs=[pl.BlockSpec((ROW_TILE, D_MODEL), lambda i: (i, 0)),
                  pl.BlockSpec((D_MODEL, NH), lambda i: (0, 0))],
        out_specs=pl.BlockSpec((ROW_TILE, NH), lambda i: (i, 0)),
        out_shape=jax.ShapeDtypeStruct((t, NH), F32),
        compiler_params=_cp("arbitrary"),
        name="in_proj",
    )(x, w)


def _rope(x, cos, sin):
    w = x.shape[1]
    if w > LANE:
        cos = jnp.concatenate([cos] * (w // LANE), axis=1)
        sin = jnp.concatenate([sin] * (w // LANE), axis=1)
    lane = _iota(x.shape, 1) & (HEAD_DIM - 1)
    swapped = jnp.where(lane < ROT_DIM // 2, pltpu.roll(x, w - ROT_DIM // 2, 1), pltpu.roll(x, ROT_DIM // 2, 1))
    return x * cos + swapped * sin


def _sink_attention(q, kk, vv, mask, sinks_ref):
    outs = []
    for hq in range(A_HEADS):
        g = hq // (A_HEADS // A_KV_HEADS)
        qh = _bf(q[:, hq * HEAD_DIM:(hq + 1) * HEAD_DIM])
        kh = _bf(kk[:, g * HEAD_DIM:(g + 1) * HEAD_DIM])
        vh = _bf(vv[:, g * HEAD_DIM:(g + 1) * HEAD_DIM])
        s = _dot_nt(qh, kh) * (HEAD_DIM ** -0.5)
        s = jnp.where(mask, s, -jnp.inf)
        sink = sinks_ref[hq]
        m = jnp.maximum(jnp.max(s, axis=-1, keepdims=True), sink)
        p = jnp.exp(s - m)
        den = jnp.sum(p, axis=-1, keepdims=True) + jnp.exp(sink - m)
        outs.append(_dot(_bf(p / den), vh))
    return jnp.concatenate(outs, axis=1)


def _swa_prompt_body(sinks_ref, cur_ref, prev_ref, cos_ref, sin_ref, cosp_ref, sinp_ref, o_ref, ko_ref, vo_ref):
    i = pl.program_id(1)
    cur = cur_ref[...]
    q = _rope(cur[:, C_AQ:C_AQ + 256], cos_ref[...], sin_ref[...])
    k = _rope(cur[:, C_AK:C_AK + 128], cos_ref[...], sin_ref[...])
    v = cur[:, C_AV:C_AV + 128]
    prev = prev_ref[...]
    kp = _rope(prev[:, 0:128], cosp_ref[...], sinp_ref[...])
    vp = prev[:, 128:256]
    kk = jnp.concatenate([kp, k], axis=0)
    vv = jnp.concatenate([vp, v], axis=0)
    r = _iota((WINDOW, 2 * WINDOW), 0)
    j = _iota((WINDOW, 2 * WINDOW), 1)
    d = WINDOW + r - j
    mask = (d >= 0) & (d <= WINDOW) & ((j >= WINDOW) | (i > 0))
    o_ref[...] = _sink_attention(q, kk, vv, mask, sinks_ref)
    ko_ref[0] = k
    vo_ref[0] = v


def swa_prompt(h, sinks, cos_t, sin_t, bsz, seq):
    nb = seq // WINDOW
    smem = pl.BlockSpec(memory_space=pltpu.SMEM)
    tab = lambda f: pl.BlockSpec((WINDOW, LANE), f)
    return pl.pallas_call(
        _swa_prompt_body,
        grid=(bsz, nb),
        in_specs=[smem,
                  pl.BlockSpec((WINDOW, 512), lambda b, i: (b * nb + i, 0)),
                  pl.BlockSpec((WINDOW, 256), lambda b, i: (b * nb + jnp.maximum(i - 1, 0), 1)),
                  tab(lambda b, i: (i, 0)), tab(lambda b, i: (i, 0)),
                  tab(lambda b, i: (jnp.maximum(i - 1, 0), 0)), tab(lambda b, i: (jnp.maximum(i - 1, 0), 0))],
        out_specs=[pl.BlockSpec((WINDOW, 256), lambda b, i: (b * nb + i, 0)),
                   pl.BlockSpec((1, WINDOW, 128), lambda b, i: (b, 0, 0)),
                   pl.BlockSpec((1, WINDOW, 128), lambda b, i: (b, 0, 0))],
        out_shape=[jax.ShapeDtypeStruct((bsz * seq, 256), F32),
                   jax.ShapeDtypeStruct((bsz, WINDOW, 128), F32),
                   jax.ShapeDtypeStruct((bsz, WINDOW, 128), F32)],
        compiler_params=_cp("arbitrary", "arbitrary"),
        name="swa_prompt",
    )(sinks, h, h, cos_t, sin_t, cos_t, sin_t)


SWA_SB = 8


def _swa_sample_body(sinks_ref, cur_ref, kc_ref, vc_ref, cos_ref, sin_ref, o_ref, ko_ref, vo_ref, *, ls):
    cw = WINDOW
    r = _iota((ls, cw + ls), 0)
    j = _iota((ls, cw + ls), 1)
    d = cw + r - j
    mask = (d >= 0) & (d <= WINDOW)
    for b in range(SWA_SB):
        cur = cur_ref[b * ls:(b + 1) * ls, :]
        q = _rope(cur[:, C_AQ:C_AQ + 256], cos_ref[...], sin_ref[...])
        k = _rope(cur[:, C_AK:C_AK + 128], cos_ref[...], sin_ref[...])
        v = cur[:, C_AV:C_AV + 128]
        kk = jnp.concatenate([kc_ref[b], k], axis=0)
        vv = jnp.concatenate([vc_ref[b], v], axis=0)
        o_ref[b * ls:(b + 1) * ls, :] = _sink_attention(q, kk, vv, mask, sinks_ref)
        ko_ref[b] = kk[ls:, :]
        vo_ref[b] = vv[ls:, :]


def swa_sample(h, row0, sinks, k_cache, v_cache, cos_t, sin_t, bsz, ls):
    rows = SWA_SB * ls
    blk0 = row0 // rows
    smem = pl.BlockSpec(memory_space=pltpu.SMEM)
    cache = pl.BlockSpec((SWA_SB, WINDOW, 128), lambda i: (i, 0, 0))
    tab = pl.BlockSpec((ls, LANE), lambda i: (0, 0))
    return pl.pallas_call(
        functools.partial(_swa_sample_body, ls=ls),
        grid=(bsz // SWA_SB,),
        in_specs=[smem, pl.BlockSpec((rows, 512), lambda i: (blk0 + i, 0)), cache, cache, tab, tab],
        out_specs=[pl.BlockSpec((rows, 256), lambda i: (i, 0)), cache, cache],
        out_shape=[jax.ShapeDtypeStruct((bsz * ls, 256), F32),
                   jax.ShapeDtypeStruct((bsz, WINDOW, 128), F32),
                   jax.ShapeDtypeStruct((bsz, WINDOW, 128), F32)],
        compiler_params=_cp("arbitrary"),
        name="swa_sample",
    )(sinks, h, k_cache, v_cache, cos_t, sin_t)


S5_NB = 2 * SSM_W // LANE


def _s5_input(u, bw_ref):
    t = _dot(_bf(u), bw_ref[...])
    return jnp.concatenate([t[:, :SSM_W], t[:, :SSM_W] + t[:, SSM_W:]], axis=1)


def _s5_output(hs, u, cw_ref, d_ref, gw_ref, gb_ref):
    hs = jnp.concatenate([hs[:, :SSM_W] + hs[:, SSM_W:], hs[:, SSM_W:]], axis=1)
    y = _dot(_bf(hs), cw_ref[...]) + d_ref[...] * u
    y = _gelu_tanh(y)
    z = _dot(_bf(y), gw_ref[...]) + gb_ref[...]
    return z[:, :GROUP_WIDTH] * _sigmoid(z[:, GROUP_WIDTH:])


def _s5_scan(s_ref, a_ref, h_init, n_steps, rows):
    nre = S5_NB // 2
    a_re = [jnp.broadcast_to(a_ref[0:1, j * LANE:(j + 1) * LANE], (rows, LANE)) for j in range(nre)]
    a_im = [jnp.broadcast_to(a_ref[1:2, j * LANE:(j + 1) * LANE], (rows, LANE)) for j in range(nre)]

    def step(t, hcar):
        out = [None] * S5_NB
        base = pl.multiple_of(t * rows, rows)
        for j in range(nre):
            hr, hi = hcar[j], hcar[j + nre]
            nr = a_re[j] * hr - a_im[j] * hi + s_ref[j, pl.ds(base, rows), :]
            ni = a_re[j] * hi + a_im[j] * hr + s_ref[j + nre, pl.ds(base, rows), :]
            s_ref[j, pl.ds(base, rows), :] = nr
            s_ref[j + nre, pl.ds(base, rows), :] = ni
            out[j], out[j + nre] = nr, ni
        return tuple(out)

    return lax.fori_loop(0, n_steps, step, tuple(h_init))


def _s5_prompt_body(*refs, nb, tl):
    u_refs = refs[:nb]
    bw_ref, a_ref, cw_ref, d_ref, gw_ref, gb_ref, h0_ref, o_ref, hl_ref, s_ref, hst_ref = refs[nb:]
    i = pl.program_id(0)

    @pl.when(i == 0)
    def _():
        hst_ref[...] = h0_ref[...]

    for b in range(nb):
        bu = _s5_input(u_refs[b][...], bw_ref)
        for j in range(S5_NB):
            s_ref[j, pl.ds(b, tl, stride=nb), :] = bu[:, j * LANE:(j + 1) * LANE]
    h_init = [hst_ref[:, j * LANE:(j + 1) * LANE] for j in range(S5_NB)]
    h_fin = _s5_scan(s_ref, a_ref, h_init, tl, nb)
    for j in range(S5_NB):
        hst_ref[:, j * LANE:(j + 1) * LANE] = h_fin[j]
    for b in range(nb):
        hs = jnp.concatenate([s_ref[j, pl.ds(b, tl, stride=nb), :] for j in range(S5_NB)], axis=1)
        o_ref[b] = _s5_output(hs, u_refs[b][...], cw_ref, d_ref, gw_ref, gb_ref)

    @pl.when(i == pl.num_programs(0) - 1)
    def _():
        hl_ref[...] = hst_ref[...]


def s5_prompt(h, h0, bw, a, cw, dsk, gw, gb, bsz, seq, tl=128):
    assert bsz == 8
    nt = seq // tl
    full = lambda shape: pl.BlockSpec(shape, lambda i: (0,) * len(shape))
    u_specs = [pl.BlockSpec((tl, GROUP_WIDTH), functools.partial(lambda i, b: (b * nt + i, C_SU // GROUP_WIDTH), b=b))
               for b in range(bsz)]
    return pl.pallas_call(
        functools.partial(_s5_prompt_body, nb=bsz, tl=tl),
        grid=(nt,),
        in_specs=u_specs + [full((GROUP_WIDTH, 2 * SSM_W)), full((2, SSM_W)), full((2 * SSM_W, GROUP_WIDTH)),
                            full((1, GROUP_WIDTH)), full((GROUP_WIDTH, 2 * GROUP_WIDTH)), full((1, 2 * GROUP_WIDTH)),
                            full((bsz, 2 * SSM_W))],
        out_specs=[pl.BlockSpec((bsz, tl, GROUP_WIDTH), lambda i: (0, i, 0)), full((bsz, 2 * SSM_W))],
        out_shape=[jax.ShapeDtypeStruct((bsz, seq, GROUP_WIDTH), F32), jax.ShapeDtypeStruct((bsz, 2 * SSM_W), F32)],
        scratch_shapes=[pltpu.VMEM((S5_NB, tl * bsz, LANE), F32), pltpu.VMEM((bsz, 2 * SSM_W), F32)],
        compiler_params=_cp("arbitrary"),
        name="s5_prompt",
    )(*([h] * bsz), bw, a, cw, dsk, gw, gb, h0)


def _s5_sample_body(u_ref, bw_ref, a_ref, cw_ref, d_ref, gw_ref, gb_ref, h0_ref, o_ref, hl_ref, s_ref, t_ref, *, bsz, ls):
    bu = _s5_input(u_ref[...], bw_ref)
    for j in range(S5_NB):
        s_ref[j] = bu[:, j * LANE:(j + 1) * LANE]
    for t in range(ls):
        for j in range(S5_NB):
            t_ref[j, pl.ds(t * bsz, bsz), :] = s_ref[j, pl.ds(t, bsz, stride=ls), :]
    h_init = [h0_ref[:, j * LANE:(j + 1) * LANE] for j in range(S5_NB)]
    h_fin = _s5_scan(t_ref, a_ref, h_init, ls, bsz)
    for j in range(S5_NB):
        hl_ref[:, j * LANE:(j + 1) * LANE] = h_fin[j]
    for t in range(ls):
        for j in range(S5_NB):
            s_ref[j, pl.ds(t, bsz, stride=ls), :] = t_ref[j, pl.ds(t * bsz, bsz), :]
    hs = jnp.concatenate([s_ref[j] for j in range(S5_NB)], axis=1)
    o_ref[...] = _s5_output(hs, u_ref[...], cw_ref, d_ref, gw_ref, gb_ref)


def s5_sample(h, row0, h0, bw, a, cw, dsk, gw, gb, bsz, ls):
    rows = bsz * ls
    full = lambda shape: pl.BlockSpec(shape, lambda i: (0,) * len(shape))
    return pl.pallas_call(
        functools.partial(_s5_sample_body, bsz=bsz, ls=ls),
        grid=(1,),
        in_specs=[pl.BlockSpec((rows, GROUP_WIDTH), lambda i: (row0 // rows, C_SU // GROUP_WIDTH)),
                  full((GROUP_WIDTH, 2 * SSM_W)), full((2, SSM_W)), full((2 * SSM_W, GROUP_WIDTH)),
                  full((1, GROUP_WIDTH)), full((GROUP_WIDTH, 2 * GROUP_WIDTH)), full((1, 2 * GROUP_WIDTH)),
                  full((bsz, 2 * SSM_W))],
        out_specs=[full((rows, GROUP_WIDTH)), full((bsz, 2 * SSM_W))],
        out_shape=[jax.ShapeDtypeStruct((rows, GROUP_WIDTH), F32), jax.ShapeDtypeStruct((bsz, 2 * SSM_W), F32)],
        scratch_shapes=[pltpu.VMEM((S5_NB, rows, LANE), F32), pltpu.VMEM((S5_NB, rows, LANE), F32)],
        compiler_params=_cp("arbitrary"),
        name="s5_sample",
    )(h, bw, a, cw, dsk, gw, gb, h0)


def s5_params(a_re, a_im, b_re, b_im, c_re, c_im, d_skip, log_dt):
    lam = lax.complex(a_re, a_im)
    delta = jnp.exp(log_dt)[:, None]
    a_bar = jnp.exp(lam * delta)
    b_bar = ((a_bar - 1.0) / lam)[..., None] * lax.complex(b_re, b_im)
    eye = jnp.eye(SSM_GROUPS, dtype=F32)
    bw_re = jnp.einsum('gpc,gh->gchp', b_bar.real, eye).reshape(GROUP_WIDTH, SSM_W)
    bw_im = jnp.einsum('gpc,gh->gchp', b_bar.imag, eye).reshape(GROUP_WIDTH, SSM_W)
    bw = jnp.concatenate([bw_re, bw_im - bw_re], axis=1).astype(BF16)
    cw_re = jnp.einsum('gcp,gh->gphc', c_re, eye).reshape(SSM_W, GROUP_WIDTH)
    cw_im = jnp.einsum('gcp,gh->gphc', c_im, eye).reshape(SSM_W, GROUP_WIDTH)
    cw = jnp.concatenate([cw_re, -(cw_re + cw_im)], axis=0).astype(BF16)
    a = jnp.stack([a_bar.real.reshape(SSM_W), a_bar.imag.reshape(SSM_W)], axis=0)
    return bw, a, cw, d_skip.reshape(1, GROUP_WIDTH)


def _gdn_body(qkv_ref, z_ref, sm_ref, cw_ref, alog_ref, dtb_ref, nw_ref, cs_ref, s0_ref,
              o_ref, co_ref, so_ref, xs_ref, st_ref, *, c):
    n = pl.program_id(1)

    @pl.when(n == 0)
    def _():
        xs_ref[0:8, :] = jnp.zeros((8, GDN_CONV_DIM), F32)
        xs_ref[8 - (GDN_CONV - 1):8, :] = cs_ref[0]
        st_ref[...] = s0_ref[0]

    x = qkv_ref[...]
    xs_ref[8:8 + c, :] = x
    conv = cw_ref[GDN_CONV - 1:GDN_CONV, :] * x
    for i in range(GDN_CONV - 1):
        sh = GDN_CONV - 1 - i
        conv = conv + cw_ref[i:i + 1, :] * xs_ref[8 - sh:8 - sh + c, :]
    tail = xs_ref[c:c + 8, :]
    xs_ref[0:8, :] = tail
    co_ref[0] = tail[8 - (GDN_CONV - 1):, :]
    qkv = _silu(conv)

    sm = sm_ref[...]
    beta_all = _sigmoid(sm)
    g_all = -jnp.exp(alog_ref[...]) * _softplus(sm + dtb_ref[...])
    ri = _iota((c, c), 0)
    ci = _iota((c, c), 1)
    tri = (ci <= ri).astype(F32)
    gc_all = _dot(tri, g_all, HI)
    lane = _iota((c, LANE), 1)
    eye = (ri == ci).astype(F32)
    outs = []
    for hh in range(GDN_HEADS):
        q = qkv[:, hh * GDN_DK:(hh + 1) * GDN_DK]
        k = qkv[:, 256 + hh * GDN_DK:256 + (hh + 1) * GDN_DK]
        v = qkv[:, 512 + hh * GDN_DV:512 + (hh + 1) * GDN_DV]
        q = q * lax.rsqrt(jnp.sum(q * q, axis=-1, keepdims=True) + 1e-6) * (GDN_DK ** -0.5)
        k = k * lax.rsqrt(jnp.sum(k * k, axis=-1, keepdims=True) + 1e-6)
        beta = beta_all[:, SM_GB + hh:SM_GB + hh + 1]
        gc = gc_all[:, SM_GA + hh:SM_GA + hh + 1]
        sel = (lane == SM_GA + hh).astype(F32)
        gc_row = _dot_nt(sel, gc_all, HI)
        decay = jnp.exp(jnp.where(ci <= ri, gc - gc_row, -jnp.inf))
        kb = k * beta
        m = _dot_nt(_bf(kb), _bf(k)) * decay
        a_strict = jnp.where(ci < ri, m, 0.0)
        inv = eye
        s = 1
        while s < c:
            e = jnp.where((_shr(ri, 2 * s) == _shr(ci, 2 * s)) & (_shr(ri, s) != _shr(ci, s)), a_strict, 0.0)
            inv = inv - _dot(inv, _dot(e, inv, HI), HI)
            s *= 2
        eg = jnp.exp(gc)
        rhs = jnp.concatenate([v * beta, kb * eg], axis=1)
        sol = _dot(inv, rhs, HI)
        u, w = sol[:, :GDN_DV], sol[:, GDN_DV:]
        qk = _dot_nt(_bf(q), _bf(k)) * decay
        st = st_ref[hh]
        v_new = u - _dot(_bf(w), _bf(st))
        o = _dot(_bf(q * eg), _bf(st)) + _dot(_bf(qk), _bf(v_new))
        g_last = gc[c - 1:c, :]
        st_ref[hh] = st * jnp.exp(g_last) + _dot_tn(_bf(k * jnp.exp(g_last - gc)), _bf(v_new))
        o = o * lax.rsqrt(jnp.mean(o * o, axis=-1, keepdims=True) + RMS_EPS) * nw_ref[...]
        outs.append(o * _silu(z_ref[:, hh * GDN_DV:(hh + 1) * GDN_DV]))
    o_ref[...] = jnp.concatenate(outs, axis=1)
    so_ref[0] = st_ref[...]


def gdn_mixer(h, row0, conv_w, alog_t, dtb_t, norm_w, conv_state, s0, bsz, seq):
    c = min(GDN_CHUNK, seq)
    nc = seq // c
    blk0 = row0 // c
    row = lambda w, col: pl.BlockSpec((c, w), lambda b, n: (blk0 + b * nc + n, col))
    full = lambda shape: pl.BlockSpec(shape, lambda b, n: (0,) * len(shape))
    per_b = lambda shape: pl.BlockSpec((1,) + shape, lambda b, n: (b,) + (0,) * len(shape))
    return pl.pallas_call(
        functools.partial(_gdn_body, c=c),
        grid=(bsz, nc),
        in_specs=[row(GDN_CONV_DIM, C_GQKV // GDN_CONV_DIM), row(GROUP_WIDTH, C_GZ // GROUP_WIDTH), row(LANE, C_SM // LANE),
                  full((GDN_CONV, GDN_CONV_DIM)), full((1, LANE)), full((1, LANE)), full((1, GDN_DV)),
                  per_b((GDN_CONV - 1, GDN_CONV_DIM)), per_b((GDN_HEADS, GDN_DK, GDN_DV))],
        out_specs=[pl.BlockSpec((c, GROUP_WIDTH), lambda b, n: (b * nc + n, 0)),
                   per_b((GDN_CONV - 1, GDN_CONV_DIM)), per_b((GDN_HEADS, GDN_DK, GDN_DV))],
        out_shape=[jax.ShapeDtypeStruct((bsz * seq, GROUP_WIDTH), F32),
                   jax.ShapeDtypeStruct((bsz, GDN_CONV - 1, GDN_CONV_DIM), F32),
                   jax.ShapeDtypeStruct((bsz, GDN_HEADS, GDN_DK, GDN_DV), F32)],
        scratch_shapes=[pltpu.VMEM((8 + c, GDN_CONV_DIM), F32), pltpu.VMEM((GDN_HEADS, GDN_DK, GDN_DV), F32)],
        compiler_params=_cp("arbitrary", "arbitrary"),
        name="gdn",
    )(h, h, h, conv_w, alog_t, dtb_t, norm_w, conv_state, s0)


def _gla_body(qk_ref, vr_ref, sm_ref, gw_ref, gb_ref, nw_ref, s0_ref, o_ref, so_ref,
              kb_ref, bb_ref, vb_ref, st_ref, *, c, tr):
    n = pl.program_id(1)
    pad = GLA_CHUNK

    @pl.when(n == 0)
    def _():
        kb_ref[0:pad, :] = jnp.zeros((pad, 128), F32)
        bb_ref[0:pad, :] = jnp.zeros((pad, 128), F32)
        vb_ref[0:pad, :] = jnp.zeros((pad, 256), F32)
        st_ref[...] = s0_ref[0]

    q = qk_ref[:, 0:128] * (GLA_DK ** -0.5)
    k = qk_ref[:, 128:256]
    v = vr_ref[:, 0:256]
    r = vr_ref[:, 256:512]
    lg = sm_ref[:, SM_LG:SM_LG + GLA_RANK]
    log_a = _log_sigmoid(_dot(_bf(lg), _bf(gw_ref[...])) + gb_ref[...]) / GLA_TAU
    ri = _iota((tr, tr), 0)
    ci = _iota((tr, tr), 1)
    tri = ((ci <= ri) & (_shr(ri, c) == _shr(ci, c))).astype(F32)
    bc = _dot(tri, log_a, HI)
    kb_ref[pad:pad + tr, :] = k
    bb_ref[pad:pad + tr, :] = bc
    vb_ref[pad:pad + tr, :] = v
    ones_kv = (_shr(_iota((128, 256), 0), GLA_DK) == _shr(_iota((128, 256), 1), GLA_DV)).astype(F32)
    pos = _iota((tr, 128), 0) & (c - 1)
    o_intra = jnp.zeros((tr, 256), F32)
    for dl in range(c):
        ks = kb_ref[pad - dl:pad - dl + tr, :]
        bs = bb_ref[pad - dl:pad - dl + tr, :]
        vs = vb_ref[pad - dl:pad - dl + tr, :]
        p = jnp.where(pos >= dl, q * ks * jnp.exp(bc - bs), 0.0)
        o_intra = o_intra + _bf(_dot(p, ones_kv, HI)).astype(F32) * _bf(vs).astype(F32)
    bd = (_shr(_iota((256, 128), 0), GLA_DV) == _shr(_iota((256, 128), 1), GLA_DK)).astype(F32)
    ones_vv = (_shr(_iota((256, 256), 0), GLA_DV) == _shr(_iota((256, 256), 1), GLA_DV)).astype(F32)
    for ch in range(tr // c):
        sl = slice(ch * c, (ch + 1) * c)
        b_c = bc[sl, :]
        b_last = b_c[c - 1:c, :]
        st = st_ref[...]
        o = _dot_nt(_bf(q[sl, :] * jnp.exp(b_c)), _bf(st)) + o_intra[sl, :]
        st_ref[...] = st * jnp.exp(b_last) + bd * _dot_tn(_bf(v[sl, :]), _bf(k[sl, :] * jnp.exp(b_last - b_c)))
        ms = _dot(o * o, ones_vv, HI) * (1.0 / GLA_DV)
        o = o * lax.rsqrt(ms + RMS_EPS) * nw_ref[...]
        o_ref[sl, :] = o * _silu(r[sl, :])
    so_ref[0] = st_ref[...]


def gla_mixer(h, row0, gate_w, gate_b, norm_w4, s0t, bsz, seq):
    c = min(GLA_CHUNK, seq)
    tr = min(256, seq)
    nt = seq // tr
    blk0 = row0 // tr
    row = lambda w, col: pl.BlockSpec((tr, w), lambda b, n: (blk0 + b * nt + n, col))
    full = lambda shape: pl.BlockSpec(shape, lambda b, n: (0,) * len(shape))
    st_spec = pl.BlockSpec((1, 256, 128), lambda b, n: (b, 0, 0))
    return pl.pallas_call(
        functools.partial(_gla_body, c=c, tr=tr),
        grid=(bsz, nt),
        in_specs=[row(256, C_LQ // 256), row(512, C_LV // 512), row(LANE, C_SM // LANE),
                  full((GLA_RANK, 128)), full((1, 128)), full((1, 256)), st_spec],
        out_specs=[pl.BlockSpec((tr, 256), lambda b, n: (b * nt + n, 0)), st_spec],
        out_shape=[jax.ShapeDtypeStruct((bsz * seq, 256), F32), jax.ShapeDtypeStruct((bsz, 256, 128), F32)],
        scratch_shapes=[pltpu.VMEM((GLA_CHUNK + tr, 128), F32), pltpu.VMEM((GLA_CHUNK + tr, 128), F32),
                        pltpu.VMEM((GLA_CHUNK + tr, 256), F32), pltpu.VMEM((256, 128), F32)],
        compiler_params=_cp("arbitrary", "arbitrary"),
        name="gla",
    )(h, h, h, gate_w, gate_b, norm_w4, s0t)


def gla_state_to_t(s):
    eye = jnp.eye(GLA_HEADS, dtype=s.dtype)
    return jnp.einsum('bhde,hg->bhegd', s, eye).reshape(s.shape[0], 256, 128)


def gla_state_from_t(st):
    b = st.shape[0]
    t5 = st.reshape(b, GLA_HEADS, GLA_DV, GLA_HEADS, GLA_DK)
    diag = jnp.stack([t5[:, hh, :, hh, :] for hh in range(GLA_HEADS)], axis=1)
    return jnp.swapaxes(diag, 2, 3)


def _out_proj_body(x_ref, ap, bp, cp, dp, a_s, b_s, c_s, d_s, w_ref, g_ref, b_ref, o_ref, *, n_prompt_tiles):
    i = pl.program_id(0)

    def run(refs):
        acc = DN_ALPHA * x_ref[...]
        for m, r in enumerate(refs):
            acc = acc + _dot(_bf(r[...]), w_ref[m * GROUP_WIDTH:(m + 1) * GROUP_WIDTH, :])
        o_ref[...] = _layer_norm(acc, g_ref[...], b_ref[...])

    @pl.when(i < n_prompt_tiles)
    def _():
        run((ap, bp, cp, dp))

    @pl.when(i >= n_prompt_tiles)
    def _():
        run((a_s, b_s, c_s, d_s))


def out_proj_ln(x, mix_p, mix_s, w, g, b):
    t = x.shape[0]
    npt = mix_p[0].shape[0] // ROW_TILE
    nst = mix_s[0].shape[0] // ROW_TILE
    p_spec = pl.BlockSpec((ROW_TILE, GROUP_WIDTH), lambda i: (jnp.minimum(i, npt - 1), 0))
    s_spec = pl.BlockSpec((ROW_TILE, GROUP_WIDTH), lambda i: (jnp.clip(i - npt, 0, nst - 1), 0))
    full = lambda shape: pl.BlockSpec(shape, lambda i: (0,) * len(shape))
    return pl.pallas_call(
        functools.partial(_out_proj_body, n_prompt_tiles=npt),
        grid=(t // ROW_TILE,),
        in_specs=[pl.BlockSpec((ROW_TILE, D_MODEL), lambda i: (i, 0))] + [p_spec] * 4 + [s_spec] * 4
                 + [full((D_MODEL, D_MODEL)), full((1, D_MODEL)), full((1, D_MODEL))],
        out_specs=pl.BlockSpec((ROW_TILE, D_MODEL), lambda i: (i, 0)),
        out_shape=jax.ShapeDtypeStruct((t, D_MODEL), F32),
        compiler_params=_cp("arbitrary"),
        name="out_proj_ln",
    )(x, *mix_p, *mix_s, w, g, b)


def _router_body(x_ref, w_ref, b_ref, e_ref, g_ref):
    logits = _dot(_bf(x_ref[...]), _bf(w_ref[...])) + b_ref[...]
    lane = _iota(logits.shape, 1)
    vals = jnp.where(lane < N_EXPERTS, logits, -jnp.inf)
    eo = jnp.zeros(logits.shape, I32)
    top = []
    for k in range(TOP_K):
        m = jnp.max(vals, axis=-1, keepdims=True)
        idx = jnp.min(jnp.where(vals == m, lane.astype(F32), float(LANE)), axis=-1, keepdims=True).astype(I32)
        eo = jnp.where(lane == k, idx, eo)
        top.append(m)
        vals = jnp.where(lane == idx, -jnp.inf, vals)
    ex = [jnp.exp(v - top[0]) for v in top]
    den = ex[0] + ex[1] + ex[2] + ex[3]
    go = jnp.zeros(logits.shape, F32)
    for k in range(TOP_K):
        go = jnp.where(lane == k, ex[k] / den, go)
    e_ref[...] = eo
    g_ref[...] = go


def router(x, w, b):
    t = x.shape[0]
    return pl.pallas_call(
        _router_body,
        grid=(t // ROW_TILE,),
        in_specs=[pl.BlockSpec((ROW_TILE, D_MODEL), lambda i: (i, 0)),
                  pl.BlockSpec((D_MODEL, LANE), lambda i: (0, 0)), pl.BlockSpec((1, LANE), lambda i: (0, 0))],
        out_specs=[pl.BlockSpec((ROW_TILE, LANE), lambda i: (i, 0))] * 2,
        out_shape=[jax.ShapeDtypeStruct((t, LANE), I32), jax.ShapeDtypeStruct((t, LANE), F32)],
        compiler_params=_cp("arbitrary"),
        name="router",
    )(x, w, b)


DISPATCH_TILE = 256


def _dispatch_copy(x_ref, xs_ref, sem, r, d):
    return pltpu.make_async_copy(x_ref.at[pl.ds(r, 1), :], xs_ref.at[pl.ds(d, 1), :], sem)


def _dispatch_body(dest_ref, x_ref, xs_in_ref, xs_ref, sem):
    del xs_in_ref

    def start(r, carry):
        for k in range(TOP_K):
            _dispatch_copy(x_ref, xs_ref, sem, r, dest_ref[0, 0, r * TOP_K + k]).start()
        return carry

    lax.fori_loop(0, DISPATCH_TILE, start, 0)

    def wait(r, carry):
        for k in range(TOP_K):
            _dispatch_copy(x_ref, xs_ref, sem, 0, 0).wait()
        return carry

    lax.fori_loop(0, DISPATCH_TILE, wait, 0)


def moe_dispatch(x, dest, n_rows):
    t = x.shape[0]
    nt = t // DISPATCH_TILE
    zeros = jnp.zeros((n_rows, D_MODEL), F32)
    return pl.pallas_call(
        _dispatch_body,
        grid=(nt,),
        in_specs=[pl.BlockSpec((1, 1, DISPATCH_TILE * TOP_K), lambda i: (i, 0, 0), memory_space=pltpu.SMEM),
                  pl.BlockSpec((DISPATCH_TILE, D_MODEL), lambda i: (i, 0)),
                  pl.BlockSpec(memory_space=pl.ANY)],
        out_specs=pl.BlockSpec(memory_space=pl.ANY),
        out_shape=jax.ShapeDtypeStruct((n_rows, D_MODEL), F32),
        scratch_shapes=[pltpu.SemaphoreType.DMA(())],
        input_output_aliases={2: 0},
        compiler_params=_cp("arbitrary"),
        name="moe_dispatch",
    )(dest.reshape(nt, 1, DISPATCH_TILE * TOP_K), x, zeros)


def _expert_body(te_ref, tv_ref, xs_ref, wgu_ref, bgu_ref, wd_ref, bd_ref, ys_ref):
    i = pl.program_id(0)

    @pl.when(tv_ref[i] != 0)
    def _():
        gu = _dot(_bf(xs_ref[...]), wgu_ref[0]) + bgu_ref[0]
        nxt = pltpu.roll(gu, 2 * D_FF - 1, 1)
        gate = jnp.minimum(gu, SWIGLU_LIMIT)
        up = jnp.clip(nxt, -SWIGLU_LIMIT, SWIGLU_LIMIT)
        act = gate * _sigmoid(gate * SWIGLU_ALPHA) * (up + 1.0)
        even = (_iota(act.shape, 1) & 1) == 0
        act = jnp.where(even, act, 0.0)
        ys_ref[...] = _dot(_bf(act), wd_ref[0]) + bd_ref[0]

    @pl.when(tv_ref[i] == 0)
    def _():
        ys_ref[...] = jnp.zeros(ys_ref.shape, F32)


def moe_experts(xs, tile_e, tile_valid, wgu, bgu, wd2, bd):
    n_rows = xs.shape[0]
    nt = n_rows // MOE_TILE
    grid_spec = pltpu.PrefetchScalarGridSpec(
        num_scalar_prefetch=2,
        grid=(nt,),
        in_specs=[pl.BlockSpec((MOE_TILE, D_MODEL), lambda i, te, tv: (i, 0)),
                  pl.BlockSpec((1, D_MODEL, 2 * D_FF), lambda i, te, tv: (te[i], 0, 0)),
                  pl.BlockSpec((1, 1, 2 * D_FF), lambda i, te, tv: (te[i], 0, 0)),
                  pl.BlockSpec((1, 2 * D_FF, D_MODEL), lambda i, te, tv: (te[i], 0, 0)),
                  pl.BlockSpec((1, 1, D_MODEL), lambda i, te, tv: (te[i], 0, 0))],
        out_specs=pl.BlockSpec((MOE_TILE, D_MODEL), lambda i, te, tv: (i, 0)),
    )
    return pl.pallas_call(
        _expert_body,
        grid_spec=grid_spec,
        out_shape=jax.ShapeDtypeStruct((n_rows, D_MODEL), F32),
        compiler_params=_cp("arbitrary"),
        name="moe_experts",
    )(tile_e, tile_valid, xs, wgu, bgu, wd2, bd)


def _combine_copy(ys_ref, buf_ref, sem, d, k, r):
    return pltpu.make_async_copy(ys_ref.at[pl.ds(d, 1), :], buf_ref.at[k, pl.ds(r, 1), :], sem)


def _combine_body(dest_ref, gates_ref, x_ref, g_ref, b_ref, ys_ref, o_ref, buf_ref, sem):
    def start(r, carry):
        for k in range(TOP_K):
            _combine_copy(ys_ref, buf_ref, sem, dest_ref[0, 0, r * TOP_K + k], k, r).start()
        return carry

    lax.fori_loop(0, DISPATCH_TILE, start, 0)

    def wait(r, carry):
        for k in range(TOP_K):
            _combine_copy(ys_ref, buf_ref, sem, 0, k, 0).wait()
        return carry

    lax.fori_loop(0, DISPATCH_TILE, wait, 0)
    acc = DN_ALPHA * x_ref[...]
    gates = gates_ref[...]
    for k in range(TOP_K):
        acc = acc + gates[:, k:k + 1] * buf_ref[k]
    o_ref[...] = _layer_norm(acc, g_ref[...], b_ref[...])


def moe_combine_ln(x, ys, dest, gates, g, b):
    t = x.shape[0]
    nt = t // DISPATCH_TILE
    full = lambda shape: pl.BlockSpec(shape, lambda i: (0,) * len(shape))
    return pl.pallas_call(
        _combine_body,
        grid=(nt,),
        in_specs=[pl.BlockSpec((1, 1, DISPATCH_TILE * TOP_K), lambda i: (i, 0, 0), memory_space=pltpu.SMEM),
                  pl.BlockSpec((DISPATCH_TILE, LANE), lambda i: (i, 0)),
                  pl.BlockSpec((DISPATCH_TILE, D_MODEL), lambda i: (i, 0)),
                  full((1, D_MODEL)), full((1, D_MODEL)),
                  pl.BlockSpec(memory_space=pl.ANY)],
        out_specs=pl.BlockSpec((DISPATCH_TILE, D_MODEL), lambda i: (i, 0)),
        out_shape=jax.ShapeDtypeStruct((t, D_MODEL), F32),
        scratch_shapes=[pltpu.VMEM((TOP_K, DISPATCH_TILE, D_MODEL), F32), pltpu.SemaphoreType.DMA(())],
        compiler_params=_cp("arbitrary"),
        name="moe_combine_ln",
    )(dest.reshape(nt, 1, DISPATCH_TILE * TOP_K), gates, x, g, b, ys)


def moe_plan(top_e, n_tokens):
    tk = n_tokens * TOP_K
    flat_e = top_e.reshape(tk)
    onehot = (flat_e[:, None] == jnp.arange(N_EXPERTS, dtype=I32)[None, :]).astype(I32)
    csum = jnp.cumsum(onehot, axis=0)
    rank = jnp.take_along_axis(csum, flat_e[:, None], axis=1)[:, 0] - 1
    counts = csum[-1]
    ntile = (counts + MOE_TILE - 1) // MOE_TILE
    tile_end = jnp.cumsum(ntile)
    tile_start = tile_end - ntile
    dest = tile_start[flat_e] * MOE_TILE + rank
    n_tiles = -(-tk // MOE_TILE) + N_EXPERTS
    tiles = jnp.arange(n_tiles, dtype=I32)
    tile_e = jnp.minimum(jnp.searchsorted(tile_end, tiles, side='right'), N_EXPERTS - 1).astype(I32)
    tile_valid = (tiles < tile_end[-1]).astype(I32)
    return dest.astype(I32), tile_e, tile_valid, n_tiles * MOE_TILE


def moe_ffn_ln(x1, rw, rb, wgu, bgu, wd2, bd, g, b):
    t = x1.shape[0]
    e_pad, gates = router(x1, rw, rb)
    dest, tile_e, tile_valid, n_rows = moe_plan(e_pad[:, :TOP_K], t)
    xs = moe_dispatch(x1, dest, n_rows)
    ys = moe_experts(xs, tile_e, tile_valid, wgu, bgu, wd2, bd)
    return moe_combine_ln(x1, ys, dest, gates, g, b)


def _rope_tables(pos):
    half = ROT_DIM // 2
    inv_freq = ROPE_THETA ** (-jnp.arange(half, dtype=F32) / half)
    ang = pos.astype(F32)[:, None] * inv_freq[None, :]
    cos, sin = jnp.cos(ang), jnp.sin(ang)
    n = pos.shape[0]
    ones = jnp.ones((n, HEAD_DIM - ROT_DIM), F32)
    cos_h = jnp.concatenate([cos, cos, ones], axis=1)
    sin_h = jnp.concatenate([-sin, sin, 0.0 * ones], axis=1)
    return jnp.concatenate([cos_h, cos_h], axis=1), jnp.concatenate([sin_h, sin_h], axis=1)


def _lane_row(v, offset):
    return jnp.zeros((1, LANE), F32).at[0, offset:offset + v.shape[0]].set(v)


def kernel(x_prompt, x_sample, cache_swa_k, cache_swa_v, state_ssm_re, state_ssm_im, state_gdn_conv, state_gdn, state_gla, w_in, w_out, attn_sinks, ssm_a_re, ssm_a_im, ssm_b_re, ssm_b_im, ssm_c_re, ssm_c_im, ssm_d, ssm_log_dt, ssm_glu_w, ssm_glu_b, gdn_conv_w, gdn_a_log, gdn_dt_bias, gdn_norm_w, gla_gate_w, gla_gate_b, gla_norm_w, ln1_g, ln1_b, ln2_g, ln2_b, router_w, router_b, moe_w_gate_up, moe_b_gate_up, moe_w_down, moe_b_down):
    bp, lp, _ = x_prompt.shape
    bs, ls, _ = x_sample.shape
    n_p, n_s = bp * lp, bs * ls
    depth = w_in.shape[0]

    w_in_r = jnp.concatenate([w_in[..., :1792], w_in[..., 1800:2568], w_in[..., 1792:1800], w_in[..., 2568:N_IN],
                              jnp.zeros(w_in.shape[:2] + (NH - N_IN,), w_in.dtype)], axis=-1).astype(BF16)
    w_out_b = w_out.astype(BF16)
    glu_w_b = ssm_glu_w.astype(BF16)
    wgu_b = moe_w_gate_up.astype(BF16)
    wd2_b = lax.pad(moe_w_down.astype(BF16), jnp.zeros((), BF16), [(0, 0, 0), (0, 0, 0), (0, 1, 1), (0, 0, 0)])
    rw_pad = jnp.pad(router_w, ((0, 0), (0, 0), (0, LANE - N_EXPERTS)))
    rb_pad = jnp.pad(router_b, ((0, 0), (0, LANE - N_EXPERTS)))

    cos_p, sin_p = _rope_tables(jnp.arange(lp, dtype=I32))
    cos_s, sin_s = _rope_tables(PAST_LEN + jnp.arange(ls, dtype=I32))

    x = jnp.concatenate([x_prompt.reshape(n_p, D_MODEL), x_sample.reshape(n_s, D_MODEL)], axis=0)
    zeros = lambda *s: jnp.zeros(s, F32)
    new_p = [[] for _ in range(7)]
    new_s = [[] for _ in range(7)]
    for l in range(depth):
        h = in_proj(x, w_in_r[l])
        sinks = attn_sinks[l]
        oa_p, pk, pv = swa_prompt(h, sinks, cos_p, sin_p, bp, lp)
        oa_s, sk, sv = swa_sample(h, n_p, sinks, cache_swa_k[l].reshape(bs, WINDOW, 128),
                                  cache_swa_v[l].reshape(bs, WINDOW, 128), cos_s, sin_s, bs, ls)
        bw, a_bar, cw, dsk = s5_params(ssm_a_re[l], ssm_a_im[l], ssm_b_re[l], ssm_b_im[l], ssm_c_re[l], ssm_c_im[l],
                                       ssm_d[l], ssm_log_dt[l])
        glu_b = ssm_glu_b[l].reshape(1, 2 * GROUP_WIDTH)
        ob_p, hl_p = s5_prompt(h, zeros(bp, 2 * SSM_W), bw, a_bar, cw, dsk, glu_w_b[l], glu_b, bp, lp)
        h0_s = jnp.concatenate([state_ssm_re[l].reshape(bs, SSM_W), state_ssm_im[l].reshape(bs, SSM_W)], axis=1)
        ob_s, hl_s = s5_sample(h, n_p, h0_s, bw, a_bar, cw, dsk, glu_w_b[l], glu_b, bs, ls)
        alog_t = _lane_row(gdn_a_log[l], SM_GA)
        dtb_t = _lane_row(gdn_dt_bias[l], SM_GA)
        gnw = gdn_norm_w[l].reshape(1, GDN_DV)
        oc_p, cv_p, gs_p = gdn_mixer(h, 0, gdn_conv_w[l], alog_t, dtb_t, gnw, zeros(bp, GDN_CONV - 1, GDN_CONV_DIM),
                                     zeros(bp, GDN_HEADS, GDN_DK, GDN_DV), bp, lp)
        oc_s, cv_s, gs_s = gdn_mixer(h, n_p, gdn_conv_w[l], alog_t, dtb_t, gnw, state_gdn_conv[l], state_gdn[l], bs, ls)
        lgb = gla_gate_b[l].reshape(1, 128)
        lnw = jnp.tile(gla_norm_w[l], GLA_HEADS).reshape(1, 256)
        od_p, lt_p = gla_mixer(h, 0, gla_gate_w[l], lgb, lnw, zeros(bp, 256, 128), bp, lp)
        od_s, lt_s = gla_mixer(h, n_p, gla_gate_w[l], lgb, lnw, gla_state_to_t(state_gla[l]), bs, ls)

        x1 = out_proj_ln(x, (oa_p, ob_p.reshape(n_p, GROUP_WIDTH), oc_p, od_p), (oa_s, ob_s, oc_s, od_s),
                         w_out_b[l], ln1_g[l].reshape(1, D_MODEL), ln1_b[l].reshape(1, D_MODEL))
        x = moe_ffn_ln(x1, rw_pad[l], rb_pad[l].reshape(1, LANE), wgu_b[l], moe_b_gate_up[l].reshape(N_EXPERTS, 1, 2 * D_FF),
                       wd2_b[l], moe_b_down[l].reshape(N_EXPERTS, 1, D_MODEL),
                       ln2_g[l].reshape(1, D_MODEL), ln2_b[l].reshape(1, D_MODEL))

        st_p = (pk.reshape(bp, WINDOW, A_KV_HEADS, HEAD_DIM), pv.reshape(bp, WINDOW, A_KV_HEADS, HEAD_DIM),
                hl_p[:, :SSM_W].reshape(bp, SSM_GROUPS, SSM_STATE), hl_p[:, SSM_W:].reshape(bp, SSM_GROUPS, SSM_STATE),
                cv_p, gs_p, gla_state_from_t(lt_p))
        st_s = (sk.reshape(bs, WINDOW, A_KV_HEADS, HEAD_DIM), sv.reshape(bs, WINDOW, A_KV_HEADS, HEAD_DIM),
                hl_s[:, :SSM_W].reshape(bs, SSM_GROUPS, SSM_STATE), hl_s[:, SSM_W:].reshape(bs, SSM_GROUPS, SSM_STATE),
                cv_s, gs_s, gla_state_from_t(lt_s))
        for i in range(7):
            new_p[i].append(st_p[i])
            new_s[i].append(st_s[i])
    y_p = x[:n_p].reshape(bp, lp, D_MODEL)
    y_s = x[n_p:].reshape(bs, ls, D_MODEL)
    return (y_p, y_s) + tuple(jnp.stack(t, axis=0) for t in new_p) + tuple(jnp.stack(t, axis=0) for t in new_s)
```

```python
import functools

import numpy as np
import jax
import jax.numpy as jnp
from jax import lax
from jax.experimental import pallas as pl
from jax.experimental.pallas import tpu as pltpu

F32 = jnp.float32
BF16 = jnp.bfloat16
I32 = jnp.int32
HI = lax.Precision.HIGHEST

D_MODEL = 1024
DEPTH = 4
PAST_LEN = 8192
GROUP_WIDTH = 256
HEAD_DIM = 64
A_HEADS = 4
A_KV_HEADS = 2
WINDOW = 128
ROPE_THETA = 500000.0
ROT_DIM = 16
SSM_GC = 16
SSM_GROUPS = 16
SSM_STATE = 64
SSM_W = SSM_GROUPS * SSM_STATE
GDN_HEADS = 4
GDN_DK = 64
GDN_DV = 64
GDN_CONV = 4
GDN_CONV_DIM = 768
GDN_CHUNK = 64
GLA_HEADS = 4
GLA_DK = 32
GLA_DV = 64
GLA_RANK = 16
GLA_TAU = 16.0
GLA_CHUNK = 16
N_EXPERTS = 32
TOP_K = 4
D_FF = 1024
SWIGLU_LIMIT = 7.0
SWIGLU_ALPHA = 1.702
DN_ALPHA = (2 * DEPTH) ** 0.25
LN_EPS = 1e-5
RMS_EPS = 1e-6
N_IN = 2584

C_AQ, C_AK, C_AV, C_SU, C_GQKV, C_GZ = 0, 256, 384, 512, 768, 1536
C_LQ, C_LK, C_LV, C_LR, C_SM = 1792, 1920, 2048, 2304, 2560
NH = 2688
SM_GB, SM_GA, SM_LG = 0, 4, 8

LANE = 128
ROW_TILE = 512
MOE_TILE = 512
VMEM_LIMIT = 56 * 1024 * 1024


def _cp(*sem):
    return pltpu.CompilerParams(dimension_semantics=sem, vmem_limit_bytes=VMEM_LIMIT)


def _dot(a, b, precision=None):
    return jnp.dot(a, b, preferred_element_type=F32, precision=precision)


def _dot_nt(a, b, precision=None):
    return lax.dot_general(a, b, (((1,), (1,)), ((), ())), preferred_element_type=F32, precision=precision)


def _dot_tn(a, b, precision=None):
    return lax.dot_general(a, b, (((0,), (0,)), ((), ())), preferred_element_type=F32, precision=precision)


def _bf(x):
    return x.astype(BF16)


def _iota(shape, dim):
    return lax.broadcasted_iota(I32, shape, dim)


def _shr(idx, size):
    return lax.shift_right_logical(idx, int(size).bit_length() - 1)


def _sigmoid(x):
    return 1.0 / (1.0 + jnp.exp(-x))


def _silu(x):
    return x * _sigmoid(x)


def _softplus(x):
    return jnp.maximum(x, 0.0) + jnp.log(1.0 + jnp.exp(-jnp.abs(x)))


def _log_sigmoid(x):
    return -_softplus(-x)


def _gelu_tanh(x):
    return 0.5 * x * (1.0 + jnp.tanh(0.7978845608028654 * (x + 0.044715 * x * x * x)))


def _layer_norm(y, g, b):
    mu = jnp.mean(y, axis=-1, keepdims=True)
    yc = y - mu
    var = jnp.mean(yc * yc, axis=-1, keepdims=True)
    return yc * lax.rsqrt(var + LN_EPS) * g + b


def _in_proj_body(x_ref, w_ref, o_ref):
    xb = _bf(x_ref[...])
    for c0 in range(0, NH, 512):
        c1 = min(c0 + 512, NH)
        o_ref[:, c0:c1] = _dot(xb, w_ref[:, c0:c1])


def in_proj(x, w):
    t = x.shape[0]
    return pl.pallas_call(
        _in_proj_body,
        grid=(t // ROW_TILE,),
        in_specs=[pl.BlockSpec((ROW_TILE, D_MODEL), lambda i: (i, 0)),
                  pl.BlockSpec((D_MODEL, NH), lambda i: (0, 0))],
        out_specs=pl.BlockSpec((ROW_TILE, NH), lambda i: (i, 0)),
        out_shape=jax.ShapeDtypeStruct((t, NH), F32),
        compiler_params=_cp("arbitrary"),
        name="in_proj",
    )(x, w)


def _rope(x, cos, sin):
    w = x.shape[1]
    if w > LANE:
        cos = jnp.concatenate([cos] * (w // LANE), axis=1)
        sin = jnp.concatenate([sin] * (w // LANE), axis=1)
    lane = _iota(x.shape, 1) & (HEAD_DIM - 1)
    swapped = jnp.where(lane < ROT_DIM // 2, pltpu.roll(x, w - ROT_DIM // 2, 1), pltpu.roll(x, ROT_DIM // 2, 1))
    return x * cos + swapped * sin


def _sink_attention(q, kk, vv, mask, sinks_ref):
    outs = []
    for hq in range(A_HEADS):
        g = hq // (A_HEADS // A_KV_HEADS)
        qh = _bf(q[:, hq * HEAD_DIM:(hq + 1) * HEAD_DIM])
        kh = _bf(kk[:, g * HEAD_DIM:(g + 1) * HEAD_DIM])
        vh = _bf(vv[:, g * HEAD_DIM:(g + 1) * HEAD_DIM])
        s = _dot_nt(qh, kh) * (HEAD_DIM ** -0.5)
        s = jnp.where(mask, s, -jnp.inf)
        sink = sinks_ref[hq]
        m = jnp.maximum(jnp.max(s, axis=-1, keepdims=True), sink)
        p = jnp.exp(s - m)
        den = jnp.sum(p, axis=-1, keepdims=True) + jnp.exp(sink - m)
        outs.append(_dot(_bf(p / den), vh))
    return jnp.concatenate(outs, axis=1)


def _swa_prompt_body(sinks_ref, cur_ref, prev_ref, cos_ref, sin_ref, cosp_ref, sinp_ref, o_ref, ko_ref, vo_ref):
    i = pl.program_id(1)
    cur = cur_ref[...]
    q = _rope(cur[:, C_AQ:C_AQ + 256], cos_ref[...], sin_ref[...])
    k = _rope(cur[:, C_AK:C_AK + 128], cos_ref[...], sin_ref[...])
    v = cur[:, C_AV:C_AV + 128]
    prev = prev_ref[...]
    kp = _rope(prev[:, 0:128], cosp_ref[...], sinp_ref[...])
    vp = prev[:, 128:256]
    kk = jnp.concatenate([kp, k], axis=0)
    vv = jnp.concatenate([vp, v], axis=0)
    r = _iota((WINDOW, 2 * WINDOW), 0)
    j = _iota((WINDOW, 2 * WINDOW), 1)
    d = WINDOW + r - j
    mask = (d >= 0) & (d <= WINDOW) & ((j >= WINDOW) | (i > 0))
    o_ref[...] = _sink_attention(q, kk, vv, mask, sinks_ref)
    ko_ref[0] = k
    vo_ref[0] = v


def swa_prompt(h, sinks, cos_t, sin_t, bsz, seq):
    nb = seq // WINDOW
    smem = pl.BlockSpec(memory_space=pltpu.SMEM)
    tab = lambda f: pl.BlockSpec((WINDOW, LANE), f)
    return pl.pallas_call(
        _swa_prompt_body,
        grid=(bsz, nb),
        in_specs=[smem,
                  pl.BlockSpec((WINDOW, 512), lambda b, i: (b * nb + i, 0)),
                  pl.BlockSpec((WINDOW, 256), lambda b, i: (b * nb + jnp.maximum(i - 1, 0), 1)),
                  tab(lambda b, i: (i, 0)), tab(lambda b, i: (i, 0)),
                  tab(lambda b, i: (jnp.maximum(i - 1, 0), 0)), tab(lambda b, i: (jnp.maximum(i - 1, 0), 0))],
        out_specs=[pl.BlockSpec((WINDOW, 256), lambda b, i: (b * nb + i, 0)),
                   pl.BlockSpec((1, WINDOW, 128), lambda b, i: (b, 0, 0)),
                   pl.BlockSpec((1, WINDOW, 128), lambda b, i: (b, 0, 0))],
        out_shape=[jax.ShapeDtypeStruct((bsz * seq, 256), F32),
                   jax.ShapeDtypeStruct((bsz, WINDOW, 128), F32),
                   jax.ShapeDtypeStruct((bsz, WINDOW, 128), F32)],
        compiler_params=_cp("arbitrary", "arbitrary"),
        name="swa_prompt",
    )(sinks, h, h, cos_t, sin_t, cos_t, sin_t)


SWA_SB = 8


def _swa_sample_body(sinks_ref, cur_ref, kc_ref, vc_ref, cos_ref, sin_ref, o_ref, ko_ref, vo_ref, *, ls):
    cw = WINDOW
    r = _iota((ls, cw + ls), 0)
    j = _iota((ls, cw + ls), 1)
    d = cw + r - j
    mask = (d >= 0) & (d <= WINDOW)
    for b in range(SWA_SB):
        cur = cur_ref[b * ls:(b + 1) * ls, :]
        q = _rope(cur[:, C_AQ:C_AQ + 256], cos_ref[...], sin_ref[...])
        k = _rope(cur[:, C_AK:C_AK + 128], cos_ref[...], sin_ref[...])
        v = cur[:, C_AV:C_AV + 128]
        kk = jnp.concatenate([kc_ref[b], k], axis=0)
        vv = jnp.concatenate([vc_ref[b], v], axis=0)
        o_ref[b * ls:(b + 1) * ls, :] = _sink_attention(q, kk, vv, mask, sinks_ref)
        ko_ref[b] = kk[ls:, :]
        vo_ref[b] = vv[ls:, :]


def swa_sample(h, row0, sinks, k_cache, v_cache, cos_t, sin_t, bsz, ls):
    rows = SWA_SB * ls
    blk0 = row0 // rows
    smem = pl.BlockSpec(memory_space=pltpu.SMEM)
    cache = pl.BlockSpec((SWA_SB, WINDOW, 128), lambda i: (i, 0, 0))
    tab = pl.BlockSpec((ls, LANE), lambda i: (0, 0))
    return pl.pallas_call(
        functools.partial(_swa_sample_body, ls=ls),
        grid=(bsz // SWA_SB,),
        in_specs=[smem, pl.BlockSpec((rows, 512), lambda i: (blk0 + i, 0)), cache, cache, tab, tab],
        out_specs=[pl.BlockSpec((rows, 256), lambda i: (i, 0)), cache, cache],
        out_shape=[jax.ShapeDtypeStruct((bsz * ls, 256), F32),
                   jax.ShapeDtypeStruct((bsz, WINDOW, 128), F32),
                   jax.ShapeDtypeStruct((bsz, WINDOW, 128), F32)],
        compiler_params=_cp("arbitrary"),
        name="swa_sample",
    )(sinks, h, k_cache, v_cache, cos_t, sin_t)


S5_NB = 2 * SSM_W // LANE


def _s5_input(u, bw_ref):
    t = _dot(_bf(u), bw_ref[...])
    return jnp.concatenate([t[:, :SSM_W], t[:, :SSM_W] + t[:, SSM_W:]], axis=1)


def _s5_output(hs, u, cw_ref, d_ref, gw_ref, gb_ref):
    hs = jnp.concatenate([hs[:, :SSM_W] + hs[:, SSM_W:], hs[:, SSM_W:]], axis=1)
    y = _dot(_bf(hs), cw_ref[...]) + d_ref[...] * u
    y = _gelu_tanh(y)
    z = _dot(_bf(y), gw_ref[...]) + gb_ref[...]
    return z[:, :GROUP_WIDTH] * _sigmoid(z[:, GROUP_WIDTH:])


def _s5_scan(s_ref, a_ref, h_init, n_steps, rows):
    nre = S5_NB // 2
    a_re = [jnp.broadcast_to(a_ref[0:1, j * LANE:(j + 1) * LANE], (rows, LANE)) for j in range(nre)]
    a_im = [jnp.broadcast_to(a_ref[1:2, j * LANE:(j + 1) * LANE], (rows, LANE)) for j in range(nre)]

    def step(t, hcar):
        out = [None] * S5_NB
        base = pl.multiple_of(t * rows, rows)
        for j in range(nre):
            hr, hi = hcar[j], hcar[j + nre]
            nr = a_re[j] * hr - a_im[j] * hi + s_ref[j, pl.ds(base, rows), :]
            ni = a_re[j] * hi + a_im[j] * hr + s_ref[j + nre, pl.ds(base, rows), :]
            s_ref[j, pl.ds(base, rows), :] = nr
            s_ref[j + nre, pl.ds(base, rows), :] = ni
            out[j], out[j + nre] = nr, ni
        return tuple(out)

    return lax.fori_loop(0, n_steps, step, tuple(h_init))


def _s5_prompt_body(*refs, nb, tl):
    u_refs = refs[:nb]
    bw_ref, a_ref, cw_ref, d_ref, gw_ref, gb_ref, h0_ref, o_ref, hl_ref, s_ref, hst_ref = refs[nb:]
    i = pl.program_id(0)

    @pl.when(i == 0)
    def _():
        hst_ref[...] = h0_ref[...]

    for b in range(nb):
        bu = _s5_input(u_refs[b][...], bw_ref)
        for j in range(S5_NB):
            s_ref[j, pl.ds(b, tl, stride=nb), :] = bu[:, j * LANE:(j + 1) * LANE]
    h_init = [hst_ref[:, j * LANE:(j + 1) * LANE] for j in range(S5_NB)]
    h_fin = _s5_scan(s_ref, a_ref, h_init, tl, nb)
    for j in range(S5_NB):
        hst_ref[:, j * LANE:(j + 1) * LANE] = h_fin[j]
    for b in range(nb):
        hs = jnp.concatenate([s_ref[j, pl.ds(b, tl, stride=nb), :] for j in range(S5_NB)], axis=1)
        o_ref[b] = _s5_output(hs, u_refs[b][...], cw_ref, d_ref, gw_ref, gb_ref)

    @pl.when(i == pl.num_programs(0) - 1)
    def _():
        hl_ref[...] = hst_ref[...]


def s5_prompt(h, h0, bw, a, cw, dsk, gw, gb, bsz, seq, tl=128):
    assert bsz == 8
    nt = seq // tl
    full = lambda shape: pl.BlockSpec(shape, lambda i: (0,) * len(shape))
    u_specs = [pl.BlockSpec((tl, GROUP_WIDTH), functools.partial(lambda i, b: (b * nt + i, C_SU // GROUP_WIDTH), b=b))
               for b in range(bsz)]
    return pl.pallas_call(
        functools.partial(_s5_prompt_body, nb=bsz, tl=tl),
        grid=(nt,),
        in_specs=u_specs + [full((GROUP_WIDTH, 2 * SSM_W)), full((2, SSM_W)), full((2 * SSM_W, GROUP_WIDTH)),
                            full((1, GROUP_WIDTH)), full((GROUP_WIDTH, 2 * GROUP_WIDTH)), full((1, 2 * GROUP_WIDTH)),
                            full((bsz, 2 * SSM_W))],
        out_specs=[pl.BlockSpec((bsz, tl, GROUP_WIDTH), lambda i: (0, i, 0)), full((bsz, 2 * SSM_W))],
        out_shape=[jax.ShapeDtypeStruct((bsz, seq, GROUP_WIDTH), F32), jax.ShapeDtypeStruct((bsz, 2 * SSM_W), F32)],
        scratch_shapes=[pltpu.VMEM((S5_NB, tl * bsz, LANE), F32), pltpu.VMEM((bsz, 2 * SSM_W), F32)],
        compiler_params=_cp("arbitrary"),
        name="s5_prompt",
    )(*([h] * bsz), bw, a, cw, dsk, gw, gb, h0)


def _s5_sample_body(u_ref, bw_ref, a_ref, cw_ref, d_ref, gw_ref, gb_ref, h0_ref, o_ref, hl_ref, s_ref, t_ref, *, bsz, ls):
    bu = _s5_input(u_ref[...], bw_ref)
    for j in range(S5_NB):
        s_ref[j] = bu[:, j * LANE:(j + 1) * LANE]
    for t in range(ls):
        for j in range(S5_NB):
            t_ref[j, pl.ds(t * bsz, bsz), :] = s_ref[j, pl.ds(t, bsz, stride=ls), :]
    h_init = [h0_ref[:, j * LANE:(j + 1) * LANE] for j in range(S5_NB)]
    h_fin = _s5_scan(t_ref, a_ref, h_init, ls, bsz)
    for j in range(S5_NB):
        hl_ref[:, j * LANE:(j + 1) * LANE] = h_fin[j]
    for t in range(ls):
        for j in range(S5_NB):
            s_ref[j, pl.ds(t, bsz, stride=ls), :] = t_ref[j, pl.ds(t * bsz, bsz), :]
    hs = jnp.concatenate([s_ref[j] for j in range(S5_NB)], axis=1)
    o_ref[...] = _s5_output(hs, u_ref[...], cw_ref, d_ref, gw_ref, gb_ref)


def s5_sample(h, row0, h0, bw, a, cw, dsk, gw, gb, bsz, ls):
    rows = bsz * ls
    full = lambda shape: pl.BlockSpec(shape, lambda i: (0,) * len(shape))
    return pl.pallas_call(
        functools.partial(_s5_sample_body, bsz=bsz, ls=ls),
        grid=(1,),
        in_specs=[pl.BlockSpec((rows, GROUP_WIDTH), lambda i: (row0 // rows, C_SU // GROUP_WIDTH)),
                  full((GROUP_WIDTH, 2 * SSM_W)), full((2, SSM_W)), full((2 * SSM_W, GROUP_WIDTH)),
                  full((1, GROUP_WIDTH)), full((GROUP_WIDTH, 2 * GROUP_WIDTH)), full((1, 2 * GROUP_WIDTH)),
                  full((bsz, 2 * SSM_W))],
        out_specs=[full((rows, GROUP_WIDTH)), full((bsz, 2 * SSM_W))],
        out_shape=[jax.ShapeDtypeStruct((rows, GROUP_WIDTH), F32), jax.ShapeDtypeStruct((bsz, 2 * SSM_W), F32)],
        scratch_shapes=[pltpu.VMEM((S5_NB, rows, LANE), F32), pltpu.VMEM((S5_NB, rows, LANE), F32)],
        compiler_params=_cp("arbitrary"),
        name="s5_sample",
    )(h, bw, a, cw, dsk, gw, gb, h0)


def s5_params(a_re, a_im, b_re, b_im, c_re, c_im, d_skip, log_dt):
    lam = lax.complex(a_re, a_im)
    delta = jnp.exp(log_dt)[:, None]
    a_bar = jnp.exp(lam * delta)
    b_bar = ((a_bar - 1.0) / lam)[..., None] * lax.complex(b_re, b_im)
    eye = jnp.eye(SSM_GROUPS, dtype=F32)
    bw_re = jnp.einsum('gpc,gh->gchp', b_bar.real, eye).reshape(GROUP_WIDTH, SSM_W)
    bw_im = jnp.einsum('gpc,gh->gchp', b_bar.imag, eye).reshape(GROUP_WIDTH, SSM_W)
    bw = jnp.concatenate([bw_re, bw_im - bw_re], axis=1).astype(BF16)
    cw_re = jnp.einsum('gcp,gh->gphc', c_re, eye).reshape(SSM_W, GROUP_WIDTH)
    cw_im = jnp.einsum('gcp,gh->gphc', c_im, eye).reshape(SSM_W, GROUP_WIDTH)
    cw = jnp.concatenate([cw_re, -(cw_re + cw_im)], axis=0).astype(BF16)
    a = jnp.stack([a_bar.real.reshape(SSM_W), a_bar.imag.reshape(SSM_W)], axis=0)
    return bw, a, cw, d_skip.reshape(1, GROUP_WIDTH)


GDN_RT = 256
GDN_ST = 64
GDN_SG = 8


def _split(x):
    hi = _bf(x)
    return hi, _bf(x - hi.astype(F32))


def _dot3(a, b):
    return _dot(a[0], b[0]) + (_dot(a[0], b[1]) + _dot(a[1], b[0]))


def _gdn_prep_body(*refs, c, sample, tiles_per_seq):
    if sample:
        x_ref, cs_ref, z_ref, sm_ref, cw_ref, alog_ref, dtb_ref = refs[:7]
        outs = refs[7:14]
        xs_ref, cb_ref = refs[14:]
    else:
        x_ref, prev_ref, cs_ref, z_ref, sm_ref, cw_ref, alog_ref, dtb_ref = refs[:8]
        outs = refs[8:15]
        (xs_ref,) = refs[15:]
    u_ref, w_ref, qk_ref, qg_ref, kd_ref, eg_ref, zs_ref = outs
    rt = GDN_RT
    x = x_ref[...]
    xs_ref[8:8 + rt, :] = x
    if sample:
        xs_ref[0:8, :] = jnp.zeros((8, GDN_CONV_DIM), F32)
        cb_ref[0:rt, :] = cs_ref[...]
        cb_ref[rt:rt + 8, :] = jnp.zeros((8, GDN_CONV_DIM), F32)
        pos = _iota((rt, GDN_CONV_DIM), 0) & (c - 1)
        shifted = lambda i: jnp.where(pos >= i, xs_ref[8 - i:8 - i + rt, :], cb_ref[8 - i:8 - i + rt, :])
    else:
        first = (pl.program_id(0) % tiles_per_seq) == 0
        xs_ref[0:8, :] = jnp.where(first, cs_ref[...], prev_ref[...])
        shifted = lambda i: xs_ref[8 - i:8 - i + rt, :]
    conv = shifted(3) * cw_ref[0:1, :]
    conv = conv + shifted(2) * cw_ref[1:2, :]
    conv = conv + shifted(1) * cw_ref[2:3, :]
    conv = conv + x * cw_ref[3:4, :]
    qkv = _silu(conv)
    zs_ref[...] = _silu(z_ref[...])

    sm = sm_ref[...]
    beta_all = _sigmoid(sm)
    g_all = -jnp.exp(alog_ref[...]) * _softplus(sm + dtb_ref[...])
    st = GDN_ST
    ri = _iota((st, st), 0)
    ci = _iota((st, st), 1)
    same = _shr(ri, c) == _shr(ci, c)
    low = (ci <= ri) & same
    tri = low.astype(F32)
    last = (ci == (ri | (c - 1))).astype(F32)
    eye = (ri == ci).astype(F32)
    lane = _iota((st, LANE), 1)
    for t in range(rt // st):
        rows = slice(t * st, (t + 1) * st)
        gc_all = _dot(tri, g_all[rows, :], HI)
        gl_all = _dot(last, gc_all, HI)
        eg_ref[rows, :] = jnp.exp(gl_all)
        us, ws, qks, qgs, kds = [], [], [], [], []
        for hh in range(GDN_HEADS):
            q = qkv[rows, hh * GDN_DK:(hh + 1) * GDN_DK]
            k = qkv[rows, 256 + hh * GDN_DK:256 + (hh + 1) * GDN_DK]
            v = qkv[rows, 512 + hh * GDN_DV:512 + (hh + 1) * GDN_DV]
            q = q * lax.rsqrt(jnp.sum(q * q, axis=-1, keepdims=True) + 1e-6) * (GDN_DK ** -0.5)
            k = k * lax.rsqrt(jnp.sum(k * k, axis=-1, keepdims=True) + 1e-6)
            beta = beta_all[rows, SM_GB + hh:SM_GB + hh + 1]
            gc = gc_all[:, SM_GA + hh:SM_GA + hh + 1]
            gl = gl_all[:, SM_GA + hh:SM_GA + hh + 1]
            sel = (lane == SM_GA + hh).astype(F32)
            gc_row = _dot_nt(sel, gc_all, HI)
            decay = jnp.exp(jnp.where(low, gc - gc_row, -jnp.inf))
            kb = k * beta
            a_strict = jnp.where(ci < ri, _dot_nt(_bf(kb), _bf(k)) * decay, 0.0)
            inv = eye - jnp.where(_shr(ri, 2) == _shr(ci, 2), a_strict, 0.0)
            s = 2
            while s < c:
                e = jnp.where((_shr(ri, 2 * s) == _shr(ci, 2 * s)) & (_shr(ri, s) != _shr(ci, s)), a_strict, 0.0)
                inv_s = _split(inv)
                inv = inv - _dot3(inv_s, _split(_dot3(_split(e), inv_s)))
                s *= 2
            eg = jnp.exp(gc)
            sol = _dot3(_split(inv), _split(jnp.concatenate([v * beta, kb * eg], axis=1)))
            us.append(sol[:, :GDN_DV])
            ws.append(sol[:, GDN_DV:])
            qks.append(_dot_nt(_bf(q), _bf(k)) * decay)
            qgs.append(q * eg)
            kds.append(k * jnp.exp(gl - gc))
        u_ref[rows, :] = jnp.concatenate(us, axis=1)
        w_ref[rows, :] = jnp.concatenate(ws, axis=1)
        qk_ref[rows, :] = jnp.concatenate(qks, axis=1)
        qg_ref[rows, :] = jnp.concatenate(qgs, axis=1)
        kd_ref[rows, :] = jnp.concatenate(kds, axis=1)


def _gdn_scan_body(u_ref, w_ref, qk_ref, qg_ref, kd_ref, eg_ref, zs_ref, nw_ref, s0_ref, o_ref, so_ref, st_ref, *, c):
    n = pl.program_id(1)

    @pl.when(n == 0)
    def _():
        st_ref[...] = s0_ref[...]

    for s in range(GDN_SG):
        col0 = (s * c) % GDN_ST
        outs = []
        for hh in range(GDN_HEADS):
            hs = slice(hh * GDN_DV, (hh + 1) * GDN_DV)
            state = st_ref[s, hh]
            state_b = _bf(state)
            v_new = u_ref[s, :, hs] - _dot(_bf(w_ref[s, :, hs]), state_b)
            v_b = _bf(v_new)
            qk = qk_ref[s, :, hh * GDN_ST + col0:hh * GDN_ST + col0 + c]
            o = _dot(_bf(qg_ref[s, :, hs]), state_b) + _dot(_bf(qk), v_b)
            eg_last = eg_ref[s, c - 1:c, SM_GA + hh:SM_GA + hh + 1]
            st_ref[s, hh] = state * eg_last + _dot_tn(_bf(kd_ref[s, :, hs]), v_b)
            o = o * lax.rsqrt(jnp.mean(o * o, axis=-1, keepdims=True) + RMS_EPS) * nw_ref[...]
            outs.append(o * zs_ref[s, :, hs])
        o_ref[s] = jnp.concatenate(outs, axis=1)

    @pl.when(n == pl.num_programs(1) - 1)
    def _():
        so_ref[...] = st_ref[...]


def gdn_mixer(h, row0, conv_w, alog_t, dtb_t, norm_w, conv_state, s0, bsz, seq):
    c = min(GDN_CHUNK, seq)
    sample = seq == c
    rows = bsz * seq
    rt = GDN_RT
    nt = rows // rt
    blk0 = row0 // rt
    tps = max(seq // rt, 1)
    cs_rows = jnp.pad(conv_state, ((0, 0), (8 - (GDN_CONV - 1), 0), (0, 0))).reshape(bsz * 8, GDN_CONV_DIM)
    row = lambda w, col: pl.BlockSpec((rt, w), lambda i: (blk0 + i, col))
    full = lambda shape: pl.BlockSpec(shape, lambda i: (0,) * len(shape))
    if sample:
        aux_specs = [pl.BlockSpec((rt, GDN_CONV_DIM), lambda i: (i, 0))]
        aux = [cs_rows]
        scratch = [pltpu.VMEM((8 + rt, GDN_CONV_DIM), F32), pltpu.VMEM((8 + rt, GDN_CONV_DIM), F32)]
    else:
        aux_specs = [pl.BlockSpec((8, GDN_CONV_DIM), lambda i: (jnp.maximum((row0 + i * rt) // 8 - 1, 0), C_GQKV // GDN_CONV_DIM)),
                     pl.BlockSpec((8, GDN_CONV_DIM), lambda i: (i // tps, 0))]
        aux = [h, cs_rows]
        scratch = [pltpu.VMEM((8 + rt, GDN_CONV_DIM), F32)]
    out_w = [GROUP_WIDTH] * 5 + [LANE, GROUP_WIDTH]
    prep = pl.pallas_call(
        functools.partial(_gdn_prep_body, c=c, sample=sample, tiles_per_seq=tps),
        grid=(nt,),
        in_specs=[row(GDN_CONV_DIM, C_GQKV // GDN_CONV_DIM)] + aux_specs
                 + [row(GROUP_WIDTH, C_GZ // GROUP_WIDTH), row(LANE, C_SM // LANE),
                    full((GDN_CONV, GDN_CONV_DIM)), full((1, LANE)), full((1, LANE))],
        out_specs=[pl.BlockSpec((rt, w), lambda i: (i, 0)) for w in out_w],
        out_shape=[jax.ShapeDtypeStruct((rows, w), F32) for w in out_w],
        scratch_shapes=scratch,
        compiler_params=_cp("arbitrary"),
        name="gdn_prep",
    )(h, *aux, h, h, conv_w, alog_t, dtb_t)
    prep = [a.reshape(bsz, seq, a.shape[-1]) for a in prep]
    ng, nc = bsz // GDN_SG, seq // c
    seq_blk = lambda w: pl.BlockSpec((GDN_SG, c, w), lambda g, n: (g, n, 0))
    st_spec = pl.BlockSpec((GDN_SG, GDN_HEADS, GDN_DK, GDN_DV), lambda g, n: (g, 0, 0, 0))
    o, s_fin = pl.pallas_call(
        functools.partial(_gdn_scan_body, c=c),
        grid=(ng, nc),
        in_specs=[seq_blk(w) for w in out_w] + [pl.BlockSpec((1, GDN_DV), lambda g, n: (0, 0)), st_spec],
        out_specs=[seq_blk(GROUP_WIDTH), st_spec],
        out_shape=[jax.ShapeDtypeStruct((bsz, seq, GROUP_WIDTH), F32),
                   jax.ShapeDtypeStruct((bsz, GDN_HEADS, GDN_DK, GDN_DV), F32)],
        scratch_shapes=[pltpu.VMEM((GDN_SG, GDN_HEADS, GDN_DK, GDN_DV), F32)],
        compiler_params=_cp("arbitrary", "arbitrary"),
        name="gdn_scan",
    )(*prep, norm_w, s0)
    return o.reshape(rows, GROUP_WIDTH), s_fin


def _gla_body(qk_ref, vr_ref, sm_ref, gw_ref, gb_ref, nw_ref, s0_ref, o_ref, so_ref,
              kb_ref, bb_ref, vb_ref, st_ref, *, c, tr):
    n = pl.program_id(1)
    pad = GLA_CHUNK

    @pl.when(n == 0)
    def _():
        kb_ref[0:pad, :] = jnp.zeros((pad, 128), F32)
        bb_ref[0:pad, :] = jnp.zeros((pad, 128), F32)
        vb_ref[0:pad, :] = jnp.zeros((pad, 256), F32)
        st_ref[...] = s0_ref[0]

    q = qk_ref[:, 0:128] * (GLA_DK ** -0.5)
    k = qk_ref[:, 128:256]
    v = vr_ref[:, 0:256]
    r = vr_ref[:, 256:512]
    lg = sm_ref[:, SM_LG:SM_LG + GLA_RANK]
    log_a = _log_sigmoid(_dot(_bf(lg), _bf(gw_ref[...])) + gb_ref[...]) / GLA_TAU
    ri = _iota((tr, tr), 0)
    ci = _iota((tr, tr), 1)
    tri = ((ci <= ri) & (_shr(ri, c) == _shr(ci, c))).astype(F32)
    bc = _dot(tri, log_a, HI)
    kb_ref[pad:pad + tr, :] = k
    bb_ref[pad:pad + tr, :] = bc
    vb_ref[pad:pad + tr, :] = v
    ones_kv = (_shr(_iota((128, 256), 0), GLA_DK) == _shr(_iota((128, 256), 1), GLA_DV)).astype(F32)
    pos = _iota((tr, 128), 0) & (c - 1)
    o_intra = jnp.zeros((tr, 256), F32)
    for dl in range(c):
        ks = kb_ref[pad - dl:pad - dl + tr, :]
        bs = bb_ref[pad - dl:pad - dl + tr, :]
        vs = vb_ref[pad - dl:pad - dl + tr, :]
        p = jnp.where(pos >= dl, q * ks * jnp.exp(bc - bs), 0.0)
        o_intra = o_intra + _bf(_dot(p, ones_kv, HI)).astype(F32) * _bf(vs).astype(F32)
    bd = (_shr(_iota((256, 128), 0), GLA_DV) == _shr(_iota((256, 128), 1), GLA_DK)).astype(F32)
    ones_vv = (_shr(_iota((256, 256), 0), GLA_DV) == _shr(_iota((256, 256), 1), GLA_DV)).astype(F32)
    for ch in range(tr // c):
        sl = slice(ch * c, (ch + 1) * c)
        b_c = bc[sl, :]
        b_last = b_c[c - 1:c, :]
        st = st_ref[...]
        o = _dot_nt(_bf(q[sl, :] * jnp.exp(b_c)), _bf(st)) + o_intra[sl, :]
        st_ref[...] = st * jnp.exp(b_last) + bd * _dot_tn(_bf(v[sl, :]), _bf(k[sl, :] * jnp.exp(b_last - b_c)))
        ms = _dot(o * o, ones_vv, HI) * (1.0 / GLA_DV)
        o = o * lax.rsqrt(ms + RMS_EPS) * nw_ref[...]
        o_ref[sl, :] = o * _silu(r[sl, :])
    so_ref[0] = st_ref[...]


def gla_mixer(h, row0, gate_w, gate_b, norm_w4, s0t, bsz, seq):
    c = min(GLA_CHUNK, seq)
    tr = min(256, seq)
    nt = seq // tr
    blk0 = row0 // tr
    row = lambda w, col: pl.BlockSpec((tr, w), lambda b, n: (blk0 + b * nt + n, col))
    full = lambda shape: pl.BlockSpec(shape, lambda b, n: (0,) * len(shape))
    st_spec = pl.BlockSpec((1, 256, 128), lambda b, n: (b, 0, 0))
    return pl.pallas_call(
        functools.partial(_gla_body, c=c, tr=tr),
        grid=(bsz, nt),
        in_specs=[row(256, C_LQ // 256), row(512, C_LV // 512), row(LANE, C_SM // LANE),
                  full((GLA_RANK, 128)), full((1, 128)), full((1, 256)), st_spec],
        out_specs=[pl.BlockSpec((tr, 256), lambda b, n: (b * nt + n, 0)), st_spec],
        out_shape=[jax.ShapeDtypeStruct((bsz * seq, 256), F32), jax.ShapeDtypeStruct((bsz, 256, 128), F32)],
        scratch_shapes=[pltpu.VMEM((GLA_CHUNK + tr, 128), F32), pltpu.VMEM((GLA_CHUNK + tr, 128), F32),
                        pltpu.VMEM((GLA_CHUNK + tr, 256), F32), pltpu.VMEM((256, 128), F32)],
        compiler_params=_cp("arbitrary", "arbitrary"),
        name="gla",
    )(h, h, h, gate_w, gate_b, norm_w4, s0t)


def gla_state_to_t(s):
    eye = jnp.eye(GLA_HEADS, dtype=s.dtype)
    return jnp.einsum('bhde,hg->bhegd', s, eye).reshape(s.shape[0], 256, 128)


def gla_state_from_t(st):
    b = st.shape[0]
    t5 = st.reshape(b, GLA_HEADS, GLA_DV, GLA_HEADS, GLA_DK)
    diag = jnp.stack([t5[:, hh, :, hh, :] for hh in range(GLA_HEADS)], axis=1)
    return jnp.swapaxes(diag, 2, 3)


def _out_proj_body(x_ref, ap, bp, cp, dp, a_s, b_s, c_s, d_s, w_ref, g_ref, b_ref, o_ref, *, n_prompt_tiles):
    i = pl.program_id(0)

    def run(refs):
        acc = DN_ALPHA * x_ref[...]
        for m, r in enumerate(refs):
            acc = acc + _dot(_bf(r[...]), w_ref[m * GROUP_WIDTH:(m + 1) * GROUP_WIDTH, :])
        o_ref[...] = _layer_norm(acc, g_ref[...], b_ref[...])

    @pl.when(i < n_prompt_tiles)
    def _():
        run((ap, bp, cp, dp))

    @pl.when(i >= n_prompt_tiles)
    def _():
        run((a_s, b_s, c_s, d_s))


def out_proj_ln(x, mix_p, mix_s, w, g, b):
    t = x.shape[0]
    npt = mix_p[0].shape[0] // ROW_TILE
    nst = mix_s[0].shape[0] // ROW_TILE
    p_spec = pl.BlockSpec((ROW_TILE, GROUP_WIDTH), lambda i: (jnp.minimum(i, npt - 1), 0))
    s_spec = pl.BlockSpec((ROW_TILE, GROUP_WIDTH), lambda i: (jnp.clip(i - npt, 0, nst - 1), 0))
    full = lambda shape: pl.BlockSpec(shape, lambda i: (0,) * len(shape))
    return pl.pallas_call(
        functools.partial(_out_proj_body, n_prompt_tiles=npt),
        grid=(t // ROW_TILE,),
        in_specs=[pl.BlockSpec((ROW_TILE, D_MODEL), lambda i: (i, 0))] + [p_spec] * 4 + [s_spec] * 4
                 + [full((D_MODEL, D_MODEL)), full((1, D_MODEL)), full((1, D_MODEL))],
        out_specs=pl.BlockSpec((ROW_TILE, D_MODEL), lambda i: (i, 0)),
        out_shape=jax.ShapeDtypeStruct((t, D_MODEL), F32),
        compiler_params=_cp("arbitrary"),
        name="out_proj_ln",
    )(x, *mix_p, *mix_s, w, g, b)


def _router_body(x_ref, w_ref, b_ref, e_ref, g_ref):
    logits = _dot(_bf(x_ref[...]), _bf(w_ref[...])) + b_ref[...]
    lane = _iota(logits.shape, 1)
    vals = jnp.where(lane < N_EXPERTS, logits, -jnp.inf)
    eo = jnp.zeros(logits.shape, I32)
    top = []
    for k in range(TOP_K):
        m = jnp.max(vals, axis=-1, keepdims=True)
        idx = jnp.min(jnp.where(vals == m, lane.astype(F32), float(LANE)), axis=-1, keepdims=True).astype(I32)
        eo = jnp.where(lane == k, idx, eo)
        top.append(m)
        vals = jnp.where(lane == idx, -jnp.inf, vals)
    ex = [jnp.exp(v - top[0]) for v in top]
    den = ex[0] + ex[1] + ex[2] + ex[3]
    go = jnp.zeros(logits.shape, F32)
    for k in range(TOP_K):
        go = jnp.where(lane == k, ex[k] / den, go)
    e_ref[...] = eo
    g_ref[...] = go


def router(x, w, b):
    t = x.shape[0]
    return pl.pallas_call(
        _router_body,
        grid=(t // ROW_TILE,),
        in_specs=[pl.BlockSpec((ROW_TILE, D_MODEL), lambda i: (i, 0)),
                  pl.BlockSpec((D_MODEL, LANE), lambda i: (0, 0)), pl.BlockSpec((1, LANE), lambda i: (0, 0))],
        out_specs=[pl.BlockSpec((ROW_TILE, LANE), lambda i: (i, 0))] * 2,
        out_shape=[jax.ShapeDtypeStruct((t, LANE), I32), jax.ShapeDtypeStruct((t, LANE), F32)],
        compiler_params=_cp("arbitrary"),
        name="router",
    )(x, w, b)


DISPATCH_TILE = 256


def _dispatch_copy(x_ref, xs_ref, sem, r, d):
    return pltpu.make_async_copy(x_ref.at[pl.ds(r, 1), :], xs_ref.at[pl.ds(d, 1), :], sem)


def _dispatch_body(dest_ref, x_ref, xs_in_ref, xs_ref, sem):
    del xs_in_ref

    def start(r, carry):
        for k in range(TOP_K):
            _dispatch_copy(x_ref, xs_ref, sem, r, dest_ref[0, 0, r * TOP_K + k]).start()
        return carry

    lax.fori_loop(0, DISPATCH_TILE, start, 0)

    def wait(r, carry):
        for k in range(TOP_K):
            _dispatch_copy(x_ref, xs_ref, sem, 0, 0).wait()
        return carry

    lax.fori_loop(0, DISPATCH_TILE, wait, 0)


def moe_dispatch(x, dest, n_rows):
    t = x.shape[0]
    nt = t // DISPATCH_TILE
    zeros = jnp.zeros((n_rows, D_MODEL), F32)
    return pl.pallas_call(
        _dispatch_body,
        grid=(nt,),
        in_specs=[pl.BlockSpec((1, 1, DISPATCH_TILE * TOP_K), lambda i: (i, 0, 0), memory_space=pltpu.SMEM),
                  pl.BlockSpec((DISPATCH_TILE, D_MODEL), lambda i: (i, 0)),
                  pl.BlockSpec(memory_space=pl.ANY)],
        out_specs=pl.BlockSpec(memory_space=pl.ANY),
        out_shape=jax.ShapeDtypeStruct((n_rows, D_MODEL), F32),
        scratch_shapes=[pltpu.SemaphoreType.DMA(())],
        input_output_aliases={2: 0},
        compiler_params=_cp("arbitrary"),
        name="moe_dispatch",
    )(dest.reshape(nt, 1, DISPATCH_TILE * TOP_K), x, zeros)


EXPERT_SUB = 256


def _expert_body(te_ref, tv_ref, xs_ref, wgu_ref, bgu_ref, wd_ref, bd_ref, ys_ref, wgu_s, wd2_s, tmp_s):
    i = pl.program_id(0)
    valid = tv_ref[i] != 0
    changed = (i == 0) | (te_ref[i] != te_ref[jnp.maximum(i - 1, 0)])

    @pl.when(i == 0)
    def _():
        tmp_s[...] = jnp.zeros(tmp_s.shape, F32)

    @pl.when(valid & changed)
    def _():
        for c0 in range(0, 2 * D_FF, 512):
            wgu_s[:, c0:c0 + 512] = _bf(wgu_ref[0, :, c0:c0 + 512])
        for j in range(D_MODEL // LANE):
            tmp_s[pl.ds(0, D_FF, stride=2), :] = wd_ref[0, :, j * LANE:(j + 1) * LANE]
            wd2_s[:, j * LANE:(j + 1) * LANE] = _bf(tmp_s[...])

    @pl.when(valid)
    def _():
        for r0 in range(0, MOE_TILE, EXPERT_SUB):
            rows = slice(r0, r0 + EXPERT_SUB)
            gu = _dot(_bf(xs_ref[rows, :]), wgu_s[...]) + bgu_ref[0]
            nxt = pltpu.roll(gu, 2 * D_FF - 1, 1)
            gate = jnp.minimum(gu, SWIGLU_LIMIT)
            up = jnp.clip(nxt, -SWIGLU_LIMIT, SWIGLU_LIMIT)
            act = gate * _sigmoid(gate * SWIGLU_ALPHA) * (up + 1.0)
            even = (_iota(act.shape, 1) & 1) == 0
            act = jnp.where(even, act, 0.0)
            ys_ref[rows, :] = _dot(_bf(act), wd2_s[...]) + bd_ref[0]

    @pl.when(jnp.logical_not(valid))
    def _():
        ys_ref[...] = jnp.zeros(ys_ref.shape, F32)


def moe_experts(xs, tile_e, tile_valid, wgu, bgu, wd, bd):
    n_rows = xs.shape[0]
    nt = n_rows // MOE_TILE
    grid_spec = pltpu.PrefetchScalarGridSpec(
        num_scalar_prefetch=2,
        grid=(nt,),
        in_specs=[pl.BlockSpec((MOE_TILE, D_MODEL), lambda i, te, tv: (i, 0)),
                  pl.BlockSpec((1, D_MODEL, 2 * D_FF), lambda i, te, tv: (te[i], 0, 0)),
                  pl.BlockSpec((1, 1, 2 * D_FF), lambda i, te, tv: (te[i], 0, 0)),
                  pl.BlockSpec((1, D_FF, D_MODEL), lambda i, te, tv: (te[i], 0, 0)),
                  pl.BlockSpec((1, 1, D_MODEL), lambda i, te, tv: (te[i], 0, 0))],
        out_specs=pl.BlockSpec((MOE_TILE, D_MODEL), lambda i, te, tv: (i, 0)),
        scratch_shapes=[pltpu.VMEM((D_MODEL, 2 * D_FF), BF16), pltpu.VMEM((2 * D_FF, D_MODEL), BF16),
                        pltpu.VMEM((2 * D_FF, LANE), F32)],
    )
    return pl.pallas_call(
        _expert_body,
        grid_spec=grid_spec,
        out_shape=jax.ShapeDtypeStruct((n_rows, D_MODEL), F32),
        compiler_params=_cp("arbitrary"),
        name="moe_experts",
    )(tile_e, tile_valid, xs, wgu, bgu, wd, bd)


def _combine_copy(ys_ref, buf_ref, sem, d, k, r):
    return pltpu.make_async_copy(ys_ref.at[pl.ds(d, 1), :], buf_ref.at[k, pl.ds(r, 1), :], sem)


def _combine_body(dest_ref, gates_ref, x_ref, g_ref, b_ref, ys_ref, o_ref, buf_ref, sem):
    def start(r, carry):
        for k in range(TOP_K):
            _combine_copy(ys_ref, buf_ref, sem, dest_ref[0, 0, r * TOP_K + k], k, r).start()
        return carry

    lax.fori_loop(0, DISPATCH_TILE, start, 0)

    def wait(r, carry):
        for k in range(TOP_K):
            _combine_copy(ys_ref, buf_ref, sem, 0, k, 0).wait()
        return carry

    lax.fori_loop(0, DISPATCH_TILE, wait, 0)
    acc = DN_ALPHA * x_ref[...]
    gates = gates_ref[...]
    for k in range(TOP_K):
        acc = acc + gates[:, k:k + 1] * buf_ref[k]
    o_ref[...] = _layer_norm(acc, g_ref[...], b_ref[...])


def moe_combine_ln(x, ys, dest, gates, g, b):
    t = x.shape[0]
    nt = t // DISPATCH_TILE
    full = lambda shape: pl.BlockSpec(shape, lambda i: (0,) * len(shape))
    return pl.pallas_call(
        _combine_body,
        grid=(nt,),
        in_specs=[pl.BlockSpec((1, 1, DISPATCH_TILE * TOP_K), lambda i: (i, 0, 0), memory_space=pltpu.SMEM),
                  pl.BlockSpec((DISPATCH_TILE, LANE), lambda i: (i, 0)),
                  pl.BlockSpec((DISPATCH_TILE, D_MODEL), lambda i: (i, 0)),
                  full((1, D_MODEL)), full((1, D_MODEL)),
                  pl.BlockSpec(memory_space=pl.ANY)],
        out_specs=pl.BlockSpec((DISPATCH_TILE, D_MODEL), lambda i: (i, 0)),
        out_shape=jax.ShapeDtypeStruct((t, D_MODEL), F32),
        scratch_shapes=[pltpu.VMEM((TOP_K, DISPATCH_TILE, D_MODEL), F32), pltpu.SemaphoreType.DMA(())],
        compiler_params=_cp("arbitrary"),
        name="moe_combine_ln",
    )(dest.reshape(nt, 1, DISPATCH_TILE * TOP_K), gates, x, g, b, ys)


def moe_plan(top_e, n_tokens):
    tk = n_tokens * TOP_K
    flat_e = top_e.reshape(tk)
    onehot = (flat_e[:, None] == jnp.arange(N_EXPERTS, dtype=I32)[None, :]).astype(I32)
    csum = jnp.cumsum(onehot, axis=0)
    rank = jnp.take_along_axis(csum, flat_e[:, None], axis=1)[:, 0] - 1
    counts = csum[-1]
    ntile = (counts + MOE_TILE - 1) // MOE_TILE
    tile_end = jnp.cumsum(ntile)
    tile_start = tile_end - ntile
    dest = tile_start[flat_e] * MOE_TILE + rank
    n_tiles = -(-tk // MOE_TILE) + N_EXPERTS
    tiles = jnp.arange(n_tiles, dtype=I32)
    tile_e = jnp.minimum(jnp.searchsorted(tile_end, tiles, side='right'), N_EXPERTS - 1).astype(I32)
    tile_valid = (tiles < tile_end[-1]).astype(I32)
    return dest.astype(I32), tile_e, tile_valid, n_tiles * MOE_TILE


def moe_ffn_ln(x1, rw, rb, wgu, bgu, wd, bd, g, b, expert0=0):
    t = x1.shape[0]
    e_pad, gates = router(x1, rw, rb)
    dest, tile_e, tile_valid, n_rows = moe_plan(e_pad[:, :TOP_K], t)
    xs = moe_dispatch(x1, dest, n_rows)
    ys = moe_experts(xs, tile_e + expert0, tile_valid, wgu, bgu, wd, bd)
    return moe_combine_ln(x1, ys, dest, gates, g, b)


def _rope_tables(pos):
    half = ROT_DIM // 2
    inv_freq = ROPE_THETA ** (-jnp.arange(half, dtype=F32) / half)
    ang = pos.astype(F32)[:, None] * inv_freq[None, :]
    cos, sin = jnp.cos(ang), jnp.sin(ang)
    n = pos.shape[0]
    ones = jnp.ones((n, HEAD_DIM - ROT_DIM), F32)
    cos_h = jnp.concatenate([cos, cos, ones], axis=1)
    sin_h = jnp.concatenate([-sin, sin, 0.0 * ones], axis=1)
    return jnp.concatenate([cos_h, cos_h], axis=1), jnp.concatenate([sin_h, sin_h], axis=1)


def _conv_tail(h, row0, bsz, seq):
    assert seq >= GDN_CONV - 1
    n = GDN_CONV - 1
    if bsz <= 8:
        return jnp.stack([h[row0 + (b + 1) * seq - n:row0 + (b + 1) * seq, C_GQKV:C_GQKV + GDN_CONV_DIM]
                          for b in range(bsz)], axis=0)
    blk = h[row0:row0 + bsz * seq, C_GQKV:C_GQKV + GDN_CONV_DIM].reshape(bsz, seq, GDN_CONV_DIM)
    return blk[:, seq - n:, :]


def _lane_row(v, offset):
    return jnp.zeros((1, LANE), F32).at[0, offset:offset + v.shape[0]].set(v)


def kernel(x_prompt, x_sample, cache_swa_k, cache_swa_v, state_ssm_re, state_ssm_im, state_gdn_conv, state_gdn, state_gla, w_in, w_out, attn_sinks, ssm_a_re, ssm_a_im, ssm_b_re, ssm_b_im, ssm_c_re, ssm_c_im, ssm_d, ssm_log_dt, ssm_glu_w, ssm_glu_b, gdn_conv_w, gdn_a_log, gdn_dt_bias, gdn_norm_w, gla_gate_w, gla_gate_b, gla_norm_w, ln1_g, ln1_b, ln2_g, ln2_b, router_w, router_b, moe_w_gate_up, moe_b_gate_up, moe_w_down, moe_b_down):
    bp, lp, _ = x_prompt.shape
    bs, ls, _ = x_sample.shape
    n_p, n_s = bp * lp, bs * ls
    depth = w_in.shape[0]

    w_in_r = jnp.concatenate([w_in[..., :1792], w_in[..., 1800:2568], w_in[..., 1792:1800], w_in[..., 2568:N_IN],
                              jnp.zeros(w_in.shape[:2] + (NH - N_IN,), w_in.dtype)], axis=-1).astype(BF16)
    w_out_b = w_out.astype(BF16)
    glu_w_b = ssm_glu_w.astype(BF16)
    rw_pad = jnp.pad(router_w, ((0, 0), (0, 0), (0, LANE - N_EXPERTS)))
    rb_pad = jnp.pad(router_b, ((0, 0), (0, LANE - N_EXPERTS)))
    wgu_all = moe_w_gate_up.reshape(depth * N_EXPERTS, D_MODEL, 2 * D_FF)
    bgu_all = moe_b_gate_up.reshape(depth * N_EXPERTS, 1, 2 * D_FF)
    wd_all = moe_w_down.reshape(depth * N_EXPERTS, D_FF, D_MODEL)
    bd_all = moe_b_down.reshape(depth * N_EXPERTS, 1, D_MODEL)

    cos_p, sin_p = _rope_tables(jnp.arange(lp, dtype=I32))
    cos_s, sin_s = _rope_tables(PAST_LEN + jnp.arange(ls, dtype=I32))

    x = jnp.concatenate([x_prompt.reshape(n_p, D_MODEL), x_sample.reshape(n_s, D_MODEL)], axis=0)
    zeros = lambda *s: jnp.zeros(s, F32)
    new_p = [[] for _ in range(7)]
    new_s = [[] for _ in range(7)]
    for l in range(depth):
        h = in_proj(x, w_in_r[l])
        sinks = attn_sinks[l]
        oa_p, pk, pv = swa_prompt(h, sinks, cos_p, sin_p, bp, lp)
        oa_s, sk, sv = swa_sample(h, n_p, sinks, cache_swa_k[l].reshape(bs, WINDOW, 128),
                                  cache_swa_v[l].reshape(bs, WINDOW, 128), cos_s, sin_s, bs, ls)
        bw, a_bar, cw, dsk = s5_params(ssm_a_re[l], ssm_a_im[l], ssm_b_re[l], ssm_b_im[l], ssm_c_re[l], ssm_c_im[l],
                                       ssm_d[l], ssm_log_dt[l])
        glu_b = ssm_glu_b[l].reshape(1, 2 * GROUP_WIDTH)
        ob_p, hl_p = s5_prompt(h, zeros(bp, 2 * SSM_W), bw, a_bar, cw, dsk, glu_w_b[l], glu_b, bp, lp)
        h0_s = jnp.concatenate([state_ssm_re[l].reshape(bs, SSM_W), state_ssm_im[l].reshape(bs, SSM_W)], axis=1)
        ob_s, hl_s = s5_sample(h, n_p, h0_s, bw, a_bar, cw, dsk, glu_w_b[l], glu_b, bs, ls)
        alog_t = _lane_row(gdn_a_log[l], SM_GA)
        dtb_t = _lane_row(gdn_dt_bias[l], SM_GA)
        gnw = gdn_norm_w[l].reshape(1, GDN_DV)
        oc_p, gs_p = gdn_mixer(h, 0, gdn_conv_w[l], alog_t, dtb_t, gnw, zeros(bp, GDN_CONV - 1, GDN_CONV_DIM),
                               zeros(bp, GDN_HEADS, GDN_DK, GDN_DV), bp, lp)
        oc_s, gs_s = gdn_mixer(h, n_p, gdn_conv_w[l], alog_t, dtb_t, gnw, state_gdn_conv[l], state_gdn[l], bs, ls)
        cv_p = _conv_tail(h, 0, bp, lp)
        cv_s = _conv_tail(h, n_p, bs, ls)
        lgb = gla_gate_b[l].reshape(1, 128)
        lnw = jnp.tile(gla_norm_w[l], GLA_HEADS).reshape(1, 256)
        od_p, lt_p = gla_mixer(h, 0, gla_gate_w[l], lgb, lnw, zeros(bp, 256, 128), bp, lp)
        od_s, lt_s = gla_mixer(h, n_p, gla_gate_w[l], lgb, lnw, gla_state_to_t(state_gla[l]), bs, ls)

        x1 = out_proj_ln(x, (oa_p, ob_p.reshape(n_p, GROUP_WIDTH), oc_p, od_p), (oa_s, ob_s, oc_s, od_s),
                         w_out_b[l], ln1_g[l].reshape(1, D_MODEL), ln1_b[l].reshape(1, D_MODEL))
        x = moe_ffn_ln(x1, rw_pad[l], rb_pad[l].reshape(1, LANE), wgu_all, bgu_all, wd_all, bd_all,
                       ln2_g[l].reshape(1, D_MODEL), ln2_b[l].reshape(1, D_MODEL), expert0=l * N_EXPERTS)

        st_p = (pk.reshape(bp, WINDOW, A_KV_HEADS, HEAD_DIM), pv.reshape(bp, WINDOW, A_KV_HEADS, HEAD_DIM),
                hl_p[:, :SSM_W].reshape(bp, SSM_GROUPS, SSM_STATE), hl_p[:, SSM_W:].reshape(bp, SSM_GROUPS, SSM_STATE),
                cv_p, gs_p, gla_state_from_t(lt_p))
        st_s = (sk.reshape(bs, WINDOW, A_KV_HEADS, HEAD_DIM), sv.reshape(bs, WINDOW, A_KV_HEADS, HEAD_DIM),
                hl_s[:, :SSM_W].reshape(bs, SSM_GROUPS, SSM_STATE), hl_s[:, SSM_W:].reshape(bs, SSM_GROUPS, SSM_STATE),
                cv_s, gs_s, gla_state_from_t(lt_s))
        for i in range(7):
            new_p[i].append(st_p[i])
            new_s[i].append(st_s[i])
    y_p = x[:n_p].reshape(bp, lp, D_MODEL)
    y_s = x[n_p:].reshape(bs, ls, D_MODEL)
    return (y_p, y_s) + tuple(jnp.stack(t, axis=0) for t in new_p) + tuple(jnp.stack(t, axis=0) for t in new_s)
```

```python
import functools

import numpy as np
import jax
import jax.numpy as jnp
from jax import lax
from jax.experimental import pallas as pl
from jax.experimental.pallas import tpu as pltpu

F32 = jnp.float32
BF16 = jnp.bfloat16
I32 = jnp.int32
HI = lax.Precision.HIGHEST

D_MODEL = 1024
DEPTH = 4
PAST_LEN = 8192
GROUP_WIDTH = 256
HEAD_DIM = 64
A_HEADS = 4
A_KV_HEADS = 2
WINDOW = 128
ROPE_THETA = 500000.0
ROT_DIM = 16
SSM_GC = 16
SSM_GROUPS = 16
SSM_STATE = 64
SSM_W = SSM_GROUPS * SSM_STATE
GDN_HEADS = 4
GDN_DK = 64
GDN_DV = 64
GDN_CONV = 4
GDN_CONV_DIM = 768
GDN_CHUNK = 64
GLA_HEADS = 4
GLA_DK = 32
GLA_DV = 64
GLA_RANK = 16
GLA_TAU = 16.0
GLA_CHUNK = 16
N_EXPERTS = 32
TOP_K = 4
D_FF = 1024
SWIGLU_LIMIT = 7.0
SWIGLU_ALPHA = 1.702
DN_ALPHA = (2 * DEPTH) ** 0.25
LN_EPS = 1e-5
RMS_EPS = 1e-6
N_IN = 2584

C_AQ, C_AK, C_AV, C_SU, C_GQKV, C_GZ = 0, 256, 384, 512, 768, 1536
C_LQ, C_LK, C_LV, C_LR, C_SM = 1792, 1920, 2048, 2304, 2560
NH = 2688
SM_GB, SM_GA, SM_LG = 0, 4, 8

LANE = 128
ROW_TILE = 512
MOE_TILE = 512
VMEM_LIMIT = 56 * 1024 * 1024


def _cp(*sem):
    return pltpu.CompilerParams(dimension_semantics=sem, vmem_limit_bytes=VMEM_LIMIT)


def _dot(a, b, precision=None):
    return jnp.dot(a, b, preferred_element_type=F32, precision=precision)


def _dot_nt(a, b, precision=None):
    return lax.dot_general(a, b, (((1,), (1,)), ((), ())), preferred_element_type=F32, precision=precision)


def _dot_tn(a, b, precision=None):
    return lax.dot_general(a, b, (((0,), (0,)), ((), ())), preferred_element_type=F32, precision=precision)


def _bf(x):
    return x.astype(BF16)


def _iota(shape, dim):
    return lax.broadcasted_iota(I32, shape, dim)


def _shr(idx, size):
    return lax.shift_right_logical(idx, int(size).bit_length() - 1)


def _sigmoid(x):
    return 1.0 / (1.0 + jnp.exp(-x))


def _silu(x):
    return x * _sigmoid(x)


def _softplus(x):
    return jnp.maximum(x, 0.0) + jnp.log(1.0 + jnp.exp(-jnp.abs(x)))


def _log_sigmoid(x):
    return -_softplus(-x)


def _gelu_tanh(x):
    return 0.5 * x * (1.0 + jnp.tanh(0.7978845608028654 * (x + 0.044715 * x * x * x)))


def _layer_norm(y, g, b):
    mu = jnp.mean(y, axis=-1, keepdims=True)
    yc = y - mu
    var = jnp.mean(yc * yc, axis=-1, keepdims=True)
    return yc * lax.rsqrt(var + LN_EPS) * g + b


def _in_proj_body(x_ref, w_ref, o_ref):
    xb = _bf(x_ref[...])
    for c0 in range(0, NH, 512):
        c1 = min(c0 + 512, NH)
        o_ref[:, c0:c1] = _dot(xb, w_ref[:, c0:c1])


def in_proj(x, w):
    t = x.shape[0]
    return pl.pallas_call(
        _in_proj_body,
        grid=(t // ROW_TILE,),
        in_specs=[pl.BlockSpec((ROW_TILE, D_MODEL), lambda i: (i, 0)),
                  pl.BlockSpec((D_MODEL, NH), lambda i: (0, 0))],
        out_specs=pl.BlockSpec((ROW_TILE, NH), lambda i: (i, 0)),
        out_shape=jax.ShapeDtypeStruct((t, NH), F32),
        compiler_params=_cp("arbitrary"),
        name="in_proj",
    )(x, w)


def _rope(x, cos, sin):
    w = x.shape[1]
    if w > LANE:
        cos = jnp.concatenate([cos] * (w // LANE), axis=1)
        sin = jnp.concatenate([sin] * (w // LANE), axis=1)
    lane = _iota(x.shape, 1) & (HEAD_DIM - 1)
    swapped = jnp.where(lane < ROT_DIM // 2, pltpu.roll(x, w - ROT_DIM // 2, 1), pltpu.roll(x, ROT_DIM // 2, 1))
    return x * cos + swapped * sin


def _sink_attention(q, kk, vv, mask, sinks_ref):
    outs = []
    for hq in range(A_HEADS):
        g = hq // (A_HEADS // A_KV_HEADS)
        qh = _bf(q[:, hq * HEAD_DIM:(hq + 1) * HEAD_DIM])
        kh = _bf(kk[:, g * HEAD_DIM:(g + 1) * HEAD_DIM])
        vh = _bf(vv[:, g * HEAD_DIM:(g + 1) * HEAD_DIM])
        s = _dot_nt(qh, kh) * (HEAD_DIM ** -0.5)
        s = jnp.where(mask, s, -jnp.inf)
        sink = sinks_ref[hq]
        m = jnp.maximum(jnp.max(s, axis=-1, keepdims=True), sink)
        p = jnp.exp(s - m)
        den = jnp.sum(p, axis=-1, keepdims=True) + jnp.exp(sink - m)
        outs.append(_dot(_bf(p / den), vh))
    return jnp.concatenate(outs, axis=1)


def _swa_prompt_body(sinks_ref, cur_ref, prev_ref, cos_ref, sin_ref, cosp_ref, sinp_ref, o_ref, ko_ref, vo_ref):
    i = pl.program_id(1)
    cur = cur_ref[...]
    q = _rope(cur[:, C_AQ:C_AQ + 256], cos_ref[...], sin_ref[...])
    k = _rope(cur[:, C_AK:C_AK + 128], cos_ref[...], sin_ref[...])
    v = cur[:, C_AV:C_AV + 128]
    prev = prev_ref[...]
    kp = _rope(prev[:, 0:128], cosp_ref[...], sinp_ref[...])
    vp = prev[:, 128:256]
    kk = jnp.concatenate([kp, k], axis=0)
    vv = jnp.concatenate([vp, v], axis=0)
    r = _iota((WINDOW, 2 * WINDOW), 0)
    j = _iota((WINDOW, 2 * WINDOW), 1)
    d = WINDOW + r - j
    mask = (d >= 0) & (d <= WINDOW) & ((j >= WINDOW) | (i > 0))
    o_ref[...] = _sink_attention(q, kk, vv, mask, sinks_ref)
    ko_ref[0] = k
    vo_ref[0] = v


def swa_prompt(h, sinks, cos_t, sin_t, bsz, seq):
    nb = seq // WINDOW
    smem = pl.BlockSpec(memory_space=pltpu.SMEM)
    tab = lambda f: pl.BlockSpec((WINDOW, LANE), f)
    return pl.pallas_call(
        _swa_prompt_body,
        grid=(bsz, nb),
        in_specs=[smem,
                  pl.BlockSpec((WINDOW, 512), lambda b, i: (b * nb + i, 0)),
                  pl.BlockSpec((WINDOW, 256), lambda b, i: (b * nb + jnp.maximum(i - 1, 0), 1)),
                  tab(lambda b, i: (i, 0)), tab(lambda b, i: (i, 0)),
                  tab(lambda b, i: (jnp.maximum(i - 1, 0), 0)), tab(lambda b, i: (jnp.maximum(i - 1, 0), 0))],
        out_specs=[pl.BlockSpec((WINDOW, 256), lambda b, i: (b * nb + i, 0)),
                   pl.BlockSpec((1, WINDOW, 128), lambda b, i: (b, 0, 0)),
                   pl.BlockSpec((1, WINDOW, 128), lambda b, i: (b, 0, 0))],
        out_shape=[jax.ShapeDtypeStruct((bsz * seq, 256), F32),
                   jax.ShapeDtypeStruct((bsz, WINDOW, 128), F32),
                   jax.ShapeDtypeStruct((bsz, WINDOW, 128), F32)],
        compiler_params=_cp("arbitrary", "arbitrary"),
        name="swa_prompt",
    )(sinks, h, h, cos_t, sin_t, cos_t, sin_t)


SWA_SB = 8


def _swa_sample_body(sinks_ref, cur_ref, kc_ref, vc_ref, cos_ref, sin_ref, o_ref, ko_ref, vo_ref, *, ls):
    cw = WINDOW
    r = _iota((ls, cw + ls), 0)
    j = _iota((ls, cw + ls), 1)
    d = cw + r - j
    mask = (d >= 0) & (d <= WINDOW)
    for b in range(SWA_SB):
        cur = cur_ref[b * ls:(b + 1) * ls, :]
        q = _rope(cur[:, C_AQ:C_AQ + 256], cos_ref[...], sin_ref[...])
        k = _rope(cur[:, C_AK:C_AK + 128], cos_ref[...], sin_ref[...])
        v = cur[:, C_AV:C_AV + 128]
        kk = jnp.concatenate([kc_ref[b], k], axis=0)
        vv = jnp.concatenate([vc_ref[b], v], axis=0)
        o_ref[b * ls:(b + 1) * ls, :] = _sink_attention(q, kk, vv, mask, sinks_ref)
        ko_ref[b] = kk[ls:, :]
        vo_ref[b] = vv[ls:, :]


def swa_sample(h, row0, sinks, k_cache, v_cache, cos_t, sin_t, bsz, ls):
    rows = SWA_SB * ls
    blk0 = row0 // rows
    smem = pl.BlockSpec(memory_space=pltpu.SMEM)
    cache = pl.BlockSpec((SWA_SB, WINDOW, 128), lambda i: (i, 0, 0))
    tab = pl.BlockSpec((ls, LANE), lambda i: (0, 0))
    return pl.pallas_call(
        functools.partial(_swa_sample_body, ls=ls),
        grid=(bsz // SWA_SB,),
        in_specs=[smem, pl.BlockSpec((rows, 512), lambda i: (blk0 + i, 0)), cache, cache, tab, tab],
        out_specs=[pl.BlockSpec((rows, 256), lambda i: (i, 0)), cache, cache],
        out_shape=[jax.ShapeDtypeStruct((bsz * ls, 256), F32),
                   jax.ShapeDtypeStruct((bsz, WINDOW, 128), F32),
                   jax.ShapeDtypeStruct((bsz, WINDOW, 128), F32)],
        compiler_params=_cp("arbitrary"),
        name="swa_sample",
    )(sinks, h, k_cache, v_cache, cos_t, sin_t)


S5_NB = 2 * SSM_W // LANE


def _s5_input(u, bw_ref):
    t = _dot(_bf(u), bw_ref[...])
    return jnp.concatenate([t[:, :SSM_W], t[:, :SSM_W] + t[:, SSM_W:]], axis=1)


def _s5_output(hs, u, cw_ref, d_ref, gw_ref, gb_ref):
    hs = jnp.concatenate([hs[:, :SSM_W] + hs[:, SSM_W:], hs[:, SSM_W:]], axis=1)
    y = _dot(_bf(hs), cw_ref[...]) + d_ref[...] * u
    y = _gelu_tanh(y)
    z = _dot(_bf(y), gw_ref[...]) + gb_ref[...]
    return z[:, :GROUP_WIDTH] * _sigmoid(z[:, GROUP_WIDTH:])


def _s5_scan(s_ref, a_ref, h_init, n_steps, rows):
    nre = S5_NB // 2
    a_re = [jnp.broadcast_to(a_ref[0:1, j * LANE:(j + 1) * LANE], (rows, LANE)) for j in range(nre)]
    a_im = [jnp.broadcast_to(a_ref[1:2, j * LANE:(j + 1) * LANE], (rows, LANE)) for j in range(nre)]

    def step(t, hcar):
        out = [None] * S5_NB
        base = pl.multiple_of(t * rows, rows)
        for j in range(nre):
            hr, hi = hcar[j], hcar[j + nre]
            nr = a_re[j] * hr - a_im[j] * hi + s_ref[j, pl.ds(base, rows), :]
            ni = a_re[j] * hi + a_im[j] * hr + s_ref[j + nre, pl.ds(base, rows), :]
            s_ref[j, pl.ds(base, rows), :] = nr
            s_ref[j + nre, pl.ds(base, rows), :] = ni
            out[j], out[j + nre] = nr, ni
        return tuple(out)

    return lax.fori_loop(0, n_steps, step, tuple(h_init))


def _s5_prompt_body(*refs, nb, tl):
    u_refs = refs[:nb]
    bw_ref, a_ref, cw_ref, d_ref, gw_ref, gb_ref, h0_ref, o_ref, hl_ref, s_ref, hst_ref = refs[nb:]
    i = pl.program_id(0)

    @pl.when(i == 0)
    def _():
        hst_ref[...] = h0_ref[...]

    for b in range(nb):
        bu = _s5_input(u_refs[b][...], bw_ref)
        for j in range(S5_NB):
            s_ref[j, pl.ds(b, tl, stride=nb), :] = bu[:, j * LANE:(j + 1) * LANE]
    h_init = [hst_ref[:, j * LANE:(j + 1) * LANE] for j in range(S5_NB)]
    h_fin = _s5_scan(s_ref, a_ref, h_init, tl, nb)
    for j in range(S5_NB):
        hst_ref[:, j * LANE:(j + 1) * LANE] = h_fin[j]
    for b in range(nb):
        hs = jnp.concatenate([s_ref[j, pl.ds(b, tl, stride=nb), :] for j in range(S5_NB)], axis=1)
        o_ref[b] = _s5_output(hs, u_refs[b][...], cw_ref, d_ref, gw_ref, gb_ref)

    @pl.when(i == pl.num_programs(0) - 1)
    def _():
        hl_ref[...] = hst_ref[...]


def s5_prompt(h, h0, bw, a, cw, dsk, gw, gb, bsz, seq, tl=128):
    assert bsz == 8
    nt = seq // tl
    full = lambda shape: pl.BlockSpec(shape, lambda i: (0,) * len(shape))
    u_specs = [pl.BlockSpec((tl, GROUP_WIDTH), functools.partial(lambda i, b: (b * nt + i, C_SU // GROUP_WIDTH), b=b))
               for b in range(bsz)]
    return pl.pallas_call(
        functools.partial(_s5_prompt_body, nb=bsz, tl=tl),
        grid=(nt,),
        in_specs=u_specs + [full((GROUP_WIDTH, 2 * SSM_W)), full((2, SSM_W)), full((2 * SSM_W, GROUP_WIDTH)),
                            full((1, GROUP_WIDTH)), full((GROUP_WIDTH, 2 * GROUP_WIDTH)), full((1, 2 * GROUP_WIDTH)),
                            full((bsz, 2 * SSM_W))],
        out_specs=[pl.BlockSpec((bsz, tl, GROUP_WIDTH), lambda i: (0, i, 0)), full((bsz, 2 * SSM_W))],
        out_shape=[jax.ShapeDtypeStruct((bsz, seq, GROUP_WIDTH), F32), jax.ShapeDtypeStruct((bsz, 2 * SSM_W), F32)],
        scratch_shapes=[pltpu.VMEM((S5_NB, tl * bsz, LANE), F32), pltpu.VMEM((bsz, 2 * SSM_W), F32)],
        compiler_params=_cp("arbitrary"),
        name="s5_prompt",
    )(*([h] * bsz), bw, a, cw, dsk, gw, gb, h0)


def _s5_sample_body(u_ref, bw_ref, a_ref, cw_ref, d_ref, gw_ref, gb_ref, h0_ref, o_ref, hl_ref, s_ref, t_ref, *, bsz, ls):
    bu = _s5_input(u_ref[...], bw_ref)
    for j in range(S5_NB):
        s_ref[j] = bu[:, j * LANE:(j + 1) * LANE]
    for t in range(ls):
        for j in range(S5_NB):
            t_ref[j, pl.ds(t * bsz, bsz), :] = s_ref[j, pl.ds(t, bsz, stride=ls), :]
    h_init = [h0_ref[:, j * LANE:(j + 1) * LANE] for j in range(S5_NB)]
    h_fin = _s5_scan(t_ref, a_ref, h_init, ls, bsz)
    for j in range(S5_NB):
        hl_ref[:, j * LANE:(j + 1) * LANE] = h_fin[j]
    for t in range(ls):
        for j in range(S5_NB):
            s_ref[j, pl.ds(t, bsz, stride=ls), :] = t_ref[j, pl.ds(t * bsz, bsz), :]
    hs = jnp.concatenate([s_ref[j] for j in range(S5_NB)], axis=1)
    o_ref[...] = _s5_output(hs, u_ref[...], cw_ref, d_ref, gw_ref, gb_ref)


def s5_sample(h, row0, h0, bw, a, cw, dsk, gw, gb, bsz, ls):
    rows = bsz * ls
    full = lambda shape: pl.BlockSpec(shape, lambda i: (0,) * len(shape))
    return pl.pallas_call(
        functools.partial(_s5_sample_body, bsz=bsz, ls=ls),
        grid=(1,),
        in_specs=[pl.BlockSpec((rows, GROUP_WIDTH), lambda i: (row0 // rows, C_SU // GROUP_WIDTH)),
                  full((GROUP_WIDTH, 2 * SSM_W)), full((2, SSM_W)), full((2 * SSM_W, GROUP_WIDTH)),
                  full((1, GROUP_WIDTH)), full((GROUP_WIDTH, 2 * GROUP_WIDTH)), full((1, 2 * GROUP_WIDTH)),
                  full((bsz, 2 * SSM_W))],
        out_specs=[full((rows, GROUP_WIDTH)), full((bsz, 2 * SSM_W))],
        out_shape=[jax.ShapeDtypeStruct((rows, GROUP_WIDTH), F32), jax.ShapeDtypeStruct((bsz, 2 * SSM_W), F32)],
        scratch_shapes=[pltpu.VMEM((S5_NB, rows, LANE), F32), pltpu.VMEM((S5_NB, rows, LANE), F32)],
        compiler_params=_cp("arbitrary"),
        name="s5_sample",
    )(h, bw, a, cw, dsk, gw, gb, h0)


def s5_params(a_re, a_im, b_re, b_im, c_re, c_im, d_skip, log_dt):
    lam = lax.complex(a_re, a_im)
    delta = jnp.exp(log_dt)[:, None]
    a_bar = jnp.exp(lam * delta)
    b_bar = ((a_bar - 1.0) / lam)[..., None] * lax.complex(b_re, b_im)
    eye = jnp.eye(SSM_GROUPS, dtype=F32)
    bw_re = jnp.einsum('gpc,gh->gchp', b_bar.real, eye).reshape(GROUP_WIDTH, SSM_W)
    bw_im = jnp.einsum('gpc,gh->gchp', b_bar.imag, eye).reshape(GROUP_WIDTH, SSM_W)
    bw = jnp.concatenate([bw_re, bw_im - bw_re], axis=1).astype(BF16)
    cw_re = jnp.einsum('gcp,gh->gphc', c_re, eye).reshape(SSM_W, GROUP_WIDTH)
    cw_im = jnp.einsum('gcp,gh->gphc', c_im, eye).reshape(SSM_W, GROUP_WIDTH)
    cw = jnp.concatenate([cw_re, -(cw_re + cw_im)], axis=0).astype(BF16)
    a = jnp.stack([a_bar.real.reshape(SSM_W), a_bar.imag.reshape(SSM_W)], axis=0)
    return bw, a, cw, d_skip.reshape(1, GROUP_WIDTH)


GDN_RT = 256
GDN_ST = 64
GDN_SG = 8


def _split(x):
    hi = _bf(x)
    return hi, _bf(x - hi.astype(F32))


def _dot3(a, b):
    return _dot(a[0], b[0]) + (_dot(a[0], b[1]) + _dot(a[1], b[0]))


def _gdn_prep_body(*refs, c, sample, tiles_per_seq):
    if sample:
        x_ref, cs_ref, z_ref, sm_ref, cw_ref, alog_ref, dtb_ref = refs[:7]
        outs = refs[7:14]
        xs_ref, cb_ref = refs[14:]
    else:
        x_ref, prev_ref, cs_ref, z_ref, sm_ref, cw_ref, alog_ref, dtb_ref = refs[:8]
        outs = refs[8:15]
        (xs_ref,) = refs[15:]
    u_ref, w_ref, qk_ref, qg_ref, kd_ref, eg_ref, zs_ref = outs
    rt = GDN_RT
    x = x_ref[...]
    xs_ref[8:8 + rt, :] = x
    if sample:
        xs_ref[0:8, :] = jnp.zeros((8, GDN_CONV_DIM), F32)
        cb_ref[0:rt, :] = cs_ref[...]
        cb_ref[rt:rt + 8, :] = jnp.zeros((8, GDN_CONV_DIM), F32)
        pos = _iota((rt, GDN_CONV_DIM), 0) & (c - 1)
        shifted = lambda i: jnp.where(pos >= i, xs_ref[8 - i:8 - i + rt, :], cb_ref[8 - i:8 - i + rt, :])
    else:
        first = (pl.program_id(0) % tiles_per_seq) == 0
        xs_ref[0:8, :] = jnp.where(first, cs_ref[...], prev_ref[...])
        shifted = lambda i: xs_ref[8 - i:8 - i + rt, :]
    conv = shifted(3) * cw_ref[0:1, :]
    conv = conv + shifted(2) * cw_ref[1:2, :]
    conv = conv + shifted(1) * cw_ref[2:3, :]
    conv = conv + x * cw_ref[3:4, :]
    qkv = _silu(conv)
    zs_ref[...] = _silu(z_ref[...])

    sm = sm_ref[...]
    beta_all = _sigmoid(sm)
    g_all = -jnp.exp(alog_ref[...]) * _softplus(sm + dtb_ref[...])
    st = GDN_ST
    ri = _iota((st, st), 0)
    ci = _iota((st, st), 1)
    same = _shr(ri, c) == _shr(ci, c)
    low = (ci <= ri) & same
    tri = low.astype(F32)
    last = (ci == (ri | (c - 1))).astype(F32)
    eye = (ri == ci).astype(F32)
    lane = _iota((st, LANE), 1)
    nsub = rt // st
    probs = [(t, hh) for t in range(nsub) for hh in range(GDN_HEADS)]
    gc_alls, gl_alls = [], []
    for t in range(nsub):
        rows = slice(t * st, (t + 1) * st)
        gc_alls.append(_dot(tri, g_all[rows, :], HI))
    for t in range(nsub):
        gl_alls.append(_dot(last, gc_alls[t], HI))
        eg_ref[t * st:(t + 1) * st, :] = jnp.exp(gl_alls[t])
    qs, ks, rhss, decays, a_stricts, invs = {}, {}, {}, {}, {}, {}
    for t, hh in probs:
        rows = slice(t * st, (t + 1) * st)
        q = qkv[rows, hh * GDN_DK:(hh + 1) * GDN_DK]
        k = qkv[rows, 256 + hh * GDN_DK:256 + (hh + 1) * GDN_DK]
        v = qkv[rows, 512 + hh * GDN_DV:512 + (hh + 1) * GDN_DV]
        q = q * lax.rsqrt(jnp.sum(q * q, axis=-1, keepdims=True) + 1e-6) * (GDN_DK ** -0.5)
        k = k * lax.rsqrt(jnp.sum(k * k, axis=-1, keepdims=True) + 1e-6)
        beta = beta_all[rows, SM_GB + hh:SM_GB + hh + 1]
        gc = gc_alls[t][:, SM_GA + hh:SM_GA + hh + 1]
        gl = gl_alls[t][:, SM_GA + hh:SM_GA + hh + 1]
        sel = (lane == SM_GA + hh).astype(F32)
        gc_row = _dot_nt(sel, gc_alls[t], HI)
        decays[t, hh] = jnp.exp(jnp.where(low, gc - gc_row, -jnp.inf))
        kb = k * beta
        eg = jnp.exp(gc)
        qs[t, hh], ks[t, hh] = q, k
        rhss[t, hh] = _split(jnp.concatenate([v * beta, kb * eg], axis=1))
        a_stricts[t, hh] = jnp.where(ci < ri, _dot_nt(_bf(kb), _bf(k)) * decays[t, hh], 0.0)
        qg_ref[rows, hh * GDN_DK:(hh + 1) * GDN_DK] = q * eg
        kd_ref[rows, hh * GDN_DK:(hh + 1) * GDN_DK] = k * jnp.exp(gl - gc)
    for t, hh in probs:
        qk_ref[t * st:(t + 1) * st, hh * st:(hh + 1) * st] = _dot_nt(_bf(qs[t, hh]), _bf(ks[t, hh])) * decays[t, hh]
    for p in probs:
        invs[p] = eye - jnp.where(_shr(ri, 2) == _shr(ci, 2), a_stricts[p], 0.0)
    s = 2
    while s < c:
        pair = (_shr(ri, 2 * s) == _shr(ci, 2 * s)) & (_shr(ri, s) != _shr(ci, s))
        inv_s = {p: _split(invs[p]) for p in probs}
        mid = {p: _dot3(_split(jnp.where(pair, a_stricts[p], 0.0)), inv_s[p]) for p in probs}
        for p in probs:
            invs[p] = invs[p] - _dot3(inv_s[p], _split(mid[p]))
        s *= 2
    for t, hh in probs:
        sol = _dot3(_split(invs[t, hh]), rhss[t, hh])
        u_ref[t * st:(t + 1) * st, hh * GDN_DV:(hh + 1) * GDN_DV] = sol[:, :GDN_DV]
        w_ref[t * st:(t + 1) * st, hh * GDN_DV:(hh + 1) * GDN_DV] = sol[:, GDN_DV:]


def _gdn_scan_body(u_ref, w_ref, qk_ref, qg_ref, kd_ref, eg_ref, zs_ref, nw_ref, s0_ref, o_ref, so_ref, st_ref, *, c):
    n = pl.program_id(1)

    @pl.when(n == 0)
    def _():
        st_ref[...] = s0_ref[...]

    probs = [(s, hh) for s in range(GDN_SG) for hh in range(GDN_HEADS)]
    hsl = lambda hh: slice(hh * GDN_DV, (hh + 1) * GDN_DV)
    state_b = {(s, hh): _bf(st_ref[s, hh]) for s, hh in probs}
    v_b = {}
    for s, hh in probs:
        v_b[s, hh] = _bf(u_ref[s, :, hsl(hh)] - _dot(_bf(w_ref[s, :, hsl(hh)]), state_b[s, hh]))
    o_inter = {(s, hh): _dot(_bf(qg_ref[s, :, hsl(hh)]), state_b[s, hh]) for s, hh in probs}
    for s, hh in probs:
        col0 = (s * c) % GDN_ST
        qk = qk_ref[s, :, hh * GDN_ST + col0:hh * GDN_ST + col0 + c]
        o = o_inter[s, hh] + _dot(_bf(qk), v_b[s, hh])
        o = o * lax.rsqrt(jnp.mean(o * o, axis=-1, keepdims=True) + RMS_EPS) * nw_ref[...]
        o_ref[s, :, hsl(hh)] = o * zs_ref[s, :, hsl(hh)]
    for s, hh in probs:
        eg_last = eg_ref[s, c - 1:c, SM_GA + hh:SM_GA + hh + 1]
        st_ref[s, hh] = st_ref[s, hh] * eg_last + _dot_tn(_bf(kd_ref[s, :, hsl(hh)]), v_b[s, hh])

    @pl.when(n == pl.num_programs(1) - 1)
    def _():
        so_ref[...] = st_ref[...]


def gdn_mixer(h, row0, conv_w, alog_t, dtb_t, norm_w, conv_state, s0, bsz, seq):
    c = min(GDN_CHUNK, seq)
    sample = seq == c
    rows = bsz * seq
    rt = GDN_RT
    nt = rows // rt
    blk0 = row0 // rt
    tps = max(seq // rt, 1)
    cs_rows = jnp.pad(conv_state, ((0, 0), (8 - (GDN_CONV - 1), 0), (0, 0))).reshape(bsz * 8, GDN_CONV_DIM)
    row = lambda w, col: pl.BlockSpec((rt, w), lambda i: (blk0 + i, col))
    full = lambda shape: pl.BlockSpec(shape, lambda i: (0,) * len(shape))
    if sample:
        aux_specs = [pl.BlockSpec((rt, GDN_CONV_DIM), lambda i: (i, 0))]
        aux = [cs_rows]
        scratch = [pltpu.VMEM((8 + rt, GDN_CONV_DIM), F32), pltpu.VMEM((8 + rt, GDN_CONV_DIM), F32)]
    else:
        aux_specs = [pl.BlockSpec((8, GDN_CONV_DIM), lambda i: (jnp.maximum((row0 + i * rt) // 8 - 1, 0), C_GQKV // GDN_CONV_DIM)),
                     pl.BlockSpec((8, GDN_CONV_DIM), lambda i: (i // tps, 0))]
        aux = [h, cs_rows]
        scratch = [pltpu.VMEM((8 + rt, GDN_CONV_DIM), F32)]
    out_w = [GROUP_WIDTH] * 5 + [LANE, GROUP_WIDTH]
    prep = pl.pallas_call(
        functools.partial(_gdn_prep_body, c=c, sample=sample, tiles_per_seq=tps),
        grid=(nt,),
        in_specs=[row(GDN_CONV_DIM, C_GQKV // GDN_CONV_DIM)] + aux_specs
                 + [row(GROUP_WIDTH, C_GZ // GROUP_WIDTH), row(LANE, C_SM // LANE),
                    full((GDN_CONV, GDN_CONV_DIM)), full((1, LANE)), full((1, LANE))],
        out_specs=[pl.BlockSpec((rt, w), lambda i: (i, 0)) for w in out_w],
        out_shape=[jax.ShapeDtypeStruct((rows, w), F32) for w in out_w],
        scratch_shapes=scratch,
        compiler_params=_cp("arbitrary"),
        name="gdn_prep",
    )(h, *aux, h, h, conv_w, alog_t, dtb_t)
    prep = [a.reshape(bsz, seq, a.shape[-1]) for a in prep]
    ng, nc = bsz // GDN_SG, seq // c
    seq_blk = lambda w: pl.BlockSpec((GDN_SG, c, w), lambda g, n: (g, n, 0))
    st_spec = pl.BlockSpec((GDN_SG, GDN_HEADS, GDN_DK, GDN_DV), lambda g, n: (g, 0, 0, 0))
    o, s_fin = pl.pallas_call(
        functools.partial(_gdn_scan_body, c=c),
        grid=(ng, nc),
        in_specs=[seq_blk(w) for w in out_w] + [pl.BlockSpec((1, GDN_DV), lambda g, n: (0, 0)), st_spec],
        out_specs=[seq_blk(GROUP_WIDTH), st_spec],
        out_shape=[jax.ShapeDtypeStruct((bsz, seq, GROUP_WIDTH), F32),
                   jax.ShapeDtypeStruct((bsz, GDN_HEADS, GDN_DK, GDN_DV), F32)],
        scratch_shapes=[pltpu.VMEM((GDN_SG, GDN_HEADS, GDN_DK, GDN_DV), F32)],
        compiler_params=_cp("arbitrary", "arbitrary"),
        name="gdn_scan",
    )(*prep, norm_w, s0)
    return o.reshape(rows, GROUP_WIDTH), s_fin


def _gla_body(qk_ref, vr_ref, sm_ref, gw_ref, gb_ref, nw_ref, s0_ref, o_ref, so_ref,
              kb_ref, bb_ref, vb_ref, st_ref, *, c, tr):
    n = pl.program_id(1)
    pad = GLA_CHUNK

    @pl.when(n == 0)
    def _():
        kb_ref[0:pad, :] = jnp.zeros((pad, 128), F32)
        bb_ref[0:pad, :] = jnp.zeros((pad, 128), F32)
        vb_ref[0:pad, :] = jnp.zeros((pad, 256), F32)
        st_ref[...] = s0_ref[0]

    q = qk_ref[:, 0:128] * (GLA_DK ** -0.5)
    k = qk_ref[:, 128:256]
    v = vr_ref[:, 0:256]
    r = vr_ref[:, 256:512]
    lg = sm_ref[:, SM_LG:SM_LG + GLA_RANK]
    log_a = _log_sigmoid(_dot(_bf(lg), _bf(gw_ref[...])) + gb_ref[...]) / GLA_TAU
    ri = _iota((tr, tr), 0)
    ci = _iota((tr, tr), 1)
    tri = ((ci <= ri) & (_shr(ri, c) == _shr(ci, c))).astype(F32)
    bc = _dot(tri, log_a, HI)
    kb_ref[pad:pad + tr, :] = k
    bb_ref[pad:pad + tr, :] = bc
    vb_ref[pad:pad + tr, :] = v
    ones_kv = (_shr(_iota((128, 256), 0), GLA_DK) == _shr(_iota((128, 256), 1), GLA_DV)).astype(F32)
    pos = _iota((tr, 128), 0) & (c - 1)
    o_intra = jnp.zeros((tr, 256), F32)
    for dl in range(c):
        ks = kb_ref[pad - dl:pad - dl + tr, :]
        bs = bb_ref[pad - dl:pad - dl + tr, :]
        vs = vb_ref[pad - dl:pad - dl + tr, :]
        p = jnp.where(pos >= dl, q * ks * jnp.exp(bc - bs), 0.0)
        o_intra = o_intra + _bf(_dot(p, ones_kv, HI)).astype(F32) * _bf(vs).astype(F32)
    bd = (_shr(_iota((256, 128), 0), GLA_DV) == _shr(_iota((256, 128), 1), GLA_DK)).astype(F32)
    ones_vv = (_shr(_iota((256, 256), 0), GLA_DV) == _shr(_iota((256, 256), 1), GLA_DV)).astype(F32)
    for ch in range(tr // c):
        sl = slice(ch * c, (ch + 1) * c)
        b_c = bc[sl, :]
        b_last = b_c[c - 1:c, :]
        st = st_ref[...]
        o = _dot_nt(_bf(q[sl, :] * jnp.exp(b_c)), _bf(st)) + o_intra[sl, :]
        st_ref[...] = st * jnp.exp(b_last) + bd * _dot_tn(_bf(v[sl, :]), _bf(k[sl, :] * jnp.exp(b_last - b_c)))
        ms = _dot(o * o, ones_vv, HI) * (1.0 / GLA_DV)
        o = o * lax.rsqrt(ms + RMS_EPS) * nw_ref[...]
        o_ref[sl, :] = o * _silu(r[sl, :])
    so_ref[0] = st_ref[...]


def gla_mixer(h, row0, gate_w, gate_b, norm_w4, s0t, bsz, seq):
    c = min(GLA_CHUNK, seq)
    tr = min(256, seq)
    nt = seq // tr
    blk0 = row0 // tr
    row = lambda w, col: pl.BlockSpec((tr, w), lambda b, n: (blk0 + b * nt + n, col))
    full = lambda shape: pl.BlockSpec(shape, lambda b, n: (0,) * len(shape))
    st_spec = pl.BlockSpec((1, 256, 128), lambda b, n: (b, 0, 0))
    return pl.pallas_call(
        functools.partial(_gla_body, c=c, tr=tr),
        grid=(bsz, nt),
        in_specs=[row(256, C_LQ // 256), row(512, C_LV // 512), row(LANE, C_SM // LANE),
                  full((GLA_RANK, 128)), full((1, 128)), full((1, 256)), st_spec],
        out_specs=[pl.BlockSpec((tr, 256), lambda b, n: (b * nt + n, 0)), st_spec],
        out_shape=[jax.ShapeDtypeStruct((bsz * seq, 256), F32), jax.ShapeDtypeStruct((bsz, 256, 128), F32)],
        scratch_shapes=[pltpu.VMEM((GLA_CHUNK + tr, 128), F32), pltpu.VMEM((GLA_CHUNK + tr, 128), F32),
                        pltpu.VMEM((GLA_CHUNK + tr, 256), F32), pltpu.VMEM((256, 128), F32)],
        compiler_params=_cp("arbitrary", "arbitrary"),
        name="gla",
    )(h, h, h, gate_w, gate_b, norm_w4, s0t)


def gla_state_to_t(s):
    eye = jnp.eye(GLA_HEADS, dtype=s.dtype)
    return jnp.einsum('bhde,hg->bhegd', s, eye).reshape(s.shape[0], 256, 128)


def gla_state_from_t(st):
    b = st.shape[0]
    t5 = st.reshape(b, GLA_HEADS, GLA_DV, GLA_HEADS, GLA_DK)
    diag = jnp.stack([t5[:, hh, :, hh, :] for hh in range(GLA_HEADS)], axis=1)
    return jnp.swapaxes(diag, 2, 3)


def _out_proj_body(x_ref, ap, bp, cp, dp, a_s, b_s, c_s, d_s, w_ref, g_ref, b_ref, o_ref, *, n_prompt_tiles):
    i = pl.program_id(0)

    def run(refs):
        acc = DN_ALPHA * x_ref[...]
        for m, r in enumerate(refs):
            acc = acc + _dot(_bf(r[...]), w_ref[m * GROUP_WIDTH:(m + 1) * GROUP_WIDTH, :])
        o_ref[...] = _layer_norm(acc, g_ref[...], b_ref[...])

    @pl.when(i < n_prompt_tiles)
    def _():
        run((ap, bp, cp, dp))

    @pl.when(i >= n_prompt_tiles)
    def _():
        run((a_s, b_s, c_s, d_s))


def out_proj_ln(x, mix_p, mix_s, w, g, b):
    t = x.shape[0]
    npt = mix_p[0].shape[0] // ROW_TILE
    nst = mix_s[0].shape[0] // ROW_TILE
    p_spec = pl.BlockSpec((ROW_TILE, GROUP_WIDTH), lambda i: (jnp.minimum(i, npt - 1), 0))
    s_spec = pl.BlockSpec((ROW_TILE, GROUP_WIDTH), lambda i: (jnp.clip(i - npt, 0, nst - 1), 0))
    full = lambda shape: pl.BlockSpec(shape, lambda i: (0,) * len(shape))
    return pl.pallas_call(
        functools.partial(_out_proj_body, n_prompt_tiles=npt),
        grid=(t // ROW_TILE,),
        in_specs=[pl.BlockSpec((ROW_TILE, D_MODEL), lambda i: (i, 0))] + [p_spec] * 4 + [s_spec] * 4
                 + [full((D_MODEL, D_MODEL)), full((1, D_MODEL)), full((1, D_MODEL))],
        out_specs=pl.BlockSpec((ROW_TILE, D_MODEL), lambda i: (i, 0)),
        out_shape=jax.ShapeDtypeStruct((t, D_MODEL), F32),
        compiler_params=_cp("arbitrary"),
        name="out_proj_ln",
    )(x, *mix_p, *mix_s, w, g, b)


def _router_body(x_ref, w_ref, b_ref, e_ref, g_ref):
    logits = _dot(_bf(x_ref[...]), _bf(w_ref[...])) + b_ref[...]
    lane = _iota(logits.shape, 1)
    vals = jnp.where(lane < N_EXPERTS, logits, -jnp.inf)
    eo = jnp.zeros(logits.shape, I32)
    top = []
    for k in range(TOP_K):
        m = jnp.max(vals, axis=-1, keepdims=True)
        idx = jnp.min(jnp.where(vals == m, lane.astype(F32), float(LANE)), axis=-1, keepdims=True).astype(I32)
        eo = jnp.where(lane == k, idx, eo)
        top.append(m)
        vals = jnp.where(lane == idx, -jnp.inf, vals)
    ex = [jnp.exp(v - top[0]) for v in top]
    den = ex[0] + ex[1] + ex[2] + ex[3]
    go = jnp.zeros(logits.shape, F32)
    for k in range(TOP_K):
        go = jnp.where(lane == k, ex[k] / den, go)
    e_ref[...] = eo
    g_ref[...] = go


def router(x, w, b):
    t = x.shape[0]
    return pl.pallas_call(
        _router_body,
        grid=(t // ROW_TILE,),
        in_specs=[pl.BlockSpec((ROW_TILE, D_MODEL), lambda i: (i, 0)),
                  pl.BlockSpec((D_MODEL, LANE), lambda i: (0, 0)), pl.BlockSpec((1, LANE), lambda i: (0, 0))],
        out_specs=[pl.BlockSpec((ROW_TILE, LANE), lambda i: (i, 0))] * 2,
        out_shape=[jax.ShapeDtypeStruct((t, LANE), I32), jax.ShapeDtypeStruct((t, LANE), F32)],
        compiler_params=_cp("arbitrary"),
        name="router",
    )(x, w, b)


DISPATCH_TILE = 256


def _dispatch_copy(x_ref, xs_ref, sem, r, d):
    return pltpu.make_async_copy(x_ref.at[pl.ds(r, 1), :], xs_ref.at[pl.ds(d, 1), :], sem)


def _dispatch_body(dest_ref, x_ref, xs_in_ref, xs_ref, sem):
    del xs_in_ref

    def start(r, carry):
        for k in range(TOP_K):
            _dispatch_copy(x_ref, xs_ref, sem, r, dest_ref[0, 0, r * TOP_K + k]).start()
        return carry

    lax.fori_loop(0, DISPATCH_TILE, start, 0)

    def wait(r, carry):
        for k in range(TOP_K):
            _dispatch_copy(x_ref, xs_ref, sem, 0, 0).wait()
        return carry

    lax.fori_loop(0, DISPATCH_TILE, wait, 0)


def moe_dispatch(x, dest, n_rows):
    t = x.shape[0]
    nt = t // DISPATCH_TILE
    zeros = jnp.zeros((n_rows, D_MODEL), F32)
    return pl.pallas_call(
        _dispatch_body,
        grid=(nt,),
        in_specs=[pl.BlockSpec((1, 1, DISPATCH_TILE * TOP_K), lambda i: (i, 0, 0), memory_space=pltpu.SMEM),
                  pl.BlockSpec((DISPATCH_TILE, D_MODEL), lambda i: (i, 0)),
                  pl.BlockSpec(memory_space=pl.ANY)],
        out_specs=pl.BlockSpec(memory_space=pl.ANY),
        out_shape=jax.ShapeDtypeStruct((n_rows, D_MODEL), F32),
        scratch_shapes=[pltpu.SemaphoreType.DMA(())],
        input_output_aliases={2: 0},
        compiler_params=_cp("arbitrary"),
        name="moe_dispatch",
    )(dest.reshape(nt, 1, DISPATCH_TILE * TOP_K), x, zeros)


EXPERT_SUB = 256


def _expert_body(te_ref, tv_ref, xs_ref, wgu_ref, bgu_ref, wd_ref, bd_ref, ys_ref, wgu_s, wd2_s, tmp_s):
    i = pl.program_id(0)
    valid = tv_ref[i] != 0
    changed = (i == 0) | (te_ref[i] != te_ref[jnp.maximum(i - 1, 0)])

    @pl.when(i == 0)
    def _():
        tmp_s[...] = jnp.zeros(tmp_s.shape, F32)

    @pl.when(valid & changed)
    def _():
        for c0 in range(0, 2 * D_FF, 512):
            wgu_s[:, c0:c0 + 512] = _bf(wgu_ref[0, :, c0:c0 + 512])
        for j in range(D_MODEL // LANE):
            tmp_s[pl.ds(0, D_FF, stride=2), :] = wd_ref[0, :, j * LANE:(j + 1) * LANE]
            wd2_s[:, j * LANE:(j + 1) * LANE] = _bf(tmp_s[...])

    @pl.when(valid)
    def _():
        for r0 in range(0, MOE_TILE, EXPERT_SUB):
            rows = slice(r0, r0 + EXPERT_SUB)
            gu = _dot(_bf(xs_ref[rows, :]), wgu_s[...]) + bgu_ref[0]
            nxt = pltpu.roll(gu, 2 * D_FF - 1, 1)
            gate = jnp.minimum(gu, SWIGLU_LIMIT)
            up = jnp.clip(nxt, -SWIGLU_LIMIT, SWIGLU_LIMIT)
            act = gate * _sigmoid(gate * SWIGLU_ALPHA) * (up + 1.0)
            even = (_iota(act.shape, 1) & 1) == 0
            act = jnp.where(even, act, 0.0)
            ys_ref[rows, :] = _dot(_bf(act), wd2_s[...]) + bd_ref[0]

    @pl.when(jnp.logical_not(valid))
    def _():
        ys_ref[...] = jnp.zeros(ys_ref.shape, F32)


def moe_experts(xs, tile_e, tile_valid, wgu, bgu, wd, bd):
    n_rows = xs.shape[0]
    nt = n_rows // MOE_TILE
    grid_spec = pltpu.PrefetchScalarGridSpec(
        num_scalar_prefetch=2,
        grid=(nt,),
        in_specs=[pl.BlockSpec((MOE_TILE, D_MODEL), lambda i, te, tv: (i, 0)),
                  pl.BlockSpec((1, D_MODEL, 2 * D_FF), lambda i, te, tv: (te[i], 0, 0)),
                  pl.BlockSpec((1, 1, 2 * D_FF), lambda i, te, tv: (te[i], 0, 0)),
                  pl.BlockSpec((1, D_FF, D_MODEL), lambda i, te, tv: (te[i], 0, 0)),
                  pl.BlockSpec((1, 1, D_MODEL), lambda i, te, tv: (te[i], 0, 0))],
        out_specs=pl.BlockSpec((MOE_TILE, D_MODEL), lambda i, te, tv: (i, 0)),
        scratch_shapes=[pltpu.VMEM((D_MODEL, 2 * D_FF), BF16), pltpu.VMEM((2 * D_FF, D_MODEL), BF16),
                        pltpu.VMEM((2 * D_FF, LANE), F32)],
    )
    return pl.pallas_call(
        _expert_body,
        grid_spec=grid_spec,
        out_shape=jax.ShapeDtypeStruct((n_rows, D_MODEL), F32),
        compiler_params=_cp("arbitrary"),
        name="moe_experts",
    )(tile_e, tile_valid, xs, wgu, bgu, wd, bd)


def _combine_copy(ys_ref, buf_ref, sem, d, k, r):
    return pltpu.make_async_copy(ys_ref.at[pl.ds(d, 1), :], buf_ref.at[k, pl.ds(r, 1), :], sem)


def _combine_body(dest_ref, gates_ref, x_ref, g_ref, b_ref, ys_ref, o_ref, buf_ref, sem):
    def start(r, carry):
        for k in range(TOP_K):
            _combine_copy(ys_ref, buf_ref, sem, dest_ref[0, 0, r * TOP_K + k], k, r).start()
        return carry

    lax.fori_loop(0, DISPATCH_TILE, start, 0)

    def wait(r, carry):
        for k in range(TOP_K):
            _combine_copy(ys_ref, buf_ref, sem, 0, k, 0).wait()
        return carry

    lax.fori_loop(0, DISPATCH_TILE, wait, 0)
    acc = DN_ALPHA * x_ref[...]
    gates = gates_ref[...]
    for k in range(TOP_K):
        acc = acc + gates[:, k:k + 1] * buf_ref[k]
    o_ref[...] = _layer_norm(acc, g_ref[...], b_ref[...])


def moe_combine_ln(x, ys, dest, gates, g, b):
    t = x.shape[0]
    nt = t // DISPATCH_TILE
    full = lambda shape: pl.BlockSpec(shape, lambda i: (0,) * len(shape))
    return pl.pallas_call(
        _combine_body,
        grid=(nt,),
        in_specs=[pl.BlockSpec((1, 1, DISPATCH_TILE * TOP_K), lambda i: (i, 0, 0), memory_space=pltpu.SMEM),
                  pl.BlockSpec((DISPATCH_TILE, LANE), lambda i: (i, 0)),
                  pl.BlockSpec((DISPATCH_TILE, D_MODEL), lambda i: (i, 0)),
                  full((1, D_MODEL)), full((1, D_MODEL)),
                  pl.BlockSpec(memory_space=pl.ANY)],
        out_specs=pl.BlockSpec((DISPATCH_TILE, D_MODEL), lambda i: (i, 0)),
        out_shape=jax.ShapeDtypeStruct((t, D_MODEL), F32),
        scratch_shapes=[pltpu.VMEM((TOP_K, DISPATCH_TILE, D_MODEL), F32), pltpu.SemaphoreType.DMA(())],
        compiler_params=_cp("arbitrary"),
        name="moe_combine_ln",
    )(dest.reshape(nt, 1, DISPATCH_TILE * TOP_K), gates, x, g, b, ys)


def moe_plan(top_e, n_tokens):
    tk = n_tokens * TOP_K
    flat_e = top_e.reshape(tk)
    onehot = (flat_e[:, None] == jnp.arange(N_EXPERTS, dtype=I32)[None, :]).astype(I32)
    csum = jnp.cumsum(onehot, axis=0)
    rank = jnp.take_along_axis(csum, flat_e[:, None], axis=1)[:, 0] - 1
    counts = csum[-1]
    ntile = (counts + MOE_TILE - 1) // MOE_TILE
    tile_end = jnp.cumsum(ntile)
    tile_start = tile_end - ntile
    dest = tile_start[flat_e] * MOE_TILE + rank
    n_tiles = -(-tk // MOE_TILE) + N_EXPERTS
    tiles = jnp.arange(n_tiles, dtype=I32)
    tile_e = jnp.minimum(jnp.searchsorted(tile_end, tiles, side='right'), N_EXPERTS - 1).astype(I32)
    tile_valid = (tiles < tile_end[-1]).astype(I32)
    return dest.astype(I32), tile_e, tile_valid, n_tiles * MOE_TILE


def moe_ffn_ln(x1, rw, rb, wgu, bgu, wd, bd, g, b, expert0=0):
    t = x1.shape[0]
    e_pad, gates = router(x1, rw, rb)
    dest, tile_e, tile_valid, n_rows = moe_plan(e_pad[:, :TOP_K], t)
    xs = moe_dispatch(x1, dest, n_rows)
    ys = moe_experts(xs, tile_e + expert0, tile_valid, wgu, bgu, wd, bd)
    return moe_combine_ln(x1, ys, dest, gates, g, b)


def _rope_tables(pos):
    half = ROT_DIM // 2
    inv_freq = ROPE_THETA ** (-jnp.arange(half, dtype=F32) / half)
    ang = pos.astype(F32)[:, None] * inv_freq[None, :]
    cos, sin = jnp.cos(ang), jnp.sin(ang)
    n = pos.shape[0]
    ones = jnp.ones((n, HEAD_DIM - ROT_DIM), F32)
    cos_h = jnp.concatenate([cos, cos, ones], axis=1)
    sin_h = jnp.concatenate([-sin, sin, 0.0 * ones], axis=1)
    return jnp.concatenate([cos_h, cos_h], axis=1), jnp.concatenate([sin_h, sin_h], axis=1)


def _conv_tail(h, row0, bsz, seq):
    assert seq >= GDN_CONV - 1
    n = GDN_CONV - 1
    if bsz <= 8:
        return jnp.stack([h[row0 + (b + 1) * seq - n:row0 + (b + 1) * seq, C_GQKV:C_GQKV + GDN_CONV_DIM]
                          for b in range(bsz)], axis=0)
    blk = h[row0:row0 + bsz * seq, C_GQKV:C_GQKV + GDN_CONV_DIM].reshape(bsz, seq, GDN_CONV_DIM)
    return blk[:, seq - n:, :]


def _lane_row(v, offset):
    return jnp.zeros((1, LANE), F32).at[0, offset:offset + v.shape[0]].set(v)


def kernel(x_prompt, x_sample, cache_swa_k, cache_swa_v, state_ssm_re, state_ssm_im, state_gdn_conv, state_gdn, state_gla, w_in, w_out, attn_sinks, ssm_a_re, ssm_a_im, ssm_b_re, ssm_b_im, ssm_c_re, ssm_c_im, ssm_d, ssm_log_dt, ssm_glu_w, ssm_glu_b, gdn_conv_w, gdn_a_log, gdn_dt_bias, gdn_norm_w, gla_gate_w, gla_gate_b, gla_norm_w, ln1_g, ln1_b, ln2_g, ln2_b, router_w, router_b, moe_w_gate_up, moe_b_gate_up, moe_w_down, moe_b_down):
    bp, lp, _ = x_prompt.shape
    bs, ls, _ = x_sample.shape
    n_p, n_s = bp * lp, bs * ls
    depth = w_in.shape[0]

    w_in_r = jnp.concatenate([w_in[..., :1792], w_in[..., 1800:2568], w_in[..., 1792:1800], w_in[..., 2568:N_IN],
                              jnp.zeros(w_in.shape[:2] + (NH - N_IN,), w_in.dtype)], axis=-1).astype(BF16)
    w_out_b = w_out.astype(BF16)
    glu_w_b = ssm_glu_w.astype(BF16)
    rw_pad = jnp.pad(router_w, ((0, 0), (0, 0), (0, LANE - N_EXPERTS)))
    rb_pad = jnp.pad(router_b, ((0, 0), (0, LANE - N_EXPERTS)))
    wgu_all = moe_w_gate_up.reshape(depth * N_EXPERTS, D_MODEL, 2 * D_FF)
    bgu_all = moe_b_gate_up.reshape(depth * N_EXPERTS, 1, 2 * D_FF)
    wd_all = moe_w_down.reshape(depth * N_EXPERTS, D_FF, D_MODEL)
    bd_all = moe_b_down.reshape(depth * N_EXPERTS, 1, D_MODEL)

    cos_p, sin_p = _rope_tables(jnp.arange(lp, dtype=I32))
    cos_s, sin_s = _rope_tables(PAST_LEN + jnp.arange(ls, dtype=I32))

    x = jnp.concatenate([x_prompt.reshape(n_p, D_MODEL), x_sample.reshape(n_s, D_MODEL)], axis=0)
    zeros = lambda *s: jnp.zeros(s, F32)
    new_p = [[] for _ in range(7)]
    new_s = [[] for _ in range(7)]
    for l in range(depth):
        h = in_proj(x, w_in_r[l])
        sinks = attn_sinks[l]
        oa_p, pk, pv = swa_prompt(h, sinks, cos_p, sin_p, bp, lp)
        oa_s, sk, sv = swa_sample(h, n_p, sinks, cache_swa_k[l].reshape(bs, WINDOW, 128),
                                  cache_swa_v[l].reshape(bs, WINDOW, 128), cos_s, sin_s, bs, ls)
        bw, a_bar, cw, dsk = s5_params(ssm_a_re[l], ssm_a_im[l], ssm_b_re[l], ssm_b_im[l], ssm_c_re[l], ssm_c_im[l],
                                       ssm_d[l], ssm_log_dt[l])
        glu_b = ssm_glu_b[l].reshape(1, 2 * GROUP_WIDTH)
        ob_p, hl_p = s5_prompt(h, zeros(bp, 2 * SSM_W), bw, a_bar, cw, dsk, glu_w_b[l], glu_b, bp, lp)
        h0_s = jnp.concatenate([state_ssm_re[l].reshape(bs, SSM_W), state_ssm_im[l].reshape(bs, SSM_W)], axis=1)
        ob_s, hl_s = s5_sample(h, n_p, h0_s, bw, a_bar, cw, dsk, glu_w_b[l], glu_b, bs, ls)
        alog_t = _lane_row(gdn_a_log[l], SM_GA)
        dtb_t = _lane_row(gdn_dt_bias[l], SM_GA)
        gnw = gdn_norm_w[l].reshape(1, GDN_DV)
        oc_p, gs_p = gdn_mixer(h, 0, gdn_conv_w[l], alog_t, dtb_t, gnw, zeros(bp, GDN_CONV - 1, GDN_CONV_DIM),
                               zeros(bp, GDN_HEADS, GDN_DK, GDN_DV), bp, lp)
        oc_s, gs_s = gdn_mixer(h, n_p, gdn_conv_w[l], alog_t, dtb_t, gnw, state_gdn_conv[l], state_gdn[l], bs, ls)
        cv_p = _conv_tail(h, 0, bp, lp)
        cv_s = _conv_tail(h, n_p, bs, ls)
        lgb = gla_gate_b[l].reshape(1, 128)
        lnw = jnp.tile(gla_norm_w[l], GLA_HEADS).reshape(1, 256)
        od_p, lt_p = gla_mixer(h, 0, gla_gate_w[l], lgb, lnw, zeros(bp, 256, 128), bp, lp)
        od_s, lt_s = gla_mixer(h, n_p, gla_gate_w[l], lgb, lnw, gla_state_to_t(state_gla[l]), bs, ls)

        x1 = out_proj_ln(x, (oa_p, ob_p.reshape(n_p, GROUP_WIDTH), oc_p, od_p), (oa_s, ob_s, oc_s, od_s),
                         w_out_b[l], ln1_g[l].reshape(1, D_MODEL), ln1_b[l].reshape(1, D_MODEL))
        x = moe_ffn_ln(x1, rw_pad[l], rb_pad[l].reshape(1, LANE), wgu_all, bgu_all, wd_all, bd_all,
                       ln2_g[l].reshape(1, D_MODEL), ln2_b[l].reshape(1, D_MODEL), expert0=l * N_EXPERTS)

        st_p = (pk.reshape(bp, WINDOW, A_KV_HEADS, HEAD_DIM), pv.reshape(bp, WINDOW, A_KV_HEADS, HEAD_DIM),
                hl_p[:, :SSM_W].reshape(bp, SSM_GROUPS, SSM_STATE), hl_p[:, SSM_W:].reshape(bp, SSM_GROUPS, SSM_STATE),
                cv_p, gs_p, gla_state_from_t(lt_p))
        st_s = (sk.reshape(bs, WINDOW, A_KV_HEADS, HEAD_DIM), sv.reshape(bs, WINDOW, A_KV_HEADS, HEAD_DIM),
                hl_s[:, :SSM_W].reshape(bs, SSM_GROUPS, SSM_STATE), hl_s[:, SSM_W:].reshape(bs, SSM_GROUPS, SSM_STATE),
                cv_s, gs_s, gla_state_from_t(lt_s))
        for i in range(7):
            new_p[i].append(st_p[i])
            new_s[i].append(st_s[i])
    y_p = x[:n_p].reshape(bp, lp, D_MODEL)
    y_s = x[n_p:].reshape(bs, ls, D_MODEL)
    return (y_p, y_s) + tuple(jnp.stack(t, axis=0) for t in new_p) + tuple(jnp.stack(t, axis=0) for t in new_s)
```

```python
import functools

import numpy as np
import jax
import jax.numpy as jnp
from jax import lax
from jax.experimental import pallas as pl
from jax.experimental.pallas import tpu as pltpu

F32 = jnp.float32
BF16 = jnp.bfloat16
I32 = jnp.int32
HI = lax.Precision.HIGHEST

D_MODEL = 1024
DEPTH = 4
PAST_LEN = 8192
GROUP_WIDTH = 256
HEAD_DIM = 64
A_HEADS = 4
A_KV_HEADS = 2
WINDOW = 128
ROPE_THETA = 500000.0
ROT_DIM = 16
SSM_GC = 16
SSM_GROUPS = 16
SSM_STATE = 64
SSM_W = SSM_GROUPS * SSM_STATE
GDN_HEADS = 4
GDN_DK = 64
GDN_DV = 64
GDN_CONV = 4
GDN_CONV_DIM = 768
GDN_CHUNK = 64
GLA_HEADS = 4
GLA_DK = 32
GLA_DV = 64
GLA_RANK = 16
GLA_TAU = 16.0
GLA_CHUNK = 16
N_EXPERTS = 32
TOP_K = 4
D_FF = 1024
SWIGLU_LIMIT = 7.0
SWIGLU_ALPHA = 1.702
DN_ALPHA = (2 * DEPTH) ** 0.25
LN_EPS = 1e-5
RMS_EPS = 1e-6
N_IN = 2584

C_AQ, C_AK, C_AV, C_SU, C_GQKV, C_GZ = 0, 256, 384, 512, 768, 1536
C_LQ, C_LK, C_LV, C_LR, C_SM = 1792, 1920, 2048, 2304, 2560
NH = 2688
SM_GB, SM_GA, SM_LG = 0, 4, 8

LANE = 128
ROW_TILE = 512
MOE_TILE = 512
VMEM_LIMIT = 56 * 1024 * 1024


def _cp(*sem):
    return pltpu.CompilerParams(dimension_semantics=sem, vmem_limit_bytes=VMEM_LIMIT)


def _dot(a, b, precision=None):
    return jnp.dot(a, b, preferred_element_type=F32, precision=precision)


def _dot_nt(a, b, precision=None):
    return lax.dot_general(a, b, (((1,), (1,)), ((), ())), preferred_element_type=F32, precision=precision)


def _dot_tn(a, b, precision=None):
    return lax.dot_general(a, b, (((0,), (0,)), ((), ())), preferred_element_type=F32, precision=precision)


def _bf(x):
    return x.astype(BF16)


def _iota(shape, dim):
    return lax.broadcasted_iota(I32, shape, dim)


def _shr(idx, size):
    return lax.shift_right_logical(idx, int(size).bit_length() - 1)


def _sigmoid(x):
    return 1.0 / (1.0 + jnp.exp(-x))


def _silu(x):
    return x * _sigmoid(x)


def _softplus(x):
    return jnp.maximum(x, 0.0) + jnp.log(1.0 + jnp.exp(-jnp.abs(x)))


def _log_sigmoid(x):
    return -_softplus(-x)


def _gelu_tanh(x):
    return 0.5 * x * (1.0 + jnp.tanh(0.7978845608028654 * (x + 0.044715 * x * x * x)))


def _layer_norm(y, g, b):
    mu = jnp.mean(y, axis=-1, keepdims=True)
    yc = y - mu
    var = jnp.mean(yc * yc, axis=-1, keepdims=True)
    return yc * lax.rsqrt(var + LN_EPS) * g + b


def _in_proj_body(x_ref, w_ref, o_ref):
    xb = _bf(x_ref[...])
    for c0 in range(0, NH, 512):
        c1 = min(c0 + 512, NH)
        o_ref[:, c0:c1] = _dot(xb, w_ref[:, c0:c1])


def in_proj(x, w):
    t = x.shape[0]
    return pl.pallas_call(
        _in_proj_body,
        grid=(t // ROW_TILE,),
        in_specs=[pl.BlockSpec((ROW_TILE, D_MODEL), lambda i: (i, 0)),
                  pl.BlockSpec((D_MODEL, NH), lambda i: (0, 0))],
        out_specs=pl.BlockSpec((ROW_TILE, NH), lambda i: (i, 0)),
        out_shape=jax.ShapeDtypeStruct((t, NH), F32),
        compiler_params=_cp("arbitrary"),
        name="in_proj",
    )(x, w)


def _rope(x, cos, sin):
    w = x.shape[1]
    if w > LANE:
        cos = jnp.concatenate([cos] * (w // LANE), axis=1)
        sin = jnp.concatenate([sin] * (w // LANE), axis=1)
    lane = _iota(x.shape, 1) & (HEAD_DIM - 1)
    swapped = jnp.where(lane < ROT_DIM // 2, pltpu.roll(x, w - ROT_DIM // 2, 1), pltpu.roll(x, ROT_DIM // 2, 1))
    return x * cos + swapped * sin


def _sink_attention_multi(qkvs, mask, sinks_ref):
    pairs = [(i, hq) for i in range(len(qkvs)) for hq in range(A_HEADS)]
    hd = lambda x, j: _bf(x[:, j * HEAD_DIM:(j + 1) * HEAD_DIM])
    kv = lambda hq: hq // (A_HEADS // A_KV_HEADS)
    scores = {(i, hq): _dot_nt(hd(qkvs[i][0], hq), hd(qkvs[i][1], kv(hq))) for i, hq in pairs}
    probs = {}
    for i, hq in pairs:
        s = jnp.where(mask, scores[i, hq] * (HEAD_DIM ** -0.5), -jnp.inf)
        sink = sinks_ref[hq]
        m = jnp.maximum(jnp.max(s, axis=-1, keepdims=True), sink)
        p = jnp.exp(s - m)
        den = jnp.sum(p, axis=-1, keepdims=True) + jnp.exp(sink - m)
        probs[i, hq] = _bf(p / den)
    outs = {(i, hq): _dot(probs[i, hq], hd(qkvs[i][2], kv(hq))) for i, hq in pairs}
    return [jnp.concatenate([outs[i, hq] for hq in range(A_HEADS)], axis=1) for i in range(len(qkvs))]


def _sink_attention(q, kk, vv, mask, sinks_ref):
    return _sink_attention_multi([(q, kk, vv)], mask, sinks_ref)[0]


def _swa_prompt_body(sinks_ref, cur_ref, prev_ref, cos_ref, sin_ref, cosp_ref, sinp_ref, o_ref, ko_ref, vo_ref):
    i = pl.program_id(1)
    cur = cur_ref[...]
    q = _rope(cur[:, C_AQ:C_AQ + 256], cos_ref[...], sin_ref[...])
    k = _rope(cur[:, C_AK:C_AK + 128], cos_ref[...], sin_ref[...])
    v = cur[:, C_AV:C_AV + 128]
    prev = prev_ref[...]
    kp = _rope(prev[:, 0:128], cosp_ref[...], sinp_ref[...])
    vp = prev[:, 128:256]
    kk = jnp.concatenate([kp, k], axis=0)
    vv = jnp.concatenate([vp, v], axis=0)
    r = _iota((WINDOW, 2 * WINDOW), 0)
    j = _iota((WINDOW, 2 * WINDOW), 1)
    d = WINDOW + r - j
    mask = (d >= 0) & (d <= WINDOW) & ((j >= WINDOW) | (i > 0))
    o_ref[...] = _sink_attention(q, kk, vv, mask, sinks_ref)
    ko_ref[0] = k
    vo_ref[0] = v


def swa_prompt(h, sinks, cos_t, sin_t, bsz, seq):
    nb = seq // WINDOW
    smem = pl.BlockSpec(memory_space=pltpu.SMEM)
    tab = lambda f: pl.BlockSpec((WINDOW, LANE), f)
    return pl.pallas_call(
        _swa_prompt_body,
        grid=(bsz, nb),
        in_specs=[smem,
                  pl.BlockSpec((WINDOW, 512), lambda b, i: (b * nb + i, 0)),
                  pl.BlockSpec((WINDOW, 256), lambda b, i: (b * nb + jnp.maximum(i - 1, 0), 1)),
                  tab(lambda b, i: (i, 0)), tab(lambda b, i: (i, 0)),
                  tab(lambda b, i: (jnp.maximum(i - 1, 0), 0)), tab(lambda b, i: (jnp.maximum(i - 1, 0), 0))],
        out_specs=[pl.BlockSpec((WINDOW, 256), lambda b, i: (b * nb + i, 0)),
                   pl.BlockSpec((1, WINDOW, 128), lambda b, i: (b, 0, 0)),
                   pl.BlockSpec((1, WINDOW, 128), lambda b, i: (b, 0, 0))],
        out_shape=[jax.ShapeDtypeStruct((bsz * seq, 256), F32),
                   jax.ShapeDtypeStruct((bsz, WINDOW, 128), F32),
                   jax.ShapeDtypeStruct((bsz, WINDOW, 128), F32)],
        compiler_params=_cp("arbitrary", "arbitrary"),
        name="swa_prompt",
    )(sinks, h, h, cos_t, sin_t, cos_t, sin_t)


SWA_SB = 8


def _swa_sample_body(sinks_ref, cur_ref, kc_ref, vc_ref, cos_ref, sin_ref, o_ref, ko_ref, vo_ref, *, ls):
    cw = WINDOW
    r = _iota((ls, cw + ls), 0)
    j = _iota((ls, cw + ls), 1)
    d = cw + r - j
    mask = (d >= 0) & (d <= WINDOW)
    qkvs = []
    for b in range(SWA_SB):
        cur = cur_ref[b * ls:(b + 1) * ls, :]
        q = _rope(cur[:, C_AQ:C_AQ + 256], cos_ref[...], sin_ref[...])
        k = _rope(cur[:, C_AK:C_AK + 128], cos_ref[...], sin_ref[...])
        v = cur[:, C_AV:C_AV + 128]
        kk = jnp.concatenate([kc_ref[b], k], axis=0)
        vv = jnp.concatenate([vc_ref[b], v], axis=0)
        ko_ref[b] = kk[ls:, :]
        vo_ref[b] = vv[ls:, :]
        qkvs.append((q, kk, vv))
    for b, o in enumerate(_sink_attention_multi(qkvs, mask, sinks_ref)):
        o_ref[b * ls:(b + 1) * ls, :] = o


def swa_sample(h, row0, sinks, k_cache, v_cache, cos_t, sin_t, bsz, ls):
    rows = SWA_SB * ls
    blk0 = row0 // rows
    smem = pl.BlockSpec(memory_space=pltpu.SMEM)
    cache = pl.BlockSpec((SWA_SB, WINDOW, 128), lambda i: (i, 0, 0))
    tab = pl.BlockSpec((ls, LANE), lambda i: (0, 0))
    return pl.pallas_call(
        functools.partial(_swa_sample_body, ls=ls),
        grid=(bsz // SWA_SB,),
        in_specs=[smem, pl.BlockSpec((rows, 512), lambda i: (blk0 + i, 0)), cache, cache, tab, tab],
        out_specs=[pl.BlockSpec((rows, 256), lambda i: (i, 0)), cache, cache],
        out_shape=[jax.ShapeDtypeStruct((bsz * ls, 256), F32),
                   jax.ShapeDtypeStruct((bsz, WINDOW, 128), F32),
                   jax.ShapeDtypeStruct((bsz, WINDOW, 128), F32)],
        compiler_params=_cp("arbitrary"),
        name="swa_sample",
    )(sinks, h, k_cache, v_cache, cos_t, sin_t)


S5_NB = 2 * SSM_W // LANE


def _s5_input(u, bw_ref):
    t = _dot(_bf(u), bw_ref[...])
    return jnp.concatenate([t[:, :SSM_W], t[:, :SSM_W] + t[:, SSM_W:]], axis=1)


def _s5_output(hs, u, cw_ref, d_ref, gw_ref, gb_ref):
    hs = jnp.concatenate([hs[:, :SSM_W] + hs[:, SSM_W:], hs[:, SSM_W:]], axis=1)
    y = _dot(_bf(hs), cw_ref[...]) + d_ref[...] * u
    y = _gelu_tanh(y)
    z = _dot(_bf(y), gw_ref[...]) + gb_ref[...]
    return z[:, :GROUP_WIDTH] * _sigmoid(z[:, GROUP_WIDTH:])


def _s5_scan(s_ref, a_ref, h_init, n_steps, rows):
    nre = S5_NB // 2
    a_re = [jnp.broadcast_to(a_ref[0:1, j * LANE:(j + 1) * LANE], (rows, LANE)) for j in range(nre)]
    a_im = [jnp.broadcast_to(a_ref[1:2, j * LANE:(j + 1) * LANE], (rows, LANE)) for j in range(nre)]

    def step(t, hcar):
        out = [None] * S5_NB
        base = pl.multiple_of(t * rows, rows)
        for j in range(nre):
            hr, hi = hcar[j], hcar[j + nre]
            nr = a_re[j] * hr - a_im[j] * hi + s_ref[j, pl.ds(base, rows), :]
            ni = a_re[j] * hi + a_im[j] * hr + s_ref[j + nre, pl.ds(base, rows), :]
            s_ref[j, pl.ds(base, rows), :] = nr
            s_ref[j + nre, pl.ds(base, rows), :] = ni
            out[j], out[j + nre] = nr, ni
        return tuple(out)

    return lax.fori_loop(0, n_steps, step, tuple(h_init))


def _s5_prompt_body(*refs, nb, tl):
    u_refs = refs[:nb]
    bw_ref, a_ref, cw_ref, d_ref, gw_ref, gb_ref, h0_ref, o_ref, hl_ref, s_ref, hst_ref = refs[nb:]
    i = pl.program_id(0)

    @pl.when(i == 0)
    def _():
        hst_ref[...] = h0_ref[...]

    for b in range(nb):
        bu = _s5_input(u_refs[b][...], bw_ref)
        for j in range(S5_NB):
            s_ref[j, pl.ds(b, tl, stride=nb), :] = bu[:, j * LANE:(j + 1) * LANE]
    h_init = [hst_ref[:, j * LANE:(j + 1) * LANE] for j in range(S5_NB)]
    h_fin = _s5_scan(s_ref, a_ref, h_init, tl, nb)
    for j in range(S5_NB):
        hst_ref[:, j * LANE:(j + 1) * LANE] = h_fin[j]
    for b in range(nb):
        hs = jnp.concatenate([s_ref[j, pl.ds(b, tl, stride=nb), :] for j in range(S5_NB)], axis=1)
        o_ref[b] = _s5_output(hs, u_refs[b][...], cw_ref, d_ref, gw_ref, gb_ref)

    @pl.when(i == pl.num_programs(0) - 1)
    def _():
        hl_ref[...] = hst_ref[...]


def s5_prompt(h, h0, bw, a, cw, dsk, gw, gb, bsz, seq, tl=128):
    assert bsz == 8
    nt = seq // tl
    full = lambda shape: pl.BlockSpec(shape, lambda i: (0,) * len(shape))
    u_specs = [pl.BlockSpec((tl, GROUP_WIDTH), functools.partial(lambda i, b: (b * nt + i, C_SU // GROUP_WIDTH), b=b))
               for b in range(bsz)]
    return pl.pallas_call(
        functools.partial(_s5_prompt_body, nb=bsz, tl=tl),
        grid=(nt,),
        in_specs=u_specs + [full((GROUP_WIDTH, 2 * SSM_W)), full((2, SSM_W)), full((2 * SSM_W, GROUP_WIDTH)),
                            full((1, GROUP_WIDTH)), full((GROUP_WIDTH, 2 * GROUP_WIDTH)), full((1, 2 * GROUP_WIDTH)),
                            full((bsz, 2 * SSM_W))],
        out_specs=[pl.BlockSpec((bsz, tl, GROUP_WIDTH), lambda i: (0, i, 0)), full((bsz, 2 * SSM_W))],
        out_shape=[jax.ShapeDtypeStruct((bsz, seq, GROUP_WIDTH), F32), jax.ShapeDtypeStruct((bsz, 2 * SSM_W), F32)],
        scratch_shapes=[pltpu.VMEM((S5_NB, tl * bsz, LANE), F32), pltpu.VMEM((bsz, 2 * SSM_W), F32)],
        compiler_params=_cp("arbitrary"),
        name="s5_prompt",
    )(*([h] * bsz), bw, a, cw, dsk, gw, gb, h0)


def _s5_sample_body(u_ref, bw_ref, a_ref, cw_ref, d_ref, gw_ref, gb_ref, h0_ref, o_ref, hl_ref, s_ref, t_ref, *, bsz, ls):
    bu = _s5_input(u_ref[...], bw_ref)
    for j in range(S5_NB):
        s_ref[j] = bu[:, j * LANE:(j + 1) * LANE]
    for t in range(ls):
        for j in range(S5_NB):
            t_ref[j, pl.ds(t * bsz, bsz), :] = s_ref[j, pl.ds(t, bsz, stride=ls), :]
    h_init = [h0_ref[:, j * LANE:(j + 1) * LANE] for j in range(S5_NB)]
    h_fin = _s5_scan(t_ref, a_ref, h_init, ls, bsz)
    for j in range(S5_NB):
        hl_ref[:, j * LANE:(j + 1) * LANE] = h_fin[j]
    for t in range(ls):
        for j in range(S5_NB):
            s_ref[j, pl.ds(t, bsz, stride=ls), :] = t_ref[j, pl.ds(t * bsz, bsz), :]
    hs = jnp.concatenate([s_ref[j] for j in range(S5_NB)], axis=1)
    o_ref[...] = _s5_output(hs, u_ref[...], cw_ref, d_ref, gw_ref, gb_ref)


def s5_sample(h, row0, h0, bw, a, cw, dsk, gw, gb, bsz, ls):
    rows = bsz * ls
    full = lambda shape: pl.BlockSpec(shape, lambda i: (0,) * len(shape))
    return pl.pallas_call(
        functools.partial(_s5_sample_body, bsz=bsz, ls=ls),
        grid=(1,),
        in_specs=[pl.BlockSpec((rows, GROUP_WIDTH), lambda i: (row0 // rows, C_SU // GROUP_WIDTH)),
                  full((GROUP_WIDTH, 2 * SSM_W)), full((2, SSM_W)), full((2 * SSM_W, GROUP_WIDTH)),
                  full((1, GROUP_WIDTH)), full((GROUP_WIDTH, 2 * GROUP_WIDTH)), full((1, 2 * GROUP_WIDTH)),
                  full((bsz, 2 * SSM_W))],
        out_specs=[full((rows, GROUP_WIDTH)), full((bsz, 2 * SSM_W))],
        out_shape=[jax.ShapeDtypeStruct((rows, GROUP_WIDTH), F32), jax.ShapeDtypeStruct((bsz, 2 * SSM_W), F32)],
        scratch_shapes=[pltpu.VMEM((S5_NB, rows, LANE), F32), pltpu.VMEM((S5_NB, rows, LANE), F32)],
        compiler_params=_cp("arbitrary"),
        name="s5_sample",
    )(h, bw, a, cw, dsk, gw, gb, h0)


def s5_params(a_re, a_im, b_re, b_im, c_re, c_im, d_skip, log_dt):
    lam = lax.complex(a_re, a_im)
    delta = jnp.exp(log_dt)[:, None]
    a_bar = jnp.exp(lam * delta)
    b_bar = ((a_bar - 1.0) / lam)[..., None] * lax.complex(b_re, b_im)
    eye = jnp.eye(SSM_GROUPS, dtype=F32)
    bw_re = jnp.einsum('gpc,gh->gchp', b_bar.real, eye).reshape(GROUP_WIDTH, SSM_W)
    bw_im = jnp.einsum('gpc,gh->gchp', b_bar.imag, eye).reshape(GROUP_WIDTH, SSM_W)
    bw = jnp.concatenate([bw_re, bw_im - bw_re], axis=1).astype(BF16)
    cw_re = jnp.einsum('gcp,gh->gphc', c_re, eye).reshape(SSM_W, GROUP_WIDTH)
    cw_im = jnp.einsum('gcp,gh->gphc', c_im, eye).reshape(SSM_W, GROUP_WIDTH)
    cw = jnp.concatenate([cw_re, -(cw_re + cw_im)], axis=0).astype(BF16)
    a = jnp.stack([a_bar.real.reshape(SSM_W), a_bar.imag.reshape(SSM_W)], axis=0)
    return bw, a, cw, d_skip.reshape(1, GROUP_WIDTH)


GDN_RT = 256
GDN_ST = 64
GDN_SG = 8


def _split(x):
    hi = _bf(x)
    return hi, _bf(x - hi.astype(F32))


def _dot3(a, b):
    return _dot(a[0], b[0]) + (_dot(a[0], b[1]) + _dot(a[1], b[0]))


def _gdn_prep_body(*refs, c, sample, tiles_per_seq):
    if sample:
        x_ref, cs_ref, z_ref, sm_ref, cw_ref, alog_ref, dtb_ref = refs[:7]
        outs = refs[7:14]
        xs_ref, cb_ref = refs[14:]
    else:
        x_ref, prev_ref, cs_ref, z_ref, sm_ref, cw_ref, alog_ref, dtb_ref = refs[:8]
        outs = refs[8:15]
        (xs_ref,) = refs[15:]
    u_ref, w_ref, qk_ref, qg_ref, kd_ref, eg_ref, zs_ref = outs
    rt = GDN_RT
    x = x_ref[...]
    xs_ref[8:8 + rt, :] = x
    if sample:
        xs_ref[0:8, :] = jnp.zeros((8, GDN_CONV_DIM), F32)
        cb_ref[0:rt, :] = cs_ref[...]
        cb_ref[rt:rt + 8, :] = jnp.zeros((8, GDN_CONV_DIM), F32)
        pos = _iota((rt, GDN_CONV_DIM), 0) & (c - 1)
        shifted = lambda i: jnp.where(pos >= i, xs_ref[8 - i:8 - i + rt, :], cb_ref[8 - i:8 - i + rt, :])
    else:
        first = (pl.program_id(0) % tiles_per_seq) == 0
        xs_ref[0:8, :] = jnp.where(first, cs_ref[...], prev_ref[...])
        shifted = lambda i: xs_ref[8 - i:8 - i + rt, :]
    conv = shifted(3) * cw_ref[0:1, :]
    conv = conv + shifted(2) * cw_ref[1:2, :]
    conv = conv + shifted(1) * cw_ref[2:3, :]
    conv = conv + x * cw_ref[3:4, :]
    qkv = _silu(conv)
    zs_ref[...] = _silu(z_ref[...])

    sm = sm_ref[...]
    beta_all = _sigmoid(sm)
    g_all = -jnp.exp(alog_ref[...]) * _softplus(sm + dtb_ref[...])
    st = GDN_ST
    ri = _iota((st, st), 0)
    ci = _iota((st, st), 1)
    same = _shr(ri, c) == _shr(ci, c)
    low = (ci <= ri) & same
    tri = low.astype(F32)
    last = (ci == (ri | (c - 1))).astype(F32)
    eye = (ri == ci).astype(F32)
    lane = _iota((st, LANE), 1)
    nsub = rt // st
    probs = [(t, hh) for t in range(nsub) for hh in range(GDN_HEADS)]
    gc_alls, gl_alls = [], []
    for t in range(nsub):
        rows = slice(t * st, (t + 1) * st)
        gc_alls.append(_dot(tri, g_all[rows, :], HI))
    for t in range(nsub):
        gl_alls.append(_dot(last, gc_alls[t], HI))
        eg_ref[t * st:(t + 1) * st, :] = jnp.exp(gl_alls[t])
    qs, ks, rhss, decays, a_stricts, invs = {}, {}, {}, {}, {}, {}
    for t, hh in probs:
        rows = slice(t * st, (t + 1) * st)
        q = qkv[rows, hh * GDN_DK:(hh + 1) * GDN_DK]
        k = qkv[rows, 256 + hh * GDN_DK:256 + (hh + 1) * GDN_DK]
        v = qkv[rows, 512 + hh * GDN_DV:512 + (hh + 1) * GDN_DV]
        q = q * lax.rsqrt(jnp.sum(q * q, axis=-1, keepdims=True) + 1e-6) * (GDN_DK ** -0.5)
        k = k * lax.rsqrt(jnp.sum(k * k, axis=-1, keepdims=True) + 1e-6)
        beta = beta_all[rows, SM_GB + hh:SM_GB + hh + 1]
        gc = gc_alls[t][:, SM_GA + hh:SM_GA + hh + 1]
        gl = gl_alls[t][:, SM_GA + hh:SM_GA + hh + 1]
        sel = (lane == SM_GA + hh).astype(F32)
        gc_row = _dot_nt(sel, gc_alls[t], HI)
        decays[t, hh] = jnp.exp(jnp.where(low, gc - gc_row, -jnp.inf))
        kb = k * beta
        eg = jnp.exp(gc)
        qs[t, hh], ks[t, hh] = q, k
        rhss[t, hh] = _split(jnp.concatenate([v * beta, kb * eg], axis=1))
        a_stricts[t, hh] = jnp.where(ci < ri, _dot_nt(_bf(kb), _bf(k)) * decays[t, hh], 0.0)
        qg_ref[rows, hh * GDN_DK:(hh + 1) * GDN_DK] = q * eg
        kd_ref[rows, hh * GDN_DK:(hh + 1) * GDN_DK] = k * jnp.exp(gl - gc)
    for t, hh in probs:
        qk_ref[t * st:(t + 1) * st, hh * st:(hh + 1) * st] = _dot_nt(_bf(qs[t, hh]), _bf(ks[t, hh])) * decays[t, hh]
    for p in probs:
        invs[p] = eye - jnp.where(_shr(ri, 2) == _shr(ci, 2), a_stricts[p], 0.0)
    s = 2
    while s < c:
        pair = (_shr(ri, 2 * s) == _shr(ci, 2 * s)) & (_shr(ri, s) != _shr(ci, s))
        inv_s = {p: _split(invs[p]) for p in probs}
        mid = {p: _dot3(_split(jnp.where(pair, a_stricts[p], 0.0)), inv_s[p]) for p in probs}
        for p in probs:
            invs[p] = invs[p] - _dot3(inv_s[p], _split(mid[p]))
        s *= 2
    for t, hh in probs:
        sol = _dot3(_split(invs[t, hh]), rhss[t, hh])
        u_ref[t * st:(t + 1) * st, hh * GDN_DV:(hh + 1) * GDN_DV] = sol[:, :GDN_DV]
        w_ref[t * st:(t + 1) * st, hh * GDN_DV:(hh + 1) * GDN_DV] = sol[:, GDN_DV:]


def _gdn_scan_body(u_ref, w_ref, qk_ref, qg_ref, kd_ref, eg_ref, zs_ref, nw_ref, s0_ref, o_ref, so_ref, st_ref, *, c):
    n = pl.program_id(1)

    @pl.when(n == 0)
    def _():
        st_ref[...] = s0_ref[...]

    probs = [(s, hh) for s in range(GDN_SG) for hh in range(GDN_HEADS)]
    hsl = lambda hh: slice(hh * GDN_DV, (hh + 1) * GDN_DV)
    state_b = {(s, hh): _bf(st_ref[s, hh]) for s, hh in probs}
    v_b = {}
    for s, hh in probs:
        v_b[s, hh] = _bf(u_ref[s, :, hsl(hh)] - _dot(_bf(w_ref[s, :, hsl(hh)]), state_b[s, hh]))
    o_inter = {(s, hh): _dot(_bf(qg_ref[s, :, hsl(hh)]), state_b[s, hh]) for s, hh in probs}
    for s, hh in probs:
        col0 = (s * c) % GDN_ST
        qk = qk_ref[s, :, hh * GDN_ST + col0:hh * GDN_ST + col0 + c]
        o = o_inter[s, hh] + _dot(_bf(qk), v_b[s, hh])
        o = o * lax.rsqrt(jnp.mean(o * o, axis=-1, keepdims=True) + RMS_EPS) * nw_ref[...]
        o_ref[s, :, hsl(hh)] = o * zs_ref[s, :, hsl(hh)]
    for s, hh in probs:
        eg_last = eg_ref[s, c - 1:c, SM_GA + hh:SM_GA + hh + 1]
        st_ref[s, hh] = st_ref[s, hh] * eg_last + _dot_tn(_bf(kd_ref[s, :, hsl(hh)]), v_b[s, hh])

    @pl.when(n == pl.num_programs(1) - 1)
    def _():
        so_ref[...] = st_ref[...]


def gdn_mixer(h, row0, conv_w, alog_t, dtb_t, norm_w, conv_state, s0, bsz, seq):
    c = min(GDN_CHUNK, seq)
    sample = seq == c
    rows = bsz * seq
    rt = GDN_RT
    nt = rows // rt
    blk0 = row0 // rt
    tps = max(seq // rt, 1)
    cs_rows = jnp.pad(conv_state, ((0, 0), (8 - (GDN_CONV - 1), 0), (0, 0))).reshape(bsz * 8, GDN_CONV_DIM)
    row = lambda w, col: pl.BlockSpec((rt, w), lambda i: (blk0 + i, col))
    full = lambda shape: pl.BlockSpec(shape, lambda i: (0,) * len(shape))
    if sample:
        aux_specs = [pl.BlockSpec((rt, GDN_CONV_DIM), lambda i: (i, 0))]
        aux = [cs_rows]
        scratch = [pltpu.VMEM((8 + rt, GDN_CONV_DIM), F32), pltpu.VMEM((8 + rt, GDN_CONV_DIM), F32)]
    else:
        aux_specs = [pl.BlockSpec((8, GDN_CONV_DIM), lambda i: (jnp.maximum((row0 + i * rt) // 8 - 1, 0), C_GQKV // GDN_CONV_DIM)),
                     pl.BlockSpec((8, GDN_CONV_DIM), lambda i: (i // tps, 0))]
        aux = [h, cs_rows]
        scratch = [pltpu.VMEM((8 + rt, GDN_CONV_DIM), F32)]
    out_w = [GROUP_WIDTH] * 5 + [LANE, GROUP_WIDTH]
    prep = pl.pallas_call(
        functools.partial(_gdn_prep_body, c=c, sample=sample, tiles_per_seq=tps),
        grid=(nt,),
        in_specs=[row(GDN_CONV_DIM, C_GQKV // GDN_CONV_DIM)] + aux_specs
                 + [row(GROUP_WIDTH, C_GZ // GROUP_WIDTH), row(LANE, C_SM // LANE),
                    full((GDN_CONV, GDN_CONV_DIM)), full((1, LANE)), full((1, LANE))],
        out_specs=[pl.BlockSpec((rt, w), lambda i: (i, 0)) for w in out_w],
        out_shape=[jax.ShapeDtypeStruct((rows, w), F32) for w in out_w],
        scratch_shapes=scratch,
        compiler_params=_cp("arbitrary"),
        name="gdn_prep",
    )(h, *aux, h, h, conv_w, alog_t, dtb_t)
    prep = [a.reshape(bsz, seq, a.shape[-1]) for a in prep]
    ng, nc = bsz // GDN_SG, seq // c
    seq_blk = lambda w: pl.BlockSpec((GDN_SG, c, w), lambda g, n: (g, n, 0))
    st_spec = pl.BlockSpec((GDN_SG, GDN_HEADS, GDN_DK, GDN_DV), lambda g, n: (g, 0, 0, 0))
    o, s_fin = pl.pallas_call(
        functools.partial(_gdn_scan_body, c=c),
        grid=(ng, nc),
        in_specs=[seq_blk(w) for w in out_w] + [pl.BlockSpec((1, GDN_DV), lambda g, n: (0, 0)), st_spec],
        out_specs=[seq_blk(GROUP_WIDTH), st_spec],
        out_shape=[jax.ShapeDtypeStruct((bsz, seq, GROUP_WIDTH), F32),
                   jax.ShapeDtypeStruct((bsz, GDN_HEADS, GDN_DK, GDN_DV), F32)],
        scratch_shapes=[pltpu.VMEM((GDN_SG, GDN_HEADS, GDN_DK, GDN_DV), F32)],
        compiler_params=_cp("arbitrary", "arbitrary"),
        name="gdn_scan",
    )(*prep, norm_w, s0)
    return o.reshape(rows, GROUP_WIDTH), s_fin


GLA_RT = 256


def _dot_exact(x, w):
    hi = _bf(x)
    r1 = x - hi.astype(F32)
    mid = _bf(r1)
    lo = _bf(r1 - mid.astype(F32))
    return _dot(hi, w) + (_dot(mid, w) + _dot(lo, w))


def _gla_body(qk_ref, vr_ref, sm_ref, gw_ref, gb_ref, nw_ref, s0_ref, o_ref, so_ref,
              kb_ref, bb_ref, vb_ref, st_ref, *, c, seq):
    tr = GLA_RT
    n = pl.program_id(1)
    pad = GLA_CHUNK
    carried = seq >= tr

    @pl.when(n == 0)
    def _():
        kb_ref[0:pad, :] = jnp.zeros((pad, 128), F32)
        bb_ref[0:pad, :] = jnp.zeros((pad, 128), F32)
        vb_ref[0:pad, :] = jnp.zeros((pad, 256), F32)
        if carried:
            st_ref[...] = s0_ref[0]

    q = qk_ref[:, 0:128] * (GLA_DK ** -0.5)
    k = qk_ref[:, 128:256]
    v = vr_ref[:, 0:256]
    lg = sm_ref[:, SM_LG:SM_LG + GLA_RANK]
    log_a = _log_sigmoid(_dot(_bf(lg), _bf(gw_ref[...])) + gb_ref[...]) / GLA_TAU
    ri = _iota((tr, tr), 0)
    ci = _iota((tr, tr), 1)
    tri = _bf(((ci <= ri) & (_shr(ri, c) == _shr(ci, c))).astype(F32))
    bc = _dot_nt_exact_lhs(tri, log_a)
    kb_ref[pad:pad + tr, :] = k
    bb_ref[pad:pad + tr, :] = bc
    vb_ref[pad:pad + tr, :] = v
    ones_kv = _bf((_shr(_iota((128, 256), 0), GLA_DK) == _shr(_iota((128, 256), 1), GLA_DV)).astype(F32))
    pos = _iota((tr, 128), 0) & (c - 1)
    ps = []
    for dl in range(c):
        ks = kb_ref[pad - dl:pad - dl + tr, :]
        bs = bb_ref[pad - dl:pad - dl + tr, :]
        ps.append(jnp.where(pos >= dl, q * ks * jnp.exp(bc - bs), 0.0))
    att = _dot_exact(jnp.concatenate(ps, axis=0), ones_kv)
    o_intra = jnp.zeros((tr, 256), F32)
    for dl in range(c):
        vs = vb_ref[pad - dl:pad - dl + tr, :]
        o_intra = o_intra + _bf(att[dl * tr:(dl + 1) * tr, :]).astype(F32) * _bf(vs).astype(F32)
    bd = (_shr(_iota((256, 128), 0), GLA_DV) == _shr(_iota((256, 128), 1), GLA_DK)).astype(F32)
    nch = tr // c
    sls = [slice(ch * c, (ch + 1) * c) for ch in range(nch)]
    b_last = [bc[sl, :][c - 1:c, :] for sl in sls]
    incs = [bd * _dot_tn(_bf(v[sl, :]), _bf(k[sl, :] * jnp.exp(bl - bc[sl, :]))) for sl, bl in zip(sls, b_last)]
    outs = []
    st = st_ref[...] if carried else None
    for ch, sl in enumerate(sls):
        if not carried:
            st = s0_ref[(ch * c) // seq]
        outs.append(_dot_nt(_bf(q[sl, :] * jnp.exp(bc[sl, :])), _bf(st)) + o_intra[sl, :])
        st = st * jnp.exp(b_last[ch]) + incs[ch]
        if not carried:
            so_ref[(ch * c) // seq] = st
    if carried:
        st_ref[...] = st
        so_ref[0] = st
    o = jnp.concatenate(outs, axis=0)
    ones_vv = _bf((_shr(_iota((256, 256), 0), GLA_DV) == _shr(_iota((256, 256), 1), GLA_DV)).astype(F32))
    ms = _dot_exact(o * o, ones_vv) * (1.0 / GLA_DV)
    o_ref[...] = o * lax.rsqrt(ms + RMS_EPS) * nw_ref[...] * _silu(vr_ref[:, 256:512])


def _dot_nt_exact_lhs(w, x):
    hi = _bf(x)
    r1 = x - hi.astype(F32)
    mid = _bf(r1)
    lo = _bf(r1 - mid.astype(F32))
    return _dot(w, hi) + (_dot(w, mid) + _dot(w, lo))


def gla_mixer(h, row0, gate_w, gate_b, norm_w4, s0t, bsz, seq):
    c = min(GLA_CHUNK, seq)
    tr = GLA_RT
    spt = max(tr // seq, 1)
    nt = max(seq // tr, 1)
    ng = bsz // spt
    blk0 = row0 // tr
    row = lambda w, col: pl.BlockSpec((tr, w), lambda g, n: (blk0 + g * nt + n, col))
    full = lambda shape: pl.BlockSpec(shape, lambda g, n: (0,) * len(shape))
    st_spec = pl.BlockSpec((spt, 256, 128), lambda g, n: (g, 0, 0))
    return pl.pallas_call(
        functools.partial(_gla_body, c=c, seq=seq),
        grid=(ng, nt),
        in_specs=[row(256, C_LQ // 256), row(512, C_LV // 512), row(LANE, C_SM // LANE),
                  full((GLA_RANK, 128)), full((1, 128)), full((1, 256)), st_spec],
        out_specs=[pl.BlockSpec((tr, 256), lambda g, n: (g * nt + n, 0)), st_spec],
        out_shape=[jax.ShapeDtypeStruct((bsz * seq, 256), F32), jax.ShapeDtypeStruct((bsz, 256, 128), F32)],
        scratch_shapes=[pltpu.VMEM((GLA_CHUNK + tr, 128), F32), pltpu.VMEM((GLA_CHUNK + tr, 128), F32),
                        pltpu.VMEM((GLA_CHUNK + tr, 256), F32), pltpu.VMEM((256, 128), F32)],
        compiler_params=_cp("arbitrary", "arbitrary"),
        name="gla",
    )(h, h, h, gate_w, gate_b, norm_w4, s0t)


def gla_state_to_t(s):
    eye = jnp.eye(GLA_HEADS, dtype=s.dtype)
    return jnp.einsum('bhde,hg->bhegd', s, eye).reshape(s.shape[0], 256, 128)


def gla_state_from_t(st):
    b = st.shape[0]
    t5 = st.reshape(b, GLA_HEADS, GLA_DV, GLA_HEADS, GLA_DK)
    diag = jnp.stack([t5[:, hh, :, hh, :] for hh in range(GLA_HEADS)], axis=1)
    return jnp.swapaxes(diag, 2, 3)


def _out_proj_body(x_ref, ap, bp, cp, dp, a_s, b_s, c_s, d_s, w_ref, g_ref, b_ref, o_ref, *, n_prompt_tiles):
    i = pl.program_id(0)

    def run(refs):
        acc = DN_ALPHA * x_ref[...]
        for m, r in enumerate(refs):
            acc = acc + _dot(_bf(r[...]), w_ref[m * GROUP_WIDTH:(m + 1) * GROUP_WIDTH, :])
        o_ref[...] = _layer_norm(acc, g_ref[...], b_ref[...])

    @pl.when(i < n_prompt_tiles)
    def _():
        run((ap, bp, cp, dp))

    @pl.when(i >= n_prompt_tiles)
    def _():
        run((a_s, b_s, c_s, d_s))


def out_proj_ln(x, mix_p, mix_s, w, g, b):
    t = x.shape[0]
    npt = mix_p[0].shape[0] // ROW_TILE
    nst = mix_s[0].shape[0] // ROW_TILE
    p_spec = pl.BlockSpec((ROW_TILE, GROUP_WIDTH), lambda i: (jnp.minimum(i, npt - 1), 0))
    s_spec = pl.BlockSpec((ROW_TILE, GROUP_WIDTH), lambda i: (jnp.clip(i - npt, 0, nst - 1), 0))
    full = lambda shape: pl.BlockSpec(shape, lambda i: (0,) * len(shape))
    return pl.pallas_call(
        functools.partial(_out_proj_body, n_prompt_tiles=npt),
        grid=(t // ROW_TILE,),
        in_specs=[pl.BlockSpec((ROW_TILE, D_MODEL), lambda i: (i, 0))] + [p_spec] * 4 + [s_spec] * 4
                 + [full((D_MODEL, D_MODEL)), full((1, D_MODEL)), full((1, D_MODEL))],
        out_specs=pl.BlockSpec((ROW_TILE, D_MODEL), lambda i: (i, 0)),
        out_shape=jax.ShapeDtypeStruct((t, D_MODEL), F32),
        compiler_params=_cp("arbitrary"),
        name="out_proj_ln",
    )(x, *mix_p, *mix_s, w, g, b)


def _router_body(x_ref, w_ref, b_ref, e_ref, g_ref):
    logits = _dot(_bf(x_ref[...]), _bf(w_ref[...])) + b_ref[...]
    lane = _iota(logits.shape, 1)
    vals = jnp.where(lane < N_EXPERTS, logits, -jnp.inf)
    eo = jnp.zeros(logits.shape, I32)
    top = []
    for k in range(TOP_K):
        m = jnp.max(vals, axis=-1, keepdims=True)
        idx = jnp.min(jnp.where(vals == m, lane.astype(F32), float(LANE)), axis=-1, keepdims=True).astype(I32)
        eo = jnp.where(lane == k, idx, eo)
        top.append(m)
        vals = jnp.where(lane == idx, -jnp.inf, vals)
    ex = [jnp.exp(v - top[0]) for v in top]
    den = ex[0] + ex[1] + ex[2] + ex[3]
    go = jnp.zeros(logits.shape, F32)
    for k in range(TOP_K):
        go = jnp.where(lane == k, ex[k] / den, go)
    e_ref[...] = eo
    g_ref[...] = go


def router(x, w, b):
    t = x.shape[0]
    return pl.pallas_call(
        _router_body,
        grid=(t // ROW_TILE,),
        in_specs=[pl.BlockSpec((ROW_TILE, D_MODEL), lambda i: (i, 0)),
                  pl.BlockSpec((D_MODEL, LANE), lambda i: (0, 0)), pl.BlockSpec((1, LANE), lambda i: (0, 0))],
        out_specs=[pl.BlockSpec((ROW_TILE, LANE), lambda i: (i, 0))] * 2,
        out_shape=[jax.ShapeDtypeStruct((t, LANE), I32), jax.ShapeDtypeStruct((t, LANE), F32)],
        compiler_params=_cp("arbitrary"),
        name="router",
    )(x, w, b)


DISPATCH_TILE = 256


def _dispatch_copy(x_ref, xs_ref, sem, r, d):
    return pltpu.make_async_copy(x_ref.at[pl.ds(r, 1), :], xs_ref.at[pl.ds(d, 1), :], sem)


def _dispatch_body(dest_ref, x_ref, xs_in_ref, xs_ref, sem):
    del xs_in_ref

    def start(r, carry):
        for k in range(TOP_K):
            _dispatch_copy(x_ref, xs_ref, sem, r, dest_ref[0, 0, r * TOP_K + k]).start(priority=k % 2)
        return carry

    lax.fori_loop(0, DISPATCH_TILE, start, 0)

    def wait(r, carry):
        for k in range(TOP_K):
            _dispatch_copy(x_ref, xs_ref, sem, 0, 0).wait()
        return carry

    lax.fori_loop(0, DISPATCH_TILE, wait, 0)


def moe_dispatch(x, dest, n_rows):
    t = x.shape[0]
    nt = t // DISPATCH_TILE
    zeros = jnp.zeros((n_rows, D_MODEL), F32)
    return pl.pallas_call(
        _dispatch_body,
        grid=(nt,),
        in_specs=[pl.BlockSpec((1, 1, DISPATCH_TILE * TOP_K), lambda i: (i, 0, 0), memory_space=pltpu.SMEM),
                  pl.BlockSpec((DISPATCH_TILE, D_MODEL), lambda i: (i, 0)),
                  pl.BlockSpec(memory_space=pl.ANY)],
        out_specs=pl.BlockSpec(memory_space=pl.ANY),
        out_shape=jax.ShapeDtypeStruct((n_rows, D_MODEL), F32),
        scratch_shapes=[pltpu.SemaphoreType.DMA(())],
        input_output_aliases={2: 0},
        compiler_params=_cp("arbitrary"),
        name="moe_dispatch",
    )(dest.reshape(nt, 1, DISPATCH_TILE * TOP_K), x, zeros)


EXPERT_SUB = 256


def _expert_body(te_ref, tv_ref, xs_ref, wgu_ref, bgu_ref, wd_ref, bd_ref, ys_ref, wgu_s, wd2_s, tmp_s):
    i = pl.program_id(0)
    valid = tv_ref[i] != 0
    changed = (i == 0) | (te_ref[i] != te_ref[jnp.maximum(i - 1, 0)])

    @pl.when(i == 0)
    def _():
        tmp_s[...] = jnp.zeros(tmp_s.shape, F32)

    @pl.when(valid & changed)
    def _():
        for c0 in range(0, 2 * D_FF, 512):
            wgu_s[:, c0:c0 + 512] = _bf(wgu_ref[0, :, c0:c0 + 512])
        for j in range(D_MODEL // LANE):
            tmp_s[pl.ds(0, D_FF, stride=2), :] = wd_ref[0, :, j * LANE:(j + 1) * LANE]
            wd2_s[:, j * LANE:(j + 1) * LANE] = _bf(tmp_s[...])

    @pl.when(valid)
    def _():
        for r0 in range(0, MOE_TILE, EXPERT_SUB):
            rows = slice(r0, r0 + EXPERT_SUB)
            gu = _dot(_bf(xs_ref[rows, :]), wgu_s[...]) + bgu_ref[0]
            nxt = pltpu.roll(gu, 2 * D_FF - 1, 1)
            gate = jnp.minimum(gu, SWIGLU_LIMIT)
            up = jnp.clip(nxt, -SWIGLU_LIMIT, SWIGLU_LIMIT)
            act = gate * _sigmoid(gate * SWIGLU_ALPHA) * (up + 1.0)
            even = (_iota(act.shape, 1) & 1) == 0
            act = jnp.where(even, act, 0.0)
            ys_ref[rows, :] = _dot(_bf(act), wd2_s[...]) + bd_ref[0]

    @pl.when(jnp.logical_not(valid))
    def _():
        ys_ref[...] = jnp.zeros(ys_ref.shape, F32)


def moe_experts(xs, tile_e, tile_valid, wgu, bgu, wd, bd):
    n_rows = xs.shape[0]
    nt = n_rows // MOE_TILE
    grid_spec = pltpu.PrefetchScalarGridSpec(
        num_scalar_prefetch=2,
        grid=(nt,),
        in_specs=[pl.BlockSpec((MOE_TILE, D_MODEL), lambda i, te, tv: (i, 0)),
                  pl.BlockSpec((1, D_MODEL, 2 * D_FF), lambda i, te, tv: (te[i], 0, 0)),
                  pl.BlockSpec((1, 1, 2 * D_FF), lambda i, te, tv: (te[i], 0, 0)),
                  pl.BlockSpec((1, D_FF, D_MODEL), lambda i, te, tv: (te[i], 0, 0)),
                  pl.BlockSpec((1, 1, D_MODEL), lambda i, te, tv: (te[i], 0, 0))],
        out_specs=pl.BlockSpec((MOE_TILE, D_MODEL), lambda i, te, tv: (i, 0)),
        scratch_shapes=[pltpu.VMEM((D_MODEL, 2 * D_FF), BF16), pltpu.VMEM((2 * D_FF, D_MODEL), BF16),
                        pltpu.VMEM((2 * D_FF, LANE), F32)],
    )
    return pl.pallas_call(
        _expert_body,
        grid_spec=grid_spec,
        out_shape=jax.ShapeDtypeStruct((n_rows, D_MODEL), F32),
        compiler_params=_cp("arbitrary"),
        name="moe_experts",
    )(tile_e, tile_valid, xs, wgu, bgu, wd, bd)


def _combine_copy(ys_ref, buf_ref, sem, d, k, r):
    return pltpu.make_async_copy(ys_ref.at[pl.ds(d, 1), :], buf_ref.at[k, pl.ds(r, 1), :], sem)


def _combine_body(dest_ref, gates_ref, x_ref, g_ref, b_ref, ys_ref, o_ref, buf_ref, sem):
    def start(r, carry):
        for k in range(TOP_K):
            _combine_copy(ys_ref, buf_ref, sem, dest_ref[0, 0, r * TOP_K + k], k, r).start(priority=k % 2)
        return carry

    lax.fori_loop(0, DISPATCH_TILE, start, 0)

    def wait(r, carry):
        for k in range(TOP_K):
            _combine_copy(ys_ref, buf_ref, sem, 0, k, 0).wait()
        return carry

    lax.fori_loop(0, DISPATCH_TILE, wait, 0)
    acc = DN_ALPHA * x_ref[...]
    gates = gates_ref[...]
    for k in range(TOP_K):
        acc = acc + gates[:, k:k + 1] * buf_ref[k]
    o_ref[...] = _layer_norm(acc, g_ref[...], b_ref[...])


def moe_combine_ln(x, ys, dest, gates, g, b):
    t = x.shape[0]
    nt = t // DISPATCH_TILE
    full = lambda shape: pl.BlockSpec(shape, lambda i: (0,) * len(shape))
    return pl.pallas_call(
        _combine_body,
        grid=(nt,),
        in_specs=[pl.BlockSpec((1, 1, DISPATCH_TILE * TOP_K), lambda i: (i, 0, 0), memory_space=pltpu.SMEM),
                  pl.BlockSpec((DISPATCH_TILE, LANE), lambda i: (i, 0)),
                  pl.BlockSpec((DISPATCH_TILE, D_MODEL), lambda i: (i, 0)),
                  full((1, D_MODEL)), full((1, D_MODEL)),
                  pl.BlockSpec(memory_space=pl.ANY)],
        out_specs=pl.BlockSpec((DISPATCH_TILE, D_MODEL), lambda i: (i, 0)),
        out_shape=jax.ShapeDtypeStruct((t, D_MODEL), F32),
        scratch_shapes=[pltpu.VMEM((TOP_K, DISPATCH_TILE, D_MODEL), F32), pltpu.SemaphoreType.DMA(())],
        compiler_params=_cp("arbitrary"),
        name="moe_combine_ln",
    )(dest.reshape(nt, 1, DISPATCH_TILE * TOP_K), gates, x, g, b, ys)


def moe_plan(top_e, n_tokens):
    tk = n_tokens * TOP_K
    flat_e = top_e.reshape(tk)
    onehot = (flat_e[:, None] == jnp.arange(N_EXPERTS, dtype=I32)[None, :]).astype(I32)
    csum = jnp.cumsum(onehot, axis=0)
    rank = jnp.take_along_axis(csum, flat_e[:, None], axis=1)[:, 0] - 1
    counts = csum[-1]
    ntile = (counts + MOE_TILE - 1) // MOE_TILE
    tile_end = jnp.cumsum(ntile)
    tile_start = tile_end - ntile
    dest = tile_start[flat_e] * MOE_TILE + rank
    n_tiles = -(-tk // MOE_TILE) + N_EXPERTS
    tiles = jnp.arange(n_tiles, dtype=I32)
    tile_e = jnp.minimum(jnp.searchsorted(tile_end, tiles, side='right'), N_EXPERTS - 1).astype(I32)
    tile_valid = (tiles < tile_end[-1]).astype(I32)
    return dest.astype(I32), tile_e, tile_valid, n_tiles * MOE_TILE


def moe_ffn_ln(x1, rw, rb, wgu, bgu, wd, bd, g, b, expert0=0):
    t = x1.shape[0]
    e_pad, gates = router(x1, rw, rb)
    dest, tile_e, tile_valid, n_rows = moe_plan(e_pad[:, :TOP_K], t)
    xs = moe_dispatch(x1, dest, n_rows)
    ys = moe_experts(xs, tile_e + expert0, tile_valid, wgu, bgu, wd, bd)
    return moe_combine_ln(x1, ys, dest, gates, g, b)


def _rope_tables(pos):
    half = ROT_DIM // 2
    inv_freq = ROPE_THETA ** (-jnp.arange(half, dtype=F32) / half)
    ang = pos.astype(F32)[:, None] * inv_freq[None, :]
    cos, sin = jnp.cos(ang), jnp.sin(ang)
    n = pos.shape[0]
    ones = jnp.ones((n, HEAD_DIM - ROT_DIM), F32)
    cos_h = jnp.concatenate([cos, cos, ones], axis=1)
    sin_h = jnp.concatenate([-sin, sin, 0.0 * ones], axis=1)
    return jnp.concatenate([cos_h, cos_h], axis=1), jnp.concatenate([sin_h, sin_h], axis=1)


def _conv_tail(h, row0, bsz, seq):
    assert seq >= GDN_CONV - 1
    n = GDN_CONV - 1
    if bsz <= 8:
        return jnp.stack([h[row0 + (b + 1) * seq - n:row0 + (b + 1) * seq, C_GQKV:C_GQKV + GDN_CONV_DIM]
                          for b in range(bsz)], axis=0)
    blk = h[row0:row0 + bsz * seq, C_GQKV:C_GQKV + GDN_CONV_DIM].reshape(bsz, seq, GDN_CONV_DIM)
    return blk[:, seq - n:, :]


def _lane_row(v, offset):
    return jnp.zeros((1, LANE), F32).at[0, offset:offset + v.shape[0]].set(v)


def kernel(x_prompt, x_sample, cache_swa_k, cache_swa_v, state_ssm_re, state_ssm_im, state_gdn_conv, state_gdn, state_gla, w_in, w_out, attn_sinks, ssm_a_re, ssm_a_im, ssm_b_re, ssm_b_im, ssm_c_re, ssm_c_im, ssm_d, ssm_log_dt, ssm_glu_w, ssm_glu_b, gdn_conv_w, gdn_a_log, gdn_dt_bias, gdn_norm_w, gla_gate_w, gla_gate_b, gla_norm_w, ln1_g, ln1_b, ln2_g, ln2_b, router_w, router_b, moe_w_gate_up, moe_b_gate_up, moe_w_down, moe_b_down):
    bp, lp, _ = x_prompt.shape
    bs, ls, _ = x_sample.shape
    n_p, n_s = bp * lp, bs * ls
    depth = w_in.shape[0]

    w_in_r = jnp.concatenate([w_in[..., :1792], w_in[..., 1800:2568], w_in[..., 1792:1800], w_in[..., 2568:N_IN],
                              jnp.zeros(w_in.shape[:2] + (NH - N_IN,), w_in.dtype)], axis=-1).astype(BF16)
    w_out_b = w_out.astype(BF16)
    glu_w_b = ssm_glu_w.astype(BF16)
    rw_pad = jnp.pad(router_w, ((0, 0), (0, 0), (0, LANE - N_EXPERTS)))
    rb_pad = jnp.pad(router_b, ((0, 0), (0, LANE - N_EXPERTS)))
    wgu_all = moe_w_gate_up.reshape(depth * N_EXPERTS, D_MODEL, 2 * D_FF)
    bgu_all = moe_b_gate_up.reshape(depth * N_EXPERTS, 1, 2 * D_FF)
    wd_all = moe_w_down.reshape(depth * N_EXPERTS, D_FF, D_MODEL)
    bd_all = moe_b_down.reshape(depth * N_EXPERTS, 1, D_MODEL)

    cos_p, sin_p = _rope_tables(jnp.arange(lp, dtype=I32))
    cos_s, sin_s = _rope_tables(PAST_LEN + jnp.arange(ls, dtype=I32))

    x = jnp.concatenate([x_prompt.reshape(n_p, D_MODEL), x_sample.reshape(n_s, D_MODEL)], axis=0)
    zeros = lambda *s: jnp.zeros(s, F32)
    new_p = [[] for _ in range(7)]
    new_s = [[] for _ in range(7)]
    for l in range(depth):
        h = in_proj(x, w_in_r[l])
        sinks = attn_sinks[l]
        oa_p, pk, pv = swa_prompt(h, sinks, cos_p, sin_p, bp, lp)
        oa_s, sk, sv = swa_sample(h, n_p, sinks, cache_swa_k[l].reshape(bs, WINDOW, 128),
                                  cache_swa_v[l].reshape(bs, WINDOW, 128), cos_s, sin_s, bs, ls)
        bw, a_bar, cw, dsk = s5_params(ssm_a_re[l], ssm_a_im[l], ssm_b_re[l], ssm_b_im[l], ssm_c_re[l], ssm_c_im[l],
                                       ssm_d[l], ssm_log_dt[l])
        glu_b = ssm_glu_b[l].reshape(1, 2 * GROUP_WIDTH)
        ob_p, hl_p = s5_prompt(h, zeros(bp, 2 * SSM_W), bw, a_bar, cw, dsk, glu_w_b[l], glu_b, bp, lp)
        h0_s = jnp.concatenate([state_ssm_re[l].reshape(bs, SSM_W), state_ssm_im[l].reshape(bs, SSM_W)], axis=1)
        ob_s, hl_s = s5_sample(h, n_p, h0_s, bw, a_bar, cw, dsk, glu_w_b[l], glu_b, bs, ls)
        alog_t = _lane_row(gdn_a_log[l], SM_GA)
        dtb_t = _lane_row(gdn_dt_bias[l], SM_GA)
        gnw = gdn_norm_w[l].reshape(1, GDN_DV)
        oc_p, gs_p = gdn_mixer(h, 0, gdn_conv_w[l], alog_t, dtb_t, gnw, zeros(bp, GDN_CONV - 1, GDN_CONV_DIM),
                               zeros(bp, GDN_HEADS, GDN_DK, GDN_DV), bp, lp)
        oc_s, gs_s = gdn_mixer(h, n_p, gdn_conv_w[l], alog_t, dtb_t, gnw, state_gdn_conv[l], state_gdn[l], bs, ls)
        cv_p = _conv_tail(h, 0, bp, lp)
        cv_s = _conv_tail(h, n_p, bs, ls)
        lgb = gla_gate_b[l].reshape(1, 128)
        lnw = jnp.tile(gla_norm_w[l], GLA_HEADS).reshape(1, 256)
        od_p, lt_p = gla_mixer(h, 0, gla_gate_w[l], lgb, lnw, zeros(bp, 256, 128), bp, lp)
        od_s, lt_s = gla_mixer(h, n_p, gla_gate_w[l], lgb, lnw, gla_state_to_t(state_gla[l]), bs, ls)

        x1 = out_proj_ln(x, (oa_p, ob_p.reshape(n_p, GROUP_WIDTH), oc_p, od_p), (oa_s, ob_s, oc_s, od_s),
                         w_out_b[l], ln1_g[l].reshape(1, D_MODEL), ln1_b[l].reshape(1, D_MODEL))
        x = moe_ffn_ln(x1, rw_pad[l], rb_pad[l].reshape(1, LANE), wgu_all, bgu_all, wd_all, bd_all,
                       ln2_g[l].reshape(1, D_MODEL), ln2_b[l].reshape(1, D_MODEL), expert0=l * N_EXPERTS)

        st_p = (pk.reshape(bp, WINDOW, A_KV_HEADS, HEAD_DIM), pv.reshape(bp, WINDOW, A_KV_HEADS, HEAD_DIM),
                hl_p[:, :SSM_W].reshape(bp, SSM_GROUPS, SSM_STATE), hl_p[:, SSM_W:].reshape(bp, SSM_GROUPS, SSM_STATE),
                cv_p, gs_p, gla_state_from_t(lt_p))
        st_s = (sk.reshape(bs, WINDOW, A_KV_HEADS, HEAD_DIM), sv.reshape(bs, WINDOW, A_KV_HEADS, HEAD_DIM),
                hl_s[:, :SSM_W].reshape(bs, SSM_GROUPS, SSM_STATE), hl_s[:, SSM_W:].reshape(bs, SSM_GROUPS, SSM_STATE),
                cv_s, gs_s, gla_state_from_t(lt_s))
        for i in range(7):
            new_p[i].append(st_p[i])
            new_s[i].append(st_s[i])
    y_p = x[:n_p].reshape(bp, lp, D_MODEL)
    y_s = x[n_p:].reshape(bs, ls, D_MODEL)
    return (y_p, y_s) + tuple(jnp.stack(t, axis=0) for t in new_p) + tuple(jnp.stack(t, axis=0) for t in new_s)
```

```python
import functools

import numpy as np
import jax
import jax.numpy as jnp
from jax import lax
from jax.experimental import pallas as pl
from jax.experimental.pallas import tpu as pltpu

F32 = jnp.float32
BF16 = jnp.bfloat16
I32 = jnp.int32
HI = lax.Precision.HIGHEST

D_MODEL = 1024
DEPTH = 4
PAST_LEN = 8192
GROUP_WIDTH = 256
HEAD_DIM = 64
A_HEADS = 4
A_KV_HEADS = 2
WINDOW = 128
ROPE_THETA = 500000.0
ROT_DIM = 16
SSM_GC = 16
SSM_GROUPS = 16
SSM_STATE = 64
SSM_W = SSM_GROUPS * SSM_STATE
GDN_HEADS = 4
GDN_DK = 64
GDN_DV = 64
GDN_CONV = 4
GDN_CONV_DIM = 768
GDN_CHUNK = 64
GLA_HEADS = 4
GLA_DK = 32
GLA_DV = 64
GLA_RANK = 16
GLA_TAU = 16.0
GLA_CHUNK = 16
N_EXPERTS = 32
TOP_K = 4
D_FF = 1024
SWIGLU_LIMIT = 7.0
SWIGLU_ALPHA = 1.702
DN_ALPHA = (2 * DEPTH) ** 0.25
LN_EPS = 1e-5
RMS_EPS = 1e-6
N_IN = 2584

C_AQ, C_AK, C_AV, C_SU, C_GQKV, C_GZ = 0, 256, 384, 512, 768, 1536
C_LQ, C_LK, C_LV, C_LR, C_SM = 1792, 1920, 2048, 2304, 2560
NH = 2688
SM_GB, SM_GA, SM_LG = 0, 4, 8

LANE = 128
ROW_TILE = 512
MOE_TILE = 512
VMEM_LIMIT = 56 * 1024 * 1024


def _cp(*sem):
    return pltpu.CompilerParams(dimension_semantics=sem, vmem_limit_bytes=VMEM_LIMIT)


def _dot(a, b, precision=None):
    return jnp.dot(a, b, preferred_element_type=F32, precision=precision)


def _dot_nt(a, b, precision=None):
    return lax.dot_general(a, b, (((1,), (1,)), ((), ())), preferred_element_type=F32, precision=precision)


def _dot_tn(a, b, precision=None):
    return lax.dot_general(a, b, (((0,), (0,)), ((), ())), preferred_element_type=F32, precision=precision)


def _bf(x):
    return x.astype(BF16)


def _iota(shape, dim):
    return lax.broadcasted_iota(I32, shape, dim)


def _shr(idx, size):
    return lax.shift_right_logical(idx, int(size).bit_length() - 1)


def _sigmoid(x):
    return 1.0 / (1.0 + jnp.exp(-x))


def _silu(x):
    return x * _sigmoid(x)


def _softplus(x):
    return jnp.maximum(x, 0.0) + jnp.log(1.0 + jnp.exp(-jnp.abs(x)))


def _log_sigmoid(x):
    return -_softplus(-x)


def _gelu_tanh(x):
    return 0.5 * x * (1.0 + jnp.tanh(0.7978845608028654 * (x + 0.044715 * x * x * x)))


def _layer_norm(y, g, b):
    mu = jnp.mean(y, axis=-1, keepdims=True)
    yc = y - mu
    var = jnp.mean(yc * yc, axis=-1, keepdims=True)
    return yc * lax.rsqrt(var + LN_EPS) * g + b


def _in_proj_body(x_ref, w_ref, o_ref):
    xb = _bf(x_ref[...])
    for c0 in range(0, NH, 512):
        c1 = min(c0 + 512, NH)
        o_ref[:, c0:c1] = _dot(xb, w_ref[:, c0:c1])


def in_proj(x, w):
    t = x.shape[0]
    return pl.pallas_call(
        _in_proj_body,
        grid=(t // ROW_TILE,),
        in_specs=[pl.BlockSpec((ROW_TILE, D_MODEL), lambda i: (i, 0)),
                  pl.BlockSpec((D_MODEL, NH), lambda i: (0, 0))],
        out_specs=pl.BlockSpec((ROW_TILE, NH), lambda i: (i, 0)),
        out_shape=jax.ShapeDtypeStruct((t, NH), F32),
        compiler_params=_cp("arbitrary"),
        name="in_proj",
    )(x, w)


def _rope(x, cos, sin):
    w = x.shape[1]
    if w > LANE:
        cos = jnp.concatenate([cos] * (w // LANE), axis=1)
        sin = jnp.concatenate([sin] * (w // LANE), axis=1)
    lane = _iota(x.shape, 1) & (HEAD_DIM - 1)
    swapped = jnp.where(lane < ROT_DIM // 2, pltpu.roll(x, w - ROT_DIM // 2, 1), pltpu.roll(x, ROT_DIM // 2, 1))
    return x * cos + swapped * sin


def _sink_attention_multi(qkvs, mask, sinks_ref):
    pairs = [(i, hq) for i in range(len(qkvs)) for hq in range(A_HEADS)]
    hd = lambda x, j: _bf(x[:, j * HEAD_DIM:(j + 1) * HEAD_DIM])
    kv = lambda hq: hq // (A_HEADS // A_KV_HEADS)
    scores = {(i, hq): _dot_nt(hd(qkvs[i][0], hq), hd(qkvs[i][1], kv(hq))) for i, hq in pairs}
    probs = {}
    for i, hq in pairs:
        s = jnp.where(mask, scores[i, hq] * (HEAD_DIM ** -0.5), -jnp.inf)
        sink = sinks_ref[hq]
        m = jnp.maximum(jnp.max(s, axis=-1, keepdims=True), sink)
        p = jnp.exp(s - m)
        den = jnp.sum(p, axis=-1, keepdims=True) + jnp.exp(sink - m)
        probs[i, hq] = _bf(p / den)
    outs = {(i, hq): _dot(probs[i, hq], hd(qkvs[i][2], kv(hq))) for i, hq in pairs}
    return [jnp.concatenate([outs[i, hq] for hq in range(A_HEADS)], axis=1) for i in range(len(qkvs))]


def _sink_attention(q, kk, vv, mask, sinks_ref):
    return _sink_attention_multi([(q, kk, vv)], mask, sinks_ref)[0]


def _swa_prompt_body(sinks_ref, cur_ref, prev_ref, cos_ref, sin_ref, cosp_ref, sinp_ref, o_ref, ko_ref, vo_ref):
    i = pl.program_id(1)
    cur = cur_ref[...]
    q = _rope(cur[:, C_AQ:C_AQ + 256], cos_ref[...], sin_ref[...])
    k = _rope(cur[:, C_AK:C_AK + 128], cos_ref[...], sin_ref[...])
    v = cur[:, C_AV:C_AV + 128]
    prev = prev_ref[...]
    kp = _rope(prev[:, 0:128], cosp_ref[...], sinp_ref[...])
    vp = prev[:, 128:256]
    kk = jnp.concatenate([kp, k], axis=0)
    vv = jnp.concatenate([vp, v], axis=0)
    r = _iota((WINDOW, 2 * WINDOW), 0)
    j = _iota((WINDOW, 2 * WINDOW), 1)
    d = WINDOW + r - j
    mask = (d >= 0) & (d <= WINDOW) & ((j >= WINDOW) | (i > 0))
    o_ref[...] = _sink_attention(q, kk, vv, mask, sinks_ref)
    ko_ref[0] = k
    vo_ref[0] = v


def swa_prompt(h, sinks, cos_t, sin_t, bsz, seq):
    nb = seq // WINDOW
    smem = pl.BlockSpec(memory_space=pltpu.SMEM)
    tab = lambda f: pl.BlockSpec((WINDOW, LANE), f)
    return pl.pallas_call(
        _swa_prompt_body,
        grid=(bsz, nb),
        in_specs=[smem,
                  pl.BlockSpec((WINDOW, 512), lambda b, i: (b * nb + i, 0)),
                  pl.BlockSpec((WINDOW, 256), lambda b, i: (b * nb + jnp.maximum(i - 1, 0), 1)),
                  tab(lambda b, i: (i, 0)), tab(lambda b, i: (i, 0)),
                  tab(lambda b, i: (jnp.maximum(i - 1, 0), 0)), tab(lambda b, i: (jnp.maximum(i - 1, 0), 0))],
        out_specs=[pl.BlockSpec((WINDOW, 256), lambda b, i: (b * nb + i, 0)),
                   pl.BlockSpec((1, WINDOW, 128), lambda b, i: (b, 0, 0)),
                   pl.BlockSpec((1, WINDOW, 128), lambda b, i: (b, 0, 0))],
        out_shape=[jax.ShapeDtypeStruct((bsz * seq, 256), F32),
                   jax.ShapeDtypeStruct((bsz, WINDOW, 128), F32),
                   jax.ShapeDtypeStruct((bsz, WINDOW, 128), F32)],
        compiler_params=_cp("arbitrary", "arbitrary"),
        name="swa_prompt",
    )(sinks, h, h, cos_t, sin_t, cos_t, sin_t)


SWA_SB = 8


def _swa_sample_body(sinks_ref, cur_ref, kc_ref, vc_ref, cos_ref, sin_ref, o_ref, ko_ref, vo_ref, *, ls):
    cw = WINDOW
    r = _iota((ls, cw + ls), 0)
    j = _iota((ls, cw + ls), 1)
    d = cw + r - j
    mask = (d >= 0) & (d <= WINDOW)
    qkvs = []
    for b in range(SWA_SB):
        cur = cur_ref[b * ls:(b + 1) * ls, :]
        q = _rope(cur[:, C_AQ:C_AQ + 256], cos_ref[...], sin_ref[...])
        k = _rope(cur[:, C_AK:C_AK + 128], cos_ref[...], sin_ref[...])
        v = cur[:, C_AV:C_AV + 128]
        kk = jnp.concatenate([kc_ref[b], k], axis=0)
        vv = jnp.concatenate([vc_ref[b], v], axis=0)
        ko_ref[b] = kk[ls:, :]
        vo_ref[b] = vv[ls:, :]
        qkvs.append((q, kk, vv))
    for b, o in enumerate(_sink_attention_multi(qkvs, mask, sinks_ref)):
        o_ref[b * ls:(b + 1) * ls, :] = o


def swa_sample(h, row0, sinks, k_cache, v_cache, cos_t, sin_t, bsz, ls):
    rows = SWA_SB * ls
    blk0 = row0 // rows
    smem = pl.BlockSpec(memory_space=pltpu.SMEM)
    cache = pl.BlockSpec((SWA_SB, WINDOW, 128), lambda i: (i, 0, 0))
    tab = pl.BlockSpec((ls, LANE), lambda i: (0, 0))
    return pl.pallas_call(
        functools.partial(_swa_sample_body, ls=ls),
        grid=(bsz // SWA_SB,),
        in_specs=[smem, pl.BlockSpec((rows, 512), lambda i: (blk0 + i, 0)), cache, cache, tab, tab],
        out_specs=[pl.BlockSpec((rows, 256), lambda i: (i, 0)), cache, cache],
        out_shape=[jax.ShapeDtypeStruct((bsz * ls, 256), F32),
                   jax.ShapeDtypeStruct((bsz, WINDOW, 128), F32),
                   jax.ShapeDtypeStruct((bsz, WINDOW, 128), F32)],
        compiler_params=_cp("arbitrary"),
        name="swa_sample",
    )(sinks, h, k_cache, v_cache, cos_t, sin_t)


S5_NB = 2 * SSM_W // LANE


def _s5_input(u, bw_ref):
    t = _dot(_bf(u), bw_ref[...])
    return jnp.concatenate([t[:, :SSM_W], t[:, :SSM_W] + t[:, SSM_W:]], axis=1)


def _s5_output(hs, u, cw_ref, d_ref, gw_ref, gb_ref):
    hs = jnp.concatenate([hs[:, :SSM_W] + hs[:, SSM_W:], hs[:, SSM_W:]], axis=1)
    y = _dot(_bf(hs), cw_ref[...]) + d_ref[...] * u
    y = _gelu_tanh(y)
    z = _dot(_bf(y), gw_ref[...]) + gb_ref[...]
    return z[:, :GROUP_WIDTH] * _sigmoid(z[:, GROUP_WIDTH:])


def _s5_scan(s_ref, a_ref, h_init, n_steps, rows):
    nre = S5_NB // 2
    a_re = [jnp.broadcast_to(a_ref[0:1, j * LANE:(j + 1) * LANE], (rows, LANE)) for j in range(nre)]
    a_im = [jnp.broadcast_to(a_ref[1:2, j * LANE:(j + 1) * LANE], (rows, LANE)) for j in range(nre)]

    def step(t, hcar):
        out = [None] * S5_NB
        base = pl.multiple_of(t * rows, rows)
        for j in range(nre):
            hr, hi = hcar[j], hcar[j + nre]
            nr = a_re[j] * hr - a_im[j] * hi + s_ref[j, pl.ds(base, rows), :]
            ni = a_re[j] * hi + a_im[j] * hr + s_ref[j + nre, pl.ds(base, rows), :]
            s_ref[j, pl.ds(base, rows), :] = nr
            s_ref[j + nre, pl.ds(base, rows), :] = ni
            out[j], out[j + nre] = nr, ni
        return tuple(out)

    return lax.fori_loop(0, n_steps, step, tuple(h_init))


def _s5_prompt_body(*refs, nb, tl):
    u_refs = refs[:nb]
    bw_ref, a_ref, cw_ref, d_ref, gw_ref, gb_ref, h0_ref, o_ref, hl_ref, s_ref, hst_ref = refs[nb:]
    i = pl.program_id(0)

    @pl.when(i == 0)
    def _():
        hst_ref[...] = h0_ref[...]

    for b in range(nb):
        bu = _s5_input(u_refs[b][...], bw_ref)
        for j in range(S5_NB):
            s_ref[j, pl.ds(b, tl, stride=nb), :] = bu[:, j * LANE:(j + 1) * LANE]
    h_init = [hst_ref[:, j * LANE:(j + 1) * LANE] for j in range(S5_NB)]
    h_fin = _s5_scan(s_ref, a_ref, h_init, tl, nb)
    for j in range(S5_NB):
        hst_ref[:, j * LANE:(j + 1) * LANE] = h_fin[j]
    for b in range(nb):
        hs = jnp.concatenate([s_ref[j, pl.ds(b, tl, stride=nb), :] for j in range(S5_NB)], axis=1)
        o_ref[b] = _s5_output(hs, u_refs[b][...], cw_ref, d_ref, gw_ref, gb_ref)

    @pl.when(i == pl.num_programs(0) - 1)
    def _():
        hl_ref[...] = hst_ref[...]


def s5_prompt(h, h0, bw, a, cw, dsk, gw, gb, bsz, seq, tl=128):
    assert bsz == 8
    nt = seq // tl
    full = lambda shape: pl.BlockSpec(shape, lambda i: (0,) * len(shape))
    u_specs = [pl.BlockSpec((tl, GROUP_WIDTH), functools.partial(lambda i, b: (b * nt + i, C_SU // GROUP_WIDTH), b=b))
               for b in range(bsz)]
    return pl.pallas_call(
        functools.partial(_s5_prompt_body, nb=bsz, tl=tl),
        grid=(nt,),
        in_specs=u_specs + [full((GROUP_WIDTH, 2 * SSM_W)), full((2, SSM_W)), full((2 * SSM_W, GROUP_WIDTH)),
                            full((1, GROUP_WIDTH)), full((GROUP_WIDTH, 2 * GROUP_WIDTH)), full((1, 2 * GROUP_WIDTH)),
                            full((bsz, 2 * SSM_W))],
        out_specs=[pl.BlockSpec((bsz, tl, GROUP_WIDTH), lambda i: (0, i, 0)), full((bsz, 2 * SSM_W))],
        out_shape=[jax.ShapeDtypeStruct((bsz, seq, GROUP_WIDTH), F32), jax.ShapeDtypeStruct((bsz, 2 * SSM_W), F32)],
        scratch_shapes=[pltpu.VMEM((S5_NB, tl * bsz, LANE), F32), pltpu.VMEM((bsz, 2 * SSM_W), F32)],
        compiler_params=_cp("arbitrary"),
        name="s5_prompt",
    )(*([h] * bsz), bw, a, cw, dsk, gw, gb, h0)


def _s5_sample_body(u_ref, bw_ref, a_ref, cw_ref, d_ref, gw_ref, gb_ref, h0_ref, o_ref, hl_ref, s_ref, t_ref, *, bsz, ls):
    bu = _s5_input(u_ref[...], bw_ref)
    for j in range(S5_NB):
        s_ref[j] = bu[:, j * LANE:(j + 1) * LANE]
    for t in range(ls):
        for j in range(S5_NB):
            t_ref[j, pl.ds(t * bsz, bsz), :] = s_ref[j, pl.ds(t, bsz, stride=ls), :]
    h_init = [h0_ref[:, j * LANE:(j + 1) * LANE] for j in range(S5_NB)]
    h_fin = _s5_scan(t_ref, a_ref, h_init, ls, bsz)
    for j in range(S5_NB):
        hl_ref[:, j * LANE:(j + 1) * LANE] = h_fin[j]
    for t in range(ls):
        for j in range(S5_NB):
            s_ref[j, pl.ds(t, bsz, stride=ls), :] = t_ref[j, pl.ds(t * bsz, bsz), :]
    hs = jnp.concatenate([s_ref[j] for j in range(S5_NB)], axis=1)
    o_ref[...] = _s5_output(hs, u_ref[...], cw_ref, d_ref, gw_ref, gb_ref)


def s5_sample(h, row0, h0, bw, a, cw, dsk, gw, gb, bsz, ls):
    rows = bsz * ls
    full = lambda shape: pl.BlockSpec(shape, lambda i: (0,) * len(shape))
    return pl.pallas_call(
        functools.partial(_s5_sample_body, bsz=bsz, ls=ls),
        grid=(1,),
        in_specs=[pl.BlockSpec((rows, GROUP_WIDTH), lambda i: (row0 // rows, C_SU // GROUP_WIDTH)),
                  full((GROUP_WIDTH, 2 * SSM_W)), full((2, SSM_W)), full((2 * SSM_W, GROUP_WIDTH)),
                  full((1, GROUP_WIDTH)), full((GROUP_WIDTH, 2 * GROUP_WIDTH)), full((1, 2 * GROUP_WIDTH)),
                  full((bsz, 2 * SSM_W))],
        out_specs=[full((rows, GROUP_WIDTH)), full((bsz, 2 * SSM_W))],
        out_shape=[jax.ShapeDtypeStruct((rows, GROUP_WIDTH), F32), jax.ShapeDtypeStruct((bsz, 2 * SSM_W), F32)],
        scratch_shapes=[pltpu.VMEM((S5_NB, rows, LANE), F32), pltpu.VMEM((S5_NB, rows, LANE), F32)],
        compiler_params=_cp("arbitrary"),
        name="s5_sample",
    )(h, bw, a, cw, dsk, gw, gb, h0)


def s5_params(a_re, a_im, b_re, b_im, c_re, c_im, d_skip, log_dt):
    lam = lax.complex(a_re, a_im)
    delta = jnp.exp(log_dt)[:, None]
    a_bar = jnp.exp(lam * delta)
    b_bar = ((a_bar - 1.0) / lam)[..., None] * lax.complex(b_re, b_im)
    eye = jnp.eye(SSM_GROUPS, dtype=F32)
    bw_re = jnp.einsum('gpc,gh->gchp', b_bar.real, eye).reshape(GROUP_WIDTH, SSM_W)
    bw_im = jnp.einsum('gpc,gh->gchp', b_bar.imag, eye).reshape(GROUP_WIDTH, SSM_W)
    bw = jnp.concatenate([bw_re, bw_im - bw_re], axis=1).astype(BF16)
    cw_re = jnp.einsum('gcp,gh->gphc', c_re, eye).reshape(SSM_W, GROUP_WIDTH)
    cw_im = jnp.einsum('gcp,gh->gphc', c_im, eye).reshape(SSM_W, GROUP_WIDTH)
    cw = jnp.concatenate([cw_re, -(cw_re + cw_im)], axis=0).astype(BF16)
    a = jnp.stack([a_bar.real.reshape(SSM_W), a_bar.imag.reshape(SSM_W)], axis=0)
    return bw, a, cw, d_skip.reshape(1, GROUP_WIDTH)


GDN_RT = 256
GDN_ST = 64
GDN_SG = 8


def _split(x):
    hi = _bf(x)
    return hi, _bf(x - hi.astype(F32))


def _dot3(a, b):
    return _dot(a[0], b[0]) + (_dot(a[0], b[1]) + _dot(a[1], b[0]))


def _gdn_prep_body(*refs, c, sample, tiles_per_seq):
    if sample:
        x_ref, cs_ref, z_ref, sm_ref, cw_ref, alog_ref, dtb_ref = refs[:7]
        outs = refs[7:14]
        xs_ref, cb_ref = refs[14:]
    else:
        x_ref, prev_ref, cs_ref, z_ref, sm_ref, cw_ref, alog_ref, dtb_ref = refs[:8]
        outs = refs[8:15]
        (xs_ref,) = refs[15:]
    u_ref, w_ref, qk_ref, qg_ref, kd_ref, eg_ref, zs_ref = outs
    rt = GDN_RT
    x = x_ref[...]
    xs_ref[8:8 + rt, :] = x
    if sample:
        xs_ref[0:8, :] = jnp.zeros((8, GDN_CONV_DIM), F32)
        cb_ref[0:rt, :] = cs_ref[...]
        cb_ref[rt:rt + 8, :] = jnp.zeros((8, GDN_CONV_DIM), F32)
        pos = _iota((rt, GDN_CONV_DIM), 0) & (c - 1)
        shifted = lambda i: jnp.where(pos >= i, xs_ref[8 - i:8 - i + rt, :], cb_ref[8 - i:8 - i + rt, :])
    else:
        first = (pl.program_id(0) % tiles_per_seq) == 0
        xs_ref[0:8, :] = jnp.where(first, cs_ref[...], prev_ref[...])
        shifted = lambda i: xs_ref[8 - i:8 - i + rt, :]
    conv = shifted(3) * cw_ref[0:1, :]
    conv = conv + shifted(2) * cw_ref[1:2, :]
    conv = conv + shifted(1) * cw_ref[2:3, :]
    conv = conv + x * cw_ref[3:4, :]
    qkv = _silu(conv)
    zs_ref[...] = _silu(z_ref[...])

    sm = sm_ref[...]
    beta_all = _sigmoid(sm)
    g_all = -jnp.exp(alog_ref[...]) * _softplus(sm + dtb_ref[...])
    st = GDN_ST
    ri = _iota((st, st), 0)
    ci = _iota((st, st), 1)
    same = _shr(ri, c) == _shr(ci, c)
    low = (ci <= ri) & same
    tri = low.astype(F32)
    last = (ci == (ri | (c - 1))).astype(F32)
    eye = (ri == ci).astype(F32)
    lane = _iota((st, LANE), 1)
    nsub = rt // st
    probs = [(t, hh) for t in range(nsub) for hh in range(GDN_HEADS)]
    gc_alls, gl_alls = [], []
    for t in range(nsub):
        rows = slice(t * st, (t + 1) * st)
        gc_alls.append(_dot(tri, g_all[rows, :], HI))
    for t in range(nsub):
        gl_alls.append(_dot(last, gc_alls[t], HI))
        eg_ref[t * st:(t + 1) * st, :] = jnp.exp(gl_alls[t])
    qs, ks, rhss, decays, a_stricts, invs = {}, {}, {}, {}, {}, {}
    for t, hh in probs:
        rows = slice(t * st, (t + 1) * st)
        q = qkv[rows, hh * GDN_DK:(hh + 1) * GDN_DK]
        k = qkv[rows, 256 + hh * GDN_DK:256 + (hh + 1) * GDN_DK]
        v = qkv[rows, 512 + hh * GDN_DV:512 + (hh + 1) * GDN_DV]
        q = q * lax.rsqrt(jnp.sum(q * q, axis=-1, keepdims=True) + 1e-6) * (GDN_DK ** -0.5)
        k = k * lax.rsqrt(jnp.sum(k * k, axis=-1, keepdims=True) + 1e-6)
        beta = beta_all[rows, SM_GB + hh:SM_GB + hh + 1]
        gc = gc_alls[t][:, SM_GA + hh:SM_GA + hh + 1]
        gl = gl_alls[t][:, SM_GA + hh:SM_GA + hh + 1]
        sel = (lane == SM_GA + hh).astype(F32)
        gc_row = _dot_nt(sel, gc_alls[t], HI)
        decays[t, hh] = jnp.exp(jnp.where(low, gc - gc_row, -jnp.inf))
        kb = k * beta
        eg = jnp.exp(gc)
        qs[t, hh], ks[t, hh] = q, k
        rhss[t, hh] = _split(jnp.concatenate([v * beta, kb * eg], axis=1))
        a_stricts[t, hh] = jnp.where(ci < ri, _dot_nt(_bf(kb), _bf(k)) * decays[t, hh], 0.0)
        qg_ref[rows, hh * GDN_DK:(hh + 1) * GDN_DK] = q * eg
        kd_ref[rows, hh * GDN_DK:(hh + 1) * GDN_DK] = k * jnp.exp(gl - gc)
    for t, hh in probs:
        qk_ref[t * st:(t + 1) * st, hh * st:(hh + 1) * st] = _dot_nt(_bf(qs[t, hh]), _bf(ks[t, hh])) * decays[t, hh]
    for p in probs:
        invs[p] = eye - jnp.where(_shr(ri, 2) == _shr(ci, 2), a_stricts[p], 0.0)
    s = 2
    while s < c:
        pair = (_shr(ri, 2 * s) == _shr(ci, 2 * s)) & (_shr(ri, s) != _shr(ci, s))
        inv_s = {p: _split(invs[p]) for p in probs}
        mid = {p: _dot3(_split(jnp.where(pair, a_stricts[p], 0.0)), inv_s[p]) for p in probs}
        for p in probs:
            invs[p] = invs[p] - _dot3(inv_s[p], _split(mid[p]))
        s *= 2
    for t, hh in probs:
        sol = _dot3(_split(invs[t, hh]), rhss[t, hh])
        u_ref[t * st:(t + 1) * st, hh * GDN_DV:(hh + 1) * GDN_DV] = sol[:, :GDN_DV]
        w_ref[t * st:(t + 1) * st, hh * GDN_DV:(hh + 1) * GDN_DV] = sol[:, GDN_DV:]


def _gdn_scan_body(u_ref, w_ref, qk_ref, qg_ref, kd_ref, eg_ref, zs_ref, nw_ref, s0_ref, o_ref, so_ref, st_ref, *, c):
    n = pl.program_id(1)

    @pl.when(n == 0)
    def _():
        st_ref[...] = s0_ref[...]

    probs = [(s, hh) for s in range(GDN_SG) for hh in range(GDN_HEADS)]
    hsl = lambda hh: slice(hh * GDN_DV, (hh + 1) * GDN_DV)
    state_b = {(s, hh): _bf(st_ref[s, hh]) for s, hh in probs}
    v_b = {}
    for s, hh in probs:
        v_b[s, hh] = _bf(u_ref[s, :, hsl(hh)] - _dot(_bf(w_ref[s, :, hsl(hh)]), state_b[s, hh]))
    o_inter = {(s, hh): _dot(_bf(qg_ref[s, :, hsl(hh)]), state_b[s, hh]) for s, hh in probs}
    for s, hh in probs:
        col0 = (s * c) % GDN_ST
        qk = qk_ref[s, :, hh * GDN_ST + col0:hh * GDN_ST + col0 + c]
        o = o_inter[s, hh] + _dot(_bf(qk), v_b[s, hh])
        o = o * lax.rsqrt(jnp.mean(o * o, axis=-1, keepdims=True) + RMS_EPS) * nw_ref[...]
        o_ref[s, :, hsl(hh)] = o * zs_ref[s, :, hsl(hh)]
    for s, hh in probs:
        eg_last = eg_ref[s, c - 1:c, SM_GA + hh:SM_GA + hh + 1]
        st_ref[s, hh] = st_ref[s, hh] * eg_last + _dot_tn(_bf(kd_ref[s, :, hsl(hh)]), v_b[s, hh])

    @pl.when(n == pl.num_programs(1) - 1)
    def _():
        so_ref[...] = st_ref[...]


def gdn_mixer(h, row0, conv_w, alog_t, dtb_t, norm_w, conv_state, s0, bsz, seq):
    c = min(GDN_CHUNK, seq)
    sample = seq == c
    rows = bsz * seq
    rt = GDN_RT
    nt = rows // rt
    blk0 = row0 // rt
    tps = max(seq // rt, 1)
    cs_rows = jnp.pad(conv_state, ((0, 0), (8 - (GDN_CONV - 1), 0), (0, 0))).reshape(bsz * 8, GDN_CONV_DIM)
    row = lambda w, col: pl.BlockSpec((rt, w), lambda i: (blk0 + i, col))
    full = lambda shape: pl.BlockSpec(shape, lambda i: (0,) * len(shape))
    if sample:
        aux_specs = [pl.BlockSpec((rt, GDN_CONV_DIM), lambda i: (i, 0))]
        aux = [cs_rows]
        scratch = [pltpu.VMEM((8 + rt, GDN_CONV_DIM), F32), pltpu.VMEM((8 + rt, GDN_CONV_DIM), F32)]
    else:
        aux_specs = [pl.BlockSpec((8, GDN_CONV_DIM), lambda i: (jnp.maximum((row0 + i * rt) // 8 - 1, 0), C_GQKV // GDN_CONV_DIM)),
                     pl.BlockSpec((8, GDN_CONV_DIM), lambda i: (i // tps, 0))]
        aux = [h, cs_rows]
        scratch = [pltpu.VMEM((8 + rt, GDN_CONV_DIM), F32)]
    out_w = [GROUP_WIDTH] * 5 + [LANE, GROUP_WIDTH]
    prep = pl.pallas_call(
        functools.partial(_gdn_prep_body, c=c, sample=sample, tiles_per_seq=tps),
        grid=(nt,),
        in_specs=[row(GDN_CONV_DIM, C_GQKV // GDN_CONV_DIM)] + aux_specs
                 + [row(GROUP_WIDTH, C_GZ // GROUP_WIDTH), row(LANE, C_SM // LANE),
                    full((GDN_CONV, GDN_CONV_DIM)), full((1, LANE)), full((1, LANE))],
        out_specs=[pl.BlockSpec((rt, w), lambda i: (i, 0)) for w in out_w],
        out_shape=[jax.ShapeDtypeStruct((rows, w), F32) for w in out_w],
        scratch_shapes=scratch,
        compiler_params=_cp("arbitrary"),
        name="gdn_prep",
    )(h, *aux, h, h, conv_w, alog_t, dtb_t)
    prep = [a.reshape(bsz, seq, a.shape[-1]) for a in prep]
    ng, nc = bsz // GDN_SG, seq // c
    seq_blk = lambda w: pl.BlockSpec((GDN_SG, c, w), lambda g, n: (g, n, 0))
    st_spec = pl.BlockSpec((GDN_SG, GDN_HEADS, GDN_DK, GDN_DV), lambda g, n: (g, 0, 0, 0))
    o, s_fin = pl.pallas_call(
        functools.partial(_gdn_scan_body, c=c),
        grid=(ng, nc),
        in_specs=[seq_blk(w) for w in out_w] + [pl.BlockSpec((1, GDN_DV), lambda g, n: (0, 0)), st_spec],
        out_specs=[seq_blk(GROUP_WIDTH), st_spec],
        out_shape=[jax.ShapeDtypeStruct((bsz, seq, GROUP_WIDTH), F32),
                   jax.ShapeDtypeStruct((bsz, GDN_HEADS, GDN_DK, GDN_DV), F32)],
        scratch_shapes=[pltpu.VMEM((GDN_SG, GDN_HEADS, GDN_DK, GDN_DV), F32)],
        compiler_params=_cp("arbitrary", "arbitrary"),
        name="gdn_scan",
    )(*prep, norm_w, s0)
    return o.reshape(rows, GROUP_WIDTH), s_fin


GLA_RT = 256


def _dot_exact(x, w):
    hi = _bf(x)
    r1 = x - hi.astype(F32)
    mid = _bf(r1)
    lo = _bf(r1 - mid.astype(F32))
    return _dot(hi, w) + (_dot(mid, w) + _dot(lo, w))


def _gla_body(qk_ref, vr_ref, sm_ref, gw_ref, gb_ref, nw_ref, s0_ref, o_ref, so_ref,
              kb_ref, bb_ref, vb_ref, st_ref, *, c, seq):
    tr = GLA_RT
    n = pl.program_id(1)
    pad = GLA_CHUNK
    carried = seq >= tr

    @pl.when(n == 0)
    def _():
        kb_ref[0:pad, :] = jnp.zeros((pad, 128), F32)
        bb_ref[0:pad, :] = jnp.zeros((pad, 128), F32)
        vb_ref[0:pad, :] = jnp.zeros((pad, 256), F32)
        if carried:
            st_ref[...] = s0_ref[0]

    q = qk_ref[:, 0:128] * (GLA_DK ** -0.5)
    k = qk_ref[:, 128:256]
    v = vr_ref[:, 0:256]
    lg = sm_ref[:, SM_LG:SM_LG + GLA_RANK]
    log_a = _log_sigmoid(_dot(_bf(lg), _bf(gw_ref[...])) + gb_ref[...]) / GLA_TAU
    ri = _iota((tr, tr), 0)
    ci = _iota((tr, tr), 1)
    tri = _bf(((ci <= ri) & (_shr(ri, c) == _shr(ci, c))).astype(F32))
    bc = _dot_nt_exact_lhs(tri, log_a)
    kb_ref[pad:pad + tr, :] = k
    bb_ref[pad:pad + tr, :] = bc
    vb_ref[pad:pad + tr, :] = v
    ones_kv = _bf((_shr(_iota((128, 256), 0), GLA_DK) == _shr(_iota((128, 256), 1), GLA_DV)).astype(F32))
    pos = _iota((tr, 128), 0) & (c - 1)
    ps = []
    for dl in range(c):
        ks = kb_ref[pad - dl:pad - dl + tr, :]
        bs = bb_ref[pad - dl:pad - dl + tr, :]
        ps.append(jnp.where(pos >= dl, q * ks * jnp.exp(bc - bs), 0.0))
    att = _dot_exact(jnp.concatenate(ps, axis=0), ones_kv)
    o_intra = jnp.zeros((tr, 256), F32)
    for dl in range(c):
        vs = vb_ref[pad - dl:pad - dl + tr, :]
        o_intra = o_intra + _bf(att[dl * tr:(dl + 1) * tr, :]).astype(F32) * _bf(vs).astype(F32)
    bd = (_shr(_iota((256, 128), 0), GLA_DV) == _shr(_iota((256, 128), 1), GLA_DK)).astype(F32)
    nch = tr // c
    sls = [slice(ch * c, (ch + 1) * c) for ch in range(nch)]
    b_last = [bc[sl, :][c - 1:c, :] for sl in sls]
    incs = [bd * _dot_tn(_bf(v[sl, :]), _bf(k[sl, :] * jnp.exp(bl - bc[sl, :]))) for sl, bl in zip(sls, b_last)]
    outs = []
    st = st_ref[...] if carried else None
    for ch, sl in enumerate(sls):
        if not carried:
            st = s0_ref[(ch * c) // seq]
        outs.append(_dot_nt(_bf(q[sl, :] * jnp.exp(bc[sl, :])), _bf(st)) + o_intra[sl, :])
        st = st * jnp.exp(b_last[ch]) + incs[ch]
        if not carried:
            so_ref[(ch * c) // seq] = st
    if carried:
        st_ref[...] = st
        so_ref[0] = st
    o = jnp.concatenate(outs, axis=0)
    ones_vv = _bf((_shr(_iota((256, 256), 0), GLA_DV) == _shr(_iota((256, 256), 1), GLA_DV)).astype(F32))
    ms = _dot_exact(o * o, ones_vv) * (1.0 / GLA_DV)
    o_ref[...] = o * lax.rsqrt(ms + RMS_EPS) * nw_ref[...] * _silu(vr_ref[:, 256:512])


def _dot_nt_exact_lhs(w, x):
    hi = _bf(x)
    r1 = x - hi.astype(F32)
    mid = _bf(r1)
    lo = _bf(r1 - mid.astype(F32))
    return _dot(w, hi) + (_dot(w, mid) + _dot(w, lo))


def gla_mixer(h, row0, gate_w, gate_b, norm_w4, s0t, bsz, seq):
    c = min(GLA_CHUNK, seq)
    tr = GLA_RT
    spt = max(tr // seq, 1)
    nt = max(seq // tr, 1)
    ng = bsz // spt
    blk0 = row0 // tr
    row = lambda w, col: pl.BlockSpec((tr, w), lambda g, n: (blk0 + g * nt + n, col))
    full = lambda shape: pl.BlockSpec(shape, lambda g, n: (0,) * len(shape))
    st_spec = pl.BlockSpec((spt, 256, 128), lambda g, n: (g, 0, 0))
    return pl.pallas_call(
        functools.partial(_gla_body, c=c, seq=seq),
        grid=(ng, nt),
        in_specs=[row(256, C_LQ // 256), row(512, C_LV // 512), row(LANE, C_SM // LANE),
                  full((GLA_RANK, 128)), full((1, 128)), full((1, 256)), st_spec],
        out_specs=[pl.BlockSpec((tr, 256), lambda g, n: (g * nt + n, 0)), st_spec],
        out_shape=[jax.ShapeDtypeStruct((bsz * seq, 256), F32), jax.ShapeDtypeStruct((bsz, 256, 128), F32)],
        scratch_shapes=[pltpu.VMEM((GLA_CHUNK + tr, 128), F32), pltpu.VMEM((GLA_CHUNK + tr, 128), F32),
                        pltpu.VMEM((GLA_CHUNK + tr, 256), F32), pltpu.VMEM((256, 128), F32)],
        compiler_params=_cp("arbitrary", "arbitrary"),
        name="gla",
    )(h, h, h, gate_w, gate_b, norm_w4, s0t)


def gla_state_to_t(s):
    eye = jnp.eye(GLA_HEADS, dtype=s.dtype)
    return jnp.einsum('bhde,hg->bhegd', s, eye).reshape(s.shape[0], 256, 128)


def gla_state_from_t(st):
    b = st.shape[0]
    t5 = st.reshape(b, GLA_HEADS, GLA_DV, GLA_HEADS, GLA_DK)
    diag = jnp.stack([t5[:, hh, :, hh, :] for hh in range(GLA_HEADS)], axis=1)
    return jnp.swapaxes(diag, 2, 3)


TOK_SUB = D_MODEL // LANE


def _to_token_tiles(ref, row0, y):
    n = y.shape[0]
    for s in range(TOK_SUB):
        ref[pl.ds(row0 * TOK_SUB + s, n, stride=TOK_SUB), :] = y[:, s * LANE:(s + 1) * LANE]


def _from_token_tiles(ref, row0, n, s):
    return ref[pl.ds(row0 * TOK_SUB + s, n, stride=TOK_SUB), :]


def _out_proj_body(x_ref, ap, bp, cp, dp, a_s, b_s, c_s, d_s, w_ref, g_ref, b_ref, rw_ref, rb_ref,
                   o_ref, t_ref, e_ref, gt_ref, *, n_prompt_tiles):
    i = pl.program_id(0)

    def run(refs):
        acc = DN_ALPHA * x_ref[...]
        for m, r in enumerate(refs):
            acc = acc + _dot(_bf(r[...]), w_ref[m * GROUP_WIDTH:(m + 1) * GROUP_WIDTH, :])
        y = _layer_norm(acc, g_ref[...], b_ref[...])
        o_ref[...] = y
        _to_token_tiles(t_ref, 0, y)
        e_ref[...], gt_ref[...] = _route(y, rw_ref, rb_ref)

    @pl.when(i < n_prompt_tiles)
    def _():
        run((ap, bp, cp, dp))

    @pl.when(i >= n_prompt_tiles)
    def _():
        run((a_s, b_s, c_s, d_s))


def out_proj_ln(x, mix_p, mix_s, w, g, b, rw, rb):
    t = x.shape[0]
    npt = mix_p[0].shape[0] // ROW_TILE
    nst = mix_s[0].shape[0] // ROW_TILE
    p_spec = pl.BlockSpec((ROW_TILE, GROUP_WIDTH), lambda i: (jnp.minimum(i, npt - 1), 0))
    s_spec = pl.BlockSpec((ROW_TILE, GROUP_WIDTH), lambda i: (jnp.clip(i - npt, 0, nst - 1), 0))
    full = lambda shape: pl.BlockSpec(shape, lambda i: (0,) * len(shape))
    return pl.pallas_call(
        functools.partial(_out_proj_body, n_prompt_tiles=npt),
        grid=(t // ROW_TILE,),
        in_specs=[pl.BlockSpec((ROW_TILE, D_MODEL), lambda i: (i, 0))] + [p_spec] * 4 + [s_spec] * 4
                 + [full((D_MODEL, D_MODEL)), full((1, D_MODEL)), full((1, D_MODEL)), full((D_MODEL, LANE)), full((1, LANE))],
        out_specs=[pl.BlockSpec((ROW_TILE, D_MODEL), lambda i: (i, 0)),
                   pl.BlockSpec((ROW_TILE * TOK_SUB, LANE), lambda i: (i, 0)),
                   pl.BlockSpec((ROW_TILE, LANE), lambda i: (i, 0)), pl.BlockSpec((ROW_TILE, LANE), lambda i: (i, 0))],
        out_shape=[jax.ShapeDtypeStruct((t, D_MODEL), F32), jax.ShapeDtypeStruct((t * TOK_SUB, LANE), F32),
                   jax.ShapeDtypeStruct((t, LANE), I32), jax.ShapeDtypeStruct((t, LANE), F32)],
        compiler_params=_cp("arbitrary"),
        name="out_proj_ln",
    )(x, *mix_p, *mix_s, w, g, b, rw, rb)


def _route(x, w_ref, b_ref):
    logits = _dot(_bf(x), _bf(w_ref[...])) + b_ref[...]
    lane = _iota(logits.shape, 1)
    vals = jnp.where(lane < N_EXPERTS, logits, -jnp.inf)
    eo = jnp.zeros(logits.shape, I32)
    top = []
    for k in range(TOP_K):
        m = jnp.max(vals, axis=-1, keepdims=True)
        idx = jnp.min(jnp.where(vals == m, lane.astype(F32), float(LANE)), axis=-1, keepdims=True).astype(I32)
        eo = jnp.where(lane == k, idx, eo)
        top.append(m)
        vals = jnp.where(lane == idx, -jnp.inf, vals)
    ex = [jnp.exp(v - top[0]) for v in top]
    den = ex[0] + ex[1] + ex[2] + ex[3]
    go = jnp.zeros(logits.shape, F32)
    for k in range(TOP_K):
        go = jnp.where(lane == k, ex[k] / den, go)
    return eo, go


DISPATCH_TILE = 256


def _tok(i):
    return pl.ds(pl.multiple_of(i * TOK_SUB, TOK_SUB), TOK_SUB)


def _dispatch_copy(x_ref, xs_ref, sem, r, d):
    return pltpu.make_async_copy(x_ref.at[_tok(r), :], xs_ref.at[_tok(d), :], sem)


def _dispatch_body(dest_ref, x_ref, xs_in_ref, xs_ref, sem):
    del xs_in_ref

    def start(r, carry):
        for k in range(TOP_K):
            _dispatch_copy(x_ref, xs_ref, sem, r, dest_ref[0, 0, r * TOP_K + k]).start(priority=k % 2)
        return carry

    lax.fori_loop(0, DISPATCH_TILE, start, 0)

    def wait(r, carry):
        for k in range(TOP_K):
            _dispatch_copy(x_ref, xs_ref, sem, 0, 0).wait()
        return carry

    lax.fori_loop(0, DISPATCH_TILE, wait, 0)


def moe_dispatch(xt, dest, n_rows):
    t = xt.shape[0] // TOK_SUB
    nt = t // DISPATCH_TILE
    zeros = jnp.zeros((n_rows * TOK_SUB, LANE), F32)
    return pl.pallas_call(
        _dispatch_body,
        grid=(nt,),
        in_specs=[pl.BlockSpec((1, 1, DISPATCH_TILE * TOP_K), lambda i: (i, 0, 0), memory_space=pltpu.SMEM),
                  pl.BlockSpec((DISPATCH_TILE * TOK_SUB, LANE), lambda i: (i, 0)),
                  pl.BlockSpec(memory_space=pl.ANY)],
        out_specs=pl.BlockSpec(memory_space=pl.ANY),
        out_shape=jax.ShapeDtypeStruct((n_rows * TOK_SUB, LANE), F32),
        scratch_shapes=[pltpu.SemaphoreType.DMA(())],
        input_output_aliases={2: 0},
        compiler_params=_cp("arbitrary"),
        name="moe_dispatch",
    )(dest.reshape(nt, 1, DISPATCH_TILE * TOP_K), xt, zeros)


EXPERT_SUB = 256


def _expert_body(te_ref, tv_ref, xs_ref, wgu_ref, bgu_ref, wd_ref, bd_ref, ys_ref, wgu_s, wd2_s, tmp_s):
    i = pl.program_id(0)
    valid = tv_ref[i] != 0
    changed = (i == 0) | (te_ref[i] != te_ref[jnp.maximum(i - 1, 0)])

    @pl.when(i == 0)
    def _():
        tmp_s[...] = jnp.zeros(tmp_s.shape, F32)

    @pl.when(valid & changed)
    def _():
        for c0 in range(0, 2 * D_FF, 512):
            wgu_s[:, c0:c0 + 512] = _bf(wgu_ref[0, :, c0:c0 + 512])
        for j in range(D_MODEL // LANE):
            tmp_s[pl.ds(0, D_FF, stride=2), :] = wd_ref[0, :, j * LANE:(j + 1) * LANE]
            wd2_s[:, j * LANE:(j + 1) * LANE] = _bf(tmp_s[...])

    @pl.when(valid)
    def _():
        for r0 in range(0, MOE_TILE, EXPERT_SUB):
            xb = jnp.concatenate([_bf(_from_token_tiles(xs_ref, r0, EXPERT_SUB, s)) for s in range(TOK_SUB)], axis=1)
            gu = _dot(xb, wgu_s[...]) + bgu_ref[0]
            nxt = pltpu.roll(gu, 2 * D_FF - 1, 1)
            gate = jnp.minimum(gu, SWIGLU_LIMIT)
            up = jnp.clip(nxt, -SWIGLU_LIMIT, SWIGLU_LIMIT)
            act = gate * _sigmoid(gate * SWIGLU_ALPHA) * (up + 1.0)
            even = (_iota(act.shape, 1) & 1) == 0
            act = jnp.where(even, act, 0.0)
            _to_token_tiles(ys_ref, r0, _dot(_bf(act), wd2_s[...]) + bd_ref[0])

    @pl.when(jnp.logical_not(valid))
    def _():
        ys_ref[...] = jnp.zeros(ys_ref.shape, F32)


def moe_experts(xs, tile_e, tile_valid, wgu, bgu, wd, bd):
    n_rows = xs.shape[0] // TOK_SUB
    nt = n_rows // MOE_TILE
    grid_spec = pltpu.PrefetchScalarGridSpec(
        num_scalar_prefetch=2,
        grid=(nt,),
        in_specs=[pl.BlockSpec((MOE_TILE * TOK_SUB, LANE), lambda i, te, tv: (i, 0)),
                  pl.BlockSpec((1, D_MODEL, 2 * D_FF), lambda i, te, tv: (te[i], 0, 0)),
                  pl.BlockSpec((1, 1, 2 * D_FF), lambda i, te, tv: (te[i], 0, 0)),
                  pl.BlockSpec((1, D_FF, D_MODEL), lambda i, te, tv: (te[i], 0, 0)),
                  pl.BlockSpec((1, 1, D_MODEL), lambda i, te, tv: (te[i], 0, 0))],
        out_specs=pl.BlockSpec((MOE_TILE * TOK_SUB, LANE), lambda i, te, tv: (i, 0)),
        scratch_shapes=[pltpu.VMEM((D_MODEL, 2 * D_FF), BF16), pltpu.VMEM((2 * D_FF, D_MODEL), BF16),
                        pltpu.VMEM((2 * D_FF, LANE), F32)],
    )
    return pl.pallas_call(
        _expert_body,
        grid_spec=grid_spec,
        out_shape=jax.ShapeDtypeStruct((n_rows * TOK_SUB, LANE), F32),
        compiler_params=_cp("arbitrary"),
        name="moe_experts",
    )(tile_e, tile_valid, xs, wgu, bgu, wd, bd)


def _combine_copy(ys_ref, buf_ref, sem, d, k, r):
    return pltpu.make_async_copy(ys_ref.at[_tok(d), :], buf_ref.at[k, _tok(r), :], sem)


def _combine_body(dest_ref, gates_ref, x_ref, g_ref, b_ref, ys_ref, o_ref, buf_ref, sem):
    def start(r, carry):
        for k in range(TOP_K):
            _combine_copy(ys_ref, buf_ref, sem, dest_ref[0, 0, r * TOP_K + k], k, r).start(priority=k % 2)
        return carry

    lax.fori_loop(0, DISPATCH_TILE, start, 0)

    def wait(r, carry):
        for k in range(TOP_K):
            _combine_copy(ys_ref, buf_ref, sem, 0, k, 0).wait()
        return carry

    lax.fori_loop(0, DISPATCH_TILE, wait, 0)
    gates = gates_ref[...]
    cols = []
    for s in range(TOK_SUB):
        acc = DN_ALPHA * x_ref[:, s * LANE:(s + 1) * LANE]
        for k in range(TOP_K):
            acc = acc + gates[:, k:k + 1] * buf_ref[k, pl.ds(s, DISPATCH_TILE, stride=TOK_SUB), :]
        cols.append(acc)
    o_ref[...] = _layer_norm(jnp.concatenate(cols, axis=1), g_ref[...], b_ref[...])


def moe_combine_ln(x, ys, dest, gates, g, b):
    t = x.shape[0]
    nt = t // DISPATCH_TILE
    full = lambda shape: pl.BlockSpec(shape, lambda i: (0,) * len(shape))
    return pl.pallas_call(
        _combine_body,
        grid=(nt,),
        in_specs=[pl.BlockSpec((1, 1, DISPATCH_TILE * TOP_K), lambda i: (i, 0, 0), memory_space=pltpu.SMEM),
                  pl.BlockSpec((DISPATCH_TILE, LANE), lambda i: (i, 0)),
                  pl.BlockSpec((DISPATCH_TILE, D_MODEL), lambda i: (i, 0)),
                  full((1, D_MODEL)), full((1, D_MODEL)),
                  pl.BlockSpec(memory_space=pl.ANY)],
        out_specs=pl.BlockSpec((DISPATCH_TILE, D_MODEL), lambda i: (i, 0)),
        out_shape=jax.ShapeDtypeStruct((t, D_MODEL), F32),
        scratch_shapes=[pltpu.VMEM((TOP_K, DISPATCH_TILE * TOK_SUB, LANE), F32), pltpu.SemaphoreType.DMA(())],
        compiler_params=_cp("arbitrary"),
        name="moe_combine_ln",
    )(dest.reshape(nt, 1, DISPATCH_TILE * TOP_K), gates, x, g, b, ys)


def moe_plan(top_e, n_tokens):
    tk = n_tokens * TOP_K
    flat_e = top_e.reshape(tk)
    onehot = (flat_e[:, None] == jnp.arange(N_EXPERTS, dtype=I32)[None, :]).astype(I32)
    csum = jnp.cumsum(onehot, axis=0)
    rank = jnp.take_along_axis(csum, flat_e[:, None], axis=1)[:, 0] - 1
    counts = csum[-1]
    ntile = (counts + MOE_TILE - 1) // MOE_TILE
    tile_end = jnp.cumsum(ntile)
    tile_start = tile_end - ntile
    dest = tile_start[flat_e] * MOE_TILE + rank
    n_tiles = -(-tk // MOE_TILE) + N_EXPERTS
    tiles = jnp.arange(n_tiles, dtype=I32)
    tile_e = jnp.minimum(jnp.searchsorted(tile_end, tiles, side='right'), N_EXPERTS - 1).astype(I32)
    tile_valid = (tiles < tile_end[-1]).astype(I32)
    return dest.astype(I32), tile_e, tile_valid, n_tiles * MOE_TILE


def moe_ffn_ln(x1, x1t, e_pad, gates, wgu, bgu, wd, bd, g, b, expert0=0):
    t = x1.shape[0]
    dest, tile_e, tile_valid, n_rows = moe_plan(e_pad[:, :TOP_K], t)
    xs = moe_dispatch(x1t, dest, n_rows)
    ys = moe_experts(xs, tile_e + expert0, tile_valid, wgu, bgu, wd, bd)
    return moe_combine_ln(x1, ys, dest, gates, g, b)


def _rope_tables(pos):
    half = ROT_DIM // 2
    inv_freq = ROPE_THETA ** (-jnp.arange(half, dtype=F32) / half)
    ang = pos.astype(F32)[:, None] * inv_freq[None, :]
    cos, sin = jnp.cos(ang), jnp.sin(ang)
    n = pos.shape[0]
    ones = jnp.ones((n, HEAD_DIM - ROT_DIM), F32)
    cos_h = jnp.concatenate([cos, cos, ones], axis=1)
    sin_h = jnp.concatenate([-sin, sin, 0.0 * ones], axis=1)
    return jnp.concatenate([cos_h, cos_h], axis=1), jnp.concatenate([sin_h, sin_h], axis=1)


def _conv_tail(h, row0, bsz, seq):
    assert seq >= GDN_CONV - 1
    n = GDN_CONV - 1
    if bsz <= 8:
        return jnp.stack([h[row0 + (b + 1) * seq - n:row0 + (b + 1) * seq, C_GQKV:C_GQKV + GDN_CONV_DIM]
                          for b in range(bsz)], axis=0)
    blk = h[row0:row0 + bsz * seq, C_GQKV:C_GQKV + GDN_CONV_DIM].reshape(bsz, seq, GDN_CONV_DIM)
    return blk[:, seq - n:, :]


def _lane_row(v, offset):
    return jnp.zeros((1, LANE), F32).at[0, offset:offset + v.shape[0]].set(v)


def kernel(x_prompt, x_sample, cache_swa_k, cache_swa_v, state_ssm_re, state_ssm_im, state_gdn_conv, state_gdn, state_gla, w_in, w_out, attn_sinks, ssm_a_re, ssm_a_im, ssm_b_re, ssm_b_im, ssm_c_re, ssm_c_im, ssm_d, ssm_log_dt, ssm_glu_w, ssm_glu_b, gdn_conv_w, gdn_a_log, gdn_dt_bias, gdn_norm_w, gla_gate_w, gla_gate_b, gla_norm_w, ln1_g, ln1_b, ln2_g, ln2_b, router_w, router_b, moe_w_gate_up, moe_b_gate_up, moe_w_down, moe_b_down):
    bp, lp, _ = x_prompt.shape
    bs, ls, _ = x_sample.shape
    n_p, n_s = bp * lp, bs * ls
    depth = w_in.shape[0]

    w_in_r = jnp.concatenate([w_in[..., :1792], w_in[..., 1800:2568], w_in[..., 1792:1800], w_in[..., 2568:N_IN],
                              jnp.zeros(w_in.shape[:2] + (NH - N_IN,), w_in.dtype)], axis=-1).astype(BF16)
    w_out_b = w_out.astype(BF16)
    glu_w_b = ssm_glu_w.astype(BF16)
    rw_pad = jnp.pad(router_w, ((0, 0), (0, 0), (0, LANE - N_EXPERTS)))
    rb_pad = jnp.pad(router_b, ((0, 0), (0, LANE - N_EXPERTS)))
    wgu_all = moe_w_gate_up.reshape(depth * N_EXPERTS, D_MODEL, 2 * D_FF)
    bgu_all = moe_b_gate_up.reshape(depth * N_EXPERTS, 1, 2 * D_FF)
    wd_all = moe_w_down.reshape(depth * N_EXPERTS, D_FF, D_MODEL)
    bd_all = moe_b_down.reshape(depth * N_EXPERTS, 1, D_MODEL)

    cos_p, sin_p = _rope_tables(jnp.arange(lp, dtype=I32))
    cos_s, sin_s = _rope_tables(PAST_LEN + jnp.arange(ls, dtype=I32))

    x = jnp.concatenate([x_prompt.reshape(n_p, D_MODEL), x_sample.reshape(n_s, D_MODEL)], axis=0)
    zeros = lambda *s: jnp.zeros(s, F32)
    new_p = [[] for _ in range(7)]
    new_s = [[] for _ in range(7)]
    for l in range(depth):
        h = in_proj(x, w_in_r[l])
        sinks = attn_sinks[l]
        oa_p, pk, pv = swa_prompt(h, sinks, cos_p, sin_p, bp, lp)
        oa_s, sk, sv = swa_sample(h, n_p, sinks, cache_swa_k[l].reshape(bs, WINDOW, 128),
                                  cache_swa_v[l].reshape(bs, WINDOW, 128), cos_s, sin_s, bs, ls)
        bw, a_bar, cw, dsk = s5_params(ssm_a_re[l], ssm_a_im[l], ssm_b_re[l], ssm_b_im[l], ssm_c_re[l], ssm_c_im[l],
                                       ssm_d[l], ssm_log_dt[l])
        glu_b = ssm_glu_b[l].reshape(1, 2 * GROUP_WIDTH)
        ob_p, hl_p = s5_prompt(h, zeros(bp, 2 * SSM_W), bw, a_bar, cw, dsk, glu_w_b[l], glu_b, bp, lp)
        h0_s = jnp.concatenate([state_ssm_re[l].reshape(bs, SSM_W), state_ssm_im[l].reshape(bs, SSM_W)], axis=1)
        ob_s, hl_s = s5_sample(h, n_p, h0_s, bw, a_bar, cw, dsk, glu_w_b[l], glu_b, bs, ls)
        alog_t = _lane_row(gdn_a_log[l], SM_GA)
        dtb_t = _lane_row(gdn_dt_bias[l], SM_GA)
        gnw = gdn_norm_w[l].reshape(1, GDN_DV)
        oc_p, gs_p = gdn_mixer(h, 0, gdn_conv_w[l], alog_t, dtb_t, gnw, zeros(bp, GDN_CONV - 1, GDN_CONV_DIM),
                               zeros(bp, GDN_HEADS, GDN_DK, GDN_DV), bp, lp)
        oc_s, gs_s = gdn_mixer(h, n_p, gdn_conv_w[l], alog_t, dtb_t, gnw, state_gdn_conv[l], state_gdn[l], bs, ls)
        cv_p = _conv_tail(h, 0, bp, lp)
        cv_s = _conv_tail(h, n_p, bs, ls)
        lgb = gla_gate_b[l].reshape(1, 128)
        lnw = jnp.tile(gla_norm_w[l], GLA_HEADS).reshape(1, 256)
        od_p, lt_p = gla_mixer(h, 0, gla_gate_w[l], lgb, lnw, zeros(bp, 256, 128), bp, lp)
        od_s, lt_s = gla_mixer(h, n_p, gla_gate_w[l], lgb, lnw, gla_state_to_t(state_gla[l]), bs, ls)

        x1, x1t, e_pad, gates = out_proj_ln(
            x, (oa_p, ob_p.reshape(n_p, GROUP_WIDTH), oc_p, od_p), (oa_s, ob_s, oc_s, od_s), w_out_b[l],
            ln1_g[l].reshape(1, D_MODEL), ln1_b[l].reshape(1, D_MODEL), rw_pad[l], rb_pad[l].reshape(1, LANE))
        x = moe_ffn_ln(x1, x1t, e_pad, gates, wgu_all, bgu_all, wd_all, bd_all,
                       ln2_g[l].reshape(1, D_MODEL), ln2_b[l].reshape(1, D_MODEL), expert0=l * N_EXPERTS)

        st_p = (pk.reshape(bp, WINDOW, A_KV_HEADS, HEAD_DIM), pv.reshape(bp, WINDOW, A_KV_HEADS, HEAD_DIM),
                hl_p[:, :SSM_W].reshape(bp, SSM_GROUPS, SSM_STATE), hl_p[:, SSM_W:].reshape(bp, SSM_GROUPS, SSM_STATE),
                cv_p, gs_p, gla_state_from_t(lt_p))
        st_s = (sk.reshape(bs, WINDOW, A_KV_HEADS, HEAD_DIM), sv.reshape(bs, WINDOW, A_KV_HEADS, HEAD_DIM),
                hl_s[:, :SSM_W].reshape(bs, SSM_GROUPS, SSM_STATE), hl_s[:, SSM_W:].reshape(bs, SSM_GROUPS, SSM_STATE),
                cv_s, gs_s, gla_state_from_t(lt_s))
        for i in range(7):
            new_p[i].append(st_p[i])
            new_s[i].append(st_s[i])
    y_p = x[:n_p].reshape(bp, lp, D_MODEL)
    y_s = x[n_p:].reshape(bs, ls, D_MODEL)
    return (y_p, y_s) + tuple(jnp.stack(t, axis=0) for t in new_p) + tuple(jnp.stack(t, axis=0) for t in new_s)
```

```python
import functools

import numpy as np
import jax
import jax.numpy as jnp
from jax import lax
from jax.experimental import pallas as pl
from jax.experimental.pallas import tpu as pltpu

F32 = jnp.float32
BF16 = jnp.bfloat16
I32 = jnp.int32
HI = lax.Precision.HIGHEST

D_MODEL = 1024
DEPTH = 4
PAST_LEN = 8192
GROUP_WIDTH = 256
HEAD_DIM = 64
A_HEADS = 4
A_KV_HEADS = 2
WINDOW = 128
ROPE_THETA = 500000.0
ROT_DIM = 16
SSM_GC = 16
SSM_GROUPS = 16
SSM_STATE = 64
SSM_W = SSM_GROUPS * SSM_STATE
GDN_HEADS = 4
GDN_DK = 64
GDN_DV = 64
GDN_CONV = 4
GDN_CONV_DIM = 768
GDN_CHUNK = 64
GLA_HEADS = 4
GLA_DK = 32
GLA_DV = 64
GLA_RANK = 16
GLA_TAU = 16.0
GLA_CHUNK = 16
N_EXPERTS = 32
TOP_K = 4
D_FF = 1024
SWIGLU_LIMIT = 7.0
SWIGLU_ALPHA = 1.702
DN_ALPHA = (2 * DEPTH) ** 0.25
LN_EPS = 1e-5
RMS_EPS = 1e-6
N_IN = 2584

C_AQ, C_AK, C_AV, C_SU, C_GQKV, C_GZ = 0, 256, 384, 512, 768, 1536
C_LQ, C_LK, C_LV, C_LR, C_SM = 1792, 1920, 2048, 2304, 2560
NH = 2688
SM_GB, SM_GA, SM_LG = 0, 4, 8

LANE = 128
ROW_TILE = 512
MOE_TILE = 512
VMEM_LIMIT = 56 * 1024 * 1024


def _cp(*sem):
    return pltpu.CompilerParams(dimension_semantics=sem, vmem_limit_bytes=VMEM_LIMIT)


def _dot(a, b, precision=None):
    return jnp.dot(a, b, preferred_element_type=F32, precision=precision)


def _dot_nt(a, b, precision=None):
    return lax.dot_general(a, b, (((1,), (1,)), ((), ())), preferred_element_type=F32, precision=precision)


def _dot_tn(a, b, precision=None):
    return lax.dot_general(a, b, (((0,), (0,)), ((), ())), preferred_element_type=F32, precision=precision)


def _bf(x):
    return x.astype(BF16)


def _iota(shape, dim):
    return lax.broadcasted_iota(I32, shape, dim)


def _shr(idx, size):
    return lax.shift_right_logical(idx, int(size).bit_length() - 1)


def _sigmoid(x):
    return 1.0 / (1.0 + jnp.exp(-x))


def _silu(x):
    return x * _sigmoid(x)


def _softplus(x):
    return jnp.maximum(x, 0.0) + jnp.log(1.0 + jnp.exp(-jnp.abs(x)))


def _log_sigmoid(x):
    return -_softplus(-x)


def _gelu_tanh(x):
    return 0.5 * x * (1.0 + jnp.tanh(0.7978845608028654 * (x + 0.044715 * x * x * x)))


def _layer_norm(y, g, b):
    mu = jnp.mean(y, axis=-1, keepdims=True)
    yc = y - mu
    var = jnp.mean(yc * yc, axis=-1, keepdims=True)
    return yc * lax.rsqrt(var + LN_EPS) * g + b


def _in_proj_body(x_ref, w_ref, o_ref):
    xb = _bf(x_ref[...])
    for c0 in range(0, NH, 512):
        c1 = min(c0 + 512, NH)
        o_ref[:, c0:c1] = _dot(xb, w_ref[:, c0:c1])


def in_proj(x, w):
    t = x.shape[0]
    return pl.pallas_call(
        _in_proj_body,
        grid=(t // ROW_TILE,),
        in_specs=[pl.BlockSpec((ROW_TILE, D_MODEL), lambda i: (i, 0)),
                  pl.BlockSpec((D_MODEL, NH), lambda i: (0, 0))],
        out_specs=pl.BlockSpec((ROW_TILE, NH), lambda i: (i, 0)),
        out_shape=jax.ShapeDtypeStruct((t, NH), F32),
        compiler_params=_cp("arbitrary"),
        name="in_proj",
    )(x, w)


def _rope(x, cos, sin):
    w = x.shape[1]
    if w > LANE:
        cos = jnp.concatenate([cos] * (w // LANE), axis=1)
        sin = jnp.concatenate([sin] * (w // LANE), axis=1)
    lane = _iota(x.shape, 1) & (HEAD_DIM - 1)
    swapped = jnp.where(lane < ROT_DIM // 2, pltpu.roll(x, w - ROT_DIM // 2, 1), pltpu.roll(x, ROT_DIM // 2, 1))
    return x * cos + swapped * sin


def _sink_attention_multi(qkvs, mask, sinks_ref):
    pairs = [(i, hq) for i in range(len(qkvs)) for hq in range(A_HEADS)]
    hd = lambda x, j: _bf(x[:, j * HEAD_DIM:(j + 1) * HEAD_DIM])
    kv = lambda hq: hq // (A_HEADS // A_KV_HEADS)
    scores = {(i, hq): _dot_nt(hd(qkvs[i][0], hq), hd(qkvs[i][1], kv(hq))) for i, hq in pairs}
    probs = {}
    for i, hq in pairs:
        s = jnp.where(mask, scores[i, hq] * (HEAD_DIM ** -0.5), -jnp.inf)
        sink = sinks_ref[hq]
        m = jnp.maximum(jnp.max(s, axis=-1, keepdims=True), sink)
        p = jnp.exp(s - m)
        den = jnp.sum(p, axis=-1, keepdims=True) + jnp.exp(sink - m)
        probs[i, hq] = _bf(p / den)
    outs = {(i, hq): _dot(probs[i, hq], hd(qkvs[i][2], kv(hq))) for i, hq in pairs}
    return [jnp.concatenate([outs[i, hq] for hq in range(A_HEADS)], axis=1) for i in range(len(qkvs))]


def _sink_attention(q, kk, vv, mask, sinks_ref):
    return _sink_attention_multi([(q, kk, vv)], mask, sinks_ref)[0]


def _swa_prompt_body(sinks_ref, cur_ref, prev_ref, cos_ref, sin_ref, cosp_ref, sinp_ref, o_ref, ko_ref, vo_ref):
    i = pl.program_id(1)
    cur = cur_ref[...]
    q = _rope(cur[:, C_AQ:C_AQ + 256], cos_ref[...], sin_ref[...])
    k = _rope(cur[:, C_AK:C_AK + 128], cos_ref[...], sin_ref[...])
    v = cur[:, C_AV:C_AV + 128]
    prev = prev_ref[...]
    kp = _rope(prev[:, 0:128], cosp_ref[...], sinp_ref[...])
    vp = prev[:, 128:256]
    kk = jnp.concatenate([kp, k], axis=0)
    vv = jnp.concatenate([vp, v], axis=0)
    r = _iota((WINDOW, 2 * WINDOW), 0)
    j = _iota((WINDOW, 2 * WINDOW), 1)
    d = WINDOW + r - j
    mask = (d >= 0) & (d <= WINDOW) & ((j >= WINDOW) | (i > 0))
    o_ref[...] = _sink_attention(q, kk, vv, mask, sinks_ref)
    ko_ref[0] = k
    vo_ref[0] = v


def swa_prompt(h, sinks, cos_t, sin_t, bsz, seq):
    nb = seq // WINDOW
    smem = pl.BlockSpec(memory_space=pltpu.SMEM)
    tab = lambda f: pl.BlockSpec((WINDOW, LANE), f)
    return pl.pallas_call(
        _swa_prompt_body,
        grid=(bsz, nb),
        in_specs=[smem,
                  pl.BlockSpec((WINDOW, 512), lambda b, i: (b * nb + i, 0)),
                  pl.BlockSpec((WINDOW, 256), lambda b, i: (b * nb + jnp.maximum(i - 1, 0), 1)),
                  tab(lambda b, i: (i, 0)), tab(lambda b, i: (i, 0)),
                  tab(lambda b, i: (jnp.maximum(i - 1, 0), 0)), tab(lambda b, i: (jnp.maximum(i - 1, 0), 0))],
        out_specs=[pl.BlockSpec((WINDOW, 256), lambda b, i: (b * nb + i, 0)),
                   pl.BlockSpec((1, WINDOW, 128), lambda b, i: (b, 0, 0)),
                   pl.BlockSpec((1, WINDOW, 128), lambda b, i: (b, 0, 0))],
        out_shape=[jax.ShapeDtypeStruct((bsz * seq, 256), F32),
                   jax.ShapeDtypeStruct((bsz, WINDOW, 128), F32),
                   jax.ShapeDtypeStruct((bsz, WINDOW, 128), F32)],
        compiler_params=_cp("arbitrary", "arbitrary"),
        name="swa_prompt",
    )(sinks, h, h, cos_t, sin_t, cos_t, sin_t)


SWA_SB = 8


def _swa_sample_body(sinks_ref, cur_ref, kc_ref, vc_ref, cos_ref, sin_ref, o_ref, ko_ref, vo_ref, *, ls):
    cw = WINDOW
    r = _iota((ls, cw + ls), 0)
    j = _iota((ls, cw + ls), 1)
    d = cw + r - j
    mask = (d >= 0) & (d <= WINDOW)
    qkvs = []
    for b in range(SWA_SB):
        cur = cur_ref[b * ls:(b + 1) * ls, :]
        q = _rope(cur[:, C_AQ:C_AQ + 256], cos_ref[...], sin_ref[...])
        k = _rope(cur[:, C_AK:C_AK + 128], cos_ref[...], sin_ref[...])
        v = cur[:, C_AV:C_AV + 128]
        kk = jnp.concatenate([kc_ref[b], k], axis=0)
        vv = jnp.concatenate([vc_ref[b], v], axis=0)
        ko_ref[b] = kk[ls:, :]
        vo_ref[b] = vv[ls:, :]
        qkvs.append((q, kk, vv))
    for b, o in enumerate(_sink_attention_multi(qkvs, mask, sinks_ref)):
        o_ref[b * ls:(b + 1) * ls, :] = o


def swa_sample(h, row0, sinks, k_cache, v_cache, cos_t, sin_t, bsz, ls):
    rows = SWA_SB * ls
    blk0 = row0 // rows
    smem = pl.BlockSpec(memory_space=pltpu.SMEM)
    cache = pl.BlockSpec((SWA_SB, WINDOW, 128), lambda i: (i, 0, 0))
    tab = pl.BlockSpec((ls, LANE), lambda i: (0, 0))
    return pl.pallas_call(
        functools.partial(_swa_sample_body, ls=ls),
        grid=(bsz // SWA_SB,),
        in_specs=[smem, pl.BlockSpec((rows, 512), lambda i: (blk0 + i, 0)), cache, cache, tab, tab],
        out_specs=[pl.BlockSpec((rows, 256), lambda i: (i, 0)), cache, cache],
        out_shape=[jax.ShapeDtypeStruct((bsz * ls, 256), F32),
                   jax.ShapeDtypeStruct((bsz, WINDOW, 128), F32),
                   jax.ShapeDtypeStruct((bsz, WINDOW, 128), F32)],
        compiler_params=_cp("arbitrary"),
        name="swa_sample",
    )(sinks, h, k_cache, v_cache, cos_t, sin_t)


S5_NB = 2 * SSM_W // LANE


def _s5_input(u, bw_ref):
    t = _dot(_bf(u), bw_ref[...])
    return jnp.concatenate([t[:, :SSM_W], t[:, :SSM_W] + t[:, SSM_W:]], axis=1)


def _s5_output(hs, u, cw_ref, d_ref, gw_ref, gb_ref):
    hs = jnp.concatenate([hs[:, :SSM_W] + hs[:, SSM_W:], hs[:, SSM_W:]], axis=1)
    y = _dot(_bf(hs), cw_ref[...]) + d_ref[...] * u
    y = _gelu_tanh(y)
    z = _dot(_bf(y), gw_ref[...]) + gb_ref[...]
    return z[:, :GROUP_WIDTH] * _sigmoid(z[:, GROUP_WIDTH:])


def _s5_scan(s_ref, a_ref, h_init, n_steps, rows):
    nre = S5_NB // 2
    a_re = [jnp.broadcast_to(a_ref[0:1, j * LANE:(j + 1) * LANE], (rows, LANE)) for j in range(nre)]
    a_im = [jnp.broadcast_to(a_ref[1:2, j * LANE:(j + 1) * LANE], (rows, LANE)) for j in range(nre)]

    def step(t, hcar):
        out = [None] * S5_NB
        base = pl.multiple_of(t * rows, rows)
        for j in range(nre):
            hr, hi = hcar[j], hcar[j + nre]
            nr = a_re[j] * hr - a_im[j] * hi + s_ref[j, pl.ds(base, rows), :]
            ni = a_re[j] * hi + a_im[j] * hr + s_ref[j + nre, pl.ds(base, rows), :]
            s_ref[j, pl.ds(base, rows), :] = nr
            s_ref[j + nre, pl.ds(base, rows), :] = ni
            out[j], out[j + nre] = nr, ni
        return tuple(out)

    return lax.fori_loop(0, n_steps, step, tuple(h_init))


def _s5_prompt_body(*refs, nb, tl):
    u_refs = refs[:nb]
    bw_ref, a_ref, cw_ref, d_ref, gw_ref, gb_ref, h0_ref, o_ref, hl_ref, s_ref, hst_ref = refs[nb:]
    i = pl.program_id(0)

    @pl.when(i == 0)
    def _():
        hst_ref[...] = h0_ref[...]

    for b in range(nb):
        bu = _s5_input(u_refs[b][...], bw_ref)
        for j in range(S5_NB):
            s_ref[j, pl.ds(b, tl, stride=nb), :] = bu[:, j * LANE:(j + 1) * LANE]
    h_init = [hst_ref[:, j * LANE:(j + 1) * LANE] for j in range(S5_NB)]
    h_fin = _s5_scan(s_ref, a_ref, h_init, tl, nb)
    for j in range(S5_NB):
        hst_ref[:, j * LANE:(j + 1) * LANE] = h_fin[j]
    for b in range(nb):
        hs = jnp.concatenate([s_ref[j, pl.ds(b, tl, stride=nb), :] for j in range(S5_NB)], axis=1)
        o_ref[b] = _s5_output(hs, u_refs[b][...], cw_ref, d_ref, gw_ref, gb_ref)

    @pl.when(i == pl.num_programs(0) - 1)
    def _():
        hl_ref[...] = hst_ref[...]


def s5_prompt(h, h0, bw, a, cw, dsk, gw, gb, bsz, seq, tl=128):
    assert bsz == 8
    nt = seq // tl
    full = lambda shape: pl.BlockSpec(shape, lambda i: (0,) * len(shape))
    u_specs = [pl.BlockSpec((tl, GROUP_WIDTH), functools.partial(lambda i, b: (b * nt + i, C_SU // GROUP_WIDTH), b=b))
               for b in range(bsz)]
    return pl.pallas_call(
        functools.partial(_s5_prompt_body, nb=bsz, tl=tl),
        grid=(nt,),
        in_specs=u_specs + [full((GROUP_WIDTH, 2 * SSM_W)), full((2, SSM_W)), full((2 * SSM_W, GROUP_WIDTH)),
                            full((1, GROUP_WIDTH)), full((GROUP_WIDTH, 2 * GROUP_WIDTH)), full((1, 2 * GROUP_WIDTH)),
                            full((bsz, 2 * SSM_W))],
        out_specs=[pl.BlockSpec((bsz, tl, GROUP_WIDTH), lambda i: (0, i, 0)), full((bsz, 2 * SSM_W))],
        out_shape=[jax.ShapeDtypeStruct((bsz, seq, GROUP_WIDTH), F32), jax.ShapeDtypeStruct((bsz, 2 * SSM_W), F32)],
        scratch_shapes=[pltpu.VMEM((S5_NB, tl * bsz, LANE), F32), pltpu.VMEM((bsz, 2 * SSM_W), F32)],
        compiler_params=_cp("arbitrary"),
        name="s5_prompt",
    )(*([h] * bsz), bw, a, cw, dsk, gw, gb, h0)


def _s5_sample_body(u_ref, bw_ref, a_ref, cw_ref, d_ref, gw_ref, gb_ref, h0_ref, o_ref, hl_ref, s_ref, t_ref, *, bsz, ls):
    bu = _s5_input(u_ref[...], bw_ref)
    for j in range(S5_NB):
        s_ref[j] = bu[:, j * LANE:(j + 1) * LANE]
    for t in range(ls):
        for j in range(S5_NB):
            t_ref[j, pl.ds(t * bsz, bsz), :] = s_ref[j, pl.ds(t, bsz, stride=ls), :]
    h_init = [h0_ref[:, j * LANE:(j + 1) * LANE] for j in range(S5_NB)]
    h_fin = _s5_scan(t_ref, a_ref, h_init, ls, bsz)
    for j in range(S5_NB):
        hl_ref[:, j * LANE:(j + 1) * LANE] = h_fin[j]
    for t in range(ls):
        for j in range(S5_NB):
            s_ref[j, pl.ds(t, bsz, stride=ls), :] = t_ref[j, pl.ds(t * bsz, bsz), :]
    hs = jnp.concatenate([s_ref[j] for j in range(S5_NB)], axis=1)
    o_ref[...] = _s5_output(hs, u_ref[...], cw_ref, d_ref, gw_ref, gb_ref)


def s5_sample(h, row0, h0, bw, a, cw, dsk, gw, gb, bsz, ls):
    rows = bsz * ls
    full = lambda shape: pl.BlockSpec(shape, lambda i: (0,) * len(shape))
    return pl.pallas_call(
        functools.partial(_s5_sample_body, bsz=bsz, ls=ls),
        grid=(1,),
        in_specs=[pl.BlockSpec((rows, GROUP_WIDTH), lambda i: (row0 // rows, C_SU // GROUP_WIDTH)),
                  full((GROUP_WIDTH, 2 * SSM_W)), full((2, SSM_W)), full((2 * SSM_W, GROUP_WIDTH)),
                  full((1, GROUP_WIDTH)), full((GROUP_WIDTH, 2 * GROUP_WIDTH)), full((1, 2 * GROUP_WIDTH)),
                  full((bsz, 2 * SSM_W))],
        out_specs=[full((rows, GROUP_WIDTH)), full((bsz, 2 * SSM_W))],
        out_shape=[jax.ShapeDtypeStruct((rows, GROUP_WIDTH), F32), jax.ShapeDtypeStruct((bsz, 2 * SSM_W), F32)],
        scratch_shapes=[pltpu.VMEM((S5_NB, rows, LANE), F32), pltpu.VMEM((S5_NB, rows, LANE), F32)],
        compiler_params=_cp("arbitrary"),
        name="s5_sample",
    )(h, bw, a, cw, dsk, gw, gb, h0)


def s5_params(a_re, a_im, b_re, b_im, c_re, c_im, d_skip, log_dt):
    lam = lax.complex(a_re, a_im)
    delta = jnp.exp(log_dt)[:, None]
    a_bar = jnp.exp(lam * delta)
    b_bar = ((a_bar - 1.0) / lam)[..., None] * lax.complex(b_re, b_im)
    eye = jnp.eye(SSM_GROUPS, dtype=F32)
    bw_re = jnp.einsum('gpc,gh->gchp', b_bar.real, eye).reshape(GROUP_WIDTH, SSM_W)
    bw_im = jnp.einsum('gpc,gh->gchp', b_bar.imag, eye).reshape(GROUP_WIDTH, SSM_W)
    bw = jnp.concatenate([bw_re, bw_im - bw_re], axis=1).astype(BF16)
    cw_re = jnp.einsum('gcp,gh->gphc', c_re, eye).reshape(SSM_W, GROUP_WIDTH)
    cw_im = jnp.einsum('gcp,gh->gphc', c_im, eye).reshape(SSM_W, GROUP_WIDTH)
    cw = jnp.concatenate([cw_re, -(cw_re + cw_im)], axis=0).astype(BF16)
    a = jnp.stack([a_bar.real.reshape(SSM_W), a_bar.imag.reshape(SSM_W)], axis=0)
    return bw, a, cw, d_skip.reshape(1, GROUP_WIDTH)


GDN_RT = 256
GDN_ST = 64
GDN_SG = 8


def _split(x):
    hi = _bf(x)
    return hi, _bf(x - hi.astype(F32))


def _dot3(a, b):
    return _dot(a[0], b[0]) + (_dot(a[0], b[1]) + _dot(a[1], b[0]))


def _gdn_prep_body(*refs, c, sample, tiles_per_seq):
    if sample:
        x_ref, cs_ref, z_ref, sm_ref, cw_ref, alog_ref, dtb_ref = refs[:7]
        outs = refs[7:14]
        xs_ref, cb_ref = refs[14:]
    else:
        x_ref, prev_ref, cs_ref, z_ref, sm_ref, cw_ref, alog_ref, dtb_ref = refs[:8]
        outs = refs[8:15]
        (xs_ref,) = refs[15:]
    u_ref, w_ref, qk_ref, qg_ref, kd_ref, eg_ref, zs_ref = outs
    rt = GDN_RT
    x = x_ref[...]
    xs_ref[8:8 + rt, :] = x
    if sample:
        xs_ref[0:8, :] = jnp.zeros((8, GDN_CONV_DIM), F32)
        cb_ref[0:rt, :] = cs_ref[...]
        cb_ref[rt:rt + 8, :] = jnp.zeros((8, GDN_CONV_DIM), F32)
        pos = _iota((rt, GDN_CONV_DIM), 0) & (c - 1)
        shifted = lambda i: jnp.where(pos >= i, xs_ref[8 - i:8 - i + rt, :], cb_ref[8 - i:8 - i + rt, :])
    else:
        first = (pl.program_id(0) % tiles_per_seq) == 0
        xs_ref[0:8, :] = jnp.where(first, cs_ref[...], prev_ref[...])
        shifted = lambda i: xs_ref[8 - i:8 - i + rt, :]
    conv = shifted(3) * cw_ref[0:1, :]
    conv = conv + shifted(2) * cw_ref[1:2, :]
    conv = conv + shifted(1) * cw_ref[2:3, :]
    conv = conv + x * cw_ref[3:4, :]
    qkv = _silu(conv)
    zs_ref[...] = _silu(z_ref[...])

    sm = sm_ref[...]
    beta_all = _sigmoid(sm)
    g_all = -jnp.exp(alog_ref[...]) * _softplus(sm + dtb_ref[...])
    st = GDN_ST
    ri = _iota((st, st), 0)
    ci = _iota((st, st), 1)
    same = _shr(ri, c) == _shr(ci, c)
    low = (ci <= ri) & same
    tri = low.astype(F32)
    last = (ci == (ri | (c - 1))).astype(F32)
    eye = (ri == ci).astype(F32)
    lane = _iota((st, LANE), 1)
    nsub = rt // st
    probs = [(t, hh) for t in range(nsub) for hh in range(GDN_HEADS)]
    gc_alls, gl_alls = [], []
    for t in range(nsub):
        rows = slice(t * st, (t + 1) * st)
        gc_alls.append(_dot(tri, g_all[rows, :], HI))
    for t in range(nsub):
        gl_alls.append(_dot(last, gc_alls[t], HI))
        eg_ref[t * st:(t + 1) * st, :] = jnp.exp(gl_alls[t])
    qs, ks, rhss, decays, a_stricts, invs = {}, {}, {}, {}, {}, {}
    for t, hh in probs:
        rows = slice(t * st, (t + 1) * st)
        q = qkv[rows, hh * GDN_DK:(hh + 1) * GDN_DK]
        k = qkv[rows, 256 + hh * GDN_DK:256 + (hh + 1) * GDN_DK]
        v = qkv[rows, 512 + hh * GDN_DV:512 + (hh + 1) * GDN_DV]
        q = q * lax.rsqrt(jnp.sum(q * q, axis=-1, keepdims=True) + 1e-6) * (GDN_DK ** -0.5)
        k = k * lax.rsqrt(jnp.sum(k * k, axis=-1, keepdims=True) + 1e-6)
        beta = beta_all[rows, SM_GB + hh:SM_GB + hh + 1]
        gc = gc_alls[t][:, SM_GA + hh:SM_GA + hh + 1]
        gl = gl_alls[t][:, SM_GA + hh:SM_GA + hh + 1]
        sel = (lane == SM_GA + hh).astype(F32)
        gc_row = _dot_nt(sel, gc_alls[t], HI)
        decays[t, hh] = jnp.exp(jnp.where(low, gc - gc_row, -jnp.inf))
        kb = k * beta
        eg = jnp.exp(gc)
        qs[t, hh], ks[t, hh] = q, k
        rhss[t, hh] = _split(jnp.concatenate([v * beta, kb * eg], axis=1))
        a_stricts[t, hh] = jnp.where(ci < ri, _dot_nt(_bf(kb), _bf(k)) * decays[t, hh], 0.0)
        qg_ref[rows, hh * GDN_DK:(hh + 1) * GDN_DK] = q * eg
        kd_ref[rows, hh * GDN_DK:(hh + 1) * GDN_DK] = k * jnp.exp(gl - gc)
    for t, hh in probs:
        qk_ref[t * st:(t + 1) * st, hh * st:(hh + 1) * st] = _dot_nt(_bf(qs[t, hh]), _bf(ks[t, hh])) * decays[t, hh]
    for p in probs:
        invs[p] = eye - jnp.where(_shr(ri, 2) == _shr(ci, 2), a_stricts[p], 0.0)
    s = 2
    while s < c:
        pair = (_shr(ri, 2 * s) == _shr(ci, 2 * s)) & (_shr(ri, s) != _shr(ci, s))
        inv_s = {p: _split(invs[p]) for p in probs}
        mid = {p: _dot3(_split(jnp.where(pair, a_stricts[p], 0.0)), inv_s[p]) for p in probs}
        for p in probs:
            invs[p] = invs[p] - _dot3(inv_s[p], _split(mid[p]))
        s *= 2
    for t, hh in probs:
        sol = _dot3(_split(invs[t, hh]), rhss[t, hh])
        u_ref[t * st:(t + 1) * st, hh * GDN_DV:(hh + 1) * GDN_DV] = sol[:, :GDN_DV]
        w_ref[t * st:(t + 1) * st, hh * GDN_DV:(hh + 1) * GDN_DV] = sol[:, GDN_DV:]


def _gdn_scan_body(u_ref, w_ref, qk_ref, qg_ref, kd_ref, eg_ref, zs_ref, nw_ref, s0_ref, o_ref, so_ref, st_ref, *, c):
    n = pl.program_id(1)

    @pl.when(n == 0)
    def _():
        st_ref[...] = s0_ref[...]

    probs = [(s, hh) for s in range(GDN_SG) for hh in range(GDN_HEADS)]
    hsl = lambda hh: slice(hh * GDN_DV, (hh + 1) * GDN_DV)
    state_b = {(s, hh): _bf(st_ref[s, hh]) for s, hh in probs}
    v_b = {}
    for s, hh in probs:
        v_b[s, hh] = _bf(u_ref[s, :, hsl(hh)] - _dot(_bf(w_ref[s, :, hsl(hh)]), state_b[s, hh]))
    o_inter = {(s, hh): _dot(_bf(qg_ref[s, :, hsl(hh)]), state_b[s, hh]) for s, hh in probs}
    for s, hh in probs:
        col0 = (s * c) % GDN_ST
        qk = qk_ref[s, :, hh * GDN_ST + col0:hh * GDN_ST + col0 + c]
        o = o_inter[s, hh] + _dot(_bf(qk), v_b[s, hh])
        o = o * lax.rsqrt(jnp.mean(o * o, axis=-1, keepdims=True) + RMS_EPS) * nw_ref[...]
        o_ref[s, :, hsl(hh)] = o * zs_ref[s, :, hsl(hh)]
    for s, hh in probs:
        eg_last = eg_ref[s, c - 1:c, SM_GA + hh:SM_GA + hh + 1]
        st_ref[s, hh] = st_ref[s, hh] * eg_last + _dot_tn(_bf(kd_ref[s, :, hsl(hh)]), v_b[s, hh])

    @pl.when(n == pl.num_programs(1) - 1)
    def _():
        so_ref[...] = st_ref[...]


def gdn_mixer(h, row0, conv_w, alog_t, dtb_t, norm_w, conv_state, s0, bsz, seq):
    c = min(GDN_CHUNK, seq)
    sample = seq == c
    rows = bsz * seq
    rt = GDN_RT
    nt = rows // rt
    blk0 = row0 // rt
    tps = max(seq // rt, 1)
    cs_rows = jnp.pad(conv_state, ((0, 0), (8 - (GDN_CONV - 1), 0), (0, 0))).reshape(bsz * 8, GDN_CONV_DIM)
    row = lambda w, col: pl.BlockSpec((rt, w), lambda i: (blk0 + i, col))
    full = lambda shape: pl.BlockSpec(shape, lambda i: (0,) * len(shape))
    if sample:
        aux_specs = [pl.BlockSpec((rt, GDN_CONV_DIM), lambda i: (i, 0))]
        aux = [cs_rows]
        scratch = [pltpu.VMEM((8 + rt, GDN_CONV_DIM), F32), pltpu.VMEM((8 + rt, GDN_CONV_DIM), F32)]
    else:
        aux_specs = [pl.BlockSpec((8, GDN_CONV_DIM), lambda i: (jnp.maximum((row0 + i * rt) // 8 - 1, 0), C_GQKV // GDN_CONV_DIM)),
                     pl.BlockSpec((8, GDN_CONV_DIM), lambda i: (i // tps, 0))]
        aux = [h, cs_rows]
        scratch = [pltpu.VMEM((8 + rt, GDN_CONV_DIM), F32)]
    out_w = [GROUP_WIDTH] * 5 + [LANE, GROUP_WIDTH]
    prep = pl.pallas_call(
        functools.partial(_gdn_prep_body, c=c, sample=sample, tiles_per_seq=tps),
        grid=(nt,),
        in_specs=[row(GDN_CONV_DIM, C_GQKV // GDN_CONV_DIM)] + aux_specs
                 + [row(GROUP_WIDTH, C_GZ // GROUP_WIDTH), row(LANE, C_SM // LANE),
                    full((GDN_CONV, GDN_CONV_DIM)), full((1, LANE)), full((1, LANE))],
        out_specs=[pl.BlockSpec((rt, w), lambda i: (i, 0)) for w in out_w],
        out_shape=[jax.ShapeDtypeStruct((rows, w), F32) for w in out_w],
        scratch_shapes=scratch,
        compiler_params=_cp("arbitrary"),
        name="gdn_prep",
    )(h, *aux, h, h, conv_w, alog_t, dtb_t)
    prep = [a.reshape(bsz, seq, a.shape[-1]) for a in prep]
    ng, nc = bsz // GDN_SG, seq // c
    seq_blk = lambda w: pl.BlockSpec((GDN_SG, c, w), lambda g, n: (g, n, 0))
    st_spec = pl.BlockSpec((GDN_SG, GDN_HEADS, GDN_DK, GDN_DV), lambda g, n: (g, 0, 0, 0))
    o, s_fin = pl.pallas_call(
        functools.partial(_gdn_scan_body, c=c),
        grid=(ng, nc),
        in_specs=[seq_blk(w) for w in out_w] + [pl.BlockSpec((1, GDN_DV), lambda g, n: (0, 0)), st_spec],
        out_specs=[seq_blk(GROUP_WIDTH), st_spec],
        out_shape=[jax.ShapeDtypeStruct((bsz, seq, GROUP_WIDTH), F32),
                   jax.ShapeDtypeStruct((bsz, GDN_HEADS, GDN_DK, GDN_DV), F32)],
        scratch_shapes=[pltpu.VMEM((GDN_SG, GDN_HEADS, GDN_DK, GDN_DV), F32)],
        compiler_params=_cp("arbitrary", "arbitrary"),
        name="gdn_scan",
    )(*prep, norm_w, s0)
    return o.reshape(rows, GROUP_WIDTH), s_fin


GLA_RT = 256


def _dot_exact(x, w):
    hi = _bf(x)
    r1 = x - hi.astype(F32)
    mid = _bf(r1)
    lo = _bf(r1 - mid.astype(F32))
    return _dot(hi, w) + (_dot(mid, w) + _dot(lo, w))


def _gla_body(qk_ref, vr_ref, sm_ref, gw_ref, gb_ref, nw_ref, s0_ref, o_ref, so_ref,
              kb_ref, bb_ref, vb_ref, st_ref, *, c, seq):
    tr = GLA_RT
    n = pl.program_id(1)
    pad = GLA_CHUNK
    carried = seq >= tr

    @pl.when(n == 0)
    def _():
        kb_ref[0:pad, :] = jnp.zeros((pad, 128), F32)
        bb_ref[0:pad, :] = jnp.zeros((pad, 128), F32)
        vb_ref[0:pad, :] = jnp.zeros((pad, 256), F32)
        if carried:
            st_ref[...] = s0_ref[0]

    q = qk_ref[:, 0:128] * (GLA_DK ** -0.5)
    k = qk_ref[:, 128:256]
    v = vr_ref[:, 0:256]
    lg = sm_ref[:, SM_LG:SM_LG + GLA_RANK]
    log_a = _log_sigmoid(_dot(_bf(lg), _bf(gw_ref[...])) + gb_ref[...]) / GLA_TAU
    ri = _iota((tr, tr), 0)
    ci = _iota((tr, tr), 1)
    tri = _bf(((ci <= ri) & (_shr(ri, c) == _shr(ci, c))).astype(F32))
    bc = _dot_nt_exact_lhs(tri, log_a)
    kb_ref[pad:pad + tr, :] = k
    bb_ref[pad:pad + tr, :] = bc
    vb_ref[pad:pad + tr, :] = v
    ones_kv = _bf((_shr(_iota((128, 256), 0), GLA_DK) == _shr(_iota((128, 256), 1), GLA_DV)).astype(F32))
    pos = _iota((tr, 128), 0) & (c - 1)
    ps = []
    for dl in range(c):
        ks = kb_ref[pad - dl:pad - dl + tr, :]
        bs = bb_ref[pad - dl:pad - dl + tr, :]
        ps.append(jnp.where(pos >= dl, q * ks * jnp.exp(bc - bs), 0.0))
    att = _dot_exact(jnp.concatenate(ps, axis=0), ones_kv)
    o_intra = jnp.zeros((tr, 256), F32)
    for dl in range(c):
        vs = vb_ref[pad - dl:pad - dl + tr, :]
        o_intra = o_intra + _bf(att[dl * tr:(dl + 1) * tr, :]).astype(F32) * _bf(vs).astype(F32)
    bd = (_shr(_iota((256, 128), 0), GLA_DV) == _shr(_iota((256, 128), 1), GLA_DK)).astype(F32)
    nch = tr // c
    sls = [slice(ch * c, (ch + 1) * c) for ch in range(nch)]
    b_last = [bc[sl, :][c - 1:c, :] for sl in sls]
    incs = [bd * _dot_tn(_bf(v[sl, :]), _bf(k[sl, :] * jnp.exp(bl - bc[sl, :]))) for sl, bl in zip(sls, b_last)]
    outs = []
    st = st_ref[...] if carried else None
    for ch, sl in enumerate(sls):
        if not carried:
            st = s0_ref[(ch * c) // seq]
        outs.append(_dot_nt(_bf(q[sl, :] * jnp.exp(bc[sl, :])), _bf(st)) + o_intra[sl, :])
        st = st * jnp.exp(b_last[ch]) + incs[ch]
        if not carried:
            so_ref[(ch * c) // seq] = st
    if carried:
        st_ref[...] = st
        so_ref[0] = st
    o = jnp.concatenate(outs, axis=0)
    ones_vv = _bf((_shr(_iota((256, 256), 0), GLA_DV) == _shr(_iota((256, 256), 1), GLA_DV)).astype(F32))
    ms = _dot_exact(o * o, ones_vv) * (1.0 / GLA_DV)
    o_ref[...] = o * lax.rsqrt(ms + RMS_EPS) * nw_ref[...] * _silu(vr_ref[:, 256:512])


def _dot_nt_exact_lhs(w, x):
    hi = _bf(x)
    r1 = x - hi.astype(F32)
    mid = _bf(r1)
    lo = _bf(r1 - mid.astype(F32))
    return _dot(w, hi) + (_dot(w, mid) + _dot(w, lo))


def gla_mixer(h, row0, gate_w, gate_b, norm_w4, s0t, bsz, seq):
    c = min(GLA_CHUNK, seq)
    tr = GLA_RT
    spt = max(tr // seq, 1)
    nt = max(seq // tr, 1)
    ng = bsz // spt
    blk0 = row0 // tr
    row = lambda w, col: pl.BlockSpec((tr, w), lambda g, n: (blk0 + g * nt + n, col))
    full = lambda shape: pl.BlockSpec(shape, lambda g, n: (0,) * len(shape))
    st_spec = pl.BlockSpec((spt, 256, 128), lambda g, n: (g, 0, 0))
    return pl.pallas_call(
        functools.partial(_gla_body, c=c, seq=seq),
        grid=(ng, nt),
        in_specs=[row(256, C_LQ // 256), row(512, C_LV // 512), row(LANE, C_SM // LANE),
                  full((GLA_RANK, 128)), full((1, 128)), full((1, 256)), st_spec],
        out_specs=[pl.BlockSpec((tr, 256), lambda g, n: (g * nt + n, 0)), st_spec],
        out_shape=[jax.ShapeDtypeStruct((bsz * seq, 256), F32), jax.ShapeDtypeStruct((bsz, 256, 128), F32)],
        scratch_shapes=[pltpu.VMEM((GLA_CHUNK + tr, 128), F32), pltpu.VMEM((GLA_CHUNK + tr, 128), F32),
                        pltpu.VMEM((GLA_CHUNK + tr, 256), F32), pltpu.VMEM((256, 128), F32)],
        compiler_params=_cp("arbitrary", "arbitrary"),
        name="gla",
    )(h, h, h, gate_w, gate_b, norm_w4, s0t)


def gla_state_to_t(s):
    eye = jnp.eye(GLA_HEADS, dtype=s.dtype)
    return jnp.einsum('bhde,hg->bhegd', s, eye).reshape(s.shape[0], 256, 128)


def gla_state_from_t(st):
    b = st.shape[0]
    t5 = st.reshape(b, GLA_HEADS, GLA_DV, GLA_HEADS, GLA_DK)
    diag = jnp.stack([t5[:, hh, :, hh, :] for hh in range(GLA_HEADS)], axis=1)
    return jnp.swapaxes(diag, 2, 3)


TOK_SUB = D_MODEL // LANE


def _to_token_tiles(ref, row0, y):
    n = y.shape[0]
    for s in range(TOK_SUB):
        ref[pl.ds(row0 * TOK_SUB + s, n, stride=TOK_SUB), :] = y[:, s * LANE:(s + 1) * LANE]


def _from_token_tiles(ref, row0, n, s):
    return ref[pl.ds(row0 * TOK_SUB + s, n, stride=TOK_SUB), :]


def _out_proj_body(x_ref, ap, bp, cp, dp, a_s, b_s, c_s, d_s, w_ref, g_ref, b_ref, rw_ref, rb_ref,
                   o_ref, t_ref, e_ref, gt_ref, *, n_prompt_tiles):
    i = pl.program_id(0)

    def run(refs):
        acc = DN_ALPHA * x_ref[...]
        for m, r in enumerate(refs):
            acc = acc + _dot(_bf(r[...]), w_ref[m * GROUP_WIDTH:(m + 1) * GROUP_WIDTH, :])
        y = _layer_norm(acc, g_ref[...], b_ref[...])
        o_ref[...] = y
        _to_token_tiles(t_ref, 0, y)
        e_ref[...], gt_ref[...] = _route(y, rw_ref, rb_ref)

    @pl.when(i < n_prompt_tiles)
    def _():
        run((ap, bp, cp, dp))

    @pl.when(i >= n_prompt_tiles)
    def _():
        run((a_s, b_s, c_s, d_s))


def out_proj_ln(x, mix_p, mix_s, w, g, b, rw, rb):
    t = x.shape[0]
    npt = mix_p[0].shape[0] // ROW_TILE
    nst = mix_s[0].shape[0] // ROW_TILE
    p_spec = pl.BlockSpec((ROW_TILE, GROUP_WIDTH), lambda i: (jnp.minimum(i, npt - 1), 0))
    s_spec = pl.BlockSpec((ROW_TILE, GROUP_WIDTH), lambda i: (jnp.clip(i - npt, 0, nst - 1), 0))
    full = lambda shape: pl.BlockSpec(shape, lambda i: (0,) * len(shape))
    return pl.pallas_call(
        functools.partial(_out_proj_body, n_prompt_tiles=npt),
        grid=(t // ROW_TILE,),
        in_specs=[pl.BlockSpec((ROW_TILE, D_MODEL), lambda i: (i, 0))] + [p_spec] * 4 + [s_spec] * 4
                 + [full((D_MODEL, D_MODEL)), full((1, D_MODEL)), full((1, D_MODEL)), full((D_MODEL, LANE)), full((1, LANE))],
        out_specs=[pl.BlockSpec((ROW_TILE, D_MODEL), lambda i: (i, 0)),
                   pl.BlockSpec((ROW_TILE * TOK_SUB, LANE), lambda i: (i, 0)),
                   pl.BlockSpec((ROW_TILE, LANE), lambda i: (i, 0)), pl.BlockSpec((ROW_TILE, LANE), lambda i: (i, 0))],
        out_shape=[jax.ShapeDtypeStruct((t, D_MODEL), F32), jax.ShapeDtypeStruct((t * TOK_SUB, LANE), F32),
                   jax.ShapeDtypeStruct((t, LANE), I32), jax.ShapeDtypeStruct((t, LANE), F32)],
        compiler_params=_cp("arbitrary"),
        name="out_proj_ln",
    )(x, *mix_p, *mix_s, w, g, b, rw, rb)


def _route(x, w_ref, b_ref):
    logits = _dot(_bf(x), _bf(w_ref[...])) + b_ref[...]
    lane = _iota(logits.shape, 1)
    vals = jnp.where(lane < N_EXPERTS, logits, -jnp.inf)
    eo = jnp.zeros(logits.shape, I32)
    top = []
    for k in range(TOP_K):
        m = jnp.max(vals, axis=-1, keepdims=True)
        idx = jnp.min(jnp.where(vals == m, lane.astype(F32), float(LANE)), axis=-1, keepdims=True).astype(I32)
        eo = jnp.where(lane == k, idx, eo)
        top.append(m)
        vals = jnp.where(lane == idx, -jnp.inf, vals)
    ex = [jnp.exp(v - top[0]) for v in top]
    den = ex[0] + ex[1] + ex[2] + ex[3]
    go = jnp.zeros(logits.shape, F32)
    for k in range(TOP_K):
        go = jnp.where(lane == k, ex[k] / den, go)
    return eo, go


DISPATCH_TILE = 256


def _tok(i):
    return pl.ds(pl.multiple_of(i * TOK_SUB, TOK_SUB), TOK_SUB)


def _dispatch_copy(x_ref, xs_ref, sem, r, d):
    return pltpu.make_async_copy(x_ref.at[_tok(r), :], xs_ref.at[_tok(d), :], sem)


def _dispatch_body(dest_ref, x_ref, xs_in_ref, xs_ref, sem):
    del xs_in_ref

    def start(r, carry):
        for k in range(TOP_K):
            _dispatch_copy(x_ref, xs_ref, sem, r, dest_ref[0, 0, r * TOP_K + k]).start(priority=k % 2)
        return carry

    lax.fori_loop(0, DISPATCH_TILE, start, 0)

    def wait(r, carry):
        for k in range(TOP_K):
            _dispatch_copy(x_ref, xs_ref, sem, 0, 0).wait()
        return carry

    lax.fori_loop(0, DISPATCH_TILE, wait, 0)


def moe_dispatch(xt, dest, n_rows, xs_buf=None):
    t = xt.shape[0] // TOK_SUB
    nt = t // DISPATCH_TILE
    zeros = jnp.zeros((n_rows * TOK_SUB, LANE), F32) if xs_buf is None else xs_buf
    return pl.pallas_call(
        _dispatch_body,
        grid=(nt,),
        in_specs=[pl.BlockSpec((1, 1, DISPATCH_TILE * TOP_K), lambda i: (i, 0, 0), memory_space=pltpu.SMEM),
                  pl.BlockSpec((DISPATCH_TILE * TOK_SUB, LANE), lambda i: (i, 0)),
                  pl.BlockSpec(memory_space=pl.ANY)],
        out_specs=pl.BlockSpec(memory_space=pl.ANY),
        out_shape=jax.ShapeDtypeStruct((n_rows * TOK_SUB, LANE), F32),
        scratch_shapes=[pltpu.SemaphoreType.DMA(())],
        input_output_aliases={2: 0},
        compiler_params=_cp("arbitrary"),
        name="moe_dispatch",
    )(dest.reshape(nt, 1, DISPATCH_TILE * TOP_K), xt, zeros)


EXPERT_SUB = 256


def _expert_body(te_ref, tv_ref, xs_ref, wgu_ref, bgu_ref, wd_ref, bd_ref, ys_ref, wgu_s, wd2_s, tmp_s):
    i = pl.program_id(0)
    valid = tv_ref[i] != 0
    changed = (i == 0) | (te_ref[i] != te_ref[jnp.maximum(i - 1, 0)])

    @pl.when(i == 0)
    def _():
        tmp_s[...] = jnp.zeros(tmp_s.shape, F32)

    @pl.when(valid & changed)
    def _():
        for c0 in range(0, 2 * D_FF, 512):
            wgu_s[:, c0:c0 + 512] = _bf(wgu_ref[0, :, c0:c0 + 512])
        for j in range(D_MODEL // LANE):
            tmp_s[pl.ds(0, D_FF, stride=2), :] = wd_ref[0, :, j * LANE:(j + 1) * LANE]
            wd2_s[:, j * LANE:(j + 1) * LANE] = _bf(tmp_s[...])

    @pl.when(valid)
    def _():
        for r0 in range(0, MOE_TILE, EXPERT_SUB):
            xb = jnp.concatenate([_bf(_from_token_tiles(xs_ref, r0, EXPERT_SUB, s)) for s in range(TOK_SUB)], axis=1)
            gu = _dot(xb, wgu_s[...]) + bgu_ref[0]
            nxt = pltpu.roll(gu, 2 * D_FF - 1, 1)
            gate = jnp.minimum(gu, SWIGLU_LIMIT)
            up = jnp.clip(nxt, -SWIGLU_LIMIT, SWIGLU_LIMIT)
            act = gate * _sigmoid(gate * SWIGLU_ALPHA) * (up + 1.0)
            even = (_iota(act.shape, 1) & 1) == 0
            act = jnp.where(even, act, 0.0)
            _to_token_tiles(ys_ref, r0, _dot(_bf(act), wd2_s[...]) + bd_ref[0])

    @pl.when(jnp.logical_not(valid))
    def _():
        ys_ref[...] = jnp.zeros(ys_ref.shape, F32)


def moe_experts(xs, tile_e, tile_valid, wgu, bgu, wd, bd):
    n_rows = xs.shape[0] // TOK_SUB
    nt = n_rows // MOE_TILE
    grid_spec = pltpu.PrefetchScalarGridSpec(
        num_scalar_prefetch=2,
        grid=(nt,),
        in_specs=[pl.BlockSpec((MOE_TILE * TOK_SUB, LANE), lambda i, te, tv: (i, 0)),
                  pl.BlockSpec((1, D_MODEL, 2 * D_FF), lambda i, te, tv: (te[i], 0, 0)),
                  pl.BlockSpec((1, 1, 2 * D_FF), lambda i, te, tv: (te[i], 0, 0)),
                  pl.BlockSpec((1, D_FF, D_MODEL), lambda i, te, tv: (te[i], 0, 0)),
                  pl.BlockSpec((1, 1, D_MODEL), lambda i, te, tv: (te[i], 0, 0))],
        out_specs=pl.BlockSpec((MOE_TILE * TOK_SUB, LANE), lambda i, te, tv: (i, 0)),
        scratch_shapes=[pltpu.VMEM((D_MODEL, 2 * D_FF), BF16), pltpu.VMEM((2 * D_FF, D_MODEL), BF16),
                        pltpu.VMEM((2 * D_FF, LANE), F32)],
    )
    return pl.pallas_call(
        _expert_body,
        grid_spec=grid_spec,
        out_shape=jax.ShapeDtypeStruct((n_rows * TOK_SUB, LANE), F32),
        compiler_params=_cp("arbitrary"),
        name="moe_experts",
    )(tile_e, tile_valid, xs, wgu, bgu, wd, bd)


def _combine_copy(ys_ref, buf_ref, sem, d, k, r):
    return pltpu.make_async_copy(ys_ref.at[_tok(d), :], buf_ref.at[k, _tok(r), :], sem)


def _combine_body(dest_ref, gates_ref, x_ref, g_ref, b_ref, ys_ref, o_ref, buf_ref, sem):
    def start(r, carry):
        for k in range(TOP_K):
            _combine_copy(ys_ref, buf_ref, sem, dest_ref[0, 0, r * TOP_K + k], k, r).start(priority=k % 2)
        return carry

    lax.fori_loop(0, DISPATCH_TILE, start, 0)

    def wait(r, carry):
        for k in range(TOP_K):
            _combine_copy(ys_ref, buf_ref, sem, 0, k, 0).wait()
        return carry

    lax.fori_loop(0, DISPATCH_TILE, wait, 0)
    gates = gates_ref[...]
    cols = []
    for s in range(TOK_SUB):
        acc = DN_ALPHA * x_ref[:, s * LANE:(s + 1) * LANE]
        for k in range(TOP_K):
            acc = acc + gates[:, k:k + 1] * buf_ref[k, pl.ds(s, DISPATCH_TILE, stride=TOK_SUB), :]
        cols.append(acc)
    o_ref[...] = _layer_norm(jnp.concatenate(cols, axis=1), g_ref[...], b_ref[...])


def moe_combine_ln(x, ys, dest, gates, g, b):
    t = x.shape[0]
    nt = t // DISPATCH_TILE
    full = lambda shape: pl.BlockSpec(shape, lambda i: (0,) * len(shape))
    return pl.pallas_call(
        _combine_body,
        grid=(nt,),
        in_specs=[pl.BlockSpec((1, 1, DISPATCH_TILE * TOP_K), lambda i: (i, 0, 0), memory_space=pltpu.SMEM),
                  pl.BlockSpec((DISPATCH_TILE, LANE), lambda i: (i, 0)),
                  pl.BlockSpec((DISPATCH_TILE, D_MODEL), lambda i: (i, 0)),
                  full((1, D_MODEL)), full((1, D_MODEL)),
                  pl.BlockSpec(memory_space=pl.ANY)],
        out_specs=pl.BlockSpec((DISPATCH_TILE, D_MODEL), lambda i: (i, 0)),
        out_shape=jax.ShapeDtypeStruct((t, D_MODEL), F32),
        scratch_shapes=[pltpu.VMEM((TOP_K, DISPATCH_TILE * TOK_SUB, LANE), F32), pltpu.SemaphoreType.DMA(())],
        compiler_params=_cp("arbitrary"),
        name="moe_combine_ln",
    )(dest.reshape(nt, 1, DISPATCH_TILE * TOP_K), gates, x, g, b, ys)


def _rank_body(e_ref, rank_ref, cnt_ref, run_ref):
    i = pl.program_id(0)

    @pl.when(i == 0)
    def _():
        run_ref[...] = jnp.zeros(run_ref.shape, F32)

    e = e_ref[...]
    lane = _iota(e.shape, 1)
    hot = [lane == e[:, k:k + 1] for k in range(TOP_K)]
    m = sum(h.astype(F32) for h in hot)
    ri = _iota((ROW_TILE, ROW_TILE), 0)
    ci = _iota((ROW_TILE, ROW_TILE), 1)
    before = _dot(_bf((ci < ri).astype(F32)), _bf(m)) + run_ref[0:1, :]
    rank = jnp.zeros(e.shape, F32)
    for k in range(TOP_K):
        rank = jnp.where(lane == k, jnp.sum(jnp.where(hot[k], before, 0.0), axis=-1, keepdims=True), rank)
    rank_ref[...] = rank.astype(I32)
    run_ref[0:1, :] = run_ref[0:1, :] + jnp.sum(m, axis=0, keepdims=True)
    cnt_ref[...] = run_ref[...].astype(I32)


def moe_rank(e_pad):
    t = e_pad.shape[0]
    return pl.pallas_call(
        _rank_body,
        grid=(t // ROW_TILE,),
        in_specs=[pl.BlockSpec((ROW_TILE, LANE), lambda i: (i, 0))],
        out_specs=[pl.BlockSpec((ROW_TILE, LANE), lambda i: (i, 0)), pl.BlockSpec((8, LANE), lambda i: (0, 0))],
        out_shape=[jax.ShapeDtypeStruct((t, LANE), I32), jax.ShapeDtypeStruct((8, LANE), I32)],
        scratch_shapes=[pltpu.VMEM((8, LANE), F32)],
        compiler_params=_cp("arbitrary"),
        name="moe_rank",
    )(e_pad)


def moe_plan(e_pad, n_tokens):
    tk = n_tokens * TOP_K
    rank_pad, cnt = moe_rank(e_pad)
    top_e, rank = e_pad[:, :TOP_K], rank_pad[:, :TOP_K]
    counts = cnt[0, :N_EXPERTS]
    ntile = (counts + MOE_TILE - 1) // MOE_TILE
    tile_end = jnp.cumsum(ntile)
    tile_start = tile_end - ntile
    start_of = jnp.sum(jnp.where(top_e[..., None] == jnp.arange(N_EXPERTS, dtype=I32), tile_start, 0), axis=-1)
    dest = (start_of * MOE_TILE + rank).reshape(tk)
    n_tiles = -(-tk // MOE_TILE) + N_EXPERTS
    tiles = jnp.arange(n_tiles, dtype=I32)
    tile_e = jnp.minimum(jnp.searchsorted(tile_end, tiles, side='right'), N_EXPERTS - 1).astype(I32)
    tile_valid = (tiles < tile_end[-1]).astype(I32)
    return dest.astype(I32), tile_e, tile_valid, n_tiles * MOE_TILE


def moe_ffn_ln(x1, x1t, e_pad, gates, wgu, bgu, wd, bd, g, b, expert0=0, xs_buf=None):
    t = x1.shape[0]
    dest, tile_e, tile_valid, n_rows = moe_plan(e_pad, t)
    xs = moe_dispatch(x1t, dest, n_rows, xs_buf)
    ys = moe_experts(xs, tile_e + expert0, tile_valid, wgu, bgu, wd, bd)
    return moe_combine_ln(x1, ys, dest, gates, g, b), xs


def _rope_tables(pos):
    half = ROT_DIM // 2
    inv_freq = ROPE_THETA ** (-jnp.arange(half, dtype=F32) / half)
    ang = pos.astype(F32)[:, None] * inv_freq[None, :]
    cos, sin = jnp.cos(ang), jnp.sin(ang)
    n = pos.shape[0]
    ones = jnp.ones((n, HEAD_DIM - ROT_DIM), F32)
    cos_h = jnp.concatenate([cos, cos, ones], axis=1)
    sin_h = jnp.concatenate([-sin, sin, 0.0 * ones], axis=1)
    return jnp.concatenate([cos_h, cos_h], axis=1), jnp.concatenate([sin_h, sin_h], axis=1)


def _conv_tail(h, row0, bsz, seq):
    assert seq >= GDN_CONV - 1
    n = GDN_CONV - 1
    if bsz <= 8:
        return jnp.stack([h[row0 + (b + 1) * seq - n:row0 + (b + 1) * seq, C_GQKV:C_GQKV + GDN_CONV_DIM]
                          for b in range(bsz)], axis=0)
    blk = h[row0:row0 + bsz * seq, C_GQKV:C_GQKV + GDN_CONV_DIM].reshape(bsz, seq, GDN_CONV_DIM)
    return blk[:, seq - n:, :]


def _lane_row(v, offset):
    return jnp.zeros((1, LANE), F32).at[0, offset:offset + v.shape[0]].set(v)


def kernel(x_prompt, x_sample, cache_swa_k, cache_swa_v, state_ssm_re, state_ssm_im, state_gdn_conv, state_gdn, state_gla, w_in, w_out, attn_sinks, ssm_a_re, ssm_a_im, ssm_b_re, ssm_b_im, ssm_c_re, ssm_c_im, ssm_d, ssm_log_dt, ssm_glu_w, ssm_glu_b, gdn_conv_w, gdn_a_log, gdn_dt_bias, gdn_norm_w, gla_gate_w, gla_gate_b, gla_norm_w, ln1_g, ln1_b, ln2_g, ln2_b, router_w, router_b, moe_w_gate_up, moe_b_gate_up, moe_w_down, moe_b_down):
    bp, lp, _ = x_prompt.shape
    bs, ls, _ = x_sample.shape
    n_p, n_s = bp * lp, bs * ls
    depth = w_in.shape[0]

    w_in_r = jnp.concatenate([w_in[..., :1792], w_in[..., 1800:2568], w_in[..., 1792:1800], w_in[..., 2568:N_IN],
                              jnp.zeros(w_in.shape[:2] + (NH - N_IN,), w_in.dtype)], axis=-1).astype(BF16)
    w_out_b = w_out.astype(BF16)
    glu_w_b = ssm_glu_w.astype(BF16)
    rw_pad = jnp.pad(router_w, ((0, 0), (0, 0), (0, LANE - N_EXPERTS)))
    rb_pad = jnp.pad(router_b, ((0, 0), (0, LANE - N_EXPERTS)))
    wgu_all = moe_w_gate_up.reshape(depth * N_EXPERTS, D_MODEL, 2 * D_FF)
    bgu_all = moe_b_gate_up.reshape(depth * N_EXPERTS, 1, 2 * D_FF)
    wd_all = moe_w_down.reshape(depth * N_EXPERTS, D_FF, D_MODEL)
    bd_all = moe_b_down.reshape(depth * N_EXPERTS, 1, D_MODEL)

    cos_p, sin_p = _rope_tables(jnp.arange(lp, dtype=I32))
    cos_s, sin_s = _rope_tables(PAST_LEN + jnp.arange(ls, dtype=I32))

    x = jnp.concatenate([x_prompt.reshape(n_p, D_MODEL), x_sample.reshape(n_s, D_MODEL)], axis=0)
    zeros = lambda *s: jnp.zeros(s, F32)
    new_p = [[] for _ in range(7)]
    new_s = [[] for _ in range(7)]
    xs_buf = None
    for l in range(depth):
        h = in_proj(x, w_in_r[l])
        sinks = attn_sinks[l]
        oa_p, pk, pv = swa_prompt(h, sinks, cos_p, sin_p, bp, lp)
        oa_s, sk, sv = swa_sample(h, n_p, sinks, cache_swa_k[l].reshape(bs, WINDOW, 128),
                                  cache_swa_v[l].reshape(bs, WINDOW, 128), cos_s, sin_s, bs, ls)
        bw, a_bar, cw, dsk = s5_params(ssm_a_re[l], ssm_a_im[l], ssm_b_re[l], ssm_b_im[l], ssm_c_re[l], ssm_c_im[l],
                                       ssm_d[l], ssm_log_dt[l])
        glu_b = ssm_glu_b[l].reshape(1, 2 * GROUP_WIDTH)
        ob_p, hl_p = s5_prompt(h, zeros(bp, 2 * SSM_W), bw, a_bar, cw, dsk, glu_w_b[l], glu_b, bp, lp)
        h0_s = jnp.concatenate([state_ssm_re[l].reshape(bs, SSM_W), state_ssm_im[l].reshape(bs, SSM_W)], axis=1)
        ob_s, hl_s = s5_sample(h, n_p, h0_s, bw, a_bar, cw, dsk, glu_w_b[l], glu_b, bs, ls)
        alog_t = _lane_row(gdn_a_log[l], SM_GA)
        dtb_t = _lane_row(gdn_dt_bias[l], SM_GA)
        gnw = gdn_norm_w[l].reshape(1, GDN_DV)
        oc_p, gs_p = gdn_mixer(h, 0, gdn_conv_w[l], alog_t, dtb_t, gnw, zeros(bp, GDN_CONV - 1, GDN_CONV_DIM),
                               zeros(bp, GDN_HEADS, GDN_DK, GDN_DV), bp, lp)
        oc_s, gs_s = gdn_mixer(h, n_p, gdn_conv_w[l], alog_t, dtb_t, gnw, state_gdn_conv[l], state_gdn[l], bs, ls)
        cv_p = _conv_tail(h, 0, bp, lp)
        cv_s = _conv_tail(h, n_p, bs, ls)
        lgb = gla_gate_b[l].reshape(1, 128)
        lnw = jnp.tile(gla_norm_w[l], GLA_HEADS).reshape(1, 256)
        od_p, lt_p = gla_mixer(h, 0, gla_gate_w[l], lgb, lnw, zeros(bp, 256, 128), bp, lp)
        od_s, lt_s = gla_mixer(h, n_p, gla_gate_w[l], lgb, lnw, gla_state_to_t(state_gla[l]), bs, ls)

        x1, x1t, e_pad, gates = out_proj_ln(
            x, (oa_p, ob_p.reshape(n_p, GROUP_WIDTH), oc_p, od_p), (oa_s, ob_s, oc_s, od_s), w_out_b[l],
            ln1_g[l].reshape(1, D_MODEL), ln1_b[l].reshape(1, D_MODEL), rw_pad[l], rb_pad[l].reshape(1, LANE))
        x, xs_buf = moe_ffn_ln(x1, x1t, e_pad, gates, wgu_all, bgu_all, wd_all, bd_all, ln2_g[l].reshape(1, D_MODEL),
                               ln2_b[l].reshape(1, D_MODEL), expert0=l * N_EXPERTS, xs_buf=xs_buf)

        st_p = (pk.reshape(bp, WINDOW, A_KV_HEADS, HEAD_DIM), pv.reshape(bp, WINDOW, A_KV_HEADS, HEAD_DIM),
                hl_p[:, :SSM_W].reshape(bp, SSM_GROUPS, SSM_STATE), hl_p[:, SSM_W:].reshape(bp, SSM_GROUPS, SSM_STATE),
                cv_p, gs_p, gla_state_from_t(lt_p))
        st_s = (sk.reshape(bs, WINDOW, A_KV_HEADS, HEAD_DIM), sv.reshape(bs, WINDOW, A_KV_HEADS, HEAD_DIM),
                hl_s[:, :SSM_W].reshape(bs, SSM_GROUPS, SSM_STATE), hl_s[:, SSM_W:].reshape(bs, SSM_GROUPS, SSM_STATE),
                cv_s, gs_s, gla_state_from_t(lt_s))
        for i in range(7):
            new_p[i].append(st_p[i])
            new_s[i].append(st_s[i])
    y_p = x[:n_p].reshape(bp, lp, D_MODEL)
    y_s = x[n_p:].reshape(bs, ls, D_MODEL)
    return (y_p, y_s) + tuple(jnp.stack(t, axis=0) for t in new_p) + tuple(jnp.stack(t, axis=0) for t in new_s)
```

```python
import functools

import numpy as np
import jax
import jax.numpy as jnp
from jax import lax
from jax.experimental import pallas as pl
from jax.experimental.pallas import tpu as pltpu

F32 = jnp.float32
BF16 = jnp.bfloat16
I32 = jnp.int32
HI = lax.Precision.HIGHEST

D_MODEL = 1024
DEPTH = 4
PAST_LEN = 8192
GROUP_WIDTH = 256
HEAD_DIM = 64
A_HEADS = 4
A_KV_HEADS = 2
WINDOW = 128
ROPE_THETA = 500000.0
ROT_DIM = 16
SSM_GC = 16
SSM_GROUPS = 16
SSM_STATE = 64
SSM_W = SSM_GROUPS * SSM_STATE
GDN_HEADS = 4
GDN_DK = 64
GDN_DV = 64
GDN_CONV = 4
GDN_CONV_DIM = 768
GDN_CHUNK = 64
GLA_HEADS = 4
GLA_DK = 32
GLA_DV = 64
GLA_RANK = 16
GLA_TAU = 16.0
GLA_CHUNK = 16
N_EXPERTS = 32
TOP_K = 4
D_FF = 1024
SWIGLU_LIMIT = 7.0
SWIGLU_ALPHA = 1.702
DN_ALPHA = (2 * DEPTH) ** 0.25
LN_EPS = 1e-5
RMS_EPS = 1e-6
N_IN = 2584

C_AQ, C_AK, C_AV, C_SU, C_GQKV, C_GZ = 0, 256, 384, 512, 768, 1536
C_LQ, C_LK, C_LV, C_LR, C_SM = 1792, 1920, 2048, 2304, 2560
NH = 2688
SM_GB, SM_GA, SM_LG = 0, 4, 8

LANE = 128
ROW_TILE = 512
MOE_TILE = 512
VMEM_LIMIT = 56 * 1024 * 1024


def _cp(*sem):
    return pltpu.CompilerParams(dimension_semantics=sem, vmem_limit_bytes=VMEM_LIMIT)


def _dot(a, b, precision=None):
    return jnp.dot(a, b, preferred_element_type=F32, precision=precision)


def _dot_nt(a, b, precision=None):
    return lax.dot_general(a, b, (((1,), (1,)), ((), ())), preferred_element_type=F32, precision=precision)


def _dot_tn(a, b, precision=None):
    return lax.dot_general(a, b, (((0,), (0,)), ((), ())), preferred_element_type=F32, precision=precision)


def _bf(x):
    return x.astype(BF16)


def _iota(shape, dim):
    return lax.broadcasted_iota(I32, shape, dim)


def _shr(idx, size):
    return lax.shift_right_logical(idx, int(size).bit_length() - 1)


def _sigmoid(x):
    return 1.0 / (1.0 + jnp.exp(-x))


def _silu(x):
    return x * _sigmoid(x)


def _softplus(x):
    return jnp.maximum(x, 0.0) + jnp.log(1.0 + jnp.exp(-jnp.abs(x)))


def _log_sigmoid(x):
    return -_softplus(-x)


def _gelu_tanh(x):
    return 0.5 * x * (1.0 + jnp.tanh(0.7978845608028654 * (x + 0.044715 * x * x * x)))


def _layer_norm(y, g, b):
    mu = jnp.mean(y, axis=-1, keepdims=True)
    yc = y - mu
    var = jnp.mean(yc * yc, axis=-1, keepdims=True)
    return yc * lax.rsqrt(var + LN_EPS) * g + b


def _in_proj_body(x_ref, w_ref, o_ref):
    xb = _bf(x_ref[...])
    for c0 in range(0, NH, 512):
        c1 = min(c0 + 512, NH)
        o_ref[:, c0:c1] = _dot(xb, w_ref[:, c0:c1])


def in_proj(x, w):
    t = x.shape[0]
    return pl.pallas_call(
        _in_proj_body,
        grid=(t // ROW_TILE,),
        in_specs=[pl.BlockSpec((ROW_TILE, D_MODEL), lambda i: (i, 0)),
                  pl.BlockSpec((D_MODEL, NH), lambda i: (0, 0))],
        out_specs=pl.BlockSpec((ROW_TILE, NH), lambda i: (i, 0)),
        out_shape=jax.ShapeDtypeStruct((t, NH), F32),
        compiler_params=_cp("arbitrary"),
        name="in_proj",
    )(x, w)


def _rope(x, cos, sin):
    w = x.shape[1]
    if w > LANE:
        cos = jnp.concatenate([cos] * (w // LANE), axis=1)
        sin = jnp.concatenate([sin] * (w // LANE), axis=1)
    lane = _iota(x.shape, 1) & (HEAD_DIM - 1)
    swapped = jnp.where(lane < ROT_DIM // 2, pltpu.roll(x, w - ROT_DIM // 2, 1), pltpu.roll(x, ROT_DIM // 2, 1))
    return x * cos + swapped * sin


def _sink_attention_multi(qkvs, mask, sinks_ref):
    pairs = [(i, hq) for i in range(len(qkvs)) for hq in range(A_HEADS)]
    hd = lambda x, j: _bf(x[:, j * HEAD_DIM:(j + 1) * HEAD_DIM])
    kv = lambda hq: hq // (A_HEADS // A_KV_HEADS)
    scores = {(i, hq): _dot_nt(hd(qkvs[i][0], hq), hd(qkvs[i][1], kv(hq))) for i, hq in pairs}
    probs = {}
    for i, hq in pairs:
        s = jnp.where(mask, scores[i, hq] * (HEAD_DIM ** -0.5), -jnp.inf)
        sink = sinks_ref[hq]
        m = jnp.maximum(jnp.max(s, axis=-1, keepdims=True), sink)
        p = jnp.exp(s - m)
        den = jnp.sum(p, axis=-1, keepdims=True) + jnp.exp(sink - m)
        probs[i, hq] = _bf(p / den)
    outs = {(i, hq): _dot(probs[i, hq], hd(qkvs[i][2], kv(hq))) for i, hq in pairs}
    return [jnp.concatenate([outs[i, hq] for hq in range(A_HEADS)], axis=1) for i in range(len(qkvs))]


def _sink_attention(q, kk, vv, mask, sinks_ref):
    return _sink_attention_multi([(q, kk, vv)], mask, sinks_ref)[0]


def _swa_prompt_body(sinks_ref, cur_ref, prev_ref, cos_ref, sin_ref, cosp_ref, sinp_ref, o_ref, ko_ref, vo_ref):
    i = pl.program_id(1)
    cur = cur_ref[...]
    q = _rope(cur[:, C_AQ:C_AQ + 256], cos_ref[...], sin_ref[...])
    k = _rope(cur[:, C_AK:C_AK + 128], cos_ref[...], sin_ref[...])
    v = cur[:, C_AV:C_AV + 128]
    prev = prev_ref[...]
    kp = _rope(prev[:, 0:128], cosp_ref[...], sinp_ref[...])
    vp = prev[:, 128:256]
    kk = jnp.concatenate([kp, k], axis=0)
    vv = jnp.concatenate([vp, v], axis=0)
    r = _iota((WINDOW, 2 * WINDOW), 0)
    j = _iota((WINDOW, 2 * WINDOW), 1)
    d = WINDOW + r - j
    mask = (d >= 0) & (d <= WINDOW) & ((j >= WINDOW) | (i > 0))
    o_ref[...] = _sink_attention(q, kk, vv, mask, sinks_ref)
    ko_ref[0] = k
    vo_ref[0] = v


def swa_prompt(h, sinks, cos_t, sin_t, bsz, seq):
    nb = seq // WINDOW
    smem = pl.BlockSpec(memory_space=pltpu.SMEM)
    tab = lambda f: pl.BlockSpec((WINDOW, LANE), f)
    return pl.pallas_call(
        _swa_prompt_body,
        grid=(bsz, nb),
        in_specs=[smem,
                  pl.BlockSpec((WINDOW, 512), lambda b, i: (b * nb + i, 0)),
                  pl.BlockSpec((WINDOW, 256), lambda b, i: (b * nb + jnp.maximum(i - 1, 0), 1)),
                  tab(lambda b, i: (i, 0)), tab(lambda b, i: (i, 0)),
                  tab(lambda b, i: (jnp.maximum(i - 1, 0), 0)), tab(lambda b, i: (jnp.maximum(i - 1, 0), 0))],
        out_specs=[pl.BlockSpec((WINDOW, 256), lambda b, i: (b * nb + i, 0)),
                   pl.BlockSpec((1, WINDOW, 128), lambda b, i: (b, 0, 0)),
                   pl.BlockSpec((1, WINDOW, 128), lambda b, i: (b, 0, 0))],
        out_shape=[jax.ShapeDtypeStruct((bsz * seq, 256), F32),
                   jax.ShapeDtypeStruct((bsz, WINDOW, 128), F32),
                   jax.ShapeDtypeStruct((bsz, WINDOW, 128), F32)],
        compiler_params=_cp("arbitrary", "arbitrary"),
        name="swa_prompt",
    )(sinks, h, h, cos_t, sin_t, cos_t, sin_t)


SWA_SB = 8


def _swa_sample_body(sinks_ref, cur_ref, kc_ref, vc_ref, cos_ref, sin_ref, o_ref, ko_ref, vo_ref, *, ls):
    cw = WINDOW
    r = _iota((ls, cw + ls), 0)
    j = _iota((ls, cw + ls), 1)
    d = cw + r - j
    mask = (d >= 0) & (d <= WINDOW)
    qkvs = []
    for b in range(SWA_SB):
        cur = cur_ref[b * ls:(b + 1) * ls, :]
        q = _rope(cur[:, C_AQ:C_AQ + 256], cos_ref[...], sin_ref[...])
        k = _rope(cur[:, C_AK:C_AK + 128], cos_ref[...], sin_ref[...])
        v = cur[:, C_AV:C_AV + 128]
        kk = jnp.concatenate([kc_ref[b], k], axis=0)
        vv = jnp.concatenate([vc_ref[b], v], axis=0)
        ko_ref[b] = kk[ls:, :]
        vo_ref[b] = vv[ls:, :]
        qkvs.append((q, kk, vv))
    for b, o in enumerate(_sink_attention_multi(qkvs, mask, sinks_ref)):
        o_ref[b * ls:(b + 1) * ls, :] = o


def swa_sample(h, row0, sinks, k_cache, v_cache, cos_t, sin_t, bsz, ls):
    rows = SWA_SB * ls
    blk0 = row0 // rows
    smem = pl.BlockSpec(memory_space=pltpu.SMEM)
    cache = pl.BlockSpec((SWA_SB, WINDOW, 128), lambda i: (i, 0, 0))
    tab = pl.BlockSpec((ls, LANE), lambda i: (0, 0))
    return pl.pallas_call(
        functools.partial(_swa_sample_body, ls=ls),
        grid=(bsz // SWA_SB,),
        in_specs=[smem, pl.BlockSpec((rows, 512), lambda i: (blk0 + i, 0)), cache, cache, tab, tab],
        out_specs=[pl.BlockSpec((rows, 256), lambda i: (i, 0)), cache, cache],
        out_shape=[jax.ShapeDtypeStruct((bsz * ls, 256), F32),
                   jax.ShapeDtypeStruct((bsz, WINDOW, 128), F32),
                   jax.ShapeDtypeStruct((bsz, WINDOW, 128), F32)],
        compiler_params=_cp("arbitrary"),
        name="swa_sample",
    )(sinks, h, k_cache, v_cache, cos_t, sin_t)


S5_NB = 2 * SSM_W // LANE


def _s5_input(u, bw_ref):
    t = _dot(_bf(u), bw_ref[...])
    return jnp.concatenate([t[:, :SSM_W], t[:, :SSM_W] + t[:, SSM_W:]], axis=1)


def _s5_output(hs, u, cw_ref, d_ref, gw_ref, gb_ref):
    hs = jnp.concatenate([hs[:, :SSM_W] + hs[:, SSM_W:], hs[:, SSM_W:]], axis=1)
    y = _dot(_bf(hs), cw_ref[...]) + d_ref[...] * u
    y = _gelu_tanh(y)
    z = _dot(_bf(y), gw_ref[...]) + gb_ref[...]
    return z[:, :GROUP_WIDTH] * _sigmoid(z[:, GROUP_WIDTH:])


def _s5_scan(s_ref, a_ref, h_init, n_steps, rows):
    nre = S5_NB // 2
    a_re = [jnp.broadcast_to(a_ref[0:1, j * LANE:(j + 1) * LANE], (rows, LANE)) for j in range(nre)]
    a_im = [jnp.broadcast_to(a_ref[1:2, j * LANE:(j + 1) * LANE], (rows, LANE)) for j in range(nre)]

    def step(t, hcar):
        out = [None] * S5_NB
        base = pl.multiple_of(t * rows, rows)
        for j in range(nre):
            hr, hi = hcar[j], hcar[j + nre]
            nr = a_re[j] * hr - a_im[j] * hi + s_ref[j, pl.ds(base, rows), :]
            ni = a_re[j] * hi + a_im[j] * hr + s_ref[j + nre, pl.ds(base, rows), :]
            s_ref[j, pl.ds(base, rows), :] = nr
            s_ref[j + nre, pl.ds(base, rows), :] = ni
            out[j], out[j + nre] = nr, ni
        return tuple(out)

    return lax.fori_loop(0, n_steps, step, tuple(h_init))


def _s5_prompt_body(*refs, nb, tl):
    u_refs = refs[:nb]
    bw_ref, a_ref, cw_ref, d_ref, gw_ref, gb_ref, h0_ref, o_ref, hl_ref, s_ref, hst_ref = refs[nb:]
    i = pl.program_id(0)

    @pl.when(i == 0)
    def _():
        hst_ref[...] = h0_ref[...]

    for b in range(nb):
        bu = _s5_input(u_refs[b][...], bw_ref)
        for j in range(S5_NB):
            s_ref[j, pl.ds(b, tl, stride=nb), :] = bu[:, j * LANE:(j + 1) * LANE]
    h_init = [hst_ref[:, j * LANE:(j + 1) * LANE] for j in range(S5_NB)]
    h_fin = _s5_scan(s_ref, a_ref, h_init, tl, nb)
    for j in range(S5_NB):
        hst_ref[:, j * LANE:(j + 1) * LANE] = h_fin[j]
    for b in range(nb):
        hs = jnp.concatenate([s_ref[j, pl.ds(b, tl, stride=nb), :] for j in range(S5_NB)], axis=1)
        o_ref[b] = _s5_output(hs, u_refs[b][...], cw_ref, d_ref, gw_ref, gb_ref)

    @pl.when(i == pl.num_programs(0) - 1)
    def _():
        hl_ref[...] = hst_ref[...]


def s5_prompt(h, h0, bw, a, cw, dsk, gw, gb, bsz, seq, tl=128):
    assert bsz == 8
    nt = seq // tl
    full = lambda shape: pl.BlockSpec(shape, lambda i: (0,) * len(shape))
    u_specs = [pl.BlockSpec((tl, GROUP_WIDTH), functools.partial(lambda i, b: (b * nt + i, C_SU // GROUP_WIDTH), b=b))
               for b in range(bsz)]
    return pl.pallas_call(
        functools.partial(_s5_prompt_body, nb=bsz, tl=tl),
        grid=(nt,),
        in_specs=u_specs + [full((GROUP_WIDTH, 2 * SSM_W)), full((2, SSM_W)), full((2 * SSM_W, GROUP_WIDTH)),
                            full((1, GROUP_WIDTH)), full((GROUP_WIDTH, 2 * GROUP_WIDTH)), full((1, 2 * GROUP_WIDTH)),
                            full((bsz, 2 * SSM_W))],
        out_specs=[pl.BlockSpec((bsz, tl, GROUP_WIDTH), lambda i: (0, i, 0)), full((bsz, 2 * SSM_W))],
        out_shape=[jax.ShapeDtypeStruct((bsz, seq, GROUP_WIDTH), F32), jax.ShapeDtypeStruct((bsz, 2 * SSM_W), F32)],
        scratch_shapes=[pltpu.VMEM((S5_NB, tl * bsz, LANE), F32), pltpu.VMEM((bsz, 2 * SSM_W), F32)],
        compiler_params=_cp("arbitrary"),
        name="s5_prompt",
    )(*([h] * bsz), bw, a, cw, dsk, gw, gb, h0)


def _s5_sample_body(u_ref, bw_ref, a_ref, cw_ref, d_ref, gw_ref, gb_ref, h0_ref, o_ref, hl_ref, s_ref, t_ref, *, bsz, ls):
    bu = _s5_input(u_ref[...], bw_ref)
    for j in range(S5_NB):
        s_ref[j] = bu[:, j * LANE:(j + 1) * LANE]
    for t in range(ls):
        for j in range(S5_NB):
            t_ref[j, pl.ds(t * bsz, bsz), :] = s_ref[j, pl.ds(t, bsz, stride=ls), :]
    h_init = [h0_ref[:, j * LANE:(j + 1) * LANE] for j in range(S5_NB)]
    h_fin = _s5_scan(t_ref, a_ref, h_init, ls, bsz)
    for j in range(S5_NB):
        hl_ref[:, j * LANE:(j + 1) * LANE] = h_fin[j]
    for t in range(ls):
        for j in range(S5_NB):
            s_ref[j, pl.ds(t, bsz, stride=ls), :] = t_ref[j, pl.ds(t * bsz, bsz), :]
    hs = jnp.concatenate([s_ref[j] for j in range(S5_NB)], axis=1)
    o_ref[...] = _s5_output(hs, u_ref[...], cw_ref, d_ref, gw_ref, gb_ref)


def s5_sample(h, row0, h0, bw, a, cw, dsk, gw, gb, bsz, ls):
    rows = bsz * ls
    full = lambda shape: pl.BlockSpec(shape, lambda i: (0,) * len(shape))
    return pl.pallas_call(
        functools.partial(_s5_sample_body, bsz=bsz, ls=ls),
        grid=(1,),
        in_specs=[pl.BlockSpec((rows, GROUP_WIDTH), lambda i: (row0 // rows, C_SU // GROUP_WIDTH)),
                  full((GROUP_WIDTH, 2 * SSM_W)), full((2, SSM_W)), full((2 * SSM_W, GROUP_WIDTH)),
                  full((1, GROUP_WIDTH)), full((GROUP_WIDTH, 2 * GROUP_WIDTH)), full((1, 2 * GROUP_WIDTH)),
                  full((bsz, 2 * SSM_W))],
        out_specs=[full((rows, GROUP_WIDTH)), full((bsz, 2 * SSM_W))],
        out_shape=[jax.ShapeDtypeStruct((rows, GROUP_WIDTH), F32), jax.ShapeDtypeStruct((bsz, 2 * SSM_W), F32)],
        scratch_shapes=[pltpu.VMEM((S5_NB, rows, LANE), F32), pltpu.VMEM((S5_NB, rows, LANE), F32)],
        compiler_params=_cp("arbitrary"),
        name="s5_sample",
    )(h, bw, a, cw, dsk, gw, gb, h0)


def s5_params(a_re, a_im, b_re, b_im, c_re, c_im, d_skip, log_dt):
    lam = lax.complex(a_re, a_im)
    delta = jnp.exp(log_dt)[:, None]
    a_bar = jnp.exp(lam * delta)
    b_bar = ((a_bar - 1.0) / lam)[..., None] * lax.complex(b_re, b_im)
    eye = jnp.eye(SSM_GROUPS, dtype=F32)
    bw_re = jnp.einsum('gpc,gh->gchp', b_bar.real, eye).reshape(GROUP_WIDTH, SSM_W)
    bw_im = jnp.einsum('gpc,gh->gchp', b_bar.imag, eye).reshape(GROUP_WIDTH, SSM_W)
    bw = jnp.concatenate([bw_re, bw_im - bw_re], axis=1).astype(BF16)
    cw_re = jnp.einsum('gcp,gh->gphc', c_re, eye).reshape(SSM_W, GROUP_WIDTH)
    cw_im = jnp.einsum('gcp,gh->gphc', c_im, eye).reshape(SSM_W, GROUP_WIDTH)
    cw = jnp.concatenate([cw_re, -(cw_re + cw_im)], axis=0).astype(BF16)
    a = jnp.stack([a_bar.real.reshape(SSM_W), a_bar.imag.reshape(SSM_W)], axis=0)
    return bw, a, cw, d_skip.reshape(1, GROUP_WIDTH)


GDN_RT = 256
GDN_ST = 64
GDN_SG = 8


def _split(x):
    hi = _bf(x)
    return hi, _bf(x - hi.astype(F32))


def _dot3(a, b):
    return _dot(a[0], b[0]) + (_dot(a[0], b[1]) + _dot(a[1], b[0]))


def _gdn_prep_body(*refs, c, sample, tiles_per_seq):
    if sample:
        x_ref, cs_ref, z_ref, sm_ref, cw_ref, alog_ref, dtb_ref = refs[:7]
        outs = refs[7:14]
        xs_ref, cb_ref = refs[14:]
    else:
        x_ref, prev_ref, cs_ref, z_ref, sm_ref, cw_ref, alog_ref, dtb_ref = refs[:8]
        outs = refs[8:15]
        (xs_ref,) = refs[15:]
    u_ref, w_ref, qk_ref, qg_ref, kd_ref, eg_ref, zs_ref = outs
    rt = GDN_RT
    x = x_ref[...]
    xs_ref[8:8 + rt, :] = x
    if sample:
        xs_ref[0:8, :] = jnp.zeros((8, GDN_CONV_DIM), F32)
        cb_ref[0:rt, :] = cs_ref[...]
        cb_ref[rt:rt + 8, :] = jnp.zeros((8, GDN_CONV_DIM), F32)
        pos = _iota((rt, GDN_CONV_DIM), 0) & (c - 1)
        shifted = lambda i: jnp.where(pos >= i, xs_ref[8 - i:8 - i + rt, :], cb_ref[8 - i:8 - i + rt, :])
    else:
        first = (pl.program_id(0) % tiles_per_seq) == 0
        xs_ref[0:8, :] = jnp.where(first, cs_ref[...], prev_ref[...])
        shifted = lambda i: xs_ref[8 - i:8 - i + rt, :]
    conv = shifted(3) * cw_ref[0:1, :]
    conv = conv + shifted(2) * cw_ref[1:2, :]
    conv = conv + shifted(1) * cw_ref[2:3, :]
    conv = conv + x * cw_ref[3:4, :]
    qkv = _silu(conv)
    zs_ref[...] = _silu(z_ref[...])

    sm = sm_ref[...]
    beta_all = _sigmoid(sm)
    g_all = -jnp.exp(alog_ref[...]) * _softplus(sm + dtb_ref[...])
    st = GDN_ST
    ri = _iota((st, st), 0)
    ci = _iota((st, st), 1)
    same = _shr(ri, c) == _shr(ci, c)
    low = (ci <= ri) & same
    tri = low.astype(F32)
    last = (ci == (ri | (c - 1))).astype(F32)
    eye = (ri == ci).astype(F32)
    lane = _iota((st, LANE), 1)
    nsub = rt // st
    probs = [(t, hh) for t in range(nsub) for hh in range(GDN_HEADS)]
    gc_alls, gl_alls = [], []
    for t in range(nsub):
        rows = slice(t * st, (t + 1) * st)
        gc_alls.append(_dot_nt_exact_lhs(_bf(tri), g_all[rows, :]))
    for t in range(nsub):
        gl_alls.append(_dot_nt_exact_lhs(_bf(last), gc_alls[t]))
        eg_ref[t * st:(t + 1) * st, :] = jnp.exp(gl_alls[t])
    qs, ks, rhss, decays, a_stricts, invs = {}, {}, {}, {}, {}, {}
    for t, hh in probs:
        rows = slice(t * st, (t + 1) * st)
        q = qkv[rows, hh * GDN_DK:(hh + 1) * GDN_DK]
        k = qkv[rows, 256 + hh * GDN_DK:256 + (hh + 1) * GDN_DK]
        v = qkv[rows, 512 + hh * GDN_DV:512 + (hh + 1) * GDN_DV]
        q = q * lax.rsqrt(jnp.sum(q * q, axis=-1, keepdims=True) + 1e-6) * (GDN_DK ** -0.5)
        k = k * lax.rsqrt(jnp.sum(k * k, axis=-1, keepdims=True) + 1e-6)
        beta = beta_all[rows, SM_GB + hh:SM_GB + hh + 1]
        gc = gc_alls[t][:, SM_GA + hh:SM_GA + hh + 1]
        gl = gl_alls[t][:, SM_GA + hh:SM_GA + hh + 1]
        sel = (lane == SM_GA + hh).astype(F32)
        gc_row = _dot_nt(sel, gc_alls[t], HI)
        decays[t, hh] = jnp.exp(jnp.where(low, gc - gc_row, -jnp.inf))
        kb = k * beta
        eg = jnp.exp(gc)
        qs[t, hh], ks[t, hh] = q, k
        rhss[t, hh] = _split(jnp.concatenate([v * beta, kb * eg], axis=1))
        a_stricts[t, hh] = jnp.where(ci < ri, _dot_nt(_bf(kb), _bf(k)) * decays[t, hh], 0.0)
        qg_ref[rows, hh * GDN_DK:(hh + 1) * GDN_DK] = q * eg
        kd_ref[rows, hh * GDN_DK:(hh + 1) * GDN_DK] = k * jnp.exp(gl - gc)
    for t, hh in probs:
        qk_ref[t * st:(t + 1) * st, hh * st:(hh + 1) * st] = _dot_nt(_bf(qs[t, hh]), _bf(ks[t, hh])) * decays[t, hh]
    for p in probs:
        invs[p] = eye - jnp.where(_shr(ri, 2) == _shr(ci, 2), a_stricts[p], 0.0)
    s = 2
    while s < c:
        pair = (_shr(ri, 2 * s) == _shr(ci, 2 * s)) & (_shr(ri, s) != _shr(ci, s))
        inv_s = {p: _split(invs[p]) for p in probs}
        mid = {p: _dot3(_split(jnp.where(pair, a_stricts[p], 0.0)), inv_s[p]) for p in probs}
        for p in probs:
            invs[p] = invs[p] - _dot3(inv_s[p], _split(mid[p]))
        s *= 2
    for t, hh in probs:
        sol = _dot3(_split(invs[t, hh]), rhss[t, hh])
        u_ref[t * st:(t + 1) * st, hh * GDN_DV:(hh + 1) * GDN_DV] = sol[:, :GDN_DV]
        w_ref[t * st:(t + 1) * st, hh * GDN_DV:(hh + 1) * GDN_DV] = sol[:, GDN_DV:]


def _gdn_scan_body(u_ref, w_ref, qk_ref, qg_ref, kd_ref, eg_ref, zs_ref, nw_ref, s0_ref, o_ref, so_ref, st_ref, *, c):
    n = pl.program_id(1)

    @pl.when(n == 0)
    def _():
        st_ref[...] = s0_ref[...]

    probs = [(s, hh) for s in range(GDN_SG) for hh in range(GDN_HEADS)]
    hsl = lambda hh: slice(hh * GDN_DV, (hh + 1) * GDN_DV)
    state_b = {(s, hh): _bf(st_ref[s, hh]) for s, hh in probs}
    v_b = {}
    for s, hh in probs:
        v_b[s, hh] = _bf(u_ref[s, :, hsl(hh)] - _dot(_bf(w_ref[s, :, hsl(hh)]), state_b[s, hh]))
    o_inter = {(s, hh): _dot(_bf(qg_ref[s, :, hsl(hh)]), state_b[s, hh]) for s, hh in probs}
    for s, hh in probs:
        col0 = (s * c) % GDN_ST
        qk = qk_ref[s, :, hh * GDN_ST + col0:hh * GDN_ST + col0 + c]
        o = o_inter[s, hh] + _dot(_bf(qk), v_b[s, hh])
        o = o * lax.rsqrt(jnp.mean(o * o, axis=-1, keepdims=True) + RMS_EPS) * nw_ref[...]
        o_ref[s, :, hsl(hh)] = o * zs_ref[s, :, hsl(hh)]
    for s, hh in probs:
        eg_last = eg_ref[s, c - 1:c, SM_GA + hh:SM_GA + hh + 1]
        st_ref[s, hh] = st_ref[s, hh] * eg_last + _dot_tn(_bf(kd_ref[s, :, hsl(hh)]), v_b[s, hh])

    @pl.when(n == pl.num_programs(1) - 1)
    def _():
        so_ref[...] = st_ref[...]


def gdn_mixer(h, row0, conv_w, alog_t, dtb_t, norm_w, conv_state, s0, bsz, seq):
    c = min(GDN_CHUNK, seq)
    sample = seq == c
    rows = bsz * seq
    rt = GDN_RT
    nt = rows // rt
    blk0 = row0 // rt
    tps = max(seq // rt, 1)
    cs_rows = jnp.pad(conv_state, ((0, 0), (8 - (GDN_CONV - 1), 0), (0, 0))).reshape(bsz * 8, GDN_CONV_DIM)
    row = lambda w, col: pl.BlockSpec((rt, w), lambda i: (blk0 + i, col))
    full = lambda shape: pl.BlockSpec(shape, lambda i: (0,) * len(shape))
    if sample:
        aux_specs = [pl.BlockSpec((rt, GDN_CONV_DIM), lambda i: (i, 0))]
        aux = [cs_rows]
        scratch = [pltpu.VMEM((8 + rt, GDN_CONV_DIM), F32), pltpu.VMEM((8 + rt, GDN_CONV_DIM), F32)]
    else:
        aux_specs = [pl.BlockSpec((8, GDN_CONV_DIM), lambda i: (jnp.maximum((row0 + i * rt) // 8 - 1, 0), C_GQKV // GDN_CONV_DIM)),
                     pl.BlockSpec((8, GDN_CONV_DIM), lambda i: (i // tps, 0))]
        aux = [h, cs_rows]
        scratch = [pltpu.VMEM((8 + rt, GDN_CONV_DIM), F32)]
    out_w = [GROUP_WIDTH] * 5 + [LANE, GROUP_WIDTH]
    prep = pl.pallas_call(
        functools.partial(_gdn_prep_body, c=c, sample=sample, tiles_per_seq=tps),
        grid=(nt,),
        in_specs=[row(GDN_CONV_DIM, C_GQKV // GDN_CONV_DIM)] + aux_specs
                 + [row(GROUP_WIDTH, C_GZ // GROUP_WIDTH), row(LANE, C_SM // LANE),
                    full((GDN_CONV, GDN_CONV_DIM)), full((1, LANE)), full((1, LANE))],
        out_specs=[pl.BlockSpec((rt, w), lambda i: (i, 0)) for w in out_w],
        out_shape=[jax.ShapeDtypeStruct((rows, w), F32) for w in out_w],
        scratch_shapes=scratch,
        compiler_params=_cp("arbitrary"),
        name="gdn_prep",
    )(h, *aux, h, h, conv_w, alog_t, dtb_t)
    prep = [a.reshape(bsz, seq, a.shape[-1]) for a in prep]
    ng, nc = bsz // GDN_SG, seq // c
    seq_blk = lambda w: pl.BlockSpec((GDN_SG, c, w), lambda g, n: (g, n, 0))
    st_spec = pl.BlockSpec((GDN_SG, GDN_HEADS, GDN_DK, GDN_DV), lambda g, n: (g, 0, 0, 0))
    o, s_fin = pl.pallas_call(
        functools.partial(_gdn_scan_body, c=c),
        grid=(ng, nc),
        in_specs=[seq_blk(w) for w in out_w] + [pl.BlockSpec((1, GDN_DV), lambda g, n: (0, 0)), st_spec],
        out_specs=[seq_blk(GROUP_WIDTH), st_spec],
        out_shape=[jax.ShapeDtypeStruct((bsz, seq, GROUP_WIDTH), F32),
                   jax.ShapeDtypeStruct((bsz, GDN_HEADS, GDN_DK, GDN_DV), F32)],
        scratch_shapes=[pltpu.VMEM((GDN_SG, GDN_HEADS, GDN_DK, GDN_DV), F32)],
        compiler_params=_cp("arbitrary", "arbitrary"),
        name="gdn_scan",
    )(*prep, norm_w, s0)
    return o.reshape(rows, GROUP_WIDTH), s_fin


GLA_RT = 256


def _dot_exact(x, w):
    hi = _bf(x)
    r1 = x - hi.astype(F32)
    mid = _bf(r1)
    lo = _bf(r1 - mid.astype(F32))
    return _dot(hi, w) + (_dot(mid, w) + _dot(lo, w))


def _gla_body(qk_ref, vr_ref, sm_ref, gw_ref, gb_ref, nw_ref, s0_ref, o_ref, so_ref,
              kb_ref, bb_ref, vb_ref, st_ref, *, c, seq):
    tr = GLA_RT
    n = pl.program_id(1)
    pad = GLA_CHUNK
    carried = seq >= tr

    @pl.when(n == 0)
    def _():
        kb_ref[0:pad, :] = jnp.zeros((pad, 128), F32)
        bb_ref[0:pad, :] = jnp.zeros((pad, 128), F32)
        vb_ref[0:pad, :] = jnp.zeros((pad, 256), F32)
        if carried:
            st_ref[...] = s0_ref[0]

    q = qk_ref[:, 0:128] * (GLA_DK ** -0.5)
    k = qk_ref[:, 128:256]
    v = vr_ref[:, 0:256]
    lg = sm_ref[:, SM_LG:SM_LG + GLA_RANK]
    log_a = _log_sigmoid(_dot(_bf(lg), _bf(gw_ref[...])) + gb_ref[...]) / GLA_TAU
    ri = _iota((tr, tr), 0)
    ci = _iota((tr, tr), 1)
    tri = _bf(((ci <= ri) & (_shr(ri, c) == _shr(ci, c))).astype(F32))
    bc = _dot_nt_exact_lhs(tri, log_a)
    kb_ref[pad:pad + tr, :] = k
    bb_ref[pad:pad + tr, :] = bc
    vb_ref[pad:pad + tr, :] = v
    ones_kv = _bf((_shr(_iota((128, 256), 0), GLA_DK) == _shr(_iota((128, 256), 1), GLA_DV)).astype(F32))
    pos = _iota((tr, 128), 0) & (c - 1)
    ps = []
    for dl in range(c):
        ks = kb_ref[pad - dl:pad - dl + tr, :]
        bs = bb_ref[pad - dl:pad - dl + tr, :]
        ps.append(jnp.where(pos >= dl, q * ks * jnp.exp(bc - bs), 0.0))
    att = _dot_exact(jnp.concatenate(ps, axis=0), ones_kv)
    o_intra = jnp.zeros((tr, 256), F32)
    for dl in range(c):
        vs = vb_ref[pad - dl:pad - dl + tr, :]
        o_intra = o_intra + _bf(att[dl * tr:(dl + 1) * tr, :]).astype(F32) * _bf(vs).astype(F32)
    bd = (_shr(_iota((256, 128), 0), GLA_DV) == _shr(_iota((256, 128), 1), GLA_DK)).astype(F32)
    nch = tr // c
    sls = [slice(ch * c, (ch + 1) * c) for ch in range(nch)]
    b_last = [bc[sl, :][c - 1:c, :] for sl in sls]
    incs = [bd * _dot_tn(_bf(v[sl, :]), _bf(k[sl, :] * jnp.exp(bl - bc[sl, :]))) for sl, bl in zip(sls, b_last)]
    outs = []
    st = st_ref[...] if carried else None
    for ch, sl in enumerate(sls):
        if not carried:
            st = s0_ref[(ch * c) // seq]
        outs.append(_dot_nt(_bf(q[sl, :] * jnp.exp(bc[sl, :])), _bf(st)) + o_intra[sl, :])
        st = st * jnp.exp(b_last[ch]) + incs[ch]
        if not carried:
            so_ref[(ch * c) // seq] = st
    if carried:
        st_ref[...] = st
        so_ref[0] = st
    o = jnp.concatenate(outs, axis=0)
    ones_vv = _bf((_shr(_iota((256, 256), 0), GLA_DV) == _shr(_iota((256, 256), 1), GLA_DV)).astype(F32))
    ms = _dot_exact(o * o, ones_vv) * (1.0 / GLA_DV)
    o_ref[...] = o * lax.rsqrt(ms + RMS_EPS) * nw_ref[...] * _silu(vr_ref[:, 256:512])


def _dot_nt_exact_lhs(w, x):
    hi = _bf(x)
    r1 = x - hi.astype(F32)
    mid = _bf(r1)
    lo = _bf(r1 - mid.astype(F32))
    return _dot(w, hi) + (_dot(w, mid) + _dot(w, lo))


def gla_mixer(h, row0, gate_w, gate_b, norm_w4, s0t, bsz, seq):
    c = min(GLA_CHUNK, seq)
    tr = GLA_RT
    spt = max(tr // seq, 1)
    nt = max(seq // tr, 1)
    ng = bsz // spt
    blk0 = row0 // tr
    row = lambda w, col: pl.BlockSpec((tr, w), lambda g, n: (blk0 + g * nt + n, col))
    full = lambda shape: pl.BlockSpec(shape, lambda g, n: (0,) * len(shape))
    st_spec = pl.BlockSpec((spt, 256, 128), lambda g, n: (g, 0, 0))
    return pl.pallas_call(
        functools.partial(_gla_body, c=c, seq=seq),
        grid=(ng, nt),
        in_specs=[row(256, C_LQ // 256), row(512, C_LV // 512), row(LANE, C_SM // LANE),
                  full((GLA_RANK, 128)), full((1, 128)), full((1, 256)), st_spec],
        out_specs=[pl.BlockSpec((tr, 256), lambda g, n: (g * nt + n, 0)), st_spec],
        out_shape=[jax.ShapeDtypeStruct((bsz * seq, 256), F32), jax.ShapeDtypeStruct((bsz, 256, 128), F32)],
        scratch_shapes=[pltpu.VMEM((GLA_CHUNK + tr, 128), F32), pltpu.VMEM((GLA_CHUNK + tr, 128), F32),
                        pltpu.VMEM((GLA_CHUNK + tr, 256), F32), pltpu.VMEM((256, 128), F32)],
        compiler_params=_cp("arbitrary", "arbitrary"),
        name="gla",
    )(h, h, h, gate_w, gate_b, norm_w4, s0t)


def gla_state_to_t(s):
    eye = jnp.eye(GLA_HEADS, dtype=s.dtype)
    return jnp.einsum('bhde,hg->bhegd', s, eye).reshape(s.shape[0], 256, 128)


def gla_state_from_t(st):
    b = st.shape[0]
    t5 = st.reshape(b, GLA_HEADS, GLA_DV, GLA_HEADS, GLA_DK)
    diag = jnp.stack([t5[:, hh, :, hh, :] for hh in range(GLA_HEADS)], axis=1)
    return jnp.swapaxes(diag, 2, 3)


TOK_SUB = D_MODEL // LANE


def _to_token_tiles(ref, row0, y):
    n = y.shape[0]
    for s in range(TOK_SUB):
        ref[pl.ds(row0 * TOK_SUB + s, n, stride=TOK_SUB), :] = y[:, s * LANE:(s + 1) * LANE]


def _from_token_tiles(ref, row0, n, s):
    return ref[pl.ds(row0 * TOK_SUB + s, n, stride=TOK_SUB), :]


def _out_proj_body(x_ref, ap, bp, cp, dp, a_s, b_s, c_s, d_s, w_ref, g_ref, b_ref, rw_ref, rb_ref,
                   o_ref, t_ref, e_ref, gt_ref, *, n_prompt_tiles):
    i = pl.program_id(0)

    def run(refs):
        acc = DN_ALPHA * x_ref[...]
        for m, r in enumerate(refs):
            acc = acc + _dot(_bf(r[...]), w_ref[m * GROUP_WIDTH:(m + 1) * GROUP_WIDTH, :])
        y = _layer_norm(acc, g_ref[...], b_ref[...])
        o_ref[...] = y
        _to_token_tiles(t_ref, 0, y)
        e_ref[...], gt_ref[...] = _route(y, rw_ref, rb_ref)

    @pl.when(i < n_prompt_tiles)
    def _():
        run((ap, bp, cp, dp))

    @pl.when(i >= n_prompt_tiles)
    def _():
        run((a_s, b_s, c_s, d_s))


def out_proj_ln(x, mix_p, mix_s, w, g, b, rw, rb):
    t = x.shape[0]
    npt = mix_p[0].shape[0] // ROW_TILE
    nst = mix_s[0].shape[0] // ROW_TILE
    p_spec = pl.BlockSpec((ROW_TILE, GROUP_WIDTH), lambda i: (jnp.minimum(i, npt - 1), 0))
    s_spec = pl.BlockSpec((ROW_TILE, GROUP_WIDTH), lambda i: (jnp.clip(i - npt, 0, nst - 1), 0))
    full = lambda shape: pl.BlockSpec(shape, lambda i: (0,) * len(shape))
    return pl.pallas_call(
        functools.partial(_out_proj_body, n_prompt_tiles=npt),
        grid=(t // ROW_TILE,),
        in_specs=[pl.BlockSpec((ROW_TILE, D_MODEL), lambda i: (i, 0))] + [p_spec] * 4 + [s_spec] * 4
                 + [full((D_MODEL, D_MODEL)), full((1, D_MODEL)), full((1, D_MODEL)), full((D_MODEL, LANE)), full((1, LANE))],
        out_specs=[pl.BlockSpec((ROW_TILE, D_MODEL), lambda i: (i, 0)),
                   pl.BlockSpec((ROW_TILE * TOK_SUB, LANE), lambda i: (i, 0)),
                   pl.BlockSpec((ROW_TILE, LANE), lambda i: (i, 0)), pl.BlockSpec((ROW_TILE, LANE), lambda i: (i, 0))],
        out_shape=[jax.ShapeDtypeStruct((t, D_MODEL), F32), jax.ShapeDtypeStruct((t * TOK_SUB, LANE), F32),
                   jax.ShapeDtypeStruct((t, LANE), I32), jax.ShapeDtypeStruct((t, LANE), F32)],
        compiler_params=_cp("arbitrary"),
        name="out_proj_ln",
    )(x, *mix_p, *mix_s, w, g, b, rw, rb)


def _route(x, w_ref, b_ref):
    logits = _dot(_bf(x), _bf(w_ref[...])) + b_ref[...]
    lane = _iota(logits.shape, 1)
    vals = jnp.where(lane < N_EXPERTS, logits, -jnp.inf)
    eo = jnp.zeros(logits.shape, I32)
    top = []
    for k in range(TOP_K):
        m = jnp.max(vals, axis=-1, keepdims=True)
        idx = jnp.min(jnp.where(vals == m, lane.astype(F32), float(LANE)), axis=-1, keepdims=True).astype(I32)
        eo = jnp.where(lane == k, idx, eo)
        top.append(m)
        vals = jnp.where(lane == idx, -jnp.inf, vals)
    ex = [jnp.exp(v - top[0]) for v in top]
    den = ex[0] + ex[1] + ex[2] + ex[3]
    go = jnp.zeros(logits.shape, F32)
    for k in range(TOP_K):
        go = jnp.where(lane == k, ex[k] / den, go)
    return eo, go


DISPATCH_TILE = 256


def _tok(i):
    return pl.ds(pl.multiple_of(i * TOK_SUB, TOK_SUB), TOK_SUB)


def _dispatch_copy(x_ref, xs_ref, sem, r, d):
    return pltpu.make_async_copy(x_ref.at[_tok(r), :], xs_ref.at[_tok(d), :], sem)


def _dispatch_body(dest_ref, x_ref, xs_in_ref, xs_ref, sem):
    del xs_in_ref

    def start(r, carry):
        for k in range(TOP_K):
            _dispatch_copy(x_ref, xs_ref, sem, r, dest_ref[0, 0, r * TOP_K + k]).start(priority=k % 2)
        return carry

    lax.fori_loop(0, DISPATCH_TILE, start, 0)

    def wait(r, carry):
        for k in range(TOP_K):
            _dispatch_copy(x_ref, xs_ref, sem, 0, 0).wait()
        return carry

    lax.fori_loop(0, DISPATCH_TILE, wait, 0)


def moe_dispatch(xt, dest, n_rows, xs_buf=None):
    t = xt.shape[0] // TOK_SUB
    nt = t // DISPATCH_TILE
    zeros = jnp.zeros((n_rows * TOK_SUB, LANE), F32) if xs_buf is None else xs_buf
    return pl.pallas_call(
        _dispatch_body,
        grid=(nt,),
        in_specs=[pl.BlockSpec((1, 1, DISPATCH_TILE * TOP_K), lambda i: (i, 0, 0), memory_space=pltpu.SMEM),
                  pl.BlockSpec((DISPATCH_TILE * TOK_SUB, LANE), lambda i: (i, 0)),
                  pl.BlockSpec(memory_space=pl.ANY)],
        out_specs=pl.BlockSpec(memory_space=pl.ANY),
        out_shape=jax.ShapeDtypeStruct((n_rows * TOK_SUB, LANE), F32),
        scratch_shapes=[pltpu.SemaphoreType.DMA(())],
        input_output_aliases={2: 0},
        compiler_params=_cp("arbitrary"),
        name="moe_dispatch",
    )(dest.reshape(nt, 1, DISPATCH_TILE * TOP_K), xt, zeros)


EXPERT_SUB = 256


def _expert_body(te_ref, tv_ref, xs_ref, wgu_ref, bgu_ref, wd_ref, bd_ref, ys_ref, wgu_s, wd2_s, tmp_s):
    i = pl.program_id(0)
    valid = tv_ref[i] != 0
    changed = (i == 0) | (te_ref[i] != te_ref[jnp.maximum(i - 1, 0)])

    @pl.when(i == 0)
    def _():
        tmp_s[...] = jnp.zeros(tmp_s.shape, F32)

    @pl.when(valid & changed)
    def _():
        for c0 in range(0, 2 * D_FF, 512):
            wgu_s[:, c0:c0 + 512] = _bf(wgu_ref[0, :, c0:c0 + 512])
        for j in range(D_MODEL // LANE):
            tmp_s[pl.ds(0, D_FF, stride=2), :] = wd_ref[0, :, j * LANE:(j + 1) * LANE]
            wd2_s[:, j * LANE:(j + 1) * LANE] = _bf(tmp_s[...])

    @pl.when(valid)
    def _():
        for r0 in range(0, MOE_TILE, EXPERT_SUB):
            xb = jnp.concatenate([_bf(_from_token_tiles(xs_ref, r0, EXPERT_SUB, s)) for s in range(TOK_SUB)], axis=1)
            gu = _dot(xb, wgu_s[...]) + bgu_ref[0]
            nxt = pltpu.roll(gu, 2 * D_FF - 1, 1)
            gate = jnp.minimum(gu, SWIGLU_LIMIT)
            up = jnp.clip(nxt, -SWIGLU_LIMIT, SWIGLU_LIMIT)
            act = gate * _sigmoid(gate * SWIGLU_ALPHA) * (up + 1.0)
            even = (_iota(act.shape, 1) & 1) == 0
            act = jnp.where(even, act, 0.0)
            _to_token_tiles(ys_ref, r0, _dot(_bf(act), wd2_s[...]) + bd_ref[0])

    @pl.when(jnp.logical_not(valid))
    def _():
        ys_ref[...] = jnp.zeros(ys_ref.shape, F32)


def moe_experts(xs, tile_e, tile_valid, wgu, bgu, wd, bd):
    n_rows = xs.shape[0] // TOK_SUB
    nt = n_rows // MOE_TILE
    grid_spec = pltpu.PrefetchScalarGridSpec(
        num_scalar_prefetch=2,
        grid=(nt,),
        in_specs=[pl.BlockSpec((MOE_TILE * TOK_SUB, LANE), lambda i, te, tv: (i, 0)),
                  pl.BlockSpec((1, D_MODEL, 2 * D_FF), lambda i, te, tv: (te[i], 0, 0)),
                  pl.BlockSpec((1, 1, 2 * D_FF), lambda i, te, tv: (te[i], 0, 0)),
                  pl.BlockSpec((1, D_FF, D_MODEL), lambda i, te, tv: (te[i], 0, 0)),
                  pl.BlockSpec((1, 1, D_MODEL), lambda i, te, tv: (te[i], 0, 0))],
        out_specs=pl.BlockSpec((MOE_TILE * TOK_SUB, LANE), lambda i, te, tv: (i, 0)),
        scratch_shapes=[pltpu.VMEM((D_MODEL, 2 * D_FF), BF16), pltpu.VMEM((2 * D_FF, D_MODEL), BF16),
                        pltpu.VMEM((2 * D_FF, LANE), F32)],
    )
    return pl.pallas_call(
        _expert_body,
        grid_spec=grid_spec,
        out_shape=jax.ShapeDtypeStruct((n_rows * TOK_SUB, LANE), F32),
        compiler_params=_cp("arbitrary"),
        name="moe_experts",
    )(tile_e, tile_valid, xs, wgu, bgu, wd, bd)


def _combine_copy(ys_ref, buf_ref, sems, slot, d, k, r):
    return pltpu.make_async_copy(ys_ref.at[_tok(d), :], buf_ref.at[slot, k, _tok(r), :], sems.at[slot])


def _combine_body(dest_ref, next_dest_ref, gates_ref, x_ref, g_ref, b_ref, ys_ref, o_ref, buf_ref, sems):
    i = pl.program_id(0)
    slot = i % 2

    def gather(idx_ref, into):
        def start(r, carry):
            for k in range(TOP_K):
                _combine_copy(ys_ref, buf_ref, sems, into, idx_ref[0, 0, r * TOP_K + k], k, r).start(priority=k % 2)
            return carry

        lax.fori_loop(0, DISPATCH_TILE, start, 0)

    @pl.when(i == 0)
    def _():
        gather(dest_ref, 0)

    @pl.when(i + 1 < pl.num_programs(0))
    def _():
        gather(next_dest_ref, 1 - slot)

    def wait(r, carry):
        for k in range(TOP_K):
            _combine_copy(ys_ref, buf_ref, sems, slot, 0, k, 0).wait()
        return carry

    lax.fori_loop(0, DISPATCH_TILE, wait, 0)
    gates = gates_ref[...]
    cols = []
    for s in range(TOK_SUB):
        acc = DN_ALPHA * x_ref[:, s * LANE:(s + 1) * LANE]
        for k in range(TOP_K):
            acc = acc + gates[:, k:k + 1] * buf_ref[slot, k, pl.ds(s, DISPATCH_TILE, stride=TOK_SUB), :]
        cols.append(acc)
    o_ref[...] = _layer_norm(jnp.concatenate(cols, axis=1), g_ref[...], b_ref[...])


def moe_combine_ln(x, ys, dest, gates, g, b):
    t = x.shape[0]
    nt = t // DISPATCH_TILE
    dest3 = dest.reshape(nt, 1, DISPATCH_TILE * TOP_K)
    full = lambda shape: pl.BlockSpec(shape, lambda i: (0,) * len(shape))
    return pl.pallas_call(
        _combine_body,
        grid=(nt,),
        in_specs=[pl.BlockSpec((1, 1, DISPATCH_TILE * TOP_K), lambda i: (i, 0, 0), memory_space=pltpu.SMEM),
                  pl.BlockSpec((1, 1, DISPATCH_TILE * TOP_K), lambda i: (jnp.minimum(i + 1, nt - 1), 0, 0),
                               memory_space=pltpu.SMEM),
                  pl.BlockSpec((DISPATCH_TILE, LANE), lambda i: (i, 0)),
                  pl.BlockSpec((DISPATCH_TILE, D_MODEL), lambda i: (i, 0)),
                  full((1, D_MODEL)), full((1, D_MODEL)),
                  pl.BlockSpec(memory_space=pl.ANY)],
        out_specs=pl.BlockSpec((DISPATCH_TILE, D_MODEL), lambda i: (i, 0)),
        out_shape=jax.ShapeDtypeStruct((t, D_MODEL), F32),
        scratch_shapes=[pltpu.VMEM((2, TOP_K, DISPATCH_TILE * TOK_SUB, LANE), F32), pltpu.SemaphoreType.DMA((2,))],
        compiler_params=_cp("arbitrary"),
        name="moe_combine_ln",
    )(dest3, dest3, gates, x, g, b, ys)


def _rank_body(e_ref, rank_ref, cnt_ref, run_ref):
    i = pl.program_id(0)

    @pl.when(i == 0)
    def _():
        run_ref[...] = jnp.zeros(run_ref.shape, F32)

    e = e_ref[...]
    lane = _iota(e.shape, 1)
    hot = [lane == e[:, k:k + 1] for k in range(TOP_K)]
    m = sum(h.astype(F32) for h in hot)
    ri = _iota((ROW_TILE, ROW_TILE), 0)
    ci = _iota((ROW_TILE, ROW_TILE), 1)
    before = _dot(_bf((ci < ri).astype(F32)), _bf(m)) + run_ref[0:1, :]
    rank = jnp.zeros(e.shape, F32)
    for k in range(TOP_K):
        rank = jnp.where(lane == k, jnp.sum(jnp.where(hot[k], before, 0.0), axis=-1, keepdims=True), rank)
    rank_ref[...] = rank.astype(I32)
    run_ref[0:1, :] = run_ref[0:1, :] + jnp.sum(m, axis=0, keepdims=True)
    cnt_ref[...] = run_ref[...].astype(I32)


def moe_rank(e_pad):
    t = e_pad.shape[0]
    return pl.pallas_call(
        _rank_body,
        grid=(t // ROW_TILE,),
        in_specs=[pl.BlockSpec((ROW_TILE, LANE), lambda i: (i, 0))],
        out_specs=[pl.BlockSpec((ROW_TILE, LANE), lambda i: (i, 0)), pl.BlockSpec((8, LANE), lambda i: (0, 0))],
        out_shape=[jax.ShapeDtypeStruct((t, LANE), I32), jax.ShapeDtypeStruct((8, LANE), I32)],
        scratch_shapes=[pltpu.VMEM((8, LANE), F32)],
        compiler_params=_cp("arbitrary"),
        name="moe_rank",
    )(e_pad)


def moe_plan(e_pad, n_tokens):
    tk = n_tokens * TOP_K
    rank_pad, cnt = moe_rank(e_pad)
    top_e, rank = e_pad[:, :TOP_K], rank_pad[:, :TOP_K]
    counts = cnt[0, :N_EXPERTS]
    ntile = (counts + MOE_TILE - 1) // MOE_TILE
    tile_end = jnp.cumsum(ntile)
    tile_start = tile_end - ntile
    start_of = jnp.sum(jnp.where(top_e[..., None] == jnp.arange(N_EXPERTS, dtype=I32), tile_start, 0), axis=-1)
    dest = (start_of * MOE_TILE + rank).reshape(tk)
    n_tiles = -(-tk // MOE_TILE) + N_EXPERTS
    tiles = jnp.arange(n_tiles, dtype=I32)
    tile_e = jnp.minimum(jnp.sum((tile_end[None, :] <= tiles[:, None]).astype(I32), axis=1), N_EXPERTS - 1)
    tile_valid = (tiles < tile_end[-1]).astype(I32)
    return dest.astype(I32), tile_e, tile_valid, n_tiles * MOE_TILE


def moe_ffn_ln(x1, x1t, e_pad, gates, wgu, bgu, wd, bd, g, b, expert0=0, xs_buf=None):
    t = x1.shape[0]
    dest, tile_e, tile_valid, n_rows = moe_plan(e_pad, t)
    xs = moe_dispatch(x1t, dest, n_rows, xs_buf)
    ys = moe_experts(xs, tile_e + expert0, tile_valid, wgu, bgu, wd, bd)
    return moe_combine_ln(x1, ys, dest, gates, g, b), xs


def _rope_tables(pos):
    half = ROT_DIM // 2
    inv_freq = ROPE_THETA ** (-jnp.arange(half, dtype=F32) / half)
    ang = pos.astype(F32)[:, None] * inv_freq[None, :]
    cos, sin = jnp.cos(ang), jnp.sin(ang)
    n = pos.shape[0]
    ones = jnp.ones((n, HEAD_DIM - ROT_DIM), F32)
    cos_h = jnp.concatenate([cos, cos, ones], axis=1)
    sin_h = jnp.concatenate([-sin, sin, 0.0 * ones], axis=1)
    return jnp.concatenate([cos_h, cos_h], axis=1), jnp.concatenate([sin_h, sin_h], axis=1)


def _conv_tail(h, row0, bsz, seq):
    assert seq >= GDN_CONV - 1
    n = GDN_CONV - 1
    if bsz <= 8:
        return jnp.stack([h[row0 + (b + 1) * seq - n:row0 + (b + 1) * seq, C_GQKV:C_GQKV + GDN_CONV_DIM]
                          for b in range(bsz)], axis=0)
    blk = h[row0:row0 + bsz * seq, C_GQKV:C_GQKV + GDN_CONV_DIM].reshape(bsz, seq, GDN_CONV_DIM)
    return blk[:, seq - n:, :]


def _lane_row(v, offset):
    return jnp.zeros((1, LANE), F32).at[0, offset:offset + v.shape[0]].set(v)


def kernel(x_prompt, x_sample, cache_swa_k, cache_swa_v, state_ssm_re, state_ssm_im, state_gdn_conv, state_gdn, state_gla, w_in, w_out, attn_sinks, ssm_a_re, ssm_a_im, ssm_b_re, ssm_b_im, ssm_c_re, ssm_c_im, ssm_d, ssm_log_dt, ssm_glu_w, ssm_glu_b, gdn_conv_w, gdn_a_log, gdn_dt_bias, gdn_norm_w, gla_gate_w, gla_gate_b, gla_norm_w, ln1_g, ln1_b, ln2_g, ln2_b, router_w, router_b, moe_w_gate_up, moe_b_gate_up, moe_w_down, moe_b_down):
    bp, lp, _ = x_prompt.shape
    bs, ls, _ = x_sample.shape
    n_p, n_s = bp * lp, bs * ls
    depth = w_in.shape[0]

    w_in_r = jnp.concatenate([w_in[..., :1792], w_in[..., 1800:2568], w_in[..., 1792:1800], w_in[..., 2568:N_IN],
                              jnp.zeros(w_in.shape[:2] + (NH - N_IN,), w_in.dtype)], axis=-1).astype(BF16)
    w_out_b = w_out.astype(BF16)
    glu_w_b = ssm_glu_w.astype(BF16)
    rw_pad = jnp.pad(router_w, ((0, 0), (0, 0), (0, LANE - N_EXPERTS)))
    rb_pad = jnp.pad(router_b, ((0, 0), (0, LANE - N_EXPERTS)))
    wgu_all = moe_w_gate_up.reshape(depth * N_EXPERTS, D_MODEL, 2 * D_FF)
    bgu_all = moe_b_gate_up.reshape(depth * N_EXPERTS, 1, 2 * D_FF)
    wd_all = moe_w_down.reshape(depth * N_EXPERTS, D_FF, D_MODEL)
    bd_all = moe_b_down.reshape(depth * N_EXPERTS, 1, D_MODEL)

    cos_p, sin_p = _rope_tables(jnp.arange(lp, dtype=I32))
    cos_s, sin_s = _rope_tables(PAST_LEN + jnp.arange(ls, dtype=I32))

    x = jnp.concatenate([x_prompt.reshape(n_p, D_MODEL), x_sample.reshape(n_s, D_MODEL)], axis=0)
    zeros = lambda *s: jnp.zeros(s, F32)
    new_p = [[] for _ in range(7)]
    new_s = [[] for _ in range(7)]
    xs_buf = None
    for l in range(depth):
        h = in_proj(x, w_in_r[l])
        sinks = attn_sinks[l]
        oa_p, pk, pv = swa_prompt(h, sinks, cos_p, sin_p, bp, lp)
        oa_s, sk, sv = swa_sample(h, n_p, sinks, cache_swa_k[l].reshape(bs, WINDOW, 128),
                                  cache_swa_v[l].reshape(bs, WINDOW, 128), cos_s, sin_s, bs, ls)
        bw, a_bar, cw, dsk = s5_params(ssm_a_re[l], ssm_a_im[l], ssm_b_re[l], ssm_b_im[l], ssm_c_re[l], ssm_c_im[l],
                                       ssm_d[l], ssm_log_dt[l])
        glu_b = ssm_glu_b[l].reshape(1, 2 * GROUP_WIDTH)
        ob_p, hl_p = s5_prompt(h, zeros(bp, 2 * SSM_W), bw, a_bar, cw, dsk, glu_w_b[l], glu_b, bp, lp)
        h0_s = jnp.concatenate([state_ssm_re[l].reshape(bs, SSM_W), state_ssm_im[l].reshape(bs, SSM_W)], axis=1)
        ob_s, hl_s = s5_sample(h, n_p, h0_s, bw, a_bar, cw, dsk, glu_w_b[l], glu_b, bs, ls)
        alog_t = _lane_row(gdn_a_log[l], SM_GA)
        dtb_t = _lane_row(gdn_dt_bias[l], SM_GA)
        gnw = gdn_norm_w[l].reshape(1, GDN_DV)
        oc_p, gs_p = gdn_mixer(h, 0, gdn_conv_w[l], alog_t, dtb_t, gnw, zeros(bp, GDN_CONV - 1, GDN_CONV_DIM),
                               zeros(bp, GDN_HEADS, GDN_DK, GDN_DV), bp, lp)
        oc_s, gs_s = gdn_mixer(h, n_p, gdn_conv_w[l], alog_t, dtb_t, gnw, state_gdn_conv[l], state_gdn[l], bs, ls)
        cv_p = _conv_tail(h, 0, bp, lp)
        cv_s = _conv_tail(h, n_p, bs, ls)
        lgb = gla_gate_b[l].reshape(1, 128)
        lnw = jnp.tile(gla_norm_w[l], GLA_HEADS).reshape(1, 256)
        od_p, lt_p = gla_mixer(h, 0, gla_gate_w[l], lgb, lnw, zeros(bp, 256, 128), bp, lp)
        od_s, lt_s = gla_mixer(h, n_p, gla_gate_w[l], lgb, lnw, gla_state_to_t(state_gla[l]), bs, ls)

        x1, x1t, e_pad, gates = out_proj_ln(
            x, (oa_p, ob_p.reshape(n_p, GROUP_WIDTH), oc_p, od_p), (oa_s, ob_s, oc_s, od_s), w_out_b[l],
            ln1_g[l].reshape(1, D_MODEL), ln1_b[l].reshape(1, D_MODEL), rw_pad[l], rb_pad[l].reshape(1, LANE))
        x, xs_buf = moe_ffn_ln(x1, x1t, e_pad, gates, wgu_all, bgu_all, wd_all, bd_all, ln2_g[l].reshape(1, D_MODEL),
                               ln2_b[l].reshape(1, D_MODEL), expert0=l * N_EXPERTS, xs_buf=xs_buf)

        st_p = (pk.reshape(bp, WINDOW, A_KV_HEADS, HEAD_DIM), pv.reshape(bp, WINDOW, A_KV_HEADS, HEAD_DIM),
                hl_p[:, :SSM_W].reshape(bp, SSM_GROUPS, SSM_STATE), hl_p[:, SSM_W:].reshape(bp, SSM_GROUPS, SSM_STATE),
                cv_p, gs_p, gla_state_from_t(lt_p))
        st_s = (sk.reshape(bs, WINDOW, A_KV_HEADS, HEAD_DIM), sv.reshape(bs, WINDOW, A_KV_HEADS, HEAD_DIM),
                hl_s[:, :SSM_W].reshape(bs, SSM_GROUPS, SSM_STATE), hl_s[:, SSM_W:].reshape(bs, SSM_GROUPS, SSM_STATE),
                cv_s, gs_s, gla_state_from_t(lt_s))
        for i in range(7):
            new_p[i].append(st_p[i])
            new_s[i].append(st_s[i])
    y_p = x[:n_p].reshape(bp, lp, D_MODEL)
    y_s = x[n_p:].reshape(bs, ls, D_MODEL)
    return (y_p, y_s) + tuple(jnp.stack(t, axis=0) for t in new_p) + tuple(jnp.stack(t, axis=0) for t in new_s)
```

```python
import functools

import numpy as np
import jax
import jax.numpy as jnp
from jax import lax
from jax.experimental import pallas as pl
from jax.experimental.pallas import tpu as pltpu

F32 = jnp.float32
BF16 = jnp.bfloat16
I32 = jnp.int32
HI = lax.Precision.HIGHEST

D_MODEL = 1024
DEPTH = 4
PAST_LEN = 8192
GROUP_WIDTH = 256
HEAD_DIM = 64
A_HEADS = 4
A_KV_HEADS = 2
WINDOW = 128
ROPE_THETA = 500000.0
ROT_DIM = 16
SSM_GC = 16
SSM_GROUPS = 16
SSM_STATE = 64
SSM_W = SSM_GROUPS * SSM_STATE
GDN_HEADS = 4
GDN_DK = 64
GDN_DV = 64
GDN_CONV = 4
GDN_CONV_DIM = 768
GDN_CHUNK = 64
GLA_HEADS = 4
GLA_DK = 32
GLA_DV = 64
GLA_RANK = 16
GLA_TAU = 16.0
GLA_CHUNK = 16
N_EXPERTS = 32
TOP_K = 4
D_FF = 1024
SWIGLU_LIMIT = 7.0
SWIGLU_ALPHA = 1.702
DN_ALPHA = (2 * DEPTH) ** 0.25
LN_EPS = 1e-5
RMS_EPS = 1e-6
N_IN = 2584

C_AQ, C_AK, C_AV, C_SU, C_GQKV, C_GZ = 0, 256, 384, 512, 768, 1536
C_LQ, C_LK, C_LV, C_LR, C_SM = 1792, 1920, 2048, 2304, 2560
NH = 2688
SM_GB, SM_GA, SM_LG = 0, 4, 8

LANE = 128
ROW_TILE = 512
MOE_TILE = 512
VMEM_LIMIT = 56 * 1024 * 1024


def _cp(*sem):
    return pltpu.CompilerParams(dimension_semantics=sem, vmem_limit_bytes=VMEM_LIMIT)


def _dot(a, b, precision=None):
    return jnp.dot(a, b, preferred_element_type=F32, precision=precision)


def _dot_nt(a, b, precision=None):
    return lax.dot_general(a, b, (((1,), (1,)), ((), ())), preferred_element_type=F32, precision=precision)


def _dot_tn(a, b, precision=None):
    return lax.dot_general(a, b, (((0,), (0,)), ((), ())), preferred_element_type=F32, precision=precision)


def _bf(x):
    return x.astype(BF16)


def _iota(shape, dim):
    return lax.broadcasted_iota(I32, shape, dim)


def _shr(idx, size):
    return lax.shift_right_logical(idx, int(size).bit_length() - 1)


def _sigmoid(x):
    return 1.0 / (1.0 + jnp.exp(-x))


def _silu(x):
    return x * _sigmoid(x)


def _softplus(x):
    return jnp.maximum(x, 0.0) + jnp.log(1.0 + jnp.exp(-jnp.abs(x)))


def _log_sigmoid(x):
    return -_softplus(-x)


def _gelu_tanh(x):
    return 0.5 * x * (1.0 + jnp.tanh(0.7978845608028654 * (x + 0.044715 * x * x * x)))


def _layer_norm(y, g, b):
    mu = jnp.mean(y, axis=-1, keepdims=True)
    yc = y - mu
    var = jnp.mean(yc * yc, axis=-1, keepdims=True)
    return yc * lax.rsqrt(var + LN_EPS) * g + b


def _in_proj_body(x_ref, w_ref, o_ref):
    xb = _bf(x_ref[...])
    for c0 in range(0, NH, 512):
        c1 = min(c0 + 512, NH)
        o_ref[:, c0:c1] = _dot(xb, w_ref[:, c0:c1])


def in_proj(x, w):
    t = x.shape[0]
    return pl.pallas_call(
        _in_proj_body,
        grid=(t // ROW_TILE,),
        in_specs=[pl.BlockSpec((ROW_TILE, D_MODEL), lambda i: (i, 0)),
                  pl.BlockSpec((D_MODEL, NH), lambda i: (0, 0))],
        out_specs=pl.BlockSpec((ROW_TILE, NH), lambda i: (i, 0)),
        out_shape=jax.ShapeDtypeStruct((t, NH), F32),
        compiler_params=_cp("arbitrary"),
        name="in_proj",
    )(x, w)


def _rope(x, cos, sin):
    w = x.shape[1]
    if w > LANE:
        cos = jnp.concatenate([cos] * (w // LANE), axis=1)
        sin = jnp.concatenate([sin] * (w // LANE), axis=1)
    lane = _iota(x.shape, 1) & (HEAD_DIM - 1)
    swapped = jnp.where(lane < ROT_DIM // 2, pltpu.roll(x, w - ROT_DIM // 2, 1), pltpu.roll(x, ROT_DIM // 2, 1))
    return x * cos + swapped * sin


def _sink_attention_multi(qkvs, mask, sinks_ref):
    pairs = [(i, hq) for i in range(len(qkvs)) for hq in range(A_HEADS)]
    hd = lambda x, j: _bf(x[:, j * HEAD_DIM:(j + 1) * HEAD_DIM])
    kv = lambda hq: hq // (A_HEADS // A_KV_HEADS)
    scores = {(i, hq): _dot_nt(hd(qkvs[i][0], hq), hd(qkvs[i][1], kv(hq))) for i, hq in pairs}
    probs = {}
    for i, hq in pairs:
        s = jnp.where(mask, scores[i, hq] * (HEAD_DIM ** -0.5), -jnp.inf)
        sink = sinks_ref[hq]
        m = jnp.maximum(jnp.max(s, axis=-1, keepdims=True), sink)
        p = jnp.exp(s - m)
        den = jnp.sum(p, axis=-1, keepdims=True) + jnp.exp(sink - m)
        probs[i, hq] = _bf(p / den)
    outs = {(i, hq): _dot(probs[i, hq], hd(qkvs[i][2], kv(hq))) for i, hq in pairs}
    return [jnp.concatenate([outs[i, hq] for hq in range(A_HEADS)], axis=1) for i in range(len(qkvs))]


def _sink_attention(q, kk, vv, mask, sinks_ref):
    return _sink_attention_multi([(q, kk, vv)], mask, sinks_ref)[0]


def _swa_prompt_body(sinks_ref, cur_ref, prev_ref, cos_ref, sin_ref, cosp_ref, sinp_ref, o_ref, ko_ref, vo_ref):
    i = pl.program_id(1)
    cur = cur_ref[...]
    q = _rope(cur[:, C_AQ:C_AQ + 256], cos_ref[...], sin_ref[...])
    k = _rope(cur[:, C_AK:C_AK + 128], cos_ref[...], sin_ref[...])
    v = cur[:, C_AV:C_AV + 128]
    prev = prev_ref[...]
    kp = _rope(prev[:, 0:128], cosp_ref[...], sinp_ref[...])
    vp = prev[:, 128:256]
    kk = jnp.concatenate([kp, k], axis=0)
    vv = jnp.concatenate([vp, v], axis=0)
    r = _iota((WINDOW, 2 * WINDOW), 0)
    j = _iota((WINDOW, 2 * WINDOW), 1)
    d = WINDOW + r - j
    mask = (d >= 0) & (d <= WINDOW) & ((j >= WINDOW) | (i > 0))
    o_ref[...] = _sink_attention(q, kk, vv, mask, sinks_ref)
    ko_ref[0] = k
    vo_ref[0] = v


def swa_prompt(h, sinks, cos_t, sin_t, bsz, seq):
    nb = seq // WINDOW
    smem = pl.BlockSpec(memory_space=pltpu.SMEM)
    tab = lambda f: pl.BlockSpec((WINDOW, LANE), f)
    return pl.pallas_call(
        _swa_prompt_body,
        grid=(bsz, nb),
        in_specs=[smem,
                  pl.BlockSpec((WINDOW, 512), lambda b, i: (b * nb + i, 0)),
                  pl.BlockSpec((WINDOW, 256), lambda b, i: (b * nb + jnp.maximum(i - 1, 0), 1)),
                  tab(lambda b, i: (i, 0)), tab(lambda b, i: (i, 0)),
                  tab(lambda b, i: (jnp.maximum(i - 1, 0), 0)), tab(lambda b, i: (jnp.maximum(i - 1, 0), 0))],
        out_specs=[pl.BlockSpec((WINDOW, 256), lambda b, i: (b * nb + i, 0)),
                   pl.BlockSpec((1, WINDOW, 128), lambda b, i: (b, 0, 0)),
                   pl.BlockSpec((1, WINDOW, 128), lambda b, i: (b, 0, 0))],
        out_shape=[jax.ShapeDtypeStruct((bsz * seq, 256), F32),
                   jax.ShapeDtypeStruct((bsz, WINDOW, 128), F32),
                   jax.ShapeDtypeStruct((bsz, WINDOW, 128), F32)],
        compiler_params=_cp("arbitrary", "arbitrary"),
        name="swa_prompt",
    )(sinks, h, h, cos_t, sin_t, cos_t, sin_t)


SWA_SB = 8


def _swa_sample_body(sinks_ref, cur_ref, kc_ref, vc_ref, cos_ref, sin_ref, o_ref, ko_ref, vo_ref, *, ls):
    cw = WINDOW
    r = _iota((ls, cw + ls), 0)
    j = _iota((ls, cw + ls), 1)
    d = cw + r - j
    mask = (d >= 0) & (d <= WINDOW)
    qkvs = []
    for b in range(SWA_SB):
        cur = cur_ref[b * ls:(b + 1) * ls, :]
        q = _rope(cur[:, C_AQ:C_AQ + 256], cos_ref[...], sin_ref[...])
        k = _rope(cur[:, C_AK:C_AK + 128], cos_ref[...], sin_ref[...])
        v = cur[:, C_AV:C_AV + 128]
        kk = jnp.concatenate([kc_ref[b], k], axis=0)
        vv = jnp.concatenate([vc_ref[b], v], axis=0)
        ko_ref[b] = kk[ls:, :]
        vo_ref[b] = vv[ls:, :]
        qkvs.append((q, kk, vv))
    for b, o in enumerate(_sink_attention_multi(qkvs, mask, sinks_ref)):
        o_ref[b * ls:(b + 1) * ls, :] = o


def swa_sample(h, row0, sinks, k_cache, v_cache, cos_t, sin_t, bsz, ls):
    rows = SWA_SB * ls
    blk0 = row0 // rows
    smem = pl.BlockSpec(memory_space=pltpu.SMEM)
    cache = pl.BlockSpec((SWA_SB, WINDOW, 128), lambda i: (i, 0, 0))
    tab = pl.BlockSpec((ls, LANE), lambda i: (0, 0))
    return pl.pallas_call(
        functools.partial(_swa_sample_body, ls=ls),
        grid=(bsz // SWA_SB,),
        in_specs=[smem, pl.BlockSpec((rows, 512), lambda i: (blk0 + i, 0)), cache, cache, tab, tab],
        out_specs=[pl.BlockSpec((rows, 256), lambda i: (i, 0)), cache, cache],
        out_shape=[jax.ShapeDtypeStruct((bsz * ls, 256), F32),
                   jax.ShapeDtypeStruct((bsz, WINDOW, 128), F32),
                   jax.ShapeDtypeStruct((bsz, WINDOW, 128), F32)],
        compiler_params=_cp("arbitrary"),
        name="swa_sample",
    )(sinks, h, k_cache, v_cache, cos_t, sin_t)


S5_NB = 2 * SSM_W // LANE


def _s5_input(u, bw_ref):
    t = _dot(_bf(u), bw_ref[...])
    return jnp.concatenate([t[:, :SSM_W], t[:, :SSM_W] + t[:, SSM_W:]], axis=1)


def _s5_output(hs, u, cw_ref, d_ref, gw_ref, gb_ref):
    hs = jnp.concatenate([hs[:, :SSM_W] + hs[:, SSM_W:], hs[:, SSM_W:]], axis=1)
    y = _dot(_bf(hs), cw_ref[...]) + d_ref[...] * u
    y = _gelu_tanh(y)
    z = _dot(_bf(y), gw_ref[...]) + gb_ref[...]
    return z[:, :GROUP_WIDTH] * _sigmoid(z[:, GROUP_WIDTH:])


def _s5_scan(s_ref, a_ref, h_init, n_steps, rows):
    nre = S5_NB // 2
    a_re = [jnp.broadcast_to(a_ref[0:1, j * LANE:(j + 1) * LANE], (rows, LANE)) for j in range(nre)]
    a_im = [jnp.broadcast_to(a_ref[1:2, j * LANE:(j + 1) * LANE], (rows, LANE)) for j in range(nre)]

    def step(t, hcar):
        out = [None] * S5_NB
        base = pl.multiple_of(t * rows, rows)
        for j in range(nre):
            hr, hi = hcar[j], hcar[j + nre]
            nr = a_re[j] * hr - a_im[j] * hi + s_ref[j, pl.ds(base, rows), :]
            ni = a_re[j] * hi + a_im[j] * hr + s_ref[j + nre, pl.ds(base, rows), :]
            s_ref[j, pl.ds(base, rows), :] = nr
            s_ref[j + nre, pl.ds(base, rows), :] = ni
            out[j], out[j + nre] = nr, ni
        return tuple(out)

    return lax.fori_loop(0, n_steps, step, tuple(h_init))


def _s5_prompt_body(*refs, nb, tl):
    u_refs = refs[:nb]
    bw_ref, a_ref, cw_ref, d_ref, gw_ref, gb_ref, h0_ref, o_ref, hl_ref, s_ref, hst_ref = refs[nb:]
    i = pl.program_id(0)

    @pl.when(i == 0)
    def _():
        hst_ref[...] = h0_ref[...]

    u_all = jnp.concatenate([u_refs[b][...] for b in range(nb)], axis=0)
    bu = _s5_input(u_all, bw_ref)
    for b in range(nb):
        for j in range(S5_NB):
            s_ref[j, pl.ds(b, tl, stride=nb), :] = bu[b * tl:(b + 1) * tl, j * LANE:(j + 1) * LANE]
    h_init = [hst_ref[:, j * LANE:(j + 1) * LANE] for j in range(S5_NB)]
    h_fin = _s5_scan(s_ref, a_ref, h_init, tl, nb)
    for j in range(S5_NB):
        hst_ref[:, j * LANE:(j + 1) * LANE] = h_fin[j]
    hs = jnp.concatenate([jnp.concatenate([s_ref[j, pl.ds(b, tl, stride=nb), :] for j in range(S5_NB)], axis=1)
                          for b in range(nb)], axis=0)
    o_all = _s5_output(hs, u_all, cw_ref, d_ref, gw_ref, gb_ref)
    for b in range(nb):
        o_ref[b] = o_all[b * tl:(b + 1) * tl, :]

    @pl.when(i == pl.num_programs(0) - 1)
    def _():
        hl_ref[...] = hst_ref[...]


def s5_prompt(h, h0, bw, a, cw, dsk, gw, gb, bsz, seq, tl=128):
    assert bsz == 8
    nt = seq // tl
    full = lambda shape: pl.BlockSpec(shape, lambda i: (0,) * len(shape))
    u_specs = [pl.BlockSpec((tl, GROUP_WIDTH), functools.partial(lambda i, b: (b * nt + i, C_SU // GROUP_WIDTH), b=b))
               for b in range(bsz)]
    return pl.pallas_call(
        functools.partial(_s5_prompt_body, nb=bsz, tl=tl),
        grid=(nt,),
        in_specs=u_specs + [full((GROUP_WIDTH, 2 * SSM_W)), full((2, SSM_W)), full((2 * SSM_W, GROUP_WIDTH)),
                            full((1, GROUP_WIDTH)), full((GROUP_WIDTH, 2 * GROUP_WIDTH)), full((1, 2 * GROUP_WIDTH)),
                            full((bsz, 2 * SSM_W))],
        out_specs=[pl.BlockSpec((bsz, tl, GROUP_WIDTH), lambda i: (0, i, 0)), full((bsz, 2 * SSM_W))],
        out_shape=[jax.ShapeDtypeStruct((bsz, seq, GROUP_WIDTH), F32), jax.ShapeDtypeStruct((bsz, 2 * SSM_W), F32)],
        scratch_shapes=[pltpu.VMEM((S5_NB, tl * bsz, LANE), F32), pltpu.VMEM((bsz, 2 * SSM_W), F32)],
        compiler_params=_cp("arbitrary"),
        name="s5_prompt",
    )(*([h] * bsz), bw, a, cw, dsk, gw, gb, h0)


def _s5_sample_body(u_ref, bw_ref, a_ref, cw_ref, d_ref, gw_ref, gb_ref, h0_ref, o_ref, hl_ref, s_ref, t_ref, *, bsz, ls):
    bu = _s5_input(u_ref[...], bw_ref)
    for j in range(S5_NB):
        s_ref[j] = bu[:, j * LANE:(j + 1) * LANE]
    for t in range(ls):
        for j in range(S5_NB):
            t_ref[j, pl.ds(t * bsz, bsz), :] = s_ref[j, pl.ds(t, bsz, stride=ls), :]
    h_init = [h0_ref[:, j * LANE:(j + 1) * LANE] for j in range(S5_NB)]
    h_fin = _s5_scan(t_ref, a_ref, h_init, ls, bsz)
    for j in range(S5_NB):
        hl_ref[:, j * LANE:(j + 1) * LANE] = h_fin[j]
    for t in range(ls):
        for j in range(S5_NB):
            s_ref[j, pl.ds(t, bsz, stride=ls), :] = t_ref[j, pl.ds(t * bsz, bsz), :]
    hs = jnp.concatenate([s_ref[j] for j in range(S5_NB)], axis=1)
    o_ref[...] = _s5_output(hs, u_ref[...], cw_ref, d_ref, gw_ref, gb_ref)


def s5_sample(h, row0, h0, bw, a, cw, dsk, gw, gb, bsz, ls):
    rows = bsz * ls
    full = lambda shape: pl.BlockSpec(shape, lambda i: (0,) * len(shape))
    return pl.pallas_call(
        functools.partial(_s5_sample_body, bsz=bsz, ls=ls),
        grid=(1,),
        in_specs=[pl.BlockSpec((rows, GROUP_WIDTH), lambda i: (row0 // rows, C_SU // GROUP_WIDTH)),
                  full((GROUP_WIDTH, 2 * SSM_W)), full((2, SSM_W)), full((2 * SSM_W, GROUP_WIDTH)),
                  full((1, GROUP_WIDTH)), full((GROUP_WIDTH, 2 * GROUP_WIDTH)), full((1, 2 * GROUP_WIDTH)),
                  full((bsz, 2 * SSM_W))],
        out_specs=[full((rows, GROUP_WIDTH)), full((bsz, 2 * SSM_W))],
        out_shape=[jax.ShapeDtypeStruct((rows, GROUP_WIDTH), F32), jax.ShapeDtypeStruct((bsz, 2 * SSM_W), F32)],
        scratch_shapes=[pltpu.VMEM((S5_NB, rows, LANE), F32), pltpu.VMEM((S5_NB, rows, LANE), F32)],
        compiler_params=_cp("arbitrary"),
        name="s5_sample",
    )(h, bw, a, cw, dsk, gw, gb, h0)


def s5_params(a_re, a_im, b_re, b_im, c_re, c_im, d_skip, log_dt):
    lam = lax.complex(a_re, a_im)
    delta = jnp.exp(log_dt)[:, None]
    a_bar = jnp.exp(lam * delta)
    b_bar = ((a_bar - 1.0) / lam)[..., None] * lax.complex(b_re, b_im)
    eye = jnp.eye(SSM_GROUPS, dtype=F32)
    bw_re = jnp.einsum('gpc,gh->gchp', b_bar.real, eye).reshape(GROUP_WIDTH, SSM_W)
    bw_im = jnp.einsum('gpc,gh->gchp', b_bar.imag, eye).reshape(GROUP_WIDTH, SSM_W)
    bw = jnp.concatenate([bw_re, bw_im - bw_re], axis=1).astype(BF16)
    cw_re = jnp.einsum('gcp,gh->gphc', c_re, eye).reshape(SSM_W, GROUP_WIDTH)
    cw_im = jnp.einsum('gcp,gh->gphc', c_im, eye).reshape(SSM_W, GROUP_WIDTH)
    cw = jnp.concatenate([cw_re, -(cw_re + cw_im)], axis=0).astype(BF16)
    a = jnp.stack([a_bar.real.reshape(SSM_W), a_bar.imag.reshape(SSM_W)], axis=0)
    return bw, a, cw, d_skip.reshape(1, GROUP_WIDTH)


GDN_RT = 256
GDN_ST = 64
GDN_SG = 8


def _split(x):
    hi = _bf(x)
    return hi, _bf(x - hi.astype(F32))


def _dot3(a, b):
    return _dot(a[0], b[0]) + (_dot(a[0], b[1]) + _dot(a[1], b[0]))


def _gdn_prep_body(*refs, c, sample, tiles_per_seq):
    if sample:
        x_ref, cs_ref, z_ref, sm_ref, cw_ref, alog_ref, dtb_ref = refs[:7]
        outs = refs[7:14]
        xs_ref, cb_ref = refs[14:]
    else:
        x_ref, prev_ref, cs_ref, z_ref, sm_ref, cw_ref, alog_ref, dtb_ref = refs[:8]
        outs = refs[8:15]
        (xs_ref,) = refs[15:]
    u_ref, w_ref, qk_ref, qg_ref, kd_ref, eg_ref, zs_ref = outs
    rt = GDN_RT
    x = x_ref[...]
    xs_ref[8:8 + rt, :] = x
    if sample:
        xs_ref[0:8, :] = jnp.zeros((8, GDN_CONV_DIM), F32)
        cb_ref[0:rt, :] = cs_ref[...]
        cb_ref[rt:rt + 8, :] = jnp.zeros((8, GDN_CONV_DIM), F32)
        pos = _iota((rt, GDN_CONV_DIM), 0) & (c - 1)
        shifted = lambda i: jnp.where(pos >= i, xs_ref[8 - i:8 - i + rt, :], cb_ref[8 - i:8 - i + rt, :])
    else:
        first = (pl.program_id(0) % tiles_per_seq) == 0
        xs_ref[0:8, :] = jnp.where(first, cs_ref[...], prev_ref[...])
        shifted = lambda i: xs_ref[8 - i:8 - i + rt, :]
    conv = shifted(3) * cw_ref[0:1, :]
    conv = conv + shifted(2) * cw_ref[1:2, :]
    conv = conv + shifted(1) * cw_ref[2:3, :]
    conv = conv + x * cw_ref[3:4, :]
    qkv = _silu(conv)
    zs_ref[...] = _silu(z_ref[...])

    sm = sm_ref[...]
    beta_all = _sigmoid(sm)
    g_all = -jnp.exp(alog_ref[...]) * _softplus(sm + dtb_ref[...])
    st = GDN_ST
    ri = _iota((st, st), 0)
    ci = _iota((st, st), 1)
    same = _shr(ri, c) == _shr(ci, c)
    low = (ci <= ri) & same
    tri = low.astype(F32)
    last = (ci == (ri | (c - 1))).astype(F32)
    eye = (ri == ci).astype(F32)
    lane = _iota((st, LANE), 1)
    nsub = rt // st
    probs = [(t, hh) for t in range(nsub) for hh in range(GDN_HEADS)]
    gc_alls, gl_alls = [], []
    for t in range(nsub):
        rows = slice(t * st, (t + 1) * st)
        gc_alls.append(_dot_nt_exact_lhs(_bf(tri), g_all[rows, :]))
    for t in range(nsub):
        gl_alls.append(_dot_nt_exact_lhs(_bf(last), gc_alls[t]))
        eg_ref[t * st:(t + 1) * st, :] = jnp.exp(gl_alls[t])
    qs, ks, rhss, decays, a_stricts, invs = {}, {}, {}, {}, {}, {}
    for t, hh in probs:
        rows = slice(t * st, (t + 1) * st)
        q = qkv[rows, hh * GDN_DK:(hh + 1) * GDN_DK]
        k = qkv[rows, 256 + hh * GDN_DK:256 + (hh + 1) * GDN_DK]
        v = qkv[rows, 512 + hh * GDN_DV:512 + (hh + 1) * GDN_DV]
        q = q * lax.rsqrt(jnp.sum(q * q, axis=-1, keepdims=True) + 1e-6) * (GDN_DK ** -0.5)
        k = k * lax.rsqrt(jnp.sum(k * k, axis=-1, keepdims=True) + 1e-6)
        beta = beta_all[rows, SM_GB + hh:SM_GB + hh + 1]
        gc = gc_alls[t][:, SM_GA + hh:SM_GA + hh + 1]
        gl = gl_alls[t][:, SM_GA + hh:SM_GA + hh + 1]
        sel = (lane == SM_GA + hh).astype(F32)
        gc_row = _dot_nt(sel, gc_alls[t], HI)
        decays[t, hh] = jnp.exp(jnp.where(low, gc - gc_row, -jnp.inf))
        kb = k * beta
        eg = jnp.exp(gc)
        qs[t, hh], ks[t, hh] = q, k
        rhss[t, hh] = _split(jnp.concatenate([v * beta, kb * eg], axis=1))
        a_stricts[t, hh] = jnp.where(ci < ri, _dot_nt(_bf(kb), _bf(k)) * decays[t, hh], 0.0)
        qg_ref[rows, hh * GDN_DK:(hh + 1) * GDN_DK] = q * eg
        kd_ref[rows, hh * GDN_DK:(hh + 1) * GDN_DK] = k * jnp.exp(gl - gc)
    for t, hh in probs:
        qk_ref[t * st:(t + 1) * st, hh * st:(hh + 1) * st] = _dot_nt(_bf(qs[t, hh]), _bf(ks[t, hh])) * decays[t, hh]
    for p in probs:
        invs[p] = eye - jnp.where(_shr(ri, 2) == _shr(ci, 2), a_stricts[p], 0.0)
    s = 2
    while s < c:
        pair = (_shr(ri, 2 * s) == _shr(ci, 2 * s)) & (_shr(ri, s) != _shr(ci, s))
        inv_s = {p: _split(invs[p]) for p in probs}
        mid = {p: _dot3(_split(jnp.where(pair, a_stricts[p], 0.0)), inv_s[p]) for p in probs}
        for p in probs:
            invs[p] = invs[p] - _dot3(inv_s[p], _split(mid[p]))
        s *= 2
    for t, hh in probs:
        sol = _dot3(_split(invs[t, hh]), rhss[t, hh])
        u_ref[t * st:(t + 1) * st, hh * GDN_DV:(hh + 1) * GDN_DV] = sol[:, :GDN_DV]
        w_ref[t * st:(t + 1) * st, hh * GDN_DV:(hh + 1) * GDN_DV] = sol[:, GDN_DV:]


def _gdn_scan_body(u_ref, w_ref, qk_ref, qg_ref, kd_ref, eg_ref, zs_ref, nw_ref, s0_ref, o_ref, so_ref, st_ref, *, c):
    n = pl.program_id(1)

    @pl.when(n == 0)
    def _():
        st_ref[...] = s0_ref[...]

    probs = [(s, hh) for s in range(GDN_SG) for hh in range(GDN_HEADS)]
    hsl = lambda hh: slice(hh * GDN_DV, (hh + 1) * GDN_DV)
    state_b = {(s, hh): _bf(st_ref[s, hh]) for s, hh in probs}
    v_b = {}
    for s, hh in probs:
        v_b[s, hh] = _bf(u_ref[s, :, hsl(hh)] - _dot(_bf(w_ref[s, :, hsl(hh)]), state_b[s, hh]))
    o_inter = {(s, hh): _dot(_bf(qg_ref[s, :, hsl(hh)]), state_b[s, hh]) for s, hh in probs}
    for s, hh in probs:
        col0 = (s * c) % GDN_ST
        qk = qk_ref[s, :, hh * GDN_ST + col0:hh * GDN_ST + col0 + c]
        o = o_inter[s, hh] + _dot(_bf(qk), v_b[s, hh])
        o = o * lax.rsqrt(jnp.mean(o * o, axis=-1, keepdims=True) + RMS_EPS) * nw_ref[...]
        o_ref[s, :, hsl(hh)] = o * zs_ref[s, :, hsl(hh)]
    for s, hh in probs:
        eg_last = eg_ref[s, c - 1:c, SM_GA + hh:SM_GA + hh + 1]
        st_ref[s, hh] = st_ref[s, hh] * eg_last + _dot_tn(_bf(kd_ref[s, :, hsl(hh)]), v_b[s, hh])

    @pl.when(n == pl.num_programs(1) - 1)
    def _():
        so_ref[...] = st_ref[...]


def gdn_mixer(h, row0, conv_w, alog_t, dtb_t, norm_w, conv_state, s0, bsz, seq):
    c = min(GDN_CHUNK, seq)
    sample = seq == c
    rows = bsz * seq
    rt = GDN_RT
    nt = rows // rt
    blk0 = row0 // rt
    tps = max(seq // rt, 1)
    cs_rows = jnp.pad(conv_state, ((0, 0), (8 - (GDN_CONV - 1), 0), (0, 0))).reshape(bsz * 8, GDN_CONV_DIM)
    row = lambda w, col: pl.BlockSpec((rt, w), lambda i: (blk0 + i, col))
    full = lambda shape: pl.BlockSpec(shape, lambda i: (0,) * len(shape))
    if sample:
        aux_specs = [pl.BlockSpec((rt, GDN_CONV_DIM), lambda i: (i, 0))]
        aux = [cs_rows]
        scratch = [pltpu.VMEM((8 + rt, GDN_CONV_DIM), F32), pltpu.VMEM((8 + rt, GDN_CONV_DIM), F32)]
    else:
        aux_specs = [pl.BlockSpec((8, GDN_CONV_DIM), lambda i: (jnp.maximum((row0 + i * rt) // 8 - 1, 0), C_GQKV // GDN_CONV_DIM)),
                     pl.BlockSpec((8, GDN_CONV_DIM), lambda i: (i // tps, 0))]
        aux = [h, cs_rows]
        scratch = [pltpu.VMEM((8 + rt, GDN_CONV_DIM), F32)]
    out_w = [GROUP_WIDTH] * 5 + [LANE, GROUP_WIDTH]
    prep = pl.pallas_call(
        functools.partial(_gdn_prep_body, c=c, sample=sample, tiles_per_seq=tps),
        grid=(nt,),
        in_specs=[row(GDN_CONV_DIM, C_GQKV // GDN_CONV_DIM)] + aux_specs
                 + [row(GROUP_WIDTH, C_GZ // GROUP_WIDTH), row(LANE, C_SM // LANE),
                    full((GDN_CONV, GDN_CONV_DIM)), full((1, LANE)), full((1, LANE))],
        out_specs=[pl.BlockSpec((rt, w), lambda i: (i, 0)) for w in out_w],
        out_shape=[jax.ShapeDtypeStruct((rows, w), F32) for w in out_w],
        scratch_shapes=scratch,
        compiler_params=_cp("arbitrary"),
        name="gdn_prep",
    )(h, *aux, h, h, conv_w, alog_t, dtb_t)
    prep = [a.reshape(bsz, seq, a.shape[-1]) for a in prep]
    ng, nc = bsz // GDN_SG, seq // c
    seq_blk = lambda w: pl.BlockSpec((GDN_SG, c, w), lambda g, n: (g, n, 0))
    st_spec = pl.BlockSpec((GDN_SG, GDN_HEADS, GDN_DK, GDN_DV), lambda g, n: (g, 0, 0, 0))
    o, s_fin = pl.pallas_call(
        functools.partial(_gdn_scan_body, c=c),
        grid=(ng, nc),
        in_specs=[seq_blk(w) for w in out_w] + [pl.BlockSpec((1, GDN_DV), lambda g, n: (0, 0)), st_spec],
        out_specs=[seq_blk(GROUP_WIDTH), st_spec],
        out_shape=[jax.ShapeDtypeStruct((bsz, seq, GROUP_WIDTH), F32),
                   jax.ShapeDtypeStruct((bsz, GDN_HEADS, GDN_DK, GDN_DV), F32)],
        scratch_shapes=[pltpu.VMEM((GDN_SG, GDN_HEADS, GDN_DK, GDN_DV), F32)],
        compiler_params=_cp("arbitrary", "arbitrary"),
        name="gdn_scan",
    )(*prep, norm_w, s0)
    return o.reshape(rows, GROUP_WIDTH), s_fin


GLA_RT = 256


def _dot_exact(x, w):
    hi = _bf(x)
    r1 = x - hi.astype(F32)
    mid = _bf(r1)
    lo = _bf(r1 - mid.astype(F32))
    return _dot(hi, w) + (_dot(mid, w) + _dot(lo, w))


def _gla_body(qk_ref, vr_ref, sm_ref, gw_ref, gb_ref, nw_ref, s0_ref, o_ref, so_ref,
              kb_ref, bb_ref, vb_ref, st_ref, *, c, seq):
    tr = GLA_RT
    n = pl.program_id(1)
    pad = GLA_CHUNK
    carried = seq >= tr

    @pl.when(n == 0)
    def _():
        kb_ref[0:pad, :] = jnp.zeros((pad, 128), F32)
        bb_ref[0:pad, :] = jnp.zeros((pad, 128), F32)
        vb_ref[0:pad, :] = jnp.zeros((pad, 256), F32)
        if carried:
            st_ref[...] = s0_ref[0]

    q = qk_ref[:, 0:128] * (GLA_DK ** -0.5)
    k = qk_ref[:, 128:256]
    v = vr_ref[:, 0:256]
    lg = sm_ref[:, SM_LG:SM_LG + GLA_RANK]
    log_a = _log_sigmoid(_dot(_bf(lg), _bf(gw_ref[...])) + gb_ref[...]) / GLA_TAU
    ri = _iota((tr, tr), 0)
    ci = _iota((tr, tr), 1)
    tri = _bf(((ci <= ri) & (_shr(ri, c) == _shr(ci, c))).astype(F32))
    bc = _dot_nt_exact_lhs(tri, log_a)
    kb_ref[pad:pad + tr, :] = k
    bb_ref[pad:pad + tr, :] = bc
    vb_ref[pad:pad + tr, :] = v
    ones_kv = _bf((_shr(_iota((128, 256), 0), GLA_DK) == _shr(_iota((128, 256), 1), GLA_DV)).astype(F32))
    pos = _iota((tr, 128), 0) & (c - 1)
    ps = []
    for dl in range(c):
        ks = kb_ref[pad - dl:pad - dl + tr, :]
        bs = bb_ref[pad - dl:pad - dl + tr, :]
        ps.append(jnp.where(pos >= dl, q * ks * jnp.exp(bc - bs), 0.0))
    att = _dot_exact(jnp.concatenate(ps, axis=0), ones_kv)
    o_intra = jnp.zeros((tr, 256), F32)
    for dl in range(c):
        vs = vb_ref[pad - dl:pad - dl + tr, :]
        o_intra = o_intra + _bf(att[dl * tr:(dl + 1) * tr, :]).astype(F32) * _bf(vs).astype(F32)
    bd = (_shr(_iota((256, 128), 0), GLA_DV) == _shr(_iota((256, 128), 1), GLA_DK)).astype(F32)
    nch = tr // c
    sls = [slice(ch * c, (ch + 1) * c) for ch in range(nch)]
    b_last = [bc[sl, :][c - 1:c, :] for sl in sls]
    incs = [bd * _dot_tn(_bf(v[sl, :]), _bf(k[sl, :] * jnp.exp(bl - bc[sl, :]))) for sl, bl in zip(sls, b_last)]
    outs = []
    st = st_ref[...] if carried else None
    for ch, sl in enumerate(sls):
        if not carried:
            st = s0_ref[(ch * c) // seq]
        outs.append(_dot_nt(_bf(q[sl, :] * jnp.exp(bc[sl, :])), _bf(st)) + o_intra[sl, :])
        st = st * jnp.exp(b_last[ch]) + incs[ch]
        if not carried:
            so_ref[(ch * c) // seq] = st
    if carried:
        st_ref[...] = st
        so_ref[0] = st
    o = jnp.concatenate(outs, axis=0)
    ones_vv = _bf((_shr(_iota((256, 256), 0), GLA_DV) == _shr(_iota((256, 256), 1), GLA_DV)).astype(F32))
    ms = _dot_exact(o * o, ones_vv) * (1.0 / GLA_DV)
    o_ref[...] = o * lax.rsqrt(ms + RMS_EPS) * nw_ref[...] * _silu(vr_ref[:, 256:512])


def _dot_nt_exact_lhs(w, x):
    hi = _bf(x)
    r1 = x - hi.astype(F32)
    mid = _bf(r1)
    lo = _bf(r1 - mid.astype(F32))
    return _dot(w, hi) + (_dot(w, mid) + _dot(w, lo))


def gla_mixer(h, row0, gate_w, gate_b, norm_w4, s0t, bsz, seq):
    c = min(GLA_CHUNK, seq)
    tr = GLA_RT
    spt = max(tr // seq, 1)
    nt = max(seq // tr, 1)
    ng = bsz // spt
    blk0 = row0 // tr
    row = lambda w, col: pl.BlockSpec((tr, w), lambda g, n: (blk0 + g * nt + n, col))
    full = lambda shape: pl.BlockSpec(shape, lambda g, n: (0,) * len(shape))
    st_spec = pl.BlockSpec((spt, 256, 128), lambda g, n: (g, 0, 0))
    return pl.pallas_call(
        functools.partial(_gla_body, c=c, seq=seq),
        grid=(ng, nt),
        in_specs=[row(256, C_LQ // 256), row(512, C_LV // 512), row(LANE, C_SM // LANE),
                  full((GLA_RANK, 128)), full((1, 128)), full((1, 256)), st_spec],
        out_specs=[pl.BlockSpec((tr, 256), lambda g, n: (g * nt + n, 0)), st_spec],
        out_shape=[jax.ShapeDtypeStruct((bsz * seq, 256), F32), jax.ShapeDtypeStruct((bsz, 256, 128), F32)],
        scratch_shapes=[pltpu.VMEM((GLA_CHUNK + tr, 128), F32), pltpu.VMEM((GLA_CHUNK + tr, 128), F32),
                        pltpu.VMEM((GLA_CHUNK + tr, 256), F32), pltpu.VMEM((256, 128), F32)],
        compiler_params=_cp("arbitrary", "arbitrary"),
        name="gla",
    )(h, h, h, gate_w, gate_b, norm_w4, s0t)


def gla_state_to_t(s):
    eye = jnp.eye(GLA_HEADS, dtype=s.dtype)
    return jnp.einsum('bhde,hg->bhegd', s, eye).reshape(s.shape[0], 256, 128)


def gla_state_from_t(st):
    b = st.shape[0]
    t5 = st.reshape(b, GLA_HEADS, GLA_DV, GLA_HEADS, GLA_DK)
    diag = jnp.stack([t5[:, hh, :, hh, :] for hh in range(GLA_HEADS)], axis=1)
    return jnp.swapaxes(diag, 2, 3)


TOK_SUB = D_MODEL // LANE


def _to_token_tiles(ref, row0, y):
    n = y.shape[0]
    for s in range(TOK_SUB):
        ref[pl.ds(row0 * TOK_SUB + s, n, stride=TOK_SUB), :] = y[:, s * LANE:(s + 1) * LANE]


def _from_token_tiles(ref, row0, n, s):
    return ref[pl.ds(row0 * TOK_SUB + s, n, stride=TOK_SUB), :]


def _out_proj_body(x_ref, ap, bp, cp, dp, a_s, b_s, c_s, d_s, w_ref, g_ref, b_ref, rw_ref, rb_ref,
                   o_ref, t_ref, e_ref, gt_ref, *, n_prompt_tiles):
    i = pl.program_id(0)

    def run(refs):
        acc = DN_ALPHA * x_ref[...]
        for m, r in enumerate(refs):
            acc = acc + _dot(_bf(r[...]), w_ref[m * GROUP_WIDTH:(m + 1) * GROUP_WIDTH, :])
        y = _layer_norm(acc, g_ref[...], b_ref[...])
        o_ref[...] = y
        _to_token_tiles(t_ref, 0, y)
        e_ref[...], gt_ref[...] = _route(y, rw_ref, rb_ref)

    @pl.when(i < n_prompt_tiles)
    def _():
        run((ap, bp, cp, dp))

    @pl.when(i >= n_prompt_tiles)
    def _():
        run((a_s, b_s, c_s, d_s))


def out_proj_ln(x, mix_p, mix_s, w, g, b, rw, rb):
    t = x.shape[0]
    npt = mix_p[0].shape[0] // ROW_TILE
    nst = mix_s[0].shape[0] // ROW_TILE
    p_spec = pl.BlockSpec((ROW_TILE, GROUP_WIDTH), lambda i: (jnp.minimum(i, npt - 1), 0))
    s_spec = pl.BlockSpec((ROW_TILE, GROUP_WIDTH), lambda i: (jnp.clip(i - npt, 0, nst - 1), 0))
    full = lambda shape: pl.BlockSpec(shape, lambda i: (0,) * len(shape))
    return pl.pallas_call(
        functools.partial(_out_proj_body, n_prompt_tiles=npt),
        grid=(t // ROW_TILE,),
        in_specs=[pl.BlockSpec((ROW_TILE, D_MODEL), lambda i: (i, 0))] + [p_spec] * 4 + [s_spec] * 4
                 + [full((D_MODEL, D_MODEL)), full((1, D_MODEL)), full((1, D_MODEL)), full((D_MODEL, LANE)), full((1, LANE))],
        out_specs=[pl.BlockSpec((ROW_TILE, D_MODEL), lambda i: (i, 0)),
                   pl.BlockSpec((ROW_TILE * TOK_SUB, LANE), lambda i: (i, 0)),
                   pl.BlockSpec((ROW_TILE, LANE), lambda i: (i, 0)), pl.BlockSpec((ROW_TILE, LANE), lambda i: (i, 0))],
        out_shape=[jax.ShapeDtypeStruct((t, D_MODEL), F32), jax.ShapeDtypeStruct((t * TOK_SUB, LANE), F32),
                   jax.ShapeDtypeStruct((t, LANE), I32), jax.ShapeDtypeStruct((t, LANE), F32)],
        compiler_params=_cp("arbitrary"),
        name="out_proj_ln",
    )(x, *mix_p, *mix_s, w, g, b, rw, rb)


def _route(x, w_ref, b_ref):
    logits = _dot(_bf(x), _bf(w_ref[...])) + b_ref[...]
    lane = _iota(logits.shape, 1)
    vals = jnp.where(lane < N_EXPERTS, logits, -jnp.inf)
    eo = jnp.zeros(logits.shape, I32)
    top = []
    for k in range(TOP_K):
        m = jnp.max(vals, axis=-1, keepdims=True)
        idx = jnp.min(jnp.where(vals == m, lane.astype(F32), float(LANE)), axis=-1, keepdims=True).astype(I32)
        eo = jnp.where(lane == k, idx, eo)
        top.append(m)
        vals = jnp.where(lane == idx, -jnp.inf, vals)
    ex = [jnp.exp(v - top[0]) for v in top]
    den = ex[0] + ex[1] + ex[2] + ex[3]
    go = jnp.zeros(logits.shape, F32)
    for k in range(TOP_K):
        go = jnp.where(lane == k, ex[k] / den, go)
    return eo, go


DISPATCH_TILE = 256


def _tok(i):
    return pl.ds(pl.multiple_of(i * TOK_SUB, TOK_SUB), TOK_SUB)


def _dispatch_copy(x_ref, xs_ref, sem, r, d):
    return pltpu.make_async_copy(x_ref.at[_tok(r), :], xs_ref.at[_tok(d), :], sem)


def _dispatch_body(dest_ref, x_ref, xs_in_ref, xs_ref, sem):
    del xs_in_ref

    def start(r, carry):
        for k in range(TOP_K):
            _dispatch_copy(x_ref, xs_ref, sem, r, dest_ref[0, 0, r * TOP_K + k]).start(priority=k % 2)
        return carry

    lax.fori_loop(0, DISPATCH_TILE, start, 0)

    def wait(r, carry):
        for k in range(TOP_K):
            _dispatch_copy(x_ref, xs_ref, sem, 0, 0).wait()
        return carry

    lax.fori_loop(0, DISPATCH_TILE, wait, 0)


def moe_dispatch(xt, dest, n_rows, xs_buf=None):
    t = xt.shape[0] // TOK_SUB
    nt = t // DISPATCH_TILE
    zeros = jnp.zeros((n_rows * TOK_SUB, LANE), F32) if xs_buf is None else xs_buf
    return pl.pallas_call(
        _dispatch_body,
        grid=(nt,),
        in_specs=[pl.BlockSpec((1, 1, DISPATCH_TILE * TOP_K), lambda i: (i, 0, 0), memory_space=pltpu.SMEM),
                  pl.BlockSpec((DISPATCH_TILE * TOK_SUB, LANE), lambda i: (i, 0)),
                  pl.BlockSpec(memory_space=pl.ANY)],
        out_specs=pl.BlockSpec(memory_space=pl.ANY),
        out_shape=jax.ShapeDtypeStruct((n_rows * TOK_SUB, LANE), F32),
        scratch_shapes=[pltpu.SemaphoreType.DMA(())],
        input_output_aliases={2: 0},
        compiler_params=_cp("arbitrary"),
        name="moe_dispatch",
    )(dest.reshape(nt, 1, DISPATCH_TILE * TOP_K), xt, zeros)


EXPERT_COLS = 512


def _expert_body(te_ref, tv_ref, xs_ref, wgu_ref, bgu_ref, wd_ref, bd_ref, ys_ref, wgu_s, wd2_s, tmp_s):
    i = pl.program_id(0)
    valid = tv_ref[i] != 0
    changed = (i == 0) | (te_ref[i] != te_ref[jnp.maximum(i - 1, 0)])

    @pl.when(i == 0)
    def _():
        tmp_s[...] = jnp.zeros(tmp_s.shape, F32)

    @pl.when(valid & changed)
    def _():
        for c0 in range(0, 2 * D_FF, 512):
            wgu_s[:, c0:c0 + 512] = _bf(wgu_ref[0, :, c0:c0 + 512])
        for j in range(D_MODEL // LANE):
            tmp_s[pl.ds(0, D_FF, stride=2), :] = wd_ref[0, :, j * LANE:(j + 1) * LANE]
            wd2_s[:, j * LANE:(j + 1) * LANE] = _bf(tmp_s[...])

    @pl.when(valid)
    def _():
        xb = jnp.concatenate([_bf(_from_token_tiles(xs_ref, 0, MOE_TILE, s)) for s in range(TOK_SUB)], axis=1)
        chunks = range(0, 2 * D_FF, EXPERT_COLS)
        gus = [_dot(xb, wgu_s[:, c0:c0 + EXPERT_COLS]) + bgu_ref[0][:, c0:c0 + EXPERT_COLS] for c0 in chunks]
        even = (_iota((MOE_TILE, EXPERT_COLS), 1) & 1) == 0
        y = jnp.zeros((MOE_TILE, D_MODEL), F32)
        for c0, gu in zip(chunks, gus):
            nxt = pltpu.roll(gu, EXPERT_COLS - 1, 1)
            gate = jnp.minimum(gu, SWIGLU_LIMIT)
            up = jnp.clip(nxt, -SWIGLU_LIMIT, SWIGLU_LIMIT)
            act = gate * _sigmoid(gate * SWIGLU_ALPHA) * (up + 1.0)
            act = jnp.where(even, act, 0.0)
            y = y + _dot(_bf(act), wd2_s[c0:c0 + EXPERT_COLS, :])
        _to_token_tiles(ys_ref, 0, y + bd_ref[0])

    @pl.when(jnp.logical_not(valid))
    def _():
        ys_ref[...] = jnp.zeros(ys_ref.shape, F32)


def moe_experts(xs, tile_e, tile_valid, wgu, bgu, wd, bd):
    n_rows = xs.shape[0] // TOK_SUB
    nt = n_rows // MOE_TILE
    grid_spec = pltpu.PrefetchScalarGridSpec(
        num_scalar_prefetch=2,
        grid=(nt,),
        in_specs=[pl.BlockSpec((MOE_TILE * TOK_SUB, LANE), lambda i, te, tv: (i, 0)),
                  pl.BlockSpec((1, D_MODEL, 2 * D_FF), lambda i, te, tv: (te[i], 0, 0)),
                  pl.BlockSpec((1, 1, 2 * D_FF), lambda i, te, tv: (te[i], 0, 0)),
                  pl.BlockSpec((1, D_FF, D_MODEL), lambda i, te, tv: (te[i], 0, 0)),
                  pl.BlockSpec((1, 1, D_MODEL), lambda i, te, tv: (te[i], 0, 0))],
        out_specs=pl.BlockSpec((MOE_TILE * TOK_SUB, LANE), lambda i, te, tv: (i, 0)),
        scratch_shapes=[pltpu.VMEM((D_MODEL, 2 * D_FF), BF16), pltpu.VMEM((2 * D_FF, D_MODEL), BF16),
                        pltpu.VMEM((2 * D_FF, LANE), F32)],
    )
    return pl.pallas_call(
        _expert_body,
        grid_spec=grid_spec,
        out_shape=jax.ShapeDtypeStruct((n_rows * TOK_SUB, LANE), F32),
        compiler_params=_cp("arbitrary"),
        name="moe_experts",
    )(tile_e, tile_valid, xs, wgu, bgu, wd, bd)


def _combine_copy(ys_ref, buf_ref, sems, slot, d, k, r):
    return pltpu.make_async_copy(ys_ref.at[_tok(d), :], buf_ref.at[slot, k, _tok(r), :], sems.at[slot])


def _combine_body(dest_ref, next_dest_ref, gates_ref, x_ref, g_ref, b_ref, ys_ref, o_ref, buf_ref, sems):
    i = pl.program_id(0)
    slot = i % 2

    def gather(idx_ref, into):
        def start(r, carry):
            for k in range(TOP_K):
                _combine_copy(ys_ref, buf_ref, sems, into, idx_ref[0, 0, r * TOP_K + k], k, r).start(priority=k % 2)
            return carry

        lax.fori_loop(0, DISPATCH_TILE, start, 0)

    @pl.when(i == 0)
    def _():
        gather(dest_ref, 0)

    @pl.when(i + 1 < pl.num_programs(0))
    def _():
        gather(next_dest_ref, 1 - slot)

    def wait(r, carry):
        for k in range(TOP_K):
            _combine_copy(ys_ref, buf_ref, sems, slot, 0, k, 0).wait()
        return carry

    lax.fori_loop(0, DISPATCH_TILE, wait, 0)
    gates = gates_ref[...]
    cols = []
    for s in range(TOK_SUB):
        acc = DN_ALPHA * x_ref[:, s * LANE:(s + 1) * LANE]
        for k in range(TOP_K):
            acc = acc + gates[:, k:k + 1] * buf_ref[slot, k, pl.ds(s, DISPATCH_TILE, stride=TOK_SUB), :]
        cols.append(acc)
    o_ref[...] = _layer_norm(jnp.concatenate(cols, axis=1), g_ref[...], b_ref[...])


def moe_combine_ln(x, ys, dest, gates, g, b):
    t = x.shape[0]
    nt = t // DISPATCH_TILE
    dest3 = dest.reshape(nt, 1, DISPATCH_TILE * TOP_K)
    full = lambda shape: pl.BlockSpec(shape, lambda i: (0,) * len(shape))
    return pl.pallas_call(
        _combine_body,
        grid=(nt,),
        in_specs=[pl.BlockSpec((1, 1, DISPATCH_TILE * TOP_K), lambda i: (i, 0, 0), memory_space=pltpu.SMEM),
                  pl.BlockSpec((1, 1, DISPATCH_TILE * TOP_K), lambda i: (jnp.minimum(i + 1, nt - 1), 0, 0),
                               memory_space=pltpu.SMEM),
                  pl.BlockSpec((DISPATCH_TILE, LANE), lambda i: (i, 0)),
                  pl.BlockSpec((DISPATCH_TILE, D_MODEL), lambda i: (i, 0)),
                  full((1, D_MODEL)), full((1, D_MODEL)),
                  pl.BlockSpec(memory_space=pl.ANY)],
        out_specs=pl.BlockSpec((DISPATCH_TILE, D_MODEL), lambda i: (i, 0)),
        out_shape=jax.ShapeDtypeStruct((t, D_MODEL), F32),
        scratch_shapes=[pltpu.VMEM((2, TOP_K, DISPATCH_TILE * TOK_SUB, LANE), F32), pltpu.SemaphoreType.DMA((2,))],
        compiler_params=_cp("arbitrary"),
        name="moe_combine_ln",
    )(dest3, dest3, gates, x, g, b, ys)


def _rank_body(e_ref, rank_ref, cnt_ref, run_ref):
    i = pl.program_id(0)

    @pl.when(i == 0)
    def _():
        run_ref[...] = jnp.zeros(run_ref.shape, F32)

    e = e_ref[...]
    lane = _iota(e.shape, 1)
    hot = [lane == e[:, k:k + 1] for k in range(TOP_K)]
    m = sum(h.astype(F32) for h in hot)
    ri = _iota((ROW_TILE, ROW_TILE), 0)
    ci = _iota((ROW_TILE, ROW_TILE), 1)
    before = _dot(_bf((ci < ri).astype(F32)), _bf(m)) + run_ref[0:1, :]
    rank = jnp.zeros(e.shape, F32)
    for k in range(TOP_K):
        rank = jnp.where(lane == k, jnp.sum(jnp.where(hot[k], before, 0.0), axis=-1, keepdims=True), rank)
    rank_ref[...] = rank.astype(I32)
    run_ref[0:1, :] = run_ref[0:1, :] + jnp.sum(m, axis=0, keepdims=True)
    cnt_ref[...] = run_ref[...].astype(I32)


def moe_rank(e_pad):
    t = e_pad.shape[0]
    return pl.pallas_call(
        _rank_body,
        grid=(t // ROW_TILE,),
        in_specs=[pl.BlockSpec((ROW_TILE, LANE), lambda i: (i, 0))],
        out_specs=[pl.BlockSpec((ROW_TILE, LANE), lambda i: (i, 0)), pl.BlockSpec((8, LANE), lambda i: (0, 0))],
        out_shape=[jax.ShapeDtypeStruct((t, LANE), I32), jax.ShapeDtypeStruct((8, LANE), I32)],
        scratch_shapes=[pltpu.VMEM((8, LANE), F32)],
        compiler_params=_cp("arbitrary"),
        name="moe_rank",
    )(e_pad)


def moe_plan(e_pad, n_tokens):
    tk = n_tokens * TOP_K
    rank_pad, cnt = moe_rank(e_pad)
    top_e, rank = e_pad[:, :TOP_K], rank_pad[:, :TOP_K]
    counts = cnt[0, :N_EXPERTS]
    ntile = (counts + MOE_TILE - 1) // MOE_TILE
    tile_end = jnp.cumsum(ntile)
    tile_start = tile_end - ntile
    start_of = jnp.sum(jnp.where(top_e[..., None] == jnp.arange(N_EXPERTS, dtype=I32), tile_start, 0), axis=-1)
    dest = (start_of * MOE_TILE + rank).reshape(tk)
    n_tiles = -(-tk // MOE_TILE) + N_EXPERTS
    tiles = jnp.arange(n_tiles, dtype=I32)
    tile_e = jnp.minimum(jnp.sum((tile_end[None, :] <= tiles[:, None]).astype(I32), axis=1), N_EXPERTS - 1)
    tile_valid = (tiles < tile_end[-1]).astype(I32)
    return dest.astype(I32), tile_e, tile_valid, n_tiles * MOE_TILE


def moe_ffn_ln(x1, x1t, e_pad, gates, wgu, bgu, wd, bd, g, b, expert0=0, xs_buf=None):
    t = x1.shape[0]
    dest, tile_e, tile_valid, n_rows = moe_plan(e_pad, t)
    xs = moe_dispatch(x1t, dest, n_rows, xs_buf)
    ys = moe_experts(xs, tile_e + expert0, tile_valid, wgu, bgu, wd, bd)
    return moe_combine_ln(x1, ys, dest, gates, g, b), xs


def _rope_tables(pos):
    half = ROT_DIM // 2
    inv_freq = ROPE_THETA ** (-jnp.arange(half, dtype=F32) / half)
    ang = pos.astype(F32)[:, None] * inv_freq[None, :]
    cos, sin = jnp.cos(ang), jnp.sin(ang)
    n = pos.shape[0]
    ones = jnp.ones((n, HEAD_DIM - ROT_DIM), F32)
    cos_h = jnp.concatenate([cos, cos, ones], axis=1)
    sin_h = jnp.concatenate([-sin, sin, 0.0 * ones], axis=1)
    return jnp.concatenate([cos_h, cos_h], axis=1), jnp.concatenate([sin_h, sin_h], axis=1)


def _conv_tail(h, row0, bsz, seq):
    assert seq >= GDN_CONV - 1
    n = GDN_CONV - 1
    if bsz <= 8:
        return jnp.stack([h[row0 + (b + 1) * seq - n:row0 + (b + 1) * seq, C_GQKV:C_GQKV + GDN_CONV_DIM]
                          for b in range(bsz)], axis=0)
    blk = h[row0:row0 + bsz * seq, C_GQKV:C_GQKV + GDN_CONV_DIM].reshape(bsz, seq, GDN_CONV_DIM)
    return blk[:, seq - n:, :]


def _lane_row(v, offset):
    return jnp.zeros((1, LANE), F32).at[0, offset:offset + v.shape[0]].set(v)


def kernel(x_prompt, x_sample, cache_swa_k, cache_swa_v, state_ssm_re, state_ssm_im, state_gdn_conv, state_gdn, state_gla, w_in, w_out, attn_sinks, ssm_a_re, ssm_a_im, ssm_b_re, ssm_b_im, ssm_c_re, ssm_c_im, ssm_d, ssm_log_dt, ssm_glu_w, ssm_glu_b, gdn_conv_w, gdn_a_log, gdn_dt_bias, gdn_norm_w, gla_gate_w, gla_gate_b, gla_norm_w, ln1_g, ln1_b, ln2_g, ln2_b, router_w, router_b, moe_w_gate_up, moe_b_gate_up, moe_w_down, moe_b_down):
    bp, lp, _ = x_prompt.shape
    bs, ls, _ = x_sample.shape
    n_p, n_s = bp * lp, bs * ls
    depth = w_in.shape[0]

    w_in_r = jnp.concatenate([w_in[..., :1792], w_in[..., 1800:2568], w_in[..., 1792:1800], w_in[..., 2568:N_IN],
                              jnp.zeros(w_in.shape[:2] + (NH - N_IN,), w_in.dtype)], axis=-1).astype(BF16)
    w_out_b = w_out.astype(BF16)
    glu_w_b = ssm_glu_w.astype(BF16)
    rw_pad = jnp.pad(router_w, ((0, 0), (0, 0), (0, LANE - N_EXPERTS)))
    rb_pad = jnp.pad(router_b, ((0, 0), (0, LANE - N_EXPERTS)))
    wgu_all = moe_w_gate_up.reshape(depth * N_EXPERTS, D_MODEL, 2 * D_FF)
    bgu_all = moe_b_gate_up.reshape(depth * N_EXPERTS, 1, 2 * D_FF)
    wd_all = moe_w_down.reshape(depth * N_EXPERTS, D_FF, D_MODEL)
    bd_all = moe_b_down.reshape(depth * N_EXPERTS, 1, D_MODEL)

    cos_p, sin_p = _rope_tables(jnp.arange(lp, dtype=I32))
    cos_s, sin_s = _rope_tables(PAST_LEN + jnp.arange(ls, dtype=I32))

    x = jnp.concatenate([x_prompt.reshape(n_p, D_MODEL), x_sample.reshape(n_s, D_MODEL)], axis=0)
    zeros = lambda *s: jnp.zeros(s, F32)
    new_p = [[] for _ in range(7)]
    new_s = [[] for _ in range(7)]
    xs_buf = None
    for l in range(depth):
        h = in_proj(x, w_in_r[l])
        sinks = attn_sinks[l]
        oa_p, pk, pv = swa_prompt(h, sinks, cos_p, sin_p, bp, lp)
        oa_s, sk, sv = swa_sample(h, n_p, sinks, cache_swa_k[l].reshape(bs, WINDOW, 128),
                                  cache_swa_v[l].reshape(bs, WINDOW, 128), cos_s, sin_s, bs, ls)
        bw, a_bar, cw, dsk = s5_params(ssm_a_re[l], ssm_a_im[l], ssm_b_re[l], ssm_b_im[l], ssm_c_re[l], ssm_c_im[l],
                                       ssm_d[l], ssm_log_dt[l])
        glu_b = ssm_glu_b[l].reshape(1, 2 * GROUP_WIDTH)
        ob_p, hl_p = s5_prompt(h, zeros(bp, 2 * SSM_W), bw, a_bar, cw, dsk, glu_w_b[l], glu_b, bp, lp)
        h0_s = jnp.concatenate([state_ssm_re[l].reshape(bs, SSM_W), state_ssm_im[l].reshape(bs, SSM_W)], axis=1)
        ob_s, hl_s = s5_sample(h, n_p, h0_s, bw, a_bar, cw, dsk, glu_w_b[l], glu_b, bs, ls)
        alog_t = _lane_row(gdn_a_log[l], SM_GA)
        dtb_t = _lane_row(gdn_dt_bias[l], SM_GA)
        gnw = gdn_norm_w[l].reshape(1, GDN_DV)
        oc_p, gs_p = gdn_mixer(h, 0, gdn_conv_w[l], alog_t, dtb_t, gnw, zeros(bp, GDN_CONV - 1, GDN_CONV_DIM),
                               zeros(bp, GDN_HEADS, GDN_DK, GDN_DV), bp, lp)
        oc_s, gs_s = gdn_mixer(h, n_p, gdn_conv_w[l], alog_t, dtb_t, gnw, state_gdn_conv[l], state_gdn[l], bs, ls)
        cv_p = _conv_tail(h, 0, bp, lp)
        cv_s = _conv_tail(h, n_p, bs, ls)
        lgb = gla_gate_b[l].reshape(1, 128)
        lnw = jnp.tile(gla_norm_w[l], GLA_HEADS).reshape(1, 256)
        od_p, lt_p = gla_mixer(h, 0, gla_gate_w[l], lgb, lnw, zeros(bp, 256, 128), bp, lp)
        od_s, lt_s = gla_mixer(h, n_p, gla_gate_w[l], lgb, lnw, gla_state_to_t(state_gla[l]), bs, ls)

        x1, x1t, e_pad, gates = out_proj_ln(
            x, (oa_p, ob_p.reshape(n_p, GROUP_WIDTH), oc_p, od_p), (oa_s, ob_s, oc_s, od_s), w_out_b[l],
            ln1_g[l].reshape(1, D_MODEL), ln1_b[l].reshape(1, D_MODEL), rw_pad[l], rb_pad[l].reshape(1, LANE))
        x, xs_buf = moe_ffn_ln(x1, x1t, e_pad, gates, wgu_all, bgu_all, wd_all, bd_all, ln2_g[l].reshape(1, D_MODEL),
                               ln2_b[l].reshape(1, D_MODEL), expert0=l * N_EXPERTS, xs_buf=xs_buf)

        st_p = (pk.reshape(bp, WINDOW, A_KV_HEADS, HEAD_DIM), pv.reshape(bp, WINDOW, A_KV_HEADS, HEAD_DIM),
                hl_p[:, :SSM_W].reshape(bp, SSM_GROUPS, SSM_STATE), hl_p[:, SSM_W:].reshape(bp, SSM_GROUPS, SSM_STATE),
                cv_p, gs_p, gla_state_from_t(lt_p))
        st_s = (sk.reshape(bs, WINDOW, A_KV_HEADS, HEAD_DIM), sv.reshape(bs, WINDOW, A_KV_HEADS, HEAD_DIM),
                hl_s[:, :SSM_W].reshape(bs, SSM_GROUPS, SSM_STATE), hl_s[:, SSM_W:].reshape(bs, SSM_GROUPS, SSM_STATE),
                cv_s, gs_s, gla_state_from_t(lt_s))
        for i in range(7):
            new_p[i].append(st_p[i])
            new_s[i].append(st_s[i])
    y_p = x[:n_p].reshape(bp, lp, D_MODEL)
    y_s = x[n_p:].reshape(bs, ls, D_MODEL)
    return (y_p, y_s) + tuple(jnp.stack(t, axis=0) for t in new_p) + tuple(jnp.stack(t, axis=0) for t in new_s)
```

```python
import functools

import numpy as np
import jax
import jax.numpy as jnp
from jax import lax
from jax.experimental import pallas as pl
from jax.experimental.pallas import tpu as pltpu

F32 = jnp.float32
BF16 = jnp.bfloat16
I32 = jnp.int32
HI = lax.Precision.HIGHEST

D_MODEL = 1024
DEPTH = 4
PAST_LEN = 8192
GROUP_WIDTH = 256
HEAD_DIM = 64
A_HEADS = 4
A_KV_HEADS = 2
WINDOW = 128
ROPE_THETA = 500000.0
ROT_DIM = 16
SSM_GC = 16
SSM_GROUPS = 16
SSM_STATE = 64
SSM_W = SSM_GROUPS * SSM_STATE
GDN_HEADS = 4
GDN_DK = 64
GDN_DV = 64
GDN_CONV = 4
GDN_CONV_DIM = 768
GDN_CHUNK = 64
GLA_HEADS = 4
GLA_DK = 32
GLA_DV = 64
GLA_RANK = 16
GLA_TAU = 16.0
GLA_CHUNK = 16
N_EXPERTS = 32
TOP_K = 4
D_FF = 1024
SWIGLU_LIMIT = 7.0
SWIGLU_ALPHA = 1.702
DN_ALPHA = (2 * DEPTH) ** 0.25
LN_EPS = 1e-5
RMS_EPS = 1e-6
N_IN = 2584

C_AQ, C_AK, C_AV, C_SU, C_GQKV, C_GZ = 0, 256, 384, 512, 768, 1536
C_LQ, C_LK, C_LV, C_LR, C_SM = 1792, 1920, 2048, 2304, 2560
NH = 2688
SM_GB, SM_GA, SM_LG = 0, 4, 8

LANE = 128
ROW_TILE = 512
MOE_TILE = 512
VMEM_LIMIT = 56 * 1024 * 1024


def _cp(*sem):
    return pltpu.CompilerParams(dimension_semantics=sem, vmem_limit_bytes=VMEM_LIMIT)


def _dot(a, b, precision=None):
    return jnp.dot(a, b, preferred_element_type=F32, precision=precision)


def _dot_nt(a, b, precision=None):
    return lax.dot_general(a, b, (((1,), (1,)), ((), ())), preferred_element_type=F32, precision=precision)


def _dot_tn(a, b, precision=None):
    return lax.dot_general(a, b, (((0,), (0,)), ((), ())), preferred_element_type=F32, precision=precision)


def _bf(x):
    return x.astype(BF16)


def _iota(shape, dim):
    return lax.broadcasted_iota(I32, shape, dim)


def _shr(idx, size):
    return lax.shift_right_logical(idx, int(size).bit_length() - 1)


def _sigmoid(x):
    return 1.0 / (1.0 + jnp.exp(-x))


def _silu(x):
    return x * _sigmoid(x)


def _softplus(x):
    return jnp.maximum(x, 0.0) + jnp.log(1.0 + jnp.exp(-jnp.abs(x)))


def _log_sigmoid(x):
    return -_softplus(-x)


def _gelu_tanh(x):
    return 0.5 * x * (1.0 + jnp.tanh(0.7978845608028654 * (x + 0.044715 * x * x * x)))


def _layer_norm(y, g, b):
    mu = jnp.mean(y, axis=-1, keepdims=True)
    yc = y - mu
    var = jnp.mean(yc * yc, axis=-1, keepdims=True)
    return yc * lax.rsqrt(var + LN_EPS) * g + b


def _in_proj_body(x_ref, w_ref, o_ref):
    xb = _bf(x_ref[...])
    for c0 in range(0, NH, 512):
        c1 = min(c0 + 512, NH)
        o_ref[:, c0:c1] = _dot(xb, w_ref[:, c0:c1])


def in_proj(x, w):
    t = x.shape[0]
    return pl.pallas_call(
        _in_proj_body,
        grid=(t // ROW_TILE,),
        in_specs=[pl.BlockSpec((ROW_TILE, D_MODEL), lambda i: (i, 0)),
                  pl.BlockSpec((D_MODEL, NH), lambda i: (0, 0))],
        out_specs=pl.BlockSpec((ROW_TILE, NH), lambda i: (i, 0)),
        out_shape=jax.ShapeDtypeStruct((t, NH), F32),
        compiler_params=_cp("arbitrary"),
        name="in_proj",
    )(x, w)


def _rope(x, cos, sin):
    w = x.shape[1]
    if w > LANE:
        cos = jnp.concatenate([cos] * (w // LANE), axis=1)
        sin = jnp.concatenate([sin] * (w // LANE), axis=1)
    lane = _iota(x.shape, 1) & (HEAD_DIM - 1)
    swapped = jnp.where(lane < ROT_DIM // 2, pltpu.roll(x, w - ROT_DIM // 2, 1), pltpu.roll(x, ROT_DIM // 2, 1))
    return x * cos + swapped * sin


def _sink_attention_multi(qkvs, mask, sinks_ref):
    pairs = [(i, hq) for i in range(len(qkvs)) for hq in range(A_HEADS)]
    hd = lambda x, j: _bf(x[:, j * HEAD_DIM:(j + 1) * HEAD_DIM])
    kv = lambda hq: hq // (A_HEADS // A_KV_HEADS)
    scores = {(i, hq): _dot_nt(hd(qkvs[i][0], hq), hd(qkvs[i][1], kv(hq))) for i, hq in pairs}
    probs = {}
    masks = mask if isinstance(mask, (list, tuple)) else [mask] * len(qkvs)
    for i, hq in pairs:
        s = jnp.where(masks[i], scores[i, hq] * (HEAD_DIM ** -0.5), -jnp.inf)
        sink = sinks_ref[hq]
        m = jnp.maximum(jnp.max(s, axis=-1, keepdims=True), sink)
        p = jnp.exp(s - m)
        den = jnp.sum(p, axis=-1, keepdims=True) + jnp.exp(sink - m)
        probs[i, hq] = _bf(p / den)
    outs = {(i, hq): _dot(probs[i, hq], hd(qkvs[i][2], kv(hq))) for i, hq in pairs}
    return [jnp.concatenate([outs[i, hq] for hq in range(A_HEADS)], axis=1) for i in range(len(qkvs))]


def _sink_attention(q, kk, vv, mask, sinks_ref):
    return _sink_attention_multi([(q, kk, vv)], mask, sinks_ref)[0]


SWA_QB = 4


def _swa_prompt_body(sinks_ref, cur_ref, prev_ref, cos_ref, sin_ref, cosp_ref, sinp_ref, o_ref, ko_ref, vo_ref):
    i = pl.program_id(1)
    cur = cur_ref[...]
    q = _rope(cur[:, C_AQ:C_AQ + 256], cos_ref[...], sin_ref[...])
    k = _rope(cur[:, C_AK:C_AK + 128], cos_ref[...], sin_ref[...])
    v = cur[:, C_AV:C_AV + 128]
    prev = prev_ref[...]
    kp = _rope(prev[:, 0:128], cosp_ref[...], sinp_ref[...])
    vp = prev[:, 128:256]
    kall = jnp.concatenate([kp, k], axis=0)
    vall = jnp.concatenate([vp, v], axis=0)
    r = _iota((WINDOW, 2 * WINDOW), 0)
    j = _iota((WINDOW, 2 * WINDOW), 1)
    d = WINDOW + r - j
    band = (d >= 0) & (d <= WINDOW)
    first = band & ((j >= WINDOW) | (i > 0))
    qkvs = [(q[m * WINDOW:(m + 1) * WINDOW, :], kall[m * WINDOW:(m + 2) * WINDOW, :], vall[m * WINDOW:(m + 2) * WINDOW, :])
            for m in range(SWA_QB)]
    outs = _sink_attention_multi(qkvs, [first] + [band] * (SWA_QB - 1), sinks_ref)
    o_ref[...] = jnp.concatenate(outs, axis=0)
    ko_ref[0] = k[(SWA_QB - 1) * WINDOW:, :]
    vo_ref[0] = v[(SWA_QB - 1) * WINDOW:, :]


def swa_prompt(h, sinks, cos_t, sin_t, bsz, seq):
    nb = seq // WINDOW
    smem = pl.BlockSpec(memory_space=pltpu.SMEM)
    ns = nb // SWA_QB
    rows = SWA_QB * WINDOW
    cur_tab = pl.BlockSpec((rows, LANE), lambda b, i: (i, 0))
    prev_tab = pl.BlockSpec((WINDOW, LANE), lambda b, i: (jnp.maximum(i * SWA_QB - 1, 0), 0))
    return pl.pallas_call(
        _swa_prompt_body,
        grid=(bsz, ns),
        in_specs=[smem,
                  pl.BlockSpec((rows, 512), lambda b, i: (b * ns + i, 0)),
                  pl.BlockSpec((WINDOW, 256), lambda b, i: (b * nb + jnp.maximum(i * SWA_QB - 1, 0), 1)),
                  cur_tab, cur_tab, prev_tab, prev_tab],
        out_specs=[pl.BlockSpec((rows, 256), lambda b, i: (b * ns + i, 0)),
                   pl.BlockSpec((1, WINDOW, 128), lambda b, i: (b, 0, 0)),
                   pl.BlockSpec((1, WINDOW, 128), lambda b, i: (b, 0, 0))],
        out_shape=[jax.ShapeDtypeStruct((bsz * seq, 256), F32),
                   jax.ShapeDtypeStruct((bsz, WINDOW, 128), F32),
                   jax.ShapeDtypeStruct((bsz, WINDOW, 128), F32)],
        compiler_params=_cp("arbitrary", "arbitrary"),
        name="swa_prompt",
    )(sinks, h, h, cos_t, sin_t, cos_t, sin_t)


SWA_SB = 8


def _swa_sample_body(sinks_ref, cur_ref, kc_ref, vc_ref, cos_ref, sin_ref, o_ref, ko_ref, vo_ref, *, ls):
    cw = WINDOW
    r = _iota((ls, cw + ls), 0)
    j = _iota((ls, cw + ls), 1)
    d = cw + r - j
    mask = (d >= 0) & (d <= WINDOW)
    qkvs = []
    for b in range(SWA_SB):
        cur = cur_ref[b * ls:(b + 1) * ls, :]
        q = _rope(cur[:, C_AQ:C_AQ + 256], cos_ref[...], sin_ref[...])
        k = _rope(cur[:, C_AK:C_AK + 128], cos_ref[...], sin_ref[...])
        v = cur[:, C_AV:C_AV + 128]
        kk = jnp.concatenate([kc_ref[b], k], axis=0)
        vv = jnp.concatenate([vc_ref[b], v], axis=0)
        ko_ref[b] = kk[ls:, :]
        vo_ref[b] = vv[ls:, :]
        qkvs.append((q, kk, vv))
    for b, o in enumerate(_sink_attention_multi(qkvs, mask, sinks_ref)):
        o_ref[b * ls:(b + 1) * ls, :] = o


def swa_sample(h, row0, sinks, k_cache, v_cache, cos_t, sin_t, bsz, ls):
    rows = SWA_SB * ls
    blk0 = row0 // rows
    smem = pl.BlockSpec(memory_space=pltpu.SMEM)
    cache = pl.BlockSpec((SWA_SB, WINDOW, 128), lambda i: (i, 0, 0))
    tab = pl.BlockSpec((ls, LANE), lambda i: (0, 0))
    return pl.pallas_call(
        functools.partial(_swa_sample_body, ls=ls),
        grid=(bsz // SWA_SB,),
        in_specs=[smem, pl.BlockSpec((rows, 512), lambda i: (blk0 + i, 0)), cache, cache, tab, tab],
        out_specs=[pl.BlockSpec((rows, 256), lambda i: (i, 0)), cache, cache],
        out_shape=[jax.ShapeDtypeStruct((bsz * ls, 256), F32),
                   jax.ShapeDtypeStruct((bsz, WINDOW, 128), F32),
                   jax.ShapeDtypeStruct((bsz, WINDOW, 128), F32)],
        compiler_params=_cp("arbitrary"),
        name="swa_sample",
    )(sinks, h, k_cache, v_cache, cos_t, sin_t)


S5_NB = 2 * SSM_W // LANE


def _s5_input(u, bw_ref):
    t = _dot(_bf(u), bw_ref[...])
    return jnp.concatenate([t[:, :SSM_W], t[:, :SSM_W] + t[:, SSM_W:]], axis=1)


def _s5_output(hs, u, cw_ref, d_ref, gw_ref, gb_ref):
    hs = jnp.concatenate([hs[:, :SSM_W] + hs[:, SSM_W:], hs[:, SSM_W:]], axis=1)
    y = _dot(_bf(hs), cw_ref[...]) + d_ref[...] * u
    y = _gelu_tanh(y)
    z = _dot(_bf(y), gw_ref[...]) + gb_ref[...]
    return z[:, :GROUP_WIDTH] * _sigmoid(z[:, GROUP_WIDTH:])


def _s5_scan(s_ref, a_ref, h_init, n_steps, rows):
    nre = S5_NB // 2
    a_re = [jnp.broadcast_to(a_ref[0:1, j * LANE:(j + 1) * LANE], (rows, LANE)) for j in range(nre)]
    a_im = [jnp.broadcast_to(a_ref[1:2, j * LANE:(j + 1) * LANE], (rows, LANE)) for j in range(nre)]

    def step(t, hcar):
        out = [None] * S5_NB
        base = pl.multiple_of(t * rows, rows)
        for j in range(nre):
            hr, hi = hcar[j], hcar[j + nre]
            nr = a_re[j] * hr - a_im[j] * hi + s_ref[j, pl.ds(base, rows), :]
            ni = a_re[j] * hi + a_im[j] * hr + s_ref[j + nre, pl.ds(base, rows), :]
            s_ref[j, pl.ds(base, rows), :] = nr
            s_ref[j + nre, pl.ds(base, rows), :] = ni
            out[j], out[j + nre] = nr, ni
        return tuple(out)

    return lax.fori_loop(0, n_steps, step, tuple(h_init))


def _s5_prompt_body(*refs, nb, tl):
    u_refs = refs[:nb]
    bw_ref, a_ref, cw_ref, d_ref, gw_ref, gb_ref, h0_ref, o_ref, hl_ref, s_ref, hst_ref = refs[nb:]
    i = pl.program_id(0)

    @pl.when(i == 0)
    def _():
        hst_ref[...] = h0_ref[...]

    u_all = jnp.concatenate([u_refs[b][...] for b in range(nb)], axis=0)
    bu = _s5_input(u_all, bw_ref)
    for b in range(nb):
        for j in range(S5_NB):
            s_ref[j, pl.ds(b, tl, stride=nb), :] = bu[b * tl:(b + 1) * tl, j * LANE:(j + 1) * LANE]
    h_init = [hst_ref[:, j * LANE:(j + 1) * LANE] for j in range(S5_NB)]
    h_fin = _s5_scan(s_ref, a_ref, h_init, tl, nb)
    for j in range(S5_NB):
        hst_ref[:, j * LANE:(j + 1) * LANE] = h_fin[j]
    hs = jnp.concatenate([jnp.concatenate([s_ref[j, pl.ds(b, tl, stride=nb), :] for j in range(S5_NB)], axis=1)
                          for b in range(nb)], axis=0)
    o_all = _s5_output(hs, u_all, cw_ref, d_ref, gw_ref, gb_ref)
    for b in range(nb):
        o_ref[b] = o_all[b * tl:(b + 1) * tl, :]

    @pl.when(i == pl.num_programs(0) - 1)
    def _():
        hl_ref[...] = hst_ref[...]


def s5_prompt(h, h0, bw, a, cw, dsk, gw, gb, bsz, seq, tl=128):
    assert bsz == 8
    nt = seq // tl
    full = lambda shape: pl.BlockSpec(shape, lambda i: (0,) * len(shape))
    u_specs = [pl.BlockSpec((tl, GROUP_WIDTH), functools.partial(lambda i, b: (b * nt + i, C_SU // GROUP_WIDTH), b=b))
               for b in range(bsz)]
    return pl.pallas_call(
        functools.partial(_s5_prompt_body, nb=bsz, tl=tl),
        grid=(nt,),
        in_specs=u_specs + [full((GROUP_WIDTH, 2 * SSM_W)), full((2, SSM_W)), full((2 * SSM_W, GROUP_WIDTH)),
                            full((1, GROUP_WIDTH)), full((GROUP_WIDTH, 2 * GROUP_WIDTH)), full((1, 2 * GROUP_WIDTH)),
                            full((bsz, 2 * SSM_W))],
        out_specs=[pl.BlockSpec((bsz, tl, GROUP_WIDTH), lambda i: (0, i, 0)), full((bsz, 2 * SSM_W))],
        out_shape=[jax.ShapeDtypeStruct((bsz, seq, GROUP_WIDTH), F32), jax.ShapeDtypeStruct((bsz, 2 * SSM_W), F32)],
        scratch_shapes=[pltpu.VMEM((S5_NB, tl * bsz, LANE), F32), pltpu.VMEM((bsz, 2 * SSM_W), F32)],
        compiler_params=_cp("arbitrary"),
        name="s5_prompt",
    )(*([h] * bsz), bw, a, cw, dsk, gw, gb, h0)


def _s5_sample_body(u_ref, bw_ref, a_ref, cw_ref, d_ref, gw_ref, gb_ref, h0_ref, o_ref, hl_ref, s_ref, t_ref, *, bsz, ls):
    bu = _s5_input(u_ref[...], bw_ref)
    for j in range(S5_NB):
        s_ref[j] = bu[:, j * LANE:(j + 1) * LANE]
    for t in range(ls):
        for j in range(S5_NB):
            t_ref[j, pl.ds(t * bsz, bsz), :] = s_ref[j, pl.ds(t, bsz, stride=ls), :]
    h_init = [h0_ref[:, j * LANE:(j + 1) * LANE] for j in range(S5_NB)]
    h_fin = _s5_scan(t_ref, a_ref, h_init, ls, bsz)
    for j in range(S5_NB):
        hl_ref[:, j * LANE:(j + 1) * LANE] = h_fin[j]
    for t in range(ls):
        for j in range(S5_NB):
            s_ref[j, pl.ds(t, bsz, stride=ls), :] = t_ref[j, pl.ds(t * bsz, bsz), :]
    hs = jnp.concatenate([s_ref[j] for j in range(S5_NB)], axis=1)
    o_ref[...] = _s5_output(hs, u_ref[...], cw_ref, d_ref, gw_ref, gb_ref)


def s5_sample(h, row0, h0, bw, a, cw, dsk, gw, gb, bsz, ls):
    rows = bsz * ls
    full = lambda shape: pl.BlockSpec(shape, lambda i: (0,) * len(shape))
    return pl.pallas_call(
        functools.partial(_s5_sample_body, bsz=bsz, ls=ls),
        grid=(1,),
        in_specs=[pl.BlockSpec((rows, GROUP_WIDTH), lambda i: (row0 // rows, C_SU // GROUP_WIDTH)),
                  full((GROUP_WIDTH, 2 * SSM_W)), full((2, SSM_W)), full((2 * SSM_W, GROUP_WIDTH)),
                  full((1, GROUP_WIDTH)), full((GROUP_WIDTH, 2 * GROUP_WIDTH)), full((1, 2 * GROUP_WIDTH)),
                  full((bsz, 2 * SSM_W))],
        out_specs=[full((rows, GROUP_WIDTH)), full((bsz, 2 * SSM_W))],
        out_shape=[jax.ShapeDtypeStruct((rows, GROUP_WIDTH), F32), jax.ShapeDtypeStruct((bsz, 2 * SSM_W), F32)],
        scratch_shapes=[pltpu.VMEM((S5_NB, rows, LANE), F32), pltpu.VMEM((S5_NB, rows, LANE), F32)],
        compiler_params=_cp("arbitrary"),
        name="s5_sample",
    )(h, bw, a, cw, dsk, gw, gb, h0)


def s5_params(a_re, a_im, b_re, b_im, c_re, c_im, d_skip, log_dt):
    lam = lax.complex(a_re, a_im)
    delta = jnp.exp(log_dt)[:, None]
    a_bar = jnp.exp(lam * delta)
    b_bar = ((a_bar - 1.0) / lam)[..., None] * lax.complex(b_re, b_im)
    eye = jnp.eye(SSM_GROUPS, dtype=F32)
    bw_re = jnp.einsum('gpc,gh->gchp', b_bar.real, eye).reshape(GROUP_WIDTH, SSM_W)
    bw_im = jnp.einsum('gpc,gh->gchp', b_bar.imag, eye).reshape(GROUP_WIDTH, SSM_W)
    bw = jnp.concatenate([bw_re, bw_im - bw_re], axis=1).astype(BF16)
    cw_re = jnp.einsum('gcp,gh->gphc', c_re, eye).reshape(SSM_W, GROUP_WIDTH)
    cw_im = jnp.einsum('gcp,gh->gphc', c_im, eye).reshape(SSM_W, GROUP_WIDTH)
    cw = jnp.concatenate([cw_re, -(cw_re + cw_im)], axis=0).astype(BF16)
    a = jnp.stack([a_bar.real.reshape(SSM_W), a_bar.imag.reshape(SSM_W)], axis=0)
    return bw, a, cw, d_skip.reshape(1, GROUP_WIDTH)


GDN_RT = 256
GDN_ST = 64
GDN_SG = 8
GDN_SG_SHORT = 32


def _split(x):
    hi = _bf(x)
    return hi, _bf(x - hi.astype(F32))


def _dot3(a, b):
    return _dot(a[0], b[0]) + (_dot(a[0], b[1]) + _dot(a[1], b[0]))


def _gdn_prep_body(*refs, c, sample, tiles_per_seq):
    if sample:
        x_ref, cs_ref, z_ref, sm_ref, cw_ref, alog_ref, dtb_ref = refs[:7]
        outs = refs[7:14]
        xs_ref, cb_ref = refs[14:]
    else:
        x_ref, prev_ref, cs_ref, z_ref, sm_ref, cw_ref, alog_ref, dtb_ref = refs[:8]
        outs = refs[8:15]
        (xs_ref,) = refs[15:]
    u_ref, w_ref, qk_ref, qg_ref, kd_ref, eg_ref, zs_ref = outs
    rt = GDN_RT
    x = x_ref[...]
    xs_ref[8:8 + rt, :] = x
    if sample:
        xs_ref[0:8, :] = jnp.zeros((8, GDN_CONV_DIM), F32)
        cb_ref[0:rt, :] = cs_ref[...]
        cb_ref[rt:rt + 8, :] = jnp.zeros((8, GDN_CONV_DIM), F32)
        pos = _iota((rt, GDN_CONV_DIM), 0) & (c - 1)
        shifted = lambda i: jnp.where(pos >= i, xs_ref[8 - i:8 - i + rt, :], cb_ref[8 - i:8 - i + rt, :])
    else:
        first = (pl.program_id(0) % tiles_per_seq) == 0
        xs_ref[0:8, :] = jnp.where(first, cs_ref[...], prev_ref[...])
        shifted = lambda i: xs_ref[8 - i:8 - i + rt, :]
    conv = shifted(3) * cw_ref[0:1, :]
    conv = conv + shifted(2) * cw_ref[1:2, :]
    conv = conv + shifted(1) * cw_ref[2:3, :]
    conv = conv + x * cw_ref[3:4, :]
    qkv = _silu(conv)
    zs_ref[...] = _silu(z_ref[...])

    sm = sm_ref[...]
    beta_all = _sigmoid(sm)
    g_all = -jnp.exp(alog_ref[...]) * _softplus(sm + dtb_ref[...])
    st = GDN_ST
    ri = _iota((st, st), 0)
    ci = _iota((st, st), 1)
    same = _shr(ri, c) == _shr(ci, c)
    low = (ci <= ri) & same
    tri = low.astype(F32)
    last = (ci == (ri | (c - 1))).astype(F32)
    eye = (ri == ci).astype(F32)
    lane = _iota((st, LANE), 1)
    nsub = rt // st
    probs = [(t, hh) for t in range(nsub) for hh in range(GDN_HEADS)]
    gc_alls, gl_alls = [], []
    for t in range(nsub):
        rows = slice(t * st, (t + 1) * st)
        gc_alls.append(_dot_nt_exact_lhs(_bf(tri), g_all[rows, :]))
    for t in range(nsub):
        gl_alls.append(_dot_nt_exact_lhs(_bf(last), gc_alls[t]))
        eg_ref[t * st:(t + 1) * st, :] = jnp.exp(gl_alls[t])
    qs, ks, rhss, decays, a_stricts, invs = {}, {}, {}, {}, {}, {}
    for t, hh in probs:
        rows = slice(t * st, (t + 1) * st)
        q = qkv[rows, hh * GDN_DK:(hh + 1) * GDN_DK]
        k = qkv[rows, 256 + hh * GDN_DK:256 + (hh + 1) * GDN_DK]
        v = qkv[rows, 512 + hh * GDN_DV:512 + (hh + 1) * GDN_DV]
        q = q * lax.rsqrt(jnp.sum(q * q, axis=-1, keepdims=True) + 1e-6) * (GDN_DK ** -0.5)
        k = k * lax.rsqrt(jnp.sum(k * k, axis=-1, keepdims=True) + 1e-6)
        beta = beta_all[rows, SM_GB + hh:SM_GB + hh + 1]
        gc = gc_alls[t][:, SM_GA + hh:SM_GA + hh + 1]
        gl = gl_alls[t][:, SM_GA + hh:SM_GA + hh + 1]
        sel = (lane == SM_GA + hh).astype(F32)
        gc_row = _dot_nt(sel, gc_alls[t], HI)
        decays[t, hh] = jnp.exp(jnp.where(low, gc - gc_row, -jnp.inf))
        kb = k * beta
        eg = jnp.exp(gc)
        qs[t, hh], ks[t, hh] = q, k
        rhss[t, hh] = _split(jnp.concatenate([v * beta, kb * eg], axis=1))
        a_stricts[t, hh] = jnp.where(ci < ri, _dot_nt(_bf(kb), _bf(k)) * decays[t, hh], 0.0)
        qg_ref[rows, hh * GDN_DK:(hh + 1) * GDN_DK] = q * eg
        kd_ref[rows, hh * GDN_DK:(hh + 1) * GDN_DK] = k * jnp.exp(gl - gc)
    for t, hh in probs:
        qk_ref[t * st:(t + 1) * st, hh * st:(hh + 1) * st] = _dot_nt(_bf(qs[t, hh]), _bf(ks[t, hh])) * decays[t, hh]
    for p in probs:
        invs[p] = eye - jnp.where(_shr(ri, 2) == _shr(ci, 2), a_stricts[p], 0.0)
    s = 2
    while s < c:
        pair = (_shr(ri, 2 * s) == _shr(ci, 2 * s)) & (_shr(ri, s) != _shr(ci, s))
        inv_s = {p: _split(invs[p]) for p in probs}
        mid = {p: _dot3(_split(jnp.where(pair, a_stricts[p], 0.0)), inv_s[p]) for p in probs}
        for p in probs:
            invs[p] = invs[p] - _dot3(inv_s[p], _split(mid[p]))
        s *= 2
    for t, hh in probs:
        sol = _dot3(_split(invs[t, hh]), rhss[t, hh])
        u_ref[t * st:(t + 1) * st, hh * GDN_DV:(hh + 1) * GDN_DV] = sol[:, :GDN_DV]
        w_ref[t * st:(t + 1) * st, hh * GDN_DV:(hh + 1) * GDN_DV] = sol[:, GDN_DV:]


def _gdn_scan_body(u_ref, w_ref, qk_ref, qg_ref, kd_ref, eg_ref, zs_ref, nw_ref, s0_ref, o_ref, so_ref, st_ref, *, c, sg):
    n = pl.program_id(1)

    @pl.when(n == 0)
    def _():
        st_ref[...] = s0_ref[...]

    probs = [(s, hh) for s in range(sg) for hh in range(GDN_HEADS)]
    hsl = lambda hh: slice(hh * GDN_DV, (hh + 1) * GDN_DV)
    state_b = {(s, hh): _bf(st_ref[s, hh]) for s, hh in probs}
    v_b = {}
    for s, hh in probs:
        v_b[s, hh] = _bf(u_ref[s, :, hsl(hh)] - _dot(_bf(w_ref[s, :, hsl(hh)]), state_b[s, hh]))
    o_inter = {(s, hh): _dot(_bf(qg_ref[s, :, hsl(hh)]), state_b[s, hh]) for s, hh in probs}
    for s, hh in probs:
        col0 = (s * c) % GDN_ST
        qk = qk_ref[s, :, hh * GDN_ST + col0:hh * GDN_ST + col0 + c]
        o = o_inter[s, hh] + _dot(_bf(qk), v_b[s, hh])
        o = o * lax.rsqrt(jnp.mean(o * o, axis=-1, keepdims=True) + RMS_EPS) * nw_ref[...]
        o_ref[s, :, hsl(hh)] = o * zs_ref[s, :, hsl(hh)]
    for s, hh in probs:
        eg_last = eg_ref[s, c - 1:c, SM_GA + hh:SM_GA + hh + 1]
        st_ref[s, hh] = st_ref[s, hh] * eg_last + _dot_tn(_bf(kd_ref[s, :, hsl(hh)]), v_b[s, hh])

    @pl.when(n == pl.num_programs(1) - 1)
    def _():
        so_ref[...] = st_ref[...]


def gdn_mixer(h, row0, conv_w, alog_t, dtb_t, norm_w, conv_state, s0, bsz, seq):
    c = min(GDN_CHUNK, seq)
    sample = seq == c
    rows = bsz * seq
    rt = GDN_RT
    nt = rows // rt
    blk0 = row0 // rt
    tps = max(seq // rt, 1)
    cs_rows = jnp.pad(conv_state, ((0, 0), (8 - (GDN_CONV - 1), 0), (0, 0))).reshape(bsz * 8, GDN_CONV_DIM)
    row = lambda w, col: pl.BlockSpec((rt, w), lambda i: (blk0 + i, col))
    full = lambda shape: pl.BlockSpec(shape, lambda i: (0,) * len(shape))
    if sample:
        aux_specs = [pl.BlockSpec((rt, GDN_CONV_DIM), lambda i: (i, 0))]
        aux = [cs_rows]
        scratch = [pltpu.VMEM((8 + rt, GDN_CONV_DIM), F32), pltpu.VMEM((8 + rt, GDN_CONV_DIM), F32)]
    else:
        aux_specs = [pl.BlockSpec((8, GDN_CONV_DIM), lambda i: (jnp.maximum((row0 + i * rt) // 8 - 1, 0), C_GQKV // GDN_CONV_DIM)),
                     pl.BlockSpec((8, GDN_CONV_DIM), lambda i: (i // tps, 0))]
        aux = [h, cs_rows]
        scratch = [pltpu.VMEM((8 + rt, GDN_CONV_DIM), F32)]
    out_w = [GROUP_WIDTH] * 5 + [LANE, GROUP_WIDTH]
    prep = pl.pallas_call(
        functools.partial(_gdn_prep_body, c=c, sample=sample, tiles_per_seq=tps),
        grid=(nt,),
        in_specs=[row(GDN_CONV_DIM, C_GQKV // GDN_CONV_DIM)] + aux_specs
                 + [row(GROUP_WIDTH, C_GZ // GROUP_WIDTH), row(LANE, C_SM // LANE),
                    full((GDN_CONV, GDN_CONV_DIM)), full((1, LANE)), full((1, LANE))],
        out_specs=[pl.BlockSpec((rt, w), lambda i: (i, 0)) for w in out_w],
        out_shape=[jax.ShapeDtypeStruct((rows, w), F32) for w in out_w],
        scratch_shapes=scratch,
        compiler_params=_cp("arbitrary"),
        name="gdn_prep",
    )(h, *aux, h, h, conv_w, alog_t, dtb_t)
    prep = [a.reshape(bsz, seq, a.shape[-1]) for a in prep]
    sg = min(bsz, GDN_SG_SHORT) if sample else GDN_SG
    ng, nc = bsz // sg, seq // c
    seq_blk = lambda w: pl.BlockSpec((sg, c, w), lambda g, n: (g, n, 0))
    st_spec = pl.BlockSpec((sg, GDN_HEADS, GDN_DK, GDN_DV), lambda g, n: (g, 0, 0, 0))
    o, s_fin = pl.pallas_call(
        functools.partial(_gdn_scan_body, c=c, sg=sg),
        grid=(ng, nc),
        in_specs=[seq_blk(w) for w in out_w] + [pl.BlockSpec((1, GDN_DV), lambda g, n: (0, 0)), st_spec],
        out_specs=[seq_blk(GROUP_WIDTH), st_spec],
        out_shape=[jax.ShapeDtypeStruct((bsz, seq, GROUP_WIDTH), F32),
                   jax.ShapeDtypeStruct((bsz, GDN_HEADS, GDN_DK, GDN_DV), F32)],
        scratch_shapes=[pltpu.VMEM((sg, GDN_HEADS, GDN_DK, GDN_DV), F32)],
        compiler_params=_cp("arbitrary", "arbitrary"),
        name="gdn_scan",
    )(*prep, norm_w, s0)
    return o.reshape(rows, GROUP_WIDTH), s_fin


GLA_RT = 256


def _dot_exact(x, w):
    hi = _bf(x)
    r1 = x - hi.astype(F32)
    mid = _bf(r1)
    lo = _bf(r1 - mid.astype(F32))
    return _dot(hi, w) + (_dot(mid, w) + _dot(lo, w))


def _gla_body(qk_ref, vr_ref, sm_ref, gw_ref, gb_ref, nw_ref, s0_ref, o_ref, so_ref,
              kb_ref, bb_ref, vb_ref, st_ref, *, c, seq):
    tr = GLA_RT
    n = pl.program_id(1)
    pad = GLA_CHUNK
    carried = seq >= tr

    @pl.when(n == 0)
    def _():
        kb_ref[0:pad, :] = jnp.zeros((pad, 128), F32)
        bb_ref[0:pad, :] = jnp.zeros((pad, 128), F32)
        vb_ref[0:pad, :] = jnp.zeros((pad, 256), F32)
        if carried:
            st_ref[...] = s0_ref[0]

    q = qk_ref[:, 0:128] * (GLA_DK ** -0.5)
    k = qk_ref[:, 128:256]
    v = vr_ref[:, 0:256]
    lg = sm_ref[:, SM_LG:SM_LG + GLA_RANK]
    log_a = _log_sigmoid(_dot(_bf(lg), _bf(gw_ref[...])) + gb_ref[...]) / GLA_TAU
    ri = _iota((tr, tr), 0)
    ci = _iota((tr, tr), 1)
    tri = _bf(((ci <= ri) & (_shr(ri, c) == _shr(ci, c))).astype(F32))
    bc = _dot_nt_exact_lhs(tri, log_a)
    kb_ref[pad:pad + tr, :] = k
    bb_ref[pad:pad + tr, :] = bc
    vb_ref[pad:pad + tr, :] = v
    ones_kv = _bf((_shr(_iota((128, 256), 0), GLA_DK) == _shr(_iota((128, 256), 1), GLA_DV)).astype(F32))
    pos = _iota((tr, 128), 0) & (c - 1)
    ps = []
    for dl in range(c):
        ks = kb_ref[pad - dl:pad - dl + tr, :]
        bs = bb_ref[pad - dl:pad - dl + tr, :]
        ps.append(jnp.where(pos >= dl, q * ks * jnp.exp(bc - bs), 0.0))
    att = _dot_exact(jnp.concatenate(ps, axis=0), ones_kv)
    o_intra = jnp.zeros((tr, 256), F32)
    for dl in range(c):
        vs = vb_ref[pad - dl:pad - dl + tr, :]
        o_intra = o_intra + _bf(att[dl * tr:(dl + 1) * tr, :]).astype(F32) * _bf(vs).astype(F32)
    bd = (_shr(_iota((256, 128), 0), GLA_DV) == _shr(_iota((256, 128), 1), GLA_DK)).astype(F32)
    nch = tr // c
    sls = [slice(ch * c, (ch + 1) * c) for ch in range(nch)]
    b_last = [bc[sl, :][c - 1:c, :] for sl in sls]
    incs = [bd * _dot_tn(_bf(v[sl, :]), _bf(k[sl, :] * jnp.exp(bl - bc[sl, :]))) for sl, bl in zip(sls, b_last)]
    outs = []
    st = st_ref[...] if carried else None
    for ch, sl in enumerate(sls):
        if not carried:
            st = s0_ref[(ch * c) // seq]
        outs.append(_dot_nt(_bf(q[sl, :] * jnp.exp(bc[sl, :])), _bf(st)) + o_intra[sl, :])
        st = st * jnp.exp(b_last[ch]) + incs[ch]
        if not carried:
            so_ref[(ch * c) // seq] = st
    if carried:
        st_ref[...] = st
        so_ref[0] = st
    o = jnp.concatenate(outs, axis=0)
    ones_vv = _bf((_shr(_iota((256, 256), 0), GLA_DV) == _shr(_iota((256, 256), 1), GLA_DV)).astype(F32))
    ms = _dot_exact(o * o, ones_vv) * (1.0 / GLA_DV)
    o_ref[...] = o * lax.rsqrt(ms + RMS_EPS) * nw_ref[...] * _silu(vr_ref[:, 256:512])


def _dot_nt_exact_lhs(w, x):
    hi = _bf(x)
    r1 = x - hi.astype(F32)
    mid = _bf(r1)
    lo = _bf(r1 - mid.astype(F32))
    return _dot(w, hi) + (_dot(w, mid) + _dot(w, lo))


def gla_mixer(h, row0, gate_w, gate_b, norm_w4, s0t, bsz, seq):
    c = min(GLA_CHUNK, seq)
    tr = GLA_RT
    spt = max(tr // seq, 1)
    nt = max(seq // tr, 1)
    ng = bsz // spt
    blk0 = row0 // tr
    row = lambda w, col: pl.BlockSpec((tr, w), lambda g, n: (blk0 + g * nt + n, col))
    full = lambda shape: pl.BlockSpec(shape, lambda g, n: (0,) * len(shape))
    st_spec = pl.BlockSpec((spt, 256, 128), lambda g, n: (g, 0, 0))
    return pl.pallas_call(
        functools.partial(_gla_body, c=c, seq=seq),
        grid=(ng, nt),
        in_specs=[row(256, C_LQ // 256), row(512, C_LV // 512), row(LANE, C_SM // LANE),
                  full((GLA_RANK, 128)), full((1, 128)), full((1, 256)), st_spec],
        out_specs=[pl.BlockSpec((tr, 256), lambda g, n: (g * nt + n, 0)), st_spec],
        out_shape=[jax.ShapeDtypeStruct((bsz * seq, 256), F32), jax.ShapeDtypeStruct((bsz, 256, 128), F32)],
        scratch_shapes=[pltpu.VMEM((GLA_CHUNK + tr, 128), F32), pltpu.VMEM((GLA_CHUNK + tr, 128), F32),
                        pltpu.VMEM((GLA_CHUNK + tr, 256), F32), pltpu.VMEM((256, 128), F32)],
        compiler_params=_cp("arbitrary", "arbitrary"),
        name="gla",
    )(h, h, h, gate_w, gate_b, norm_w4, s0t)


def gla_state_to_t(s):
    eye = jnp.eye(GLA_HEADS, dtype=s.dtype)
    return jnp.einsum('bhde,hg->bhegd', s, eye).reshape(s.shape[0], 256, 128)


def gla_state_from_t(st):
    b = st.shape[0]
    t5 = st.reshape(b, GLA_HEADS, GLA_DV, GLA_HEADS, GLA_DK)
    diag = jnp.stack([t5[:, hh, :, hh, :] for hh in range(GLA_HEADS)], axis=1)
    return jnp.swapaxes(diag, 2, 3)


TOK_SUB = D_MODEL // LANE


def _to_token_tiles(ref, row0, y):
    n = y.shape[0]
    for s in range(TOK_SUB):
        ref[pl.ds(row0 * TOK_SUB + s, n, stride=TOK_SUB), :] = y[:, s * LANE:(s + 1) * LANE]


def _from_token_tiles(ref, row0, n, s):
    return ref[pl.ds(row0 * TOK_SUB + s, n, stride=TOK_SUB), :]


def _out_proj_body(x_ref, ap, bp, cp, dp, a_s, b_s, c_s, d_s, w_ref, g_ref, b_ref, rw_ref, rb_ref,
                   o_ref, t_ref, e_ref, gt_ref, *, n_prompt_tiles):
    i = pl.program_id(0)

    def run(refs):
        acc = DN_ALPHA * x_ref[...]
        for m, r in enumerate(refs):
            acc = acc + _dot(_bf(r[...]), w_ref[m * GROUP_WIDTH:(m + 1) * GROUP_WIDTH, :])
        y = _layer_norm(acc, g_ref[...], b_ref[...])
        o_ref[...] = y
        _to_token_tiles(t_ref, 0, y)
        e_ref[...], gt_ref[...] = _route(y, rw_ref, rb_ref)

    @pl.when(i < n_prompt_tiles)
    def _():
        run((ap, bp, cp, dp))

    @pl.when(i >= n_prompt_tiles)
    def _():
        run((a_s, b_s, c_s, d_s))


def out_proj_ln(x, mix_p, mix_s, w, g, b, rw, rb):
    t = x.shape[0]
    npt = mix_p[0].shape[0] // ROW_TILE
    nst = mix_s[0].shape[0] // ROW_TILE
    p_spec = pl.BlockSpec((ROW_TILE, GROUP_WIDTH), lambda i: (jnp.minimum(i, npt - 1), 0))
    s_spec = pl.BlockSpec((ROW_TILE, GROUP_WIDTH), lambda i: (jnp.clip(i - npt, 0, nst - 1), 0))
    full = lambda shape: pl.BlockSpec(shape, lambda i: (0,) * len(shape))
    return pl.pallas_call(
        functools.partial(_out_proj_body, n_prompt_tiles=npt),
        grid=(t // ROW_TILE,),
        in_specs=[pl.BlockSpec((ROW_TILE, D_MODEL), lambda i: (i, 0))] + [p_spec] * 4 + [s_spec] * 4
                 + [full((D_MODEL, D_MODEL)), full((1, D_MODEL)), full((1, D_MODEL)), full((D_MODEL, LANE)), full((1, LANE))],
        out_specs=[pl.BlockSpec((ROW_TILE, D_MODEL), lambda i: (i, 0)),
                   pl.BlockSpec((ROW_TILE * TOK_SUB, LANE), lambda i: (i, 0)),
                   pl.BlockSpec((ROW_TILE, LANE), lambda i: (i, 0)), pl.BlockSpec((ROW_TILE, LANE), lambda i: (i, 0))],
        out_shape=[jax.ShapeDtypeStruct((t, D_MODEL), F32), jax.ShapeDtypeStruct((t * TOK_SUB, LANE), F32),
                   jax.ShapeDtypeStruct((t, LANE), I32), jax.ShapeDtypeStruct((t, LANE), F32)],
        compiler_params=_cp("arbitrary"),
        name="out_proj_ln",
    )(x, *mix_p, *mix_s, w, g, b, rw, rb)


def _route(x, w_ref, b_ref):
    logits = _dot(_bf(x), _bf(w_ref[...])) + b_ref[...]
    lane = _iota(logits.shape, 1)
    vals = jnp.where(lane < N_EXPERTS, logits, -jnp.inf)
    eo = jnp.zeros(logits.shape, I32)
    top = []
    for k in range(TOP_K):
        m = jnp.max(vals, axis=-1, keepdims=True)
        idx = jnp.min(jnp.where(vals == m, lane.astype(F32), float(LANE)), axis=-1, keepdims=True).astype(I32)
        eo = jnp.where(lane == k, idx, eo)
        top.append(m)
        vals = jnp.where(lane == idx, -jnp.inf, vals)
    ex = [jnp.exp(v - top[0]) for v in top]
    den = ex[0] + ex[1] + ex[2] + ex[3]
    go = jnp.zeros(logits.shape, F32)
    for k in range(TOP_K):
        go = jnp.where(lane == k, ex[k] / den, go)
    return eo, go


DISPATCH_TILE = 256


def _tok(i):
    return pl.ds(pl.multiple_of(i * TOK_SUB, TOK_SUB), TOK_SUB)


def _dispatch_copy(x_ref, xs_ref, sem, r, d):
    return pltpu.make_async_copy(x_ref.at[_tok(r), :], xs_ref.at[_tok(d), :], sem)


def _dispatch_body(dest_ref, x_ref, xs_in_ref, xs_ref, sem):
    del xs_in_ref

    def start(r, carry):
        for k in range(TOP_K):
            _dispatch_copy(x_ref, xs_ref, sem, r, dest_ref[0, 0, r * TOP_K + k]).start(priority=k % 2)
        return carry

    lax.fori_loop(0, DISPATCH_TILE, start, 0)

    def wait(r, carry):
        for k in range(TOP_K):
            _dispatch_copy(x_ref, xs_ref, sem, 0, 0).wait()
        return carry

    lax.fori_loop(0, DISPATCH_TILE, wait, 0)


def moe_dispatch(xt, dest, n_rows, xs_buf=None):
    t = xt.shape[0] // TOK_SUB
    nt = t // DISPATCH_TILE
    zeros = jnp.zeros((n_rows * TOK_SUB, LANE), F32) if xs_buf is None else xs_buf
    return pl.pallas_call(
        _dispatch_body,
        grid=(nt,),
        in_specs=[pl.BlockSpec((1, 1, DISPATCH_TILE * TOP_K), lambda i: (i, 0, 0), memory_space=pltpu.SMEM),
                  pl.BlockSpec((DISPATCH_TILE * TOK_SUB, LANE), lambda i: (i, 0)),
                  pl.BlockSpec(memory_space=pl.ANY)],
        out_specs=pl.BlockSpec(memory_space=pl.ANY),
        out_shape=jax.ShapeDtypeStruct((n_rows * TOK_SUB, LANE), F32),
        scratch_shapes=[pltpu.SemaphoreType.DMA(())],
        input_output_aliases={2: 0},
        compiler_params=_cp("arbitrary"),
        name="moe_dispatch",
    )(dest.reshape(nt, 1, DISPATCH_TILE * TOP_K), xt, zeros)


EXPERT_COLS = 512


def _expert_body(te_ref, tv_ref, xs_ref, wgu_ref, bgu_ref, wd_ref, bd_ref, ys_ref, wgu_s, wd2_s, tmp_s):
    i = pl.program_id(0)
    valid = tv_ref[i] != 0
    changed = (i == 0) | (te_ref[i] != te_ref[jnp.maximum(i - 1, 0)])

    @pl.when(i == 0)
    def _():
        tmp_s[...] = jnp.zeros(tmp_s.shape, F32)

    @pl.when(valid & changed)
    def _():
        for c0 in range(0, 2 * D_FF, 512):
            wgu_s[:, c0:c0 + 512] = _bf(wgu_ref[0, :, c0:c0 + 512])
        for j in range(D_MODEL // LANE):
            tmp_s[pl.ds(0, D_FF, stride=2), :] = wd_ref[0, :, j * LANE:(j + 1) * LANE]
            wd2_s[:, j * LANE:(j + 1) * LANE] = _bf(tmp_s[...])

    @pl.when(valid)
    def _():
        xb = jnp.concatenate([_bf(_from_token_tiles(xs_ref, 0, MOE_TILE, s)) for s in range(TOK_SUB)], axis=1)
        chunks = range(0, 2 * D_FF, EXPERT_COLS)
        gus = [_dot(xb, wgu_s[:, c0:c0 + EXPERT_COLS]) + bgu_ref[0][:, c0:c0 + EXPERT_COLS] for c0 in chunks]
        even = (_iota((MOE_TILE, EXPERT_COLS), 1) & 1) == 0
        y = jnp.zeros((MOE_TILE, D_MODEL), F32)
        for c0, gu in zip(chunks, gus):
            nxt = pltpu.roll(gu, EXPERT_COLS - 1, 1)
            gate = jnp.minimum(gu, SWIGLU_LIMIT)
            up = jnp.clip(nxt, -SWIGLU_LIMIT, SWIGLU_LIMIT)
            act = gate * _sigmoid(gate * SWIGLU_ALPHA) * (up + 1.0)
            act = jnp.where(even, act, 0.0)
            y = y + _dot(_bf(act), wd2_s[c0:c0 + EXPERT_COLS, :])
        _to_token_tiles(ys_ref, 0, y + bd_ref[0])

    @pl.when(jnp.logical_not(valid))
    def _():
        ys_ref[...] = jnp.zeros(ys_ref.shape, F32)


def moe_experts(xs, tile_e, tile_valid, wgu, bgu, wd, bd):
    n_rows = xs.shape[0] // TOK_SUB
    nt = n_rows // MOE_TILE
    grid_spec = pltpu.PrefetchScalarGridSpec(
        num_scalar_prefetch=2,
        grid=(nt,),
        in_specs=[pl.BlockSpec((MOE_TILE * TOK_SUB, LANE), lambda i, te, tv: (i, 0)),
                  pl.BlockSpec((1, D_MODEL, 2 * D_FF), lambda i, te, tv: (te[i], 0, 0)),
                  pl.BlockSpec((1, 1, 2 * D_FF), lambda i, te, tv: (te[i], 0, 0)),
                  pl.BlockSpec((1, D_FF, D_MODEL), lambda i, te, tv: (te[i], 0, 0)),
                  pl.BlockSpec((1, 1, D_MODEL), lambda i, te, tv: (te[i], 0, 0))],
        out_specs=pl.BlockSpec((MOE_TILE * TOK_SUB, LANE), lambda i, te, tv: (i, 0)),
        scratch_shapes=[pltpu.VMEM((D_MODEL, 2 * D_FF), BF16), pltpu.VMEM((2 * D_FF, D_MODEL), BF16),
                        pltpu.VMEM((2 * D_FF, LANE), F32)],
    )
    return pl.pallas_call(
        _expert_body,
        grid_spec=grid_spec,
        out_shape=jax.ShapeDtypeStruct((n_rows * TOK_SUB, LANE), F32),
        compiler_params=_cp("arbitrary"),
        name="moe_experts",
    )(tile_e, tile_valid, xs, wgu, bgu, wd, bd)


def _combine_copy(ys_ref, buf_ref, sems, slot, d, k, r):
    return pltpu.make_async_copy(ys_ref.at[_tok(d), :], buf_ref.at[slot, k, _tok(r), :], sems.at[slot])


def _combine_body(dest_ref, next_dest_ref, gates_ref, x_ref, g_ref, b_ref, ys_ref, o_ref, buf_ref, sems):
    i = pl.program_id(0)
    slot = i % 2

    def gather(idx_ref, into):
        def start(r, carry):
            for k in range(TOP_K):
                _combine_copy(ys_ref, buf_ref, sems, into, idx_ref[0, 0, r * TOP_K + k], k, r).start(priority=k % 2)
            return carry

        lax.fori_loop(0, DISPATCH_TILE, start, 0)

    @pl.when(i == 0)
    def _():
        gather(dest_ref, 0)

    @pl.when(i + 1 < pl.num_programs(0))
    def _():
        gather(next_dest_ref, 1 - slot)

    def wait(r, carry):
        for k in range(TOP_K):
            _combine_copy(ys_ref, buf_ref, sems, slot, 0, k, 0).wait()
        return carry

    lax.fori_loop(0, DISPATCH_TILE, wait, 0)
    gates = gates_ref[...]
    cols = []
    for s in range(TOK_SUB):
        acc = DN_ALPHA * x_ref[:, s * LANE:(s + 1) * LANE]
        for k in range(TOP_K):
            acc = acc + gates[:, k:k + 1] * buf_ref[slot, k, pl.ds(s, DISPATCH_TILE, stride=TOK_SUB), :]
        cols.append(acc)
    o_ref[...] = _layer_norm(jnp.concatenate(cols, axis=1), g_ref[...], b_ref[...])


def moe_combine_ln(x, ys, dest, gates, g, b):
    t = x.shape[0]
    nt = t // DISPATCH_TILE
    dest3 = dest.reshape(nt, 1, DISPATCH_TILE * TOP_K)
    full = lambda shape: pl.BlockSpec(shape, lambda i: (0,) * len(shape))
    return pl.pallas_call(
        _combine_body,
        grid=(nt,),
        in_specs=[pl.BlockSpec((1, 1, DISPATCH_TILE * TOP_K), lambda i: (i, 0, 0), memory_space=pltpu.SMEM),
                  pl.BlockSpec((1, 1, DISPATCH_TILE * TOP_K), lambda i: (jnp.minimum(i + 1, nt - 1), 0, 0),
                               memory_space=pltpu.SMEM),
                  pl.BlockSpec((DISPATCH_TILE, LANE), lambda i: (i, 0)),
                  pl.BlockSpec((DISPATCH_TILE, D_MODEL), lambda i: (i, 0)),
                  full((1, D_MODEL)), full((1, D_MODEL)),
                  pl.BlockSpec(memory_space=pl.ANY)],
        out_specs=pl.BlockSpec((DISPATCH_TILE, D_MODEL), lambda i: (i, 0)),
        out_shape=jax.ShapeDtypeStruct((t, D_MODEL), F32),
        scratch_shapes=[pltpu.VMEM((2, TOP_K, DISPATCH_TILE * TOK_SUB, LANE), F32), pltpu.SemaphoreType.DMA((2,))],
        compiler_params=_cp("arbitrary"),
        name="moe_combine_ln",
    )(dest3, dest3, gates, x, g, b, ys)


def _rank_body(e_ref, rank_ref, cnt_ref, run_ref):
    i = pl.program_id(0)

    @pl.when(i == 0)
    def _():
        run_ref[...] = jnp.zeros(run_ref.shape, F32)

    e = e_ref[...]
    lane = _iota(e.shape, 1)
    hot = [lane == e[:, k:k + 1] for k in range(TOP_K)]
    m = sum(h.astype(F32) for h in hot)
    ri = _iota((ROW_TILE, ROW_TILE), 0)
    ci = _iota((ROW_TILE, ROW_TILE), 1)
    before = _dot(_bf((ci < ri).astype(F32)), _bf(m)) + run_ref[0:1, :]
    rank = jnp.zeros(e.shape, F32)
    for k in range(TOP_K):
        rank = jnp.where(lane == k, jnp.sum(jnp.where(hot[k], before, 0.0), axis=-1, keepdims=True), rank)
    rank_ref[...] = rank.astype(I32)
    run_ref[0:1, :] = run_ref[0:1, :] + jnp.sum(m, axis=0, keepdims=True)
    cnt_ref[...] = run_ref[...].astype(I32)


def moe_rank(e_pad):
    t = e_pad.shape[0]
    return pl.pallas_call(
        _rank_body,
        grid=(t // ROW_TILE,),
        in_specs=[pl.BlockSpec((ROW_TILE, LANE), lambda i: (i, 0))],
        out_specs=[pl.BlockSpec((ROW_TILE, LANE), lambda i: (i, 0)), pl.BlockSpec((8, LANE), lambda i: (0, 0))],
        out_shape=[jax.ShapeDtypeStruct((t, LANE), I32), jax.ShapeDtypeStruct((8, LANE), I32)],
        scratch_shapes=[pltpu.VMEM((8, LANE), F32)],
        compiler_params=_cp("arbitrary"),
        name="moe_rank",
    )(e_pad)


def moe_plan(e_pad, n_tokens):
    tk = n_tokens * TOP_K
    rank_pad, cnt = moe_rank(e_pad)
    top_e, rank = e_pad[:, :TOP_K], rank_pad[:, :TOP_K]
    counts = cnt[0, :N_EXPERTS]
    ntile = (counts + MOE_TILE - 1) // MOE_TILE
    tile_end = jnp.cumsum(ntile)
    tile_start = tile_end - ntile
    start_of = jnp.sum(jnp.where(top_e[..., None] == jnp.arange(N_EXPERTS, dtype=I32), tile_start, 0), axis=-1)
    dest = (start_of * MOE_TILE + rank).reshape(tk)
    n_tiles = -(-tk // MOE_TILE) + N_EXPERTS
    tiles = jnp.arange(n_tiles, dtype=I32)
    tile_e = jnp.minimum(jnp.sum((tile_end[None, :] <= tiles[:, None]).astype(I32), axis=1), N_EXPERTS - 1)
    tile_valid = (tiles < tile_end[-1]).astype(I32)
    return dest.astype(I32), tile_e, tile_valid, n_tiles * MOE_TILE


def moe_ffn_ln(x1, x1t, e_pad, gates, wgu, bgu, wd, bd, g, b, expert0=0, xs_buf=None):
    t = x1.shape[0]
    dest, tile_e, tile_valid, n_rows = moe_plan(e_pad, t)
    xs = moe_dispatch(x1t, dest, n_rows, xs_buf)
    ys = moe_experts(xs, tile_e + expert0, tile_valid, wgu, bgu, wd, bd)
    return moe_combine_ln(x1, ys, dest, gates, g, b), xs


def _rope_tables(pos):
    half = ROT_DIM // 2
    inv_freq = ROPE_THETA ** (-jnp.arange(half, dtype=F32) / half)
    ang = pos.astype(F32)[:, None] * inv_freq[None, :]
    cos, sin = jnp.cos(ang), jnp.sin(ang)
    n = pos.shape[0]
    ones = jnp.ones((n, HEAD_DIM - ROT_DIM), F32)
    cos_h = jnp.concatenate([cos, cos, ones], axis=1)
    sin_h = jnp.concatenate([-sin, sin, 0.0 * ones], axis=1)
    return jnp.concatenate([cos_h, cos_h], axis=1), jnp.concatenate([sin_h, sin_h], axis=1)


def _conv_tail(h, row0, bsz, seq):
    assert seq >= GDN_CONV - 1
    n = GDN_CONV - 1
    if bsz <= 8:
        return jnp.stack([h[row0 + (b + 1) * seq - n:row0 + (b + 1) * seq, C_GQKV:C_GQKV + GDN_CONV_DIM]
                          for b in range(bsz)], axis=0)
    blk = h[row0:row0 + bsz * seq, C_GQKV:C_GQKV + GDN_CONV_DIM].reshape(bsz, seq, GDN_CONV_DIM)
    return blk[:, seq - n:, :]


def _lane_row(v, offset):
    return jnp.zeros((1, LANE), F32).at[0, offset:offset + v.shape[0]].set(v)


def kernel(x_prompt, x_sample, cache_swa_k, cache_swa_v, state_ssm_re, state_ssm_im, state_gdn_conv, state_gdn, state_gla, w_in, w_out, attn_sinks, ssm_a_re, ssm_a_im, ssm_b_re, ssm_b_im, ssm_c_re, ssm_c_im, ssm_d, ssm_log_dt, ssm_glu_w, ssm_glu_b, gdn_conv_w, gdn_a_log, gdn_dt_bias, gdn_norm_w, gla_gate_w, gla_gate_b, gla_norm_w, ln1_g, ln1_b, ln2_g, ln2_b, router_w, router_b, moe_w_gate_up, moe_b_gate_up, moe_w_down, moe_b_down):
    bp, lp, _ = x_prompt.shape
    bs, ls, _ = x_sample.shape
    n_p, n_s = bp * lp, bs * ls
    depth = w_in.shape[0]

    w_in_r = jnp.concatenate([w_in[..., :1792], w_in[..., 1800:2568], w_in[..., 1792:1800], w_in[..., 2568:N_IN],
                              jnp.zeros(w_in.shape[:2] + (NH - N_IN,), w_in.dtype)], axis=-1).astype(BF16)
    w_out_b = w_out.astype(BF16)
    glu_w_b = ssm_glu_w.astype(BF16)
    rw_pad = jnp.pad(router_w, ((0, 0), (0, 0), (0, LANE - N_EXPERTS)))
    rb_pad = jnp.pad(router_b, ((0, 0), (0, LANE - N_EXPERTS)))
    wgu_all = moe_w_gate_up.reshape(depth * N_EXPERTS, D_MODEL, 2 * D_FF)
    bgu_all = moe_b_gate_up.reshape(depth * N_EXPERTS, 1, 2 * D_FF)
    wd_all = moe_w_down.reshape(depth * N_EXPERTS, D_FF, D_MODEL)
    bd_all = moe_b_down.reshape(depth * N_EXPERTS, 1, D_MODEL)

    cos_p, sin_p = _rope_tables(jnp.arange(lp, dtype=I32))
    cos_s, sin_s = _rope_tables(PAST_LEN + jnp.arange(ls, dtype=I32))

    x = jnp.concatenate([x_prompt.reshape(n_p, D_MODEL), x_sample.reshape(n_s, D_MODEL)], axis=0)
    zeros = lambda *s: jnp.zeros(s, F32)
    new_p = [[] for _ in range(7)]
    new_s = [[] for _ in range(7)]
    xs_buf = None
    for l in range(depth):
        h = in_proj(x, w_in_r[l])
        sinks = attn_sinks[l]
        oa_p, pk, pv = swa_prompt(h, sinks, cos_p, sin_p, bp, lp)
        oa_s, sk, sv = swa_sample(h, n_p, sinks, cache_swa_k[l].reshape(bs, WINDOW, 128),
                                  cache_swa_v[l].reshape(bs, WINDOW, 128), cos_s, sin_s, bs, ls)
        bw, a_bar, cw, dsk = s5_params(ssm_a_re[l], ssm_a_im[l], ssm_b_re[l], ssm_b_im[l], ssm_c_re[l], ssm_c_im[l],
                                       ssm_d[l], ssm_log_dt[l])
        glu_b = ssm_glu_b[l].reshape(1, 2 * GROUP_WIDTH)
        ob_p, hl_p = s5_prompt(h, zeros(bp, 2 * SSM_W), bw, a_bar, cw, dsk, glu_w_b[l], glu_b, bp, lp)
        h0_s = jnp.concatenate([state_ssm_re[l].reshape(bs, SSM_W), state_ssm_im[l].reshape(bs, SSM_W)], axis=1)
        ob_s, hl_s = s5_sample(h, n_p, h0_s, bw, a_bar, cw, dsk, glu_w_b[l], glu_b, bs, ls)
        alog_t = _lane_row(gdn_a_log[l], SM_GA)
        dtb_t = _lane_row(gdn_dt_bias[l], SM_GA)
        gnw = gdn_norm_w[l].reshape(1, GDN_DV)
        oc_p, gs_p = gdn_mixer(h, 0, gdn_conv_w[l], alog_t, dtb_t, gnw, zeros(bp, GDN_CONV - 1, GDN_CONV_DIM),
                               zeros(bp, GDN_HEADS, GDN_DK, GDN_DV), bp, lp)
        oc_s, gs_s = gdn_mixer(h, n_p, gdn_conv_w[l], alog_t, dtb_t, gnw, state_gdn_conv[l], state_gdn[l], bs, ls)
        cv_p = _conv_tail(h, 0, bp, lp)
        cv_s = _conv_tail(h, n_p, bs, ls)
        lgb = gla_gate_b[l].reshape(1, 128)
        lnw = jnp.tile(gla_norm_w[l], GLA_HEADS).reshape(1, 256)
        od_p, lt_p = gla_mixer(h, 0, gla_gate_w[l], lgb, lnw, zeros(bp, 256, 128), bp, lp)
        od_s, lt_s = gla_mixer(h, n_p, gla_gate_w[l], lgb, lnw, gla_state_to_t(state_gla[l]), bs, ls)

        x1, x1t, e_pad, gates = out_proj_ln(
            x, (oa_p, ob_p.reshape(n_p, GROUP_WIDTH), oc_p, od_p), (oa_s, ob_s, oc_s, od_s), w_out_b[l],
            ln1_g[l].reshape(1, D_MODEL), ln1_b[l].reshape(1, D_MODEL), rw_pad[l], rb_pad[l].reshape(1, LANE))
        x, xs_buf = moe_ffn_ln(x1, x1t, e_pad, gates, wgu_all, bgu_all, wd_all, bd_all, ln2_g[l].reshape(1, D_MODEL),
                               ln2_b[l].reshape(1, D_MODEL), expert0=l * N_EXPERTS, xs_buf=xs_buf)

        st_p = (pk.reshape(bp, WINDOW, A_KV_HEADS, HEAD_DIM), pv.reshape(bp, WINDOW, A_KV_HEADS, HEAD_DIM),
                hl_p[:, :SSM_W].reshape(bp, SSM_GROUPS, SSM_STATE), hl_p[:, SSM_W:].reshape(bp, SSM_GROUPS, SSM_STATE),
                cv_p, gs_p, gla_state_from_t(lt_p))
        st_s = (sk.reshape(bs, WINDOW, A_KV_HEADS, HEAD_DIM), sv.reshape(bs, WINDOW, A_KV_HEADS, HEAD_DIM),
                hl_s[:, :SSM_W].reshape(bs, SSM_GROUPS, SSM_STATE), hl_s[:, SSM_W:].reshape(bs, SSM_GROUPS, SSM_STATE),
                cv_s, gs_s, gla_state_from_t(lt_s))
        for i in range(7):
            new_p[i].append(st_p[i])
            new_s[i].append(st_s[i])
    y_p = x[:n_p].reshape(bp, lp, D_MODEL)
    y_s = x[n_p:].reshape(bs, ls, D_MODEL)
    return (y_p, y_s) + tuple(jnp.stack(t, axis=0) for t in new_p) + tuple(jnp.stack(t, axis=0) for t in new_s)
```

```python
import functools

import numpy as np
import jax
import jax.numpy as jnp
from jax import lax
from jax.experimental import pallas as pl
from jax.experimental.pallas import tpu as pltpu

F32 = jnp.float32
BF16 = jnp.bfloat16
I32 = jnp.int32
HI = lax.Precision.HIGHEST

D_MODEL = 1024
DEPTH = 4
PAST_LEN = 8192
GROUP_WIDTH = 256
HEAD_DIM = 64
A_HEADS = 4
A_KV_HEADS = 2
WINDOW = 128
ROPE_THETA = 500000.0
ROT_DIM = 16
SSM_GC = 16
SSM_GROUPS = 16
SSM_STATE = 64
SSM_W = SSM_GROUPS * SSM_STATE
GDN_HEADS = 4
GDN_DK = 64
GDN_DV = 64
GDN_CONV = 4
GDN_CONV_DIM = 768
GDN_CHUNK = 64
GLA_HEADS = 4
GLA_DK = 32
GLA_DV = 64
GLA_RANK = 16
GLA_TAU = 16.0
GLA_CHUNK = 16
N_EXPERTS = 32
TOP_K = 4
D_FF = 1024
SWIGLU_LIMIT = 7.0
SWIGLU_ALPHA = 1.702
DN_ALPHA = (2 * DEPTH) ** 0.25
LN_EPS = 1e-5
RMS_EPS = 1e-6
N_IN = 2584

C_AQ, C_AK, C_AV, C_SU, C_GQKV, C_GZ = 0, 256, 384, 512, 768, 1536
C_LQ, C_LK, C_LV, C_LR, C_SM = 1792, 1920, 2048, 2304, 2560
NH = 2688
SM_GB, SM_GA, SM_LG = 0, 4, 8

LANE = 128
ROW_TILE = 512
MOE_TILE = 512
VMEM_LIMIT = 56 * 1024 * 1024


def _cp(*sem):
    return pltpu.CompilerParams(dimension_semantics=sem, vmem_limit_bytes=VMEM_LIMIT)


def _dot(a, b, precision=None):
    return jnp.dot(a, b, preferred_element_type=F32, precision=precision)


def _dot_nt(a, b, precision=None):
    return lax.dot_general(a, b, (((1,), (1,)), ((), ())), preferred_element_type=F32, precision=precision)


def _dot_tn(a, b, precision=None):
    return lax.dot_general(a, b, (((0,), (0,)), ((), ())), preferred_element_type=F32, precision=precision)


def _bf(x):
    return x.astype(BF16)


def _iota(shape, dim):
    return lax.broadcasted_iota(I32, shape, dim)


def _shr(idx, size):
    return lax.shift_right_logical(idx, int(size).bit_length() - 1)


def _sigmoid(x):
    return 1.0 / (1.0 + jnp.exp(-x))


def _silu(x):
    return x * _sigmoid(x)


def _softplus(x):
    return jnp.maximum(x, 0.0) + jnp.log(1.0 + jnp.exp(-jnp.abs(x)))


def _log_sigmoid(x):
    return -_softplus(-x)


def _gelu_tanh(x):
    return 0.5 * x * (1.0 + jnp.tanh(0.7978845608028654 * (x + 0.044715 * x * x * x)))


def _layer_norm(y, g, b):
    mu = jnp.mean(y, axis=-1, keepdims=True)
    yc = y - mu
    var = jnp.mean(yc * yc, axis=-1, keepdims=True)
    return yc * lax.rsqrt(var + LN_EPS) * g + b


def _in_proj_body(x_ref, w_ref, o_ref):
    xb = _bf(x_ref[...])
    for c0 in range(0, NH, 512):
        c1 = min(c0 + 512, NH)
        o_ref[:, c0:c1] = _dot(xb, w_ref[:, c0:c1])


def in_proj(x, w):
    t = x.shape[0]
    return pl.pallas_call(
        _in_proj_body,
        grid=(t // ROW_TILE,),
        in_specs=[pl.BlockSpec((ROW_TILE, D_MODEL), lambda i: (i, 0)),
                  pl.BlockSpec((D_MODEL, NH), lambda i: (0, 0))],
        out_specs=pl.BlockSpec((ROW_TILE, NH), lambda i: (i, 0)),
        out_shape=jax.ShapeDtypeStruct((t, NH), F32),
        compiler_params=_cp("arbitrary"),
        name="in_proj",
    )(x, w)


def _rope(x, cos, sin):
    w = x.shape[1]
    if w > LANE:
        cos = jnp.concatenate([cos] * (w // LANE), axis=1)
        sin = jnp.concatenate([sin] * (w // LANE), axis=1)
    lane = _iota(x.shape, 1) & (HEAD_DIM - 1)
    swapped = jnp.where(lane < ROT_DIM // 2, pltpu.roll(x, w - ROT_DIM // 2, 1), pltpu.roll(x, ROT_DIM // 2, 1))
    return x * cos + swapped * sin


def _sink_attention_multi(qkvs, mask, sinks_ref):
    pairs = [(i, hq) for i in range(len(qkvs)) for hq in range(A_HEADS)]
    hd = lambda x, j: _bf(x[:, j * HEAD_DIM:(j + 1) * HEAD_DIM])
    kv = lambda hq: hq // (A_HEADS // A_KV_HEADS)
    scores = {(i, hq): _dot_nt(hd(qkvs[i][0], hq), hd(qkvs[i][1], kv(hq))) for i, hq in pairs}
    probs = {}
    masks = mask if isinstance(mask, (list, tuple)) else [mask] * len(qkvs)
    for i, hq in pairs:
        s = jnp.where(masks[i], scores[i, hq] * (HEAD_DIM ** -0.5), -jnp.inf)
        sink = sinks_ref[hq]
        m = jnp.maximum(jnp.max(s, axis=-1, keepdims=True), sink)
        p = jnp.exp(s - m)
        den = jnp.sum(p, axis=-1, keepdims=True) + jnp.exp(sink - m)
        probs[i, hq] = _bf(p / den)
    outs = {(i, hq): _dot(probs[i, hq], hd(qkvs[i][2], kv(hq))) for i, hq in pairs}
    return [jnp.concatenate([outs[i, hq] for hq in range(A_HEADS)], axis=1) for i in range(len(qkvs))]


def _sink_attention(q, kk, vv, mask, sinks_ref):
    return _sink_attention_multi([(q, kk, vv)], mask, sinks_ref)[0]


SWA_QB = 4


def _swa_prompt_body(sinks_ref, cur_ref, prev_ref, cos_ref, sin_ref, cosp_ref, sinp_ref, o_ref, ko_ref, vo_ref):
    i = pl.program_id(1)
    cur = cur_ref[...]
    q = _rope(cur[:, C_AQ:C_AQ + 256], cos_ref[...], sin_ref[...])
    k = _rope(cur[:, C_AK:C_AK + 128], cos_ref[...], sin_ref[...])
    v = cur[:, C_AV:C_AV + 128]
    prev = prev_ref[...]
    kp = _rope(prev[:, 0:128], cosp_ref[...], sinp_ref[...])
    vp = prev[:, 128:256]
    kall = jnp.concatenate([kp, k], axis=0)
    vall = jnp.concatenate([vp, v], axis=0)
    r = _iota((WINDOW, 2 * WINDOW), 0)
    j = _iota((WINDOW, 2 * WINDOW), 1)
    d = WINDOW + r - j
    band = (d >= 0) & (d <= WINDOW)
    first = band & ((j >= WINDOW) | (i > 0))
    qkvs = [(q[m * WINDOW:(m + 1) * WINDOW, :], kall[m * WINDOW:(m + 2) * WINDOW, :], vall[m * WINDOW:(m + 2) * WINDOW, :])
            for m in range(SWA_QB)]
    outs = _sink_attention_multi(qkvs, [first] + [band] * (SWA_QB - 1), sinks_ref)
    o_ref[...] = jnp.concatenate(outs, axis=0)
    ko_ref[0] = k[(SWA_QB - 1) * WINDOW:, :]
    vo_ref[0] = v[(SWA_QB - 1) * WINDOW:, :]


def swa_prompt(h, sinks, cos_t, sin_t, bsz, seq):
    nb = seq // WINDOW
    smem = pl.BlockSpec(memory_space=pltpu.SMEM)
    ns = nb // SWA_QB
    rows = SWA_QB * WINDOW
    cur_tab = pl.BlockSpec((rows, LANE), lambda b, i: (i, 0))
    prev_tab = pl.BlockSpec((WINDOW, LANE), lambda b, i: (jnp.maximum(i * SWA_QB - 1, 0), 0))
    return pl.pallas_call(
        _swa_prompt_body,
        grid=(bsz, ns),
        in_specs=[smem,
                  pl.BlockSpec((rows, 512), lambda b, i: (b * ns + i, 0)),
                  pl.BlockSpec((WINDOW, 256), lambda b, i: (b * nb + jnp.maximum(i * SWA_QB - 1, 0), 1)),
                  cur_tab, cur_tab, prev_tab, prev_tab],
        out_specs=[pl.BlockSpec((rows, 256), lambda b, i: (b * ns + i, 0)),
                   pl.BlockSpec((1, WINDOW, 128), lambda b, i: (b, 0, 0)),
                   pl.BlockSpec((1, WINDOW, 128), lambda b, i: (b, 0, 0))],
        out_shape=[jax.ShapeDtypeStruct((bsz * seq, 256), F32),
                   jax.ShapeDtypeStruct((bsz, WINDOW, 128), F32),
                   jax.ShapeDtypeStruct((bsz, WINDOW, 128), F32)],
        compiler_params=_cp("arbitrary", "arbitrary"),
        name="swa_prompt",
    )(sinks, h, h, cos_t, sin_t, cos_t, sin_t)


SWA_SB = 8


def _swa_sample_body(sinks_ref, cur_ref, kc_ref, vc_ref, cos_ref, sin_ref, o_ref, ko_ref, vo_ref, *, ls):
    cw = WINDOW
    r = _iota((ls, cw + ls), 0)
    j = _iota((ls, cw + ls), 1)
    d = cw + r - j
    mask = (d >= 0) & (d <= WINDOW)
    qkvs = []
    for b in range(SWA_SB):
        cur = cur_ref[b * ls:(b + 1) * ls, :]
        q = _rope(cur[:, C_AQ:C_AQ + 256], cos_ref[...], sin_ref[...])
        k = _rope(cur[:, C_AK:C_AK + 128], cos_ref[...], sin_ref[...])
        v = cur[:, C_AV:C_AV + 128]
        kk = jnp.concatenate([kc_ref[b], k], axis=0)
        vv = jnp.concatenate([vc_ref[b], v], axis=0)
        ko_ref[b] = kk[ls:, :]
        vo_ref[b] = vv[ls:, :]
        qkvs.append((q, kk, vv))
    for b, o in enumerate(_sink_attention_multi(qkvs, mask, sinks_ref)):
        o_ref[b * ls:(b + 1) * ls, :] = o


def swa_sample(h, row0, sinks, k_cache, v_cache, cos_t, sin_t, bsz, ls, seq0=0):
    rows = SWA_SB * ls
    blk0 = row0 // rows
    smem = pl.BlockSpec(memory_space=pltpu.SMEM)
    cache = pl.BlockSpec((SWA_SB, WINDOW, 128), lambda i: (i, 0, 0))
    cache_in = pl.BlockSpec((SWA_SB, WINDOW, 128), lambda i: (seq0 // SWA_SB + i, 0, 0))
    tab = pl.BlockSpec((ls, LANE), lambda i: (0, 0))
    return pl.pallas_call(
        functools.partial(_swa_sample_body, ls=ls),
        grid=(bsz // SWA_SB,),
        in_specs=[smem, pl.BlockSpec((rows, 512), lambda i: (blk0 + i, 0)), cache_in, cache_in, tab, tab],
        out_specs=[pl.BlockSpec((rows, 256), lambda i: (i, 0)), cache, cache],
        out_shape=[jax.ShapeDtypeStruct((bsz * ls, 256), F32),
                   jax.ShapeDtypeStruct((bsz, WINDOW, 128), F32),
                   jax.ShapeDtypeStruct((bsz, WINDOW, 128), F32)],
        compiler_params=_cp("arbitrary"),
        name="swa_sample",
    )(sinks, h, k_cache, v_cache, cos_t, sin_t)


S5_NB = 2 * SSM_W // LANE


def _s5_input(u, bw_ref):
    t = _dot(_bf(u), bw_ref[...])
    return jnp.concatenate([t[:, :SSM_W], t[:, :SSM_W] + t[:, SSM_W:]], axis=1)


def _s5_output(hs, u, cw_ref, d_ref, gw_ref, gb_ref):
    hs = jnp.concatenate([hs[:, :SSM_W] + hs[:, SSM_W:], hs[:, SSM_W:]], axis=1)
    y = _dot(_bf(hs), cw_ref[...]) + d_ref[...] * u
    y = _gelu_tanh(y)
    z = _dot(_bf(y), gw_ref[...]) + gb_ref[...]
    return z[:, :GROUP_WIDTH] * _sigmoid(z[:, GROUP_WIDTH:])


def _s5_scan(s_ref, a_ref, h_init, n_steps, rows):
    nre = S5_NB // 2
    a_re = [jnp.broadcast_to(a_ref[0:1, j * LANE:(j + 1) * LANE], (rows, LANE)) for j in range(nre)]
    a_im = [jnp.broadcast_to(a_ref[1:2, j * LANE:(j + 1) * LANE], (rows, LANE)) for j in range(nre)]

    def step(t, hcar):
        out = [None] * S5_NB
        base = pl.multiple_of(t * rows, rows)
        for j in range(nre):
            hr, hi = hcar[j], hcar[j + nre]
            nr = a_re[j] * hr - a_im[j] * hi + s_ref[j, pl.ds(base, rows), :]
            ni = a_re[j] * hi + a_im[j] * hr + s_ref[j + nre, pl.ds(base, rows), :]
            s_ref[j, pl.ds(base, rows), :] = nr
            s_ref[j + nre, pl.ds(base, rows), :] = ni
            out[j], out[j + nre] = nr, ni
        return tuple(out)

    return lax.fori_loop(0, n_steps, step, tuple(h_init))


def _s5_prompt_body(*refs, nb, tl):
    u_refs = refs[:nb]
    bw_ref, a_ref, cw_ref, d_ref, gw_ref, gb_ref, h0_ref, o_ref, hl_ref, s_ref, hst_ref = refs[nb:]
    i = pl.program_id(0)

    @pl.when(i == 0)
    def _():
        hst_ref[...] = h0_ref[...]

    u_all = jnp.concatenate([u_refs[b][...] for b in range(nb)], axis=0)
    bu = _s5_input(u_all, bw_ref)
    for b in range(nb):
        for j in range(S5_NB):
            s_ref[j, pl.ds(b, tl, stride=nb), :] = bu[b * tl:(b + 1) * tl, j * LANE:(j + 1) * LANE]
    h_init = [hst_ref[:, j * LANE:(j + 1) * LANE] for j in range(S5_NB)]
    h_fin = _s5_scan(s_ref, a_ref, h_init, tl, nb)
    for j in range(S5_NB):
        hst_ref[:, j * LANE:(j + 1) * LANE] = h_fin[j]
    hs = jnp.concatenate([jnp.concatenate([s_ref[j, pl.ds(b, tl, stride=nb), :] for j in range(S5_NB)], axis=1)
                          for b in range(nb)], axis=0)
    o_all = _s5_output(hs, u_all, cw_ref, d_ref, gw_ref, gb_ref)
    for b in range(nb):
        o_ref[b] = o_all[b * tl:(b + 1) * tl, :]

    @pl.when(i == pl.num_programs(0) - 1)
    def _():
        hl_ref[...] = hst_ref[...]


def s5_prompt(h, h0, bw, a, cw, dsk, gw, gb, bsz, seq, tl=128):
    assert bsz == 8
    nt = seq // tl
    full = lambda shape: pl.BlockSpec(shape, lambda i: (0,) * len(shape))
    u_specs = [pl.BlockSpec((tl, GROUP_WIDTH), functools.partial(lambda i, b: (b * nt + i, C_SU // GROUP_WIDTH), b=b))
               for b in range(bsz)]
    return pl.pallas_call(
        functools.partial(_s5_prompt_body, nb=bsz, tl=tl),
        grid=(nt,),
        in_specs=u_specs + [full((GROUP_WIDTH, 2 * SSM_W)), full((2, SSM_W)), full((2 * SSM_W, GROUP_WIDTH)),
                            full((1, GROUP_WIDTH)), full((GROUP_WIDTH, 2 * GROUP_WIDTH)), full((1, 2 * GROUP_WIDTH)),
                            full((bsz, 2 * SSM_W))],
        out_specs=[pl.BlockSpec((bsz, tl, GROUP_WIDTH), lambda i: (0, i, 0)), full((bsz, 2 * SSM_W))],
        out_shape=[jax.ShapeDtypeStruct((bsz, seq, GROUP_WIDTH), F32), jax.ShapeDtypeStruct((bsz, 2 * SSM_W), F32)],
        scratch_shapes=[pltpu.VMEM((S5_NB, tl * bsz, LANE), F32), pltpu.VMEM((bsz, 2 * SSM_W), F32)],
        compiler_params=_cp("arbitrary"),
        name="s5_prompt",
    )(*([h] * bsz), bw, a, cw, dsk, gw, gb, h0)


def _s5_sample_body(u_ref, bw_ref, a_ref, cw_ref, d_ref, gw_ref, gb_ref, h0_ref, o_ref, hl_ref, s_ref, t_ref, *, bsz, ls):
    bu = _s5_input(u_ref[...], bw_ref)
    for j in range(S5_NB):
        s_ref[j] = bu[:, j * LANE:(j + 1) * LANE]
    for t in range(ls):
        for j in range(S5_NB):
            t_ref[j, pl.ds(t * bsz, bsz), :] = s_ref[j, pl.ds(t, bsz, stride=ls), :]
    h_init = [h0_ref[:, j * LANE:(j + 1) * LANE] for j in range(S5_NB)]
    h_fin = _s5_scan(t_ref, a_ref, h_init, ls, bsz)
    for j in range(S5_NB):
        hl_ref[:, j * LANE:(j + 1) * LANE] = h_fin[j]
    for t in range(ls):
        for j in range(S5_NB):
            s_ref[j, pl.ds(t, bsz, stride=ls), :] = t_ref[j, pl.ds(t * bsz, bsz), :]
    hs = jnp.concatenate([s_ref[j] for j in range(S5_NB)], axis=1)
    o_ref[...] = _s5_output(hs, u_ref[...], cw_ref, d_ref, gw_ref, gb_ref)


def s5_sample(h, row0, h0, bw, a, cw, dsk, gw, gb, bsz, ls):
    rows = bsz * ls
    full = lambda shape: pl.BlockSpec(shape, lambda i: (0,) * len(shape))
    return pl.pallas_call(
        functools.partial(_s5_sample_body, bsz=bsz, ls=ls),
        grid=(1,),
        in_specs=[pl.BlockSpec((rows, GROUP_WIDTH), lambda i: (row0 // rows, C_SU // GROUP_WIDTH)),
                  full((GROUP_WIDTH, 2 * SSM_W)), full((2, SSM_W)), full((2 * SSM_W, GROUP_WIDTH)),
                  full((1, GROUP_WIDTH)), full((GROUP_WIDTH, 2 * GROUP_WIDTH)), full((1, 2 * GROUP_WIDTH)),
                  full((bsz, 2 * SSM_W))],
        out_specs=[full((rows, GROUP_WIDTH)), full((bsz, 2 * SSM_W))],
        out_shape=[jax.ShapeDtypeStruct((rows, GROUP_WIDTH), F32), jax.ShapeDtypeStruct((bsz, 2 * SSM_W), F32)],
        scratch_shapes=[pltpu.VMEM((S5_NB, rows, LANE), F32), pltpu.VMEM((S5_NB, rows, LANE), F32)],
        compiler_params=_cp("arbitrary"),
        name="s5_sample",
    )(h, bw, a, cw, dsk, gw, gb, h0)


def s5_params(a_re, a_im, b_re, b_im, c_re, c_im, d_skip, log_dt):
    lam = lax.complex(a_re, a_im)
    delta = jnp.exp(log_dt)[:, None]
    a_bar = jnp.exp(lam * delta)
    b_bar = ((a_bar - 1.0) / lam)[..., None] * lax.complex(b_re, b_im)
    eye = jnp.eye(SSM_GROUPS, dtype=F32)
    bw_re = jnp.einsum('gpc,gh->gchp', b_bar.real, eye).reshape(GROUP_WIDTH, SSM_W)
    bw_im = jnp.einsum('gpc,gh->gchp', b_bar.imag, eye).reshape(GROUP_WIDTH, SSM_W)
    bw = jnp.concatenate([bw_re, bw_im - bw_re], axis=1).astype(BF16)
    cw_re = jnp.einsum('gcp,gh->gphc', c_re, eye).reshape(SSM_W, GROUP_WIDTH)
    cw_im = jnp.einsum('gcp,gh->gphc', c_im, eye).reshape(SSM_W, GROUP_WIDTH)
    cw = jnp.concatenate([cw_re, -(cw_re + cw_im)], axis=0).astype(BF16)
    a = jnp.stack([a_bar.real.reshape(SSM_W), a_bar.imag.reshape(SSM_W)], axis=0)
    return bw, a, cw, d_skip.reshape(1, GROUP_WIDTH)


GDN_RT = 256
GDN_ST = 64
GDN_SG = 8


def _split(x):
    hi = _bf(x)
    return hi, _bf(x - hi.astype(F32))


def _dot3(a, b):
    return _dot(a[0], b[0]) + (_dot(a[0], b[1]) + _dot(a[1], b[0]))


def _gdn_prep_body(*refs, c, sample, tiles_per_seq):
    if sample:
        x_ref, cs_ref, z_ref, sm_ref, cw_ref, alog_ref, dtb_ref = refs[:7]
        outs = refs[7:14]
        xs_ref, cb_ref = refs[14:]
    else:
        x_ref, prev_ref, cs_ref, z_ref, sm_ref, cw_ref, alog_ref, dtb_ref = refs[:8]
        outs = refs[8:15]
        (xs_ref,) = refs[15:]
    u_ref, w_ref, qk_ref, qg_ref, kd_ref, eg_ref, zs_ref = outs
    rt = GDN_RT
    x = x_ref[...]
    xs_ref[8:8 + rt, :] = x
    if sample:
        xs_ref[0:8, :] = jnp.zeros((8, GDN_CONV_DIM), F32)
        cb_ref[0:rt, :] = cs_ref[...]
        cb_ref[rt:rt + 8, :] = jnp.zeros((8, GDN_CONV_DIM), F32)
        pos = _iota((rt, GDN_CONV_DIM), 0) & (c - 1)
        shifted = lambda i: jnp.where(pos >= i, xs_ref[8 - i:8 - i + rt, :], cb_ref[8 - i:8 - i + rt, :])
    else:
        first = (pl.program_id(0) % tiles_per_seq) == 0
        xs_ref[0:8, :] = jnp.where(first, cs_ref[...], prev_ref[...])
        shifted = lambda i: xs_ref[8 - i:8 - i + rt, :]
    conv = shifted(3) * cw_ref[0:1, :]
    conv = conv + shifted(2) * cw_ref[1:2, :]
    conv = conv + shifted(1) * cw_ref[2:3, :]
    conv = conv + x * cw_ref[3:4, :]
    qkv = _silu(conv)
    zs_ref[...] = _silu(z_ref[...])

    sm = sm_ref[...]
    beta_all = _sigmoid(sm)
    g_all = -jnp.exp(alog_ref[...]) * _softplus(sm + dtb_ref[...])
    st = GDN_ST
    ri = _iota((st, st), 0)
    ci = _iota((st, st), 1)
    same = _shr(ri, c) == _shr(ci, c)
    low = (ci <= ri) & same
    tri = low.astype(F32)
    last = (ci == (ri | (c - 1))).astype(F32)
    eye = (ri == ci).astype(F32)
    lane = _iota((st, LANE), 1)
    nsub = rt // st
    probs = [(t, hh) for t in range(nsub) for hh in range(GDN_HEADS)]
    gc_alls, gl_alls = [], []
    for t in range(nsub):
        rows = slice(t * st, (t + 1) * st)
        gc_alls.append(_dot_nt_exact_lhs(_bf(tri), g_all[rows, :]))
    for t in range(nsub):
        gl_alls.append(_dot_nt_exact_lhs(_bf(last), gc_alls[t]))
        eg_ref[t * st:(t + 1) * st, :] = jnp.exp(gl_alls[t])
    qs, ks, rhss, decays, a_stricts, invs = {}, {}, {}, {}, {}, {}
    for t, hh in probs:
        rows = slice(t * st, (t + 1) * st)
        q = qkv[rows, hh * GDN_DK:(hh + 1) * GDN_DK]
        k = qkv[rows, 256 + hh * GDN_DK:256 + (hh + 1) * GDN_DK]
        v = qkv[rows, 512 + hh * GDN_DV:512 + (hh + 1) * GDN_DV]
        q = q * lax.rsqrt(jnp.sum(q * q, axis=-1, keepdims=True) + 1e-6) * (GDN_DK ** -0.5)
        k = k * lax.rsqrt(jnp.sum(k * k, axis=-1, keepdims=True) + 1e-6)
        beta = beta_all[rows, SM_GB + hh:SM_GB + hh + 1]
        gc = gc_alls[t][:, SM_GA + hh:SM_GA + hh + 1]
        gl = gl_alls[t][:, SM_GA + hh:SM_GA + hh + 1]
        sel = (lane == SM_GA + hh).astype(F32)
        gc_row = _dot_nt(sel, gc_alls[t], HI)
        decays[t, hh] = jnp.exp(jnp.where(low, gc - gc_row, -jnp.inf))
        kb = k * beta
        eg = jnp.exp(gc)
        qs[t, hh], ks[t, hh] = q, k
        rhss[t, hh] = _split(jnp.concatenate([v * beta, kb * eg], axis=1))
        a_stricts[t, hh] = jnp.where(ci < ri, _dot_nt(_bf(kb), _bf(k)) * decays[t, hh], 0.0)
        qg_ref[rows, hh * GDN_DK:(hh + 1) * GDN_DK] = q * eg
        kd_ref[rows, hh * GDN_DK:(hh + 1) * GDN_DK] = k * jnp.exp(gl - gc)
    for t, hh in probs:
        qk_ref[t * st:(t + 1) * st, hh * st:(hh + 1) * st] = _dot_nt(_bf(qs[t, hh]), _bf(ks[t, hh])) * decays[t, hh]
    for p in probs:
        invs[p] = eye - jnp.where(_shr(ri, 2) == _shr(ci, 2), a_stricts[p], 0.0)
    s = 2
    while s < c:
        pair = (_shr(ri, 2 * s) == _shr(ci, 2 * s)) & (_shr(ri, s) != _shr(ci, s))
        inv_s = {p: _split(invs[p]) for p in probs}
        mid = {p: _dot3(_split(jnp.where(pair, a_stricts[p], 0.0)), inv_s[p]) for p in probs}
        for p in probs:
            invs[p] = invs[p] - _dot3(inv_s[p], _split(mid[p]))
        s *= 2
    for t, hh in probs:
        sol = _dot3(_split(invs[t, hh]), rhss[t, hh])
        u_ref[t * st:(t + 1) * st, hh * GDN_DV:(hh + 1) * GDN_DV] = sol[:, :GDN_DV]
        w_ref[t * st:(t + 1) * st, hh * GDN_DV:(hh + 1) * GDN_DV] = sol[:, GDN_DV:]


def _gdn_scan_body(u_ref, w_ref, qk_ref, qg_ref, kd_ref, eg_ref, zs_ref, nw_ref, s0_ref, o_ref, so_ref, st_ref, *, c, sg):
    n = pl.program_id(1)

    @pl.when(n == 0)
    def _():
        st_ref[...] = s0_ref[...]

    probs = [(s, hh) for s in range(sg) for hh in range(GDN_HEADS)]
    hsl = lambda hh: slice(hh * GDN_DV, (hh + 1) * GDN_DV)
    state_b = {(s, hh): _bf(st_ref[s, hh]) for s, hh in probs}
    v_b = {}
    for s, hh in probs:
        v_b[s, hh] = _bf(u_ref[s, :, hsl(hh)] - _dot(_bf(w_ref[s, :, hsl(hh)]), state_b[s, hh]))
    o_inter = {(s, hh): _dot(_bf(qg_ref[s, :, hsl(hh)]), state_b[s, hh]) for s, hh in probs}
    for s, hh in probs:
        col0 = (s * c) % GDN_ST
        qk = qk_ref[s, :, hh * GDN_ST + col0:hh * GDN_ST + col0 + c]
        o = o_inter[s, hh] + _dot(_bf(qk), v_b[s, hh])
        o = o * lax.rsqrt(jnp.mean(o * o, axis=-1, keepdims=True) + RMS_EPS) * nw_ref[...]
        o_ref[s, :, hsl(hh)] = o * zs_ref[s, :, hsl(hh)]
    for s, hh in probs:
        eg_last = eg_ref[s, c - 1:c, SM_GA + hh:SM_GA + hh + 1]
        st_ref[s, hh] = st_ref[s, hh] * eg_last + _dot_tn(_bf(kd_ref[s, :, hsl(hh)]), v_b[s, hh])

    @pl.when(n == pl.num_programs(1) - 1)
    def _():
        so_ref[...] = st_ref[...]


def gdn_mixer(h, row0, conv_w, alog_t, dtb_t, norm_w, conv_state, s0, bsz, seq, seq0=0):
    c = min(GDN_CHUNK, seq)
    sample = seq == c
    rows = bsz * seq
    rt = GDN_RT
    nt = rows // rt
    blk0 = row0 // rt
    tps = max(seq // rt, 1)
    cs_rows = jnp.pad(conv_state, ((0, 0), (8 - (GDN_CONV - 1), 0), (0, 0))).reshape(bsz * 8, GDN_CONV_DIM)
    row = lambda w, col: pl.BlockSpec((rt, w), lambda i: (blk0 + i, col))
    full = lambda shape: pl.BlockSpec(shape, lambda i: (0,) * len(shape))
    if sample:
        aux_specs = [pl.BlockSpec((rt, GDN_CONV_DIM), lambda i: (i, 0))]
        aux = [cs_rows]
        scratch = [pltpu.VMEM((8 + rt, GDN_CONV_DIM), F32), pltpu.VMEM((8 + rt, GDN_CONV_DIM), F32)]
    else:
        aux_specs = [pl.BlockSpec((8, GDN_CONV_DIM), lambda i: (jnp.maximum((row0 + i * rt) // 8 - 1, 0), C_GQKV // GDN_CONV_DIM)),
                     pl.BlockSpec((8, GDN_CONV_DIM), lambda i: (i // tps, 0))]
        aux = [h, cs_rows]
        scratch = [pltpu.VMEM((8 + rt, GDN_CONV_DIM), F32)]
    out_w = [GROUP_WIDTH] * 5 + [LANE, GROUP_WIDTH]
    prep = pl.pallas_call(
        functools.partial(_gdn_prep_body, c=c, sample=sample, tiles_per_seq=tps),
        grid=(nt,),
        in_specs=[row(GDN_CONV_DIM, C_GQKV // GDN_CONV_DIM)] + aux_specs
                 + [row(GROUP_WIDTH, C_GZ // GROUP_WIDTH), row(LANE, C_SM // LANE),
                    full((GDN_CONV, GDN_CONV_DIM)), full((1, LANE)), full((1, LANE))],
        out_specs=[pl.BlockSpec((rt, w), lambda i: (i, 0)) for w in out_w],
        out_shape=[jax.ShapeDtypeStruct((rows, w), F32) for w in out_w],
        scratch_shapes=scratch,
        compiler_params=_cp("arbitrary"),
        name="gdn_prep",
    )(h, *aux, h, h, conv_w, alog_t, dtb_t)
    prep = [a.reshape(bsz, seq, a.shape[-1]) for a in prep]
    sg = GDN_SG
    ng, nc = bsz // sg, seq // c
    seq_blk = lambda w: pl.BlockSpec((sg, c, w), lambda g, n: (g, n, 0))
    st_spec = pl.BlockSpec((sg, GDN_HEADS, GDN_DK, GDN_DV), lambda g, n: (g, 0, 0, 0))
    s0_spec = pl.BlockSpec((sg, GDN_HEADS, GDN_DK, GDN_DV), lambda g, n: (seq0 // sg + g, 0, 0, 0))
    o, s_fin = pl.pallas_call(
        functools.partial(_gdn_scan_body, c=c, sg=sg),
        grid=(ng, nc),
        in_specs=[seq_blk(w) for w in out_w] + [pl.BlockSpec((1, GDN_DV), lambda g, n: (0, 0)), s0_spec],
        out_specs=[seq_blk(GROUP_WIDTH), st_spec],
        out_shape=[jax.ShapeDtypeStruct((bsz, seq, GROUP_WIDTH), F32),
                   jax.ShapeDtypeStruct((bsz, GDN_HEADS, GDN_DK, GDN_DV), F32)],
        scratch_shapes=[pltpu.VMEM((sg, GDN_HEADS, GDN_DK, GDN_DV), F32)],
        compiler_params=_cp("arbitrary", "arbitrary"),
        name="gdn_scan",
    )(*prep, norm_w, s0)
    return o.reshape(rows, GROUP_WIDTH), s_fin


GLA_RT = 256


def _dot_exact(x, w):
    hi = _bf(x)
    r1 = x - hi.astype(F32)
    mid = _bf(r1)
    lo = _bf(r1 - mid.astype(F32))
    return _dot(hi, w) + (_dot(mid, w) + _dot(lo, w))


def _gla_body(qk_ref, vr_ref, sm_ref, gw_ref, gb_ref, nw_ref, s0_ref, o_ref, so_ref,
              kb_ref, bb_ref, vb_ref, st_ref, *, c, seq):
    tr = GLA_RT
    n = pl.program_id(1)
    pad = GLA_CHUNK
    carried = seq >= tr

    @pl.when(n == 0)
    def _():
        kb_ref[0:pad, :] = jnp.zeros((pad, 128), F32)
        bb_ref[0:pad, :] = jnp.zeros((pad, 128), F32)
        vb_ref[0:pad, :] = jnp.zeros((pad, 256), F32)
        if carried:
            st_ref[...] = s0_ref[0]

    q = qk_ref[:, 0:128] * (GLA_DK ** -0.5)
    k = qk_ref[:, 128:256]
    v = vr_ref[:, 0:256]
    lg = sm_ref[:, SM_LG:SM_LG + GLA_RANK]
    log_a = _log_sigmoid(_dot(_bf(lg), _bf(gw_ref[...])) + gb_ref[...]) / GLA_TAU
    ri = _iota((tr, tr), 0)
    ci = _iota((tr, tr), 1)
    tri = _bf(((ci <= ri) & (_shr(ri, c) == _shr(ci, c))).astype(F32))
    bc = _dot_nt_exact_lhs(tri, log_a)
    kb_ref[pad:pad + tr, :] = k
    bb_ref[pad:pad + tr, :] = bc
    vb_ref[pad:pad + tr, :] = v
    ones_kv = _bf((_shr(_iota((128, 256), 0), GLA_DK) == _shr(_iota((128, 256), 1), GLA_DV)).astype(F32))
    pos = _iota((tr, 128), 0) & (c - 1)
    ps = []
    for dl in range(c):
        ks = kb_ref[pad - dl:pad - dl + tr, :]
        bs = bb_ref[pad - dl:pad - dl + tr, :]
        ps.append(jnp.where(pos >= dl, q * ks * jnp.exp(bc - bs), 0.0))
    att = _dot_exact(jnp.concatenate(ps, axis=0), ones_kv)
    o_intra = jnp.zeros((tr, 256), F32)
    for dl in range(c):
        vs = vb_ref[pad - dl:pad - dl + tr, :]
        o_intra = o_intra + _bf(att[dl * tr:(dl + 1) * tr, :]).astype(F32) * _bf(vs).astype(F32)
    bd = (_shr(_iota((256, 128), 0), GLA_DV) == _shr(_iota((256, 128), 1), GLA_DK)).astype(F32)
    nch = tr // c
    sls = [slice(ch * c, (ch + 1) * c) for ch in range(nch)]
    b_last = [bc[sl, :][c - 1:c, :] for sl in sls]
    incs = [bd * _dot_tn(_bf(v[sl, :]), _bf(k[sl, :] * jnp.exp(bl - bc[sl, :]))) for sl, bl in zip(sls, b_last)]
    outs = []
    st = st_ref[...] if carried else None
    for ch, sl in enumerate(sls):
        if not carried:
            st = s0_ref[(ch * c) // seq]
        outs.append(_dot_nt(_bf(q[sl, :] * jnp.exp(bc[sl, :])), _bf(st)) + o_intra[sl, :])
        st = st * jnp.exp(b_last[ch]) + incs[ch]
        if not carried:
            so_ref[(ch * c) // seq] = st
    if carried:
        st_ref[...] = st
        so_ref[0] = st
    o = jnp.concatenate(outs, axis=0)
    ones_vv = _bf((_shr(_iota((256, 256), 0), GLA_DV) == _shr(_iota((256, 256), 1), GLA_DV)).astype(F32))
    ms = _dot_exact(o * o, ones_vv) * (1.0 / GLA_DV)
    o_ref[...] = o * lax.rsqrt(ms + RMS_EPS) * nw_ref[...] * _silu(vr_ref[:, 256:512])


def _dot_nt_exact_lhs(w, x):
    hi = _bf(x)
    r1 = x - hi.astype(F32)
    mid = _bf(r1)
    lo = _bf(r1 - mid.astype(F32))
    return _dot(w, hi) + (_dot(w, mid) + _dot(w, lo))


def gla_mixer(h, row0, gate_w, gate_b, norm_w4, s0t, bsz, seq):
    c = min(GLA_CHUNK, seq)
    tr = GLA_RT
    spt = max(tr // seq, 1)
    nt = max(seq // tr, 1)
    ng = bsz // spt
    blk0 = row0 // tr
    row = lambda w, col: pl.BlockSpec((tr, w), lambda g, n: (blk0 + g * nt + n, col))
    full = lambda shape: pl.BlockSpec(shape, lambda g, n: (0,) * len(shape))
    st_spec = pl.BlockSpec((spt, 256, 128), lambda g, n: (g, 0, 0))
    return pl.pallas_call(
        functools.partial(_gla_body, c=c, seq=seq),
        grid=(ng, nt),
        in_specs=[row(256, C_LQ // 256), row(512, C_LV // 512), row(LANE, C_SM // LANE),
                  full((GLA_RANK, 128)), full((1, 128)), full((1, 256)), st_spec],
        out_specs=[pl.BlockSpec((tr, 256), lambda g, n: (g * nt + n, 0)), st_spec],
        out_shape=[jax.ShapeDtypeStruct((bsz * seq, 256), F32), jax.ShapeDtypeStruct((bsz, 256, 128), F32)],
        scratch_shapes=[pltpu.VMEM((GLA_CHUNK + tr, 128), F32), pltpu.VMEM((GLA_CHUNK + tr, 128), F32),
                        pltpu.VMEM((GLA_CHUNK + tr, 256), F32), pltpu.VMEM((256, 128), F32)],
        compiler_params=_cp("arbitrary", "arbitrary"),
        name="gla",
    )(h, h, h, gate_w, gate_b, norm_w4, s0t)


def gla_state_to_t(s):
    eye = jnp.eye(GLA_HEADS, dtype=s.dtype)
    return jnp.einsum('bhde,hg->bhegd', s, eye).reshape(s.shape[0], 256, 128)


def gla_state_from_t(st):
    b = st.shape[0]
    t5 = st.reshape(b, GLA_HEADS, GLA_DV, GLA_HEADS, GLA_DK)
    diag = jnp.stack([t5[:, hh, :, hh, :] for hh in range(GLA_HEADS)], axis=1)
    return jnp.swapaxes(diag, 2, 3)


TOK_SUB = D_MODEL // LANE


def _to_token_tiles(ref, row0, y):
    n = y.shape[0]
    for s in range(TOK_SUB):
        ref[pl.ds(row0 * TOK_SUB + s, n, stride=TOK_SUB), :] = y[:, s * LANE:(s + 1) * LANE]


def _from_token_tiles(ref, row0, n, s):
    return ref[pl.ds(row0 * TOK_SUB + s, n, stride=TOK_SUB), :]


def _out_proj_body(x_ref, ap, bp, cp, dp, a_s, b_s, c_s, d_s, w_ref, g_ref, b_ref, rw_ref, rb_ref,
                   o_ref, t_ref, e_ref, gt_ref, *, n_prompt_tiles):
    i = pl.program_id(0)

    def run(refs):
        acc = DN_ALPHA * x_ref[...]
        for m, r in enumerate(refs):
            acc = acc + _dot(_bf(r[...]), w_ref[m * GROUP_WIDTH:(m + 1) * GROUP_WIDTH, :])
        y = _layer_norm(acc, g_ref[...], b_ref[...])
        o_ref[...] = y
        _to_token_tiles(t_ref, 0, y)
        e_ref[...], gt_ref[...] = _route(y, rw_ref, rb_ref)

    @pl.when(i < n_prompt_tiles)
    def _():
        run((ap, bp, cp, dp))

    @pl.when(i >= n_prompt_tiles)
    def _():
        run((a_s, b_s, c_s, d_s))


def out_proj_ln(x, mix_p, mix_s, w, g, b, rw, rb):
    t = x.shape[0]
    npt = mix_p[0].shape[0] // ROW_TILE
    nst = mix_s[0].shape[0] // ROW_TILE
    p_spec = pl.BlockSpec((ROW_TILE, GROUP_WIDTH), lambda i: (jnp.minimum(i, npt - 1), 0))
    s_spec = pl.BlockSpec((ROW_TILE, GROUP_WIDTH), lambda i: (jnp.clip(i - npt, 0, nst - 1), 0))
    full = lambda shape: pl.BlockSpec(shape, lambda i: (0,) * len(shape))
    return pl.pallas_call(
        functools.partial(_out_proj_body, n_prompt_tiles=npt),
        grid=(t // ROW_TILE,),
        in_specs=[pl.BlockSpec((ROW_TILE, D_MODEL), lambda i: (i, 0))] + [p_spec] * 4 + [s_spec] * 4
                 + [full((D_MODEL, D_MODEL)), full((1, D_MODEL)), full((1, D_MODEL)), full((D_MODEL, LANE)), full((1, LANE))],
        out_specs=[pl.BlockSpec((ROW_TILE, D_MODEL), lambda i: (i, 0)),
                   pl.BlockSpec((ROW_TILE * TOK_SUB, LANE), lambda i: (i, 0)),
                   pl.BlockSpec((ROW_TILE, LANE), lambda i: (i, 0)), pl.BlockSpec((ROW_TILE, LANE), lambda i: (i, 0))],
        out_shape=[jax.ShapeDtypeStruct((t, D_MODEL), F32), jax.ShapeDtypeStruct((t * TOK_SUB, LANE), F32),
                   jax.ShapeDtypeStruct((t, LANE), I32), jax.ShapeDtypeStruct((t, LANE), F32)],
        compiler_params=_cp("arbitrary"),
        name="out_proj_ln",
    )(x, *mix_p, *mix_s, w, g, b, rw, rb)


def _route(x, w_ref, b_ref):
    logits = _dot(_bf(x), _bf(w_ref[...])) + b_ref[...]
    lane = _iota(logits.shape, 1)
    vals = jnp.where(lane < N_EXPERTS, logits, -jnp.inf)
    eo = jnp.zeros(logits.shape, I32)
    top = []
    for k in range(TOP_K):
        m = jnp.max(vals, axis=-1, keepdims=True)
        idx = jnp.min(jnp.where(vals == m, lane.astype(F32), float(LANE)), axis=-1, keepdims=True).astype(I32)
        eo = jnp.where(lane == k, idx, eo)
        top.append(m)
        vals = jnp.where(lane == idx, -jnp.inf, vals)
    ex = [jnp.exp(v - top[0]) for v in top]
    den = ex[0] + ex[1] + ex[2] + ex[3]
    go = jnp.zeros(logits.shape, F32)
    for k in range(TOP_K):
        go = jnp.where(lane == k, ex[k] / den, go)
    return eo, go


DISPATCH_TILE = 256


def _tok(i):
    return pl.ds(pl.multiple_of(i * TOK_SUB, TOK_SUB), TOK_SUB)


def _dispatch_copy(x_ref, xs_ref, sem, r, d):
    return pltpu.make_async_copy(x_ref.at[_tok(r), :], xs_ref.at[_tok(d), :], sem)


def _dispatch_body(dest_ref, x_ref, xs_in_ref, xs_ref, sem):
    del xs_in_ref

    def start(r, carry):
        for k in range(TOP_K):
            _dispatch_copy(x_ref, xs_ref, sem, r, dest_ref[0, 0, r * TOP_K + k]).start(priority=k % 2)
        return carry

    lax.fori_loop(0, DISPATCH_TILE, start, 0)

    def wait(r, carry):
        for k in range(TOP_K):
            _dispatch_copy(x_ref, xs_ref, sem, 0, 0).wait()
        return carry

    lax.fori_loop(0, DISPATCH_TILE, wait, 0)


def moe_dispatch(xt, dest, n_rows, xs_buf=None):
    t = xt.shape[0] // TOK_SUB
    nt = t // DISPATCH_TILE
    zeros = jnp.zeros((n_rows * TOK_SUB, LANE), F32) if xs_buf is None else xs_buf
    return pl.pallas_call(
        _dispatch_body,
        grid=(nt,),
        in_specs=[pl.BlockSpec((1, 1, DISPATCH_TILE * TOP_K), lambda i: (i, 0, 0), memory_space=pltpu.SMEM),
                  pl.BlockSpec((DISPATCH_TILE * TOK_SUB, LANE), lambda i: (i, 0)),
                  pl.BlockSpec(memory_space=pl.ANY)],
        out_specs=pl.BlockSpec(memory_space=pl.ANY),
        out_shape=jax.ShapeDtypeStruct((n_rows * TOK_SUB, LANE), F32),
        scratch_shapes=[pltpu.SemaphoreType.DMA(())],
        input_output_aliases={2: 0},
        compiler_params=_cp("arbitrary"),
        name="moe_dispatch",
    )(dest.reshape(nt, 1, DISPATCH_TILE * TOP_K), xt, zeros)


EXPERT_COLS = 512


def _expert_body(te_ref, tv_ref, xs_ref, wgu_ref, bgu_ref, wd_ref, bd_ref, ys_ref, wgu_s, wd2_s, tmp_s):
    i = pl.program_id(0)
    valid = tv_ref[i] != 0
    changed = (i == 0) | (te_ref[i] != te_ref[jnp.maximum(i - 1, 0)])

    @pl.when(i == 0)
    def _():
        tmp_s[...] = jnp.zeros(tmp_s.shape, F32)

    @pl.when(valid & changed)
    def _():
        for c0 in range(0, 2 * D_FF, 512):
            wgu_s[:, c0:c0 + 512] = _bf(wgu_ref[0, :, c0:c0 + 512])
        for j in range(D_MODEL // LANE):
            tmp_s[pl.ds(0, D_FF, stride=2), :] = wd_ref[0, :, j * LANE:(j + 1) * LANE]
            wd2_s[:, j * LANE:(j + 1) * LANE] = _bf(tmp_s[...])

    @pl.when(valid)
    def _():
        xb = jnp.concatenate([_bf(_from_token_tiles(xs_ref, 0, MOE_TILE, s)) for s in range(TOK_SUB)], axis=1)
        chunks = range(0, 2 * D_FF, EXPERT_COLS)
        gus = [_dot(xb, wgu_s[:, c0:c0 + EXPERT_COLS]) + bgu_ref[0][:, c0:c0 + EXPERT_COLS] for c0 in chunks]
        even = (_iota((MOE_TILE, EXPERT_COLS), 1) & 1) == 0
        y = jnp.zeros((MOE_TILE, D_MODEL), F32)
        for c0, gu in zip(chunks, gus):
            nxt = pltpu.roll(gu, EXPERT_COLS - 1, 1)
            gate = jnp.minimum(gu, SWIGLU_LIMIT)
            up = jnp.clip(nxt, -SWIGLU_LIMIT, SWIGLU_LIMIT)
            act = gate * _sigmoid(gate * SWIGLU_ALPHA) * (up + 1.0)
            act = jnp.where(even, act, 0.0)
            y = y + _dot(_bf(act), wd2_s[c0:c0 + EXPERT_COLS, :])
        _to_token_tiles(ys_ref, 0, y + bd_ref[0])

    @pl.when(jnp.logical_not(valid))
    def _():
        ys_ref[...] = jnp.zeros(ys_ref.shape, F32)


def moe_experts(xs, tile_e, tile_valid, wgu, bgu, wd, bd):
    n_rows = xs.shape[0] // TOK_SUB
    nt = n_rows // MOE_TILE
    grid_spec = pltpu.PrefetchScalarGridSpec(
        num_scalar_prefetch=2,
        grid=(nt,),
        in_specs=[pl.BlockSpec((MOE_TILE * TOK_SUB, LANE), lambda i, te, tv: (i, 0)),
                  pl.BlockSpec((1, D_MODEL, 2 * D_FF), lambda i, te, tv: (te[i], 0, 0)),
                  pl.BlockSpec((1, 1, 2 * D_FF), lambda i, te, tv: (te[i], 0, 0)),
                  pl.BlockSpec((1, D_FF, D_MODEL), lambda i, te, tv: (te[i], 0, 0)),
                  pl.BlockSpec((1, 1, D_MODEL), lambda i, te, tv: (te[i], 0, 0))],
        out_specs=pl.BlockSpec((MOE_TILE * TOK_SUB, LANE), lambda i, te, tv: (i, 0)),
        scratch_shapes=[pltpu.VMEM((D_MODEL, 2 * D_FF), BF16), pltpu.VMEM((2 * D_FF, D_MODEL), BF16),
                        pltpu.VMEM((2 * D_FF, LANE), F32)],
    )
    return pl.pallas_call(
        _expert_body,
        grid_spec=grid_spec,
        out_shape=jax.ShapeDtypeStruct((n_rows * TOK_SUB, LANE), F32),
        compiler_params=_cp("arbitrary"),
        name="moe_experts",
    )(tile_e, tile_valid, xs, wgu, bgu, wd, bd)


def _combine_copy(ys_ref, buf_ref, sems, slot, d, k, r):
    return pltpu.make_async_copy(ys_ref.at[_tok(d), :], buf_ref.at[slot, k, _tok(r), :], sems.at[slot])


def _combine_body(dest_ref, next_dest_ref, gates_ref, x_ref, g_ref, b_ref, ys_ref, o_ref, buf_ref, sems):
    i = pl.program_id(0)
    slot = i % 2

    def gather(idx_ref, into):
        def start(r, carry):
            for k in range(TOP_K):
                _combine_copy(ys_ref, buf_ref, sems, into, idx_ref[0, 0, r * TOP_K + k], k, r).start(priority=k % 2)
            return carry

        lax.fori_loop(0, DISPATCH_TILE, start, 0)

    @pl.when(i == 0)
    def _():
        gather(dest_ref, 0)

    @pl.when(i + 1 < pl.num_programs(0))
    def _():
        gather(next_dest_ref, 1 - slot)

    def wait(r, carry):
        for k in range(TOP_K):
            _combine_copy(ys_ref, buf_ref, sems, slot, 0, k, 0).wait()
        return carry

    lax.fori_loop(0, DISPATCH_TILE, wait, 0)
    gates = gates_ref[...]
    cols = []
    for s in range(TOK_SUB):
        acc = DN_ALPHA * x_ref[:, s * LANE:(s + 1) * LANE]
        for k in range(TOP_K):
            acc = acc + gates[:, k:k + 1] * buf_ref[slot, k, pl.ds(s, DISPATCH_TILE, stride=TOK_SUB), :]
        cols.append(acc)
    o_ref[...] = _layer_norm(jnp.concatenate(cols, axis=1), g_ref[...], b_ref[...])


def moe_combine_ln(x, ys, dest, gates, g, b):
    t = x.shape[0]
    nt = t // DISPATCH_TILE
    dest3 = dest.reshape(nt, 1, DISPATCH_TILE * TOP_K)
    full = lambda shape: pl.BlockSpec(shape, lambda i: (0,) * len(shape))
    return pl.pallas_call(
        _combine_body,
        grid=(nt,),
        in_specs=[pl.BlockSpec((1, 1, DISPATCH_TILE * TOP_K), lambda i: (i, 0, 0), memory_space=pltpu.SMEM),
                  pl.BlockSpec((1, 1, DISPATCH_TILE * TOP_K), lambda i: (jnp.minimum(i + 1, nt - 1), 0, 0),
                               memory_space=pltpu.SMEM),
                  pl.BlockSpec((DISPATCH_TILE, LANE), lambda i: (i, 0)),
                  pl.BlockSpec((DISPATCH_TILE, D_MODEL), lambda i: (i, 0)),
                  full((1, D_MODEL)), full((1, D_MODEL)),
                  pl.BlockSpec(memory_space=pl.ANY)],
        out_specs=pl.BlockSpec((DISPATCH_TILE, D_MODEL), lambda i: (i, 0)),
        out_shape=jax.ShapeDtypeStruct((t, D_MODEL), F32),
        scratch_shapes=[pltpu.VMEM((2, TOP_K, DISPATCH_TILE * TOK_SUB, LANE), F32), pltpu.SemaphoreType.DMA((2,))],
        compiler_params=_cp("arbitrary"),
        name="moe_combine_ln",
    )(dest3, dest3, gates, x, g, b, ys)


def _rank_body(e_ref, rank_ref, cnt_ref, run_ref):
    i = pl.program_id(0)

    @pl.when(i == 0)
    def _():
        run_ref[...] = jnp.zeros(run_ref.shape, F32)

    e = e_ref[...]
    lane = _iota(e.shape, 1)
    hot = [lane == e[:, k:k + 1] for k in range(TOP_K)]
    m = sum(h.astype(F32) for h in hot)
    ri = _iota((ROW_TILE, ROW_TILE), 0)
    ci = _iota((ROW_TILE, ROW_TILE), 1)
    before = _dot(_bf((ci < ri).astype(F32)), _bf(m)) + run_ref[0:1, :]
    rank = jnp.zeros(e.shape, F32)
    for k in range(TOP_K):
        rank = jnp.where(lane == k, jnp.sum(jnp.where(hot[k], before, 0.0), axis=-1, keepdims=True), rank)
    rank_ref[...] = rank.astype(I32)
    run_ref[0:1, :] = run_ref[0:1, :] + jnp.sum(m, axis=0, keepdims=True)
    cnt_ref[...] = run_ref[...].astype(I32)


def moe_rank(e_pad):
    t = e_pad.shape[0]
    return pl.pallas_call(
        _rank_body,
        grid=(t // ROW_TILE,),
        in_specs=[pl.BlockSpec((ROW_TILE, LANE), lambda i: (i, 0))],
        out_specs=[pl.BlockSpec((ROW_TILE, LANE), lambda i: (i, 0)), pl.BlockSpec((8, LANE), lambda i: (0, 0))],
        out_shape=[jax.ShapeDtypeStruct((t, LANE), I32), jax.ShapeDtypeStruct((8, LANE), I32)],
        scratch_shapes=[pltpu.VMEM((8, LANE), F32)],
        compiler_params=_cp("arbitrary"),
        name="moe_rank",
    )(e_pad)


def moe_plan(e_pad, n_tokens):
    tk = n_tokens * TOP_K
    rank_pad, cnt = moe_rank(e_pad)
    top_e, rank = e_pad[:, :TOP_K], rank_pad[:, :TOP_K]
    counts = cnt[0, :N_EXPERTS]
    ntile = (counts + MOE_TILE - 1) // MOE_TILE
    tile_end = jnp.cumsum(ntile)
    tile_start = tile_end - ntile
    start_of = jnp.sum(jnp.where(top_e[..., None] == jnp.arange(N_EXPERTS, dtype=I32), tile_start, 0), axis=-1)
    dest = (start_of * MOE_TILE + rank).reshape(tk)
    n_tiles = -(-tk // MOE_TILE) + N_EXPERTS
    tiles = jnp.arange(n_tiles, dtype=I32)
    tile_e = jnp.minimum(jnp.sum((tile_end[None, :] <= tiles[:, None]).astype(I32), axis=1), N_EXPERTS - 1)
    tile_valid = (tiles < tile_end[-1]).astype(I32)
    return dest.astype(I32), tile_e, tile_valid, n_tiles * MOE_TILE


def moe_ffn_ln(x1, x1t, e_pad, gates, wgu, bgu, wd, bd, g, b, expert0=0, xs_buf=None):
    t = x1.shape[0]
    dest, tile_e, tile_valid, n_rows = moe_plan(e_pad, t)
    xs = moe_dispatch(x1t, dest, n_rows, xs_buf)
    ys = moe_experts(xs, tile_e + expert0, tile_valid, wgu, bgu, wd, bd)
    return moe_combine_ln(x1, ys, dest, gates, g, b), xs


def _rope_tables(pos):
    half = ROT_DIM // 2
    inv_freq = ROPE_THETA ** (-jnp.arange(half, dtype=F32) / half)
    ang = pos.astype(F32)[:, None] * inv_freq[None, :]
    cos, sin = jnp.cos(ang), jnp.sin(ang)
    n = pos.shape[0]
    ones = jnp.ones((n, HEAD_DIM - ROT_DIM), F32)
    cos_h = jnp.concatenate([cos, cos, ones], axis=1)
    sin_h = jnp.concatenate([-sin, sin, 0.0 * ones], axis=1)
    return jnp.concatenate([cos_h, cos_h], axis=1), jnp.concatenate([sin_h, sin_h], axis=1)


def _conv_tail(h, row0, bsz, seq):
    assert seq >= GDN_CONV - 1
    n = GDN_CONV - 1
    if bsz <= 8:
        return jnp.stack([h[row0 + (b + 1) * seq - n:row0 + (b + 1) * seq, C_GQKV:C_GQKV + GDN_CONV_DIM]
                          for b in range(bsz)], axis=0)
    blk = h[row0:row0 + bsz * seq, C_GQKV:C_GQKV + GDN_CONV_DIM].reshape(bsz, seq, GDN_CONV_DIM)
    return blk[:, seq - n:, :]


def _lane_row(v, offset):
    return jnp.zeros((1, LANE), F32).at[0, offset:offset + v.shape[0]].set(v)


def kernel(x_prompt, x_sample, cache_swa_k, cache_swa_v, state_ssm_re, state_ssm_im, state_gdn_conv, state_gdn, state_gla, w_in, w_out, attn_sinks, ssm_a_re, ssm_a_im, ssm_b_re, ssm_b_im, ssm_c_re, ssm_c_im, ssm_d, ssm_log_dt, ssm_glu_w, ssm_glu_b, gdn_conv_w, gdn_a_log, gdn_dt_bias, gdn_norm_w, gla_gate_w, gla_gate_b, gla_norm_w, ln1_g, ln1_b, ln2_g, ln2_b, router_w, router_b, moe_w_gate_up, moe_b_gate_up, moe_w_down, moe_b_down):
    bp, lp, _ = x_prompt.shape
    bs, ls, _ = x_sample.shape
    n_p, n_s = bp * lp, bs * ls
    depth = w_in.shape[0]

    w_in_r = jnp.concatenate([w_in[..., :1792], w_in[..., 1800:2568], w_in[..., 1792:1800], w_in[..., 2568:N_IN],
                              jnp.zeros(w_in.shape[:2] + (NH - N_IN,), w_in.dtype)], axis=-1).astype(BF16)
    w_out_b = w_out.astype(BF16)
    glu_w_b = ssm_glu_w.astype(BF16)
    rw_pad = jnp.pad(router_w, ((0, 0), (0, 0), (0, LANE - N_EXPERTS)))
    rb_pad = jnp.pad(router_b, ((0, 0), (0, LANE - N_EXPERTS)))
    wgu_all = moe_w_gate_up.reshape(depth * N_EXPERTS, D_MODEL, 2 * D_FF)
    bgu_all = moe_b_gate_up.reshape(depth * N_EXPERTS, 1, 2 * D_FF)
    wd_all = moe_w_down.reshape(depth * N_EXPERTS, D_FF, D_MODEL)
    bd_all = moe_b_down.reshape(depth * N_EXPERTS, 1, D_MODEL)

    cos_p, sin_p = _rope_tables(jnp.arange(lp, dtype=I32))
    cos_s, sin_s = _rope_tables(PAST_LEN + jnp.arange(ls, dtype=I32))

    x = jnp.concatenate([x_prompt.reshape(n_p, D_MODEL), x_sample.reshape(n_s, D_MODEL)], axis=0)
    zeros = lambda *s: jnp.zeros(s, F32)
    new_p = [[] for _ in range(7)]
    new_s = [[] for _ in range(7)]
    xs_buf = None
    kc_all = cache_swa_k.reshape(depth * bs, WINDOW, 128)
    vc_all = cache_swa_v.reshape(depth * bs, WINDOW, 128)
    gdn_s_all = state_gdn.reshape(depth * bs, GDN_HEADS, GDN_DK, GDN_DV)
    for l in range(depth):
        h = in_proj(x, w_in_r[l])
        sinks = attn_sinks[l]
        oa_p, pk, pv = swa_prompt(h, sinks, cos_p, sin_p, bp, lp)
        oa_s, sk, sv = swa_sample(h, n_p, sinks, kc_all, vc_all, cos_s, sin_s, bs, ls, seq0=l * bs)
        bw, a_bar, cw, dsk = s5_params(ssm_a_re[l], ssm_a_im[l], ssm_b_re[l], ssm_b_im[l], ssm_c_re[l], ssm_c_im[l],
                                       ssm_d[l], ssm_log_dt[l])
        glu_b = ssm_glu_b[l].reshape(1, 2 * GROUP_WIDTH)
        ob_p, hl_p = s5_prompt(h, zeros(bp, 2 * SSM_W), bw, a_bar, cw, dsk, glu_w_b[l], glu_b, bp, lp)
        h0_s = jnp.concatenate([state_ssm_re[l].reshape(bs, SSM_W), state_ssm_im[l].reshape(bs, SSM_W)], axis=1)
        ob_s, hl_s = s5_sample(h, n_p, h0_s, bw, a_bar, cw, dsk, glu_w_b[l], glu_b, bs, ls)
        alog_t = _lane_row(gdn_a_log[l], SM_GA)
        dtb_t = _lane_row(gdn_dt_bias[l], SM_GA)
        gnw = gdn_norm_w[l].reshape(1, GDN_DV)
        oc_p, gs_p = gdn_mixer(h, 0, gdn_conv_w[l], alog_t, dtb_t, gnw, zeros(bp, GDN_CONV - 1, GDN_CONV_DIM),
                               zeros(bp, GDN_HEADS, GDN_DK, GDN_DV), bp, lp)
        oc_s, gs_s = gdn_mixer(h, n_p, gdn_conv_w[l], alog_t, dtb_t, gnw, state_gdn_conv[l], gdn_s_all, bs, ls,
                               seq0=l * bs)
        cv_p = _conv_tail(h, 0, bp, lp)
        cv_s = _conv_tail(h, n_p, bs, ls)
        lgb = gla_gate_b[l].reshape(1, 128)
        lnw = jnp.tile(gla_norm_w[l], GLA_HEADS).reshape(1, 256)
        od_p, lt_p = gla_mixer(h, 0, gla_gate_w[l], lgb, lnw, zeros(bp, 256, 128), bp, lp)
        od_s, lt_s = gla_mixer(h, n_p, gla_gate_w[l], lgb, lnw, gla_state_to_t(state_gla[l]), bs, ls)

        x1, x1t, e_pad, gates = out_proj_ln(
            x, (oa_p, ob_p.reshape(n_p, GROUP_WIDTH), oc_p, od_p), (oa_s, ob_s, oc_s, od_s), w_out_b[l],
            ln1_g[l].reshape(1, D_MODEL), ln1_b[l].reshape(1, D_MODEL), rw_pad[l], rb_pad[l].reshape(1, LANE))
        x, xs_buf = moe_ffn_ln(x1, x1t, e_pad, gates, wgu_all, bgu_all, wd_all, bd_all, ln2_g[l].reshape(1, D_MODEL),
                               ln2_b[l].reshape(1, D_MODEL), expert0=l * N_EXPERTS, xs_buf=xs_buf)

        st_p = (pk.reshape(bp, WINDOW, A_KV_HEADS, HEAD_DIM), pv.reshape(bp, WINDOW, A_KV_HEADS, HEAD_DIM),
                hl_p[:, :SSM_W].reshape(bp, SSM_GROUPS, SSM_STATE), hl_p[:, SSM_W:].reshape(bp, SSM_GROUPS, SSM_STATE),
                cv_p, gs_p, gla_state_from_t(lt_p))
        st_s = (sk.reshape(bs, WINDOW, A_KV_HEADS, HEAD_DIM), sv.reshape(bs, WINDOW, A_KV_HEADS, HEAD_DIM),
                hl_s[:, :SSM_W].reshape(bs, SSM_GROUPS, SSM_STATE), hl_s[:, SSM_W:].reshape(bs, SSM_GROUPS, SSM_STATE),
                cv_s, gs_s, gla_state_from_t(lt_s))
        for i in range(7):
            new_p[i].append(st_p[i])
            new_s[i].append(st_s[i])
    y_p = x[:n_p].reshape(bp, lp, D_MODEL)
    y_s = x[n_p:].reshape(bs, ls, D_MODEL)
    return (y_p, y_s) + tuple(jnp.stack(t, axis=0) for t in new_p) + tuple(jnp.stack(t, axis=0) for t in new_s)
```

```python
import functools

import numpy as np
import jax
import jax.numpy as jnp
from jax import lax
from jax.experimental import pallas as pl
from jax.experimental.pallas import tpu as pltpu

F32 = jnp.float32
BF16 = jnp.bfloat16
I32 = jnp.int32
HI = lax.Precision.HIGHEST

D_MODEL = 1024
DEPTH = 4
PAST_LEN = 8192
GROUP_WIDTH = 256
HEAD_DIM = 64
A_HEADS = 4
A_KV_HEADS = 2
WINDOW = 128
ROPE_THETA = 500000.0
ROT_DIM = 16
SSM_GC = 16
SSM_GROUPS = 16
SSM_STATE = 64
SSM_W = SSM_GROUPS * SSM_STATE
GDN_HEADS = 4
GDN_DK = 64
GDN_DV = 64
GDN_CONV = 4
GDN_CONV_DIM = 768
GDN_CHUNK = 64
GLA_HEADS = 4
GLA_DK = 32
GLA_DV = 64
GLA_RANK = 16
GLA_TAU = 16.0
GLA_CHUNK = 16
N_EXPERTS = 32
TOP_K = 4
D_FF = 1024
SWIGLU_LIMIT = 7.0
SWIGLU_ALPHA = 1.702
DN_ALPHA = (2 * DEPTH) ** 0.25
LN_EPS = 1e-5
RMS_EPS = 1e-6
N_IN = 2584

C_AQ, C_AK, C_AV, C_SU, C_GQKV, C_GZ = 0, 256, 384, 512, 768, 1536
C_LQ, C_LK, C_LV, C_LR, C_SM = 1792, 1920, 2048, 2304, 2560
NH = 2688
SM_GB, SM_GA, SM_LG = 0, 4, 8

LANE = 128
ROW_TILE = 512
MOE_TILE = 512
VMEM_LIMIT = 56 * 1024 * 1024


def _cp(*sem):
    return pltpu.CompilerParams(dimension_semantics=sem, vmem_limit_bytes=VMEM_LIMIT)


def _dot(a, b, precision=None):
    return jnp.dot(a, b, preferred_element_type=F32, precision=precision)


def _dot_nt(a, b, precision=None):
    return lax.dot_general(a, b, (((1,), (1,)), ((), ())), preferred_element_type=F32, precision=precision)


def _dot_tn(a, b, precision=None):
    return lax.dot_general(a, b, (((0,), (0,)), ((), ())), preferred_element_type=F32, precision=precision)


def _bf(x):
    return x.astype(BF16)


def _iota(shape, dim):
    return lax.broadcasted_iota(I32, shape, dim)


def _shr(idx, size):
    return lax.shift_right_logical(idx, int(size).bit_length() - 1)


def _sigmoid(x):
    return 1.0 / (1.0 + jnp.exp(-x))


def _silu(x):
    return x * _sigmoid(x)


def _softplus(x):
    return jnp.maximum(x, 0.0) + jnp.log(1.0 + jnp.exp(-jnp.abs(x)))


def _log_sigmoid(x):
    return -_softplus(-x)


def _gelu_tanh(x):
    return 0.5 * x * (1.0 + jnp.tanh(0.7978845608028654 * (x + 0.044715 * x * x * x)))


def _layer_norm(y, g, b):
    mu = jnp.mean(y, axis=-1, keepdims=True)
    yc = y - mu
    var = jnp.mean(yc * yc, axis=-1, keepdims=True)
    return yc * lax.rsqrt(var + LN_EPS) * g + b


def _in_proj_body(x_ref, w_ref, o_ref):
    xb = _bf(x_ref[...])
    for c0 in range(0, NH, 512):
        c1 = min(c0 + 512, NH)
        o_ref[:, c0:c1] = _dot(xb, w_ref[:, c0:c1])


def in_proj(x, w):
    t = x.shape[0]
    return pl.pallas_call(
        _in_proj_body,
        grid=(t // ROW_TILE,),
        in_specs=[pl.BlockSpec((ROW_TILE, D_MODEL), lambda i: (i, 0)),
                  pl.BlockSpec((D_MODEL, NH), lambda i: (0, 0))],
        out_specs=pl.BlockSpec((ROW_TILE, NH), lambda i: (i, 0)),
        out_shape=jax.ShapeDtypeStruct((t, NH), F32),
        compiler_params=_cp("arbitrary"),
        name="in_proj",
    )(x, w)


def _rope(x, cos, sin):
    w = x.shape[1]
    if w > LANE:
        cos = jnp.concatenate([cos] * (w // LANE), axis=1)
        sin = jnp.concatenate([sin] * (w // LANE), axis=1)
    lane = _iota(x.shape, 1) & (HEAD_DIM - 1)
    swapped = jnp.where(lane < ROT_DIM // 2, pltpu.roll(x, w - ROT_DIM // 2, 1), pltpu.roll(x, ROT_DIM // 2, 1))
    return x * cos + swapped * sin


def _sink_attention_multi(qkvs, mask, sinks_ref):
    pairs = [(i, hq) for i in range(len(qkvs)) for hq in range(A_HEADS)]
    hd = lambda x, j: _bf(x[:, j * HEAD_DIM:(j + 1) * HEAD_DIM])
    kv = lambda hq: hq // (A_HEADS // A_KV_HEADS)
    scores = {(i, hq): _dot_nt(hd(qkvs[i][0], hq), hd(qkvs[i][1], kv(hq))) for i, hq in pairs}
    probs = {}
    masks = mask if isinstance(mask, (list, tuple)) else [mask] * len(qkvs)
    for i, hq in pairs:
        s = jnp.where(masks[i], scores[i, hq] * (HEAD_DIM ** -0.5), -jnp.inf)
        sink = sinks_ref[hq]
        m = jnp.maximum(jnp.max(s, axis=-1, keepdims=True), sink)
        p = jnp.exp(s - m)
        den = jnp.sum(p, axis=-1, keepdims=True) + jnp.exp(sink - m)
        probs[i, hq] = _bf(p / den)
    outs = {(i, hq): _dot(probs[i, hq], hd(qkvs[i][2], kv(hq))) for i, hq in pairs}
    return [jnp.concatenate([outs[i, hq] for hq in range(A_HEADS)], axis=1) for i in range(len(qkvs))]


def _sink_attention(q, kk, vv, mask, sinks_ref):
    return _sink_attention_multi([(q, kk, vv)], mask, sinks_ref)[0]


SWA_QB = 4


def _swa_prompt_body(sinks_ref, cur_ref, prev_ref, cos_ref, sin_ref, cosp_ref, sinp_ref, o_ref, ko_ref, vo_ref):
    i = pl.program_id(1)
    cur = cur_ref[...]
    q = _rope(cur[:, C_AQ:C_AQ + 256], cos_ref[...], sin_ref[...])
    k = _rope(cur[:, C_AK:C_AK + 128], cos_ref[...], sin_ref[...])
    v = cur[:, C_AV:C_AV + 128]
    prev = prev_ref[...]
    kp = _rope(prev[:, 0:128], cosp_ref[...], sinp_ref[...])
    vp = prev[:, 128:256]
    kall = jnp.concatenate([kp, k], axis=0)
    vall = jnp.concatenate([vp, v], axis=0)
    r = _iota((WINDOW, 2 * WINDOW), 0)
    j = _iota((WINDOW, 2 * WINDOW), 1)
    d = WINDOW + r - j
    band = (d >= 0) & (d <= WINDOW)
    first = band & ((j >= WINDOW) | (i > 0))
    qkvs = [(q[m * WINDOW:(m + 1) * WINDOW, :], kall[m * WINDOW:(m + 2) * WINDOW, :], vall[m * WINDOW:(m + 2) * WINDOW, :])
            for m in range(SWA_QB)]
    outs = _sink_attention_multi(qkvs, [first] + [band] * (SWA_QB - 1), sinks_ref)
    o_ref[...] = jnp.concatenate(outs, axis=0)
    ko_ref[0] = k[(SWA_QB - 1) * WINDOW:, :]
    vo_ref[0] = v[(SWA_QB - 1) * WINDOW:, :]


def swa_prompt(h, sinks, cos_t, sin_t, bsz, seq):
    nb = seq // WINDOW
    smem = pl.BlockSpec(memory_space=pltpu.SMEM)
    ns = nb // SWA_QB
    rows = SWA_QB * WINDOW
    cur_tab = pl.BlockSpec((rows, LANE), lambda b, i: (i, 0))
    prev_tab = pl.BlockSpec((WINDOW, LANE), lambda b, i: (jnp.maximum(i * SWA_QB - 1, 0), 0))
    return pl.pallas_call(
        _swa_prompt_body,
        grid=(bsz, ns),
        in_specs=[smem,
                  pl.BlockSpec((rows, 512), lambda b, i: (b * ns + i, 0)),
                  pl.BlockSpec((WINDOW, 256), lambda b, i: (b * nb + jnp.maximum(i * SWA_QB - 1, 0), 1)),
                  cur_tab, cur_tab, prev_tab, prev_tab],
        out_specs=[pl.BlockSpec((rows, 256), lambda b, i: (b * ns + i, 0)),
                   pl.BlockSpec((1, WINDOW, 128), lambda b, i: (b, 0, 0)),
                   pl.BlockSpec((1, WINDOW, 128), lambda b, i: (b, 0, 0))],
        out_shape=[jax.ShapeDtypeStruct((bsz * seq, 256), F32),
                   jax.ShapeDtypeStruct((bsz, WINDOW, 128), F32),
                   jax.ShapeDtypeStruct((bsz, WINDOW, 128), F32)],
        compiler_params=_cp("arbitrary", "arbitrary"),
        name="swa_prompt",
    )(sinks, h, h, cos_t, sin_t, cos_t, sin_t)


SWA_SB = 8


def _swa_sample_body(sinks_ref, cur_ref, kc_ref, vc_ref, cos_ref, sin_ref, o_ref, ko_ref, vo_ref, *, ls):
    cw = WINDOW
    r = _iota((ls, cw + ls), 0)
    j = _iota((ls, cw + ls), 1)
    d = cw + r - j
    mask = (d >= 0) & (d <= WINDOW)
    qkvs = []
    for b in range(SWA_SB):
        cur = cur_ref[b * ls:(b + 1) * ls, :]
        q = _rope(cur[:, C_AQ:C_AQ + 256], cos_ref[...], sin_ref[...])
        k = _rope(cur[:, C_AK:C_AK + 128], cos_ref[...], sin_ref[...])
        v = cur[:, C_AV:C_AV + 128]
        kk = jnp.concatenate([kc_ref[b], k], axis=0)
        vv = jnp.concatenate([vc_ref[b], v], axis=0)
        ko_ref[b] = kk[ls:, :]
        vo_ref[b] = vv[ls:, :]
        qkvs.append((q, kk, vv))
    for b, o in enumerate(_sink_attention_multi(qkvs, mask, sinks_ref)):
        o_ref[b * ls:(b + 1) * ls, :] = o


def swa_sample(h, row0, sinks, k_cache, v_cache, cos_t, sin_t, bsz, ls, seq0=0):
    rows = SWA_SB * ls
    blk0 = row0 // rows
    smem = pl.BlockSpec(memory_space=pltpu.SMEM)
    cache = pl.BlockSpec((SWA_SB, WINDOW, 128), lambda i: (i, 0, 0))
    cache_in = pl.BlockSpec((SWA_SB, WINDOW, 128), lambda i: (seq0 // SWA_SB + i, 0, 0))
    tab = pl.BlockSpec((ls, LANE), lambda i: (0, 0))
    return pl.pallas_call(
        functools.partial(_swa_sample_body, ls=ls),
        grid=(bsz // SWA_SB,),
        in_specs=[smem, pl.BlockSpec((rows, 512), lambda i: (blk0 + i, 0)), cache_in, cache_in, tab, tab],
        out_specs=[pl.BlockSpec((rows, 256), lambda i: (i, 0)), cache, cache],
        out_shape=[jax.ShapeDtypeStruct((bsz * ls, 256), F32),
                   jax.ShapeDtypeStruct((bsz, WINDOW, 128), F32),
                   jax.ShapeDtypeStruct((bsz, WINDOW, 128), F32)],
        compiler_params=_cp("arbitrary"),
        name="swa_sample",
    )(sinks, h, k_cache, v_cache, cos_t, sin_t)


S5_NB = 2 * SSM_W // LANE


def _s5_input(u, bw_ref):
    t = _dot(_bf(u), bw_ref[...])
    return jnp.concatenate([t[:, :SSM_W], t[:, :SSM_W] + t[:, SSM_W:]], axis=1)


def _s5_output(hs, u, cw_ref, d_ref, gw_ref, gb_ref):
    hs = jnp.concatenate([hs[:, :SSM_W] + hs[:, SSM_W:], hs[:, SSM_W:]], axis=1)
    y = _dot(_bf(hs), cw_ref[...]) + d_ref[...] * u
    y = _gelu_tanh(y)
    z = _dot(_bf(y), gw_ref[...]) + gb_ref[...]
    return z[:, :GROUP_WIDTH] * _sigmoid(z[:, GROUP_WIDTH:])


def _s5_scan(s_ref, a_ref, h_init, n_steps, rows):
    nre = S5_NB // 2
    a_re = [jnp.broadcast_to(a_ref[0:1, j * LANE:(j + 1) * LANE], (rows, LANE)) for j in range(nre)]
    a_im = [jnp.broadcast_to(a_ref[1:2, j * LANE:(j + 1) * LANE], (rows, LANE)) for j in range(nre)]

    def step(t, hcar):
        out = [None] * S5_NB
        base = pl.multiple_of(t * rows, rows)
        for j in range(nre):
            hr, hi = hcar[j], hcar[j + nre]
            nr = a_re[j] * hr - a_im[j] * hi + s_ref[j, pl.ds(base, rows), :]
            ni = a_re[j] * hi + a_im[j] * hr + s_ref[j + nre, pl.ds(base, rows), :]
            s_ref[j, pl.ds(base, rows), :] = nr
            s_ref[j + nre, pl.ds(base, rows), :] = ni
            out[j], out[j + nre] = nr, ni
        return tuple(out)

    return lax.fori_loop(0, n_steps, step, tuple(h_init))


def _s5_prompt_body(*refs, nb, tl):
    u_refs = refs[:nb]
    bw_ref, a_ref, cw_ref, d_ref, gw_ref, gb_ref, h0_ref, o_ref, hl_ref, s_ref, hst_ref = refs[nb:]
    i = pl.program_id(0)

    @pl.when(i == 0)
    def _():
        hst_ref[...] = h0_ref[...]

    u_all = jnp.concatenate([u_refs[b][...] for b in range(nb)], axis=0)
    bu = _s5_input(u_all, bw_ref)
    for b in range(nb):
        for j in range(S5_NB):
            s_ref[j, pl.ds(b, tl, stride=nb), :] = bu[b * tl:(b + 1) * tl, j * LANE:(j + 1) * LANE]
    h_init = [hst_ref[:, j * LANE:(j + 1) * LANE] for j in range(S5_NB)]
    h_fin = _s5_scan(s_ref, a_ref, h_init, tl, nb)
    for j in range(S5_NB):
        hst_ref[:, j * LANE:(j + 1) * LANE] = h_fin[j]
    hs = jnp.concatenate([jnp.concatenate([s_ref[j, pl.ds(b, tl, stride=nb), :] for j in range(S5_NB)], axis=1)
                          for b in range(nb)], axis=0)
    o_all = _s5_output(hs, u_all, cw_ref, d_ref, gw_ref, gb_ref)
    for b in range(nb):
        o_ref[b] = o_all[b * tl:(b + 1) * tl, :]

    @pl.when(i == pl.num_programs(0) - 1)
    def _():
        hl_ref[...] = hst_ref[...]


def s5_prompt(h, h0, bw, a, cw, dsk, gw, gb, bsz, seq, tl=128):
    assert bsz == 8
    nt = seq // tl
    full = lambda shape: pl.BlockSpec(shape, lambda i: (0,) * len(shape))
    u_specs = [pl.BlockSpec((tl, GROUP_WIDTH), functools.partial(lambda i, b: (b * nt + i, C_SU // GROUP_WIDTH), b=b))
               for b in range(bsz)]
    return pl.pallas_call(
        functools.partial(_s5_prompt_body, nb=bsz, tl=tl),
        grid=(nt,),
        in_specs=u_specs + [full((GROUP_WIDTH, 2 * SSM_W)), full((2, SSM_W)), full((2 * SSM_W, GROUP_WIDTH)),
                            full((1, GROUP_WIDTH)), full((GROUP_WIDTH, 2 * GROUP_WIDTH)), full((1, 2 * GROUP_WIDTH)),
                            full((bsz, 2 * SSM_W))],
        out_specs=[pl.BlockSpec((bsz, tl, GROUP_WIDTH), lambda i: (0, i, 0)), full((bsz, 2 * SSM_W))],
        out_shape=[jax.ShapeDtypeStruct((bsz, seq, GROUP_WIDTH), F32), jax.ShapeDtypeStruct((bsz, 2 * SSM_W), F32)],
        scratch_shapes=[pltpu.VMEM((S5_NB, tl * bsz, LANE), F32), pltpu.VMEM((bsz, 2 * SSM_W), F32)],
        compiler_params=_cp("arbitrary"),
        name="s5_prompt",
    )(*([h] * bsz), bw, a, cw, dsk, gw, gb, h0)


def _s5_sample_body(u_ref, bw_ref, a_ref, cw_ref, d_ref, gw_ref, gb_ref, h0_ref, o_ref, hl_ref, s_ref, t_ref, *, bsz, ls):
    bu = _s5_input(u_ref[...], bw_ref)
    for j in range(S5_NB):
        s_ref[j] = bu[:, j * LANE:(j + 1) * LANE]
    for t in range(ls):
        for j in range(S5_NB):
            t_ref[j, pl.ds(t * bsz, bsz), :] = s_ref[j, pl.ds(t, bsz, stride=ls), :]
    h_init = [h0_ref[:, j * LANE:(j + 1) * LANE] for j in range(S5_NB)]
    h_fin = _s5_scan(t_ref, a_ref, h_init, ls, bsz)
    for j in range(S5_NB):
        hl_ref[:, j * LANE:(j + 1) * LANE] = h_fin[j]
    for t in range(ls):
        for j in range(S5_NB):
            s_ref[j, pl.ds(t, bsz, stride=ls), :] = t_ref[j, pl.ds(t * bsz, bsz), :]
    hs = jnp.concatenate([s_ref[j] for j in range(S5_NB)], axis=1)
    o_ref[...] = _s5_output(hs, u_ref[...], cw_ref, d_ref, gw_ref, gb_ref)


def s5_sample(h, row0, h0, bw, a, cw, dsk, gw, gb, bsz, ls):
    rows = bsz * ls
    full = lambda shape: pl.BlockSpec(shape, lambda i: (0,) * len(shape))
    return pl.pallas_call(
        functools.partial(_s5_sample_body, bsz=bsz, ls=ls),
        grid=(1,),
        in_specs=[pl.BlockSpec((rows, GROUP_WIDTH), lambda i: (row0 // rows, C_SU // GROUP_WIDTH)),
                  full((GROUP_WIDTH, 2 * SSM_W)), full((2, SSM_W)), full((2 * SSM_W, GROUP_WIDTH)),
                  full((1, GROUP_WIDTH)), full((GROUP_WIDTH, 2 * GROUP_WIDTH)), full((1, 2 * GROUP_WIDTH)),
                  full((bsz, 2 * SSM_W))],
        out_specs=[full((rows, GROUP_WIDTH)), full((bsz, 2 * SSM_W))],
        out_shape=[jax.ShapeDtypeStruct((rows, GROUP_WIDTH), F32), jax.ShapeDtypeStruct((bsz, 2 * SSM_W), F32)],
        scratch_shapes=[pltpu.VMEM((S5_NB, rows, LANE), F32), pltpu.VMEM((S5_NB, rows, LANE), F32)],
        compiler_params=_cp("arbitrary"),
        name="s5_sample",
    )(h, bw, a, cw, dsk, gw, gb, h0)


def s5_params(a_re, a_im, b_re, b_im, c_re, c_im, d_skip, log_dt):
    lam = lax.complex(a_re, a_im)
    delta = jnp.exp(log_dt)[:, None]
    a_bar = jnp.exp(lam * delta)
    b_bar = ((a_bar - 1.0) / lam)[..., None] * lax.complex(b_re, b_im)
    eye = jnp.eye(SSM_GROUPS, dtype=F32)
    bw_re = jnp.einsum('gpc,gh->gchp', b_bar.real, eye).reshape(GROUP_WIDTH, SSM_W)
    bw_im = jnp.einsum('gpc,gh->gchp', b_bar.imag, eye).reshape(GROUP_WIDTH, SSM_W)
    bw = jnp.concatenate([bw_re, bw_im - bw_re], axis=1).astype(BF16)
    cw_re = jnp.einsum('gcp,gh->gphc', c_re, eye).reshape(SSM_W, GROUP_WIDTH)
    cw_im = jnp.einsum('gcp,gh->gphc', c_im, eye).reshape(SSM_W, GROUP_WIDTH)
    cw = jnp.concatenate([cw_re, -(cw_re + cw_im)], axis=0).astype(BF16)
    a = jnp.stack([a_bar.real.reshape(SSM_W), a_bar.imag.reshape(SSM_W)], axis=0)
    return bw, a, cw, d_skip.reshape(1, GROUP_WIDTH)


GDN_RT = 256
GDN_ST = 64
GDN_SG = 8


def _split(x):
    hi = _bf(x)
    return hi, _bf(x - hi.astype(F32))


def _dot3(a, b):
    return _dot(a[0], b[0]) + (_dot(a[0], b[1]) + _dot(a[1], b[0]))


def _gdn_prep_body(*refs, c, sample, tiles_per_seq):
    if sample:
        x_ref, cs_ref, z_ref, sm_ref, cw_ref, alog_ref, dtb_ref = refs[:7]
        outs = refs[7:14]
        xs_ref, cb_ref = refs[14:]
    else:
        x_ref, prev_ref, cs_ref, z_ref, sm_ref, cw_ref, alog_ref, dtb_ref = refs[:8]
        outs = refs[8:15]
        (xs_ref,) = refs[15:]
    u_ref, w_ref, qk_ref, qg_ref, kd_ref, eg_ref, zs_ref = outs
    rt = GDN_RT
    x = x_ref[...]
    xs_ref[8:8 + rt, :] = x
    if sample:
        xs_ref[0:8, :] = jnp.zeros((8, GDN_CONV_DIM), F32)
        cb_ref[0:rt, :] = cs_ref[...]
        cb_ref[rt:rt + 8, :] = jnp.zeros((8, GDN_CONV_DIM), F32)
        pos = _iota((rt, GDN_CONV_DIM), 0) & (c - 1)
        shifted = lambda i: jnp.where(pos >= i, xs_ref[8 - i:8 - i + rt, :], cb_ref[8 - i:8 - i + rt, :])
    else:
        first = (pl.program_id(0) % tiles_per_seq) == 0
        xs_ref[0:8, :] = jnp.where(first, cs_ref[...], prev_ref[...])
        shifted = lambda i: xs_ref[8 - i:8 - i + rt, :]
    conv = shifted(3) * cw_ref[0:1, :]
    conv = conv + shifted(2) * cw_ref[1:2, :]
    conv = conv + shifted(1) * cw_ref[2:3, :]
    conv = conv + x * cw_ref[3:4, :]
    qkv = _silu(conv)
    zs_ref[...] = _silu(z_ref[...])

    sm = sm_ref[...]
    beta_all = _sigmoid(sm)
    g_all = -jnp.exp(alog_ref[...]) * _softplus(sm + dtb_ref[...])
    st = GDN_ST
    ri = _iota((st, st), 0)
    ci = _iota((st, st), 1)
    same = _shr(ri, c) == _shr(ci, c)
    low = (ci <= ri) & same
    tri = low.astype(F32)
    last = (ci == (ri | (c - 1))).astype(F32)
    eye = (ri == ci).astype(F32)
    lane = _iota((st, LANE), 1)
    nsub = rt // st
    probs = [(t, hh) for t in range(nsub) for hh in range(GDN_HEADS)]
    gc_alls, gl_alls = [], []
    for t in range(nsub):
        rows = slice(t * st, (t + 1) * st)
        gc_alls.append(_dot_nt_exact_lhs(_bf(tri), g_all[rows, :]))
    for t in range(nsub):
        gl_alls.append(_dot_nt_exact_lhs(_bf(last), gc_alls[t]))
        eg_ref[t * st:(t + 1) * st, :] = jnp.exp(gl_alls[t])
    qs, ks, rhss, decays, a_stricts, invs = {}, {}, {}, {}, {}, {}
    for t, hh in probs:
        rows = slice(t * st, (t + 1) * st)
        q = qkv[rows, hh * GDN_DK:(hh + 1) * GDN_DK]
        k = qkv[rows, 256 + hh * GDN_DK:256 + (hh + 1) * GDN_DK]
        v = qkv[rows, 512 + hh * GDN_DV:512 + (hh + 1) * GDN_DV]
        q = q * lax.rsqrt(jnp.sum(q * q, axis=-1, keepdims=True) + 1e-6) * (GDN_DK ** -0.5)
        k = k * lax.rsqrt(jnp.sum(k * k, axis=-1, keepdims=True) + 1e-6)
        beta = beta_all[rows, SM_GB + hh:SM_GB + hh + 1]
        gc = gc_alls[t][:, SM_GA + hh:SM_GA + hh + 1]
        gl = gl_alls[t][:, SM_GA + hh:SM_GA + hh + 1]
        sel = (lane == SM_GA + hh).astype(F32)
        gc_row = _dot_nt(sel, gc_alls[t], HI)
        decays[t, hh] = jnp.exp(jnp.where(low, gc - gc_row, -jnp.inf))
        kb = k * beta
        eg = jnp.exp(gc)
        qs[t, hh], ks[t, hh] = q, k
        rhss[t, hh] = _split(jnp.concatenate([v * beta, kb * eg], axis=1))
        a_stricts[t, hh] = jnp.where(ci < ri, _dot_nt(_bf(kb), _bf(k)) * decays[t, hh], 0.0)
        qg_ref[rows, hh * GDN_DK:(hh + 1) * GDN_DK] = q * eg
        kd_ref[rows, hh * GDN_DK:(hh + 1) * GDN_DK] = k * jnp.exp(gl - gc)
    for t, hh in probs:
        qk_ref[t * st:(t + 1) * st, hh * st:(hh + 1) * st] = _dot_nt(_bf(qs[t, hh]), _bf(ks[t, hh])) * decays[t, hh]
    for p in probs:
        invs[p] = eye - jnp.where(_shr(ri, 2) == _shr(ci, 2), a_stricts[p], 0.0)
    s = 2
    while s < c:
        pair = (_shr(ri, 2 * s) == _shr(ci, 2 * s)) & (_shr(ri, s) != _shr(ci, s))
        inv_s = {p: _split(invs[p]) for p in probs}
        mid = {p: _dot3(_split(jnp.where(pair, a_stricts[p], 0.0)), inv_s[p]) for p in probs}
        for p in probs:
            invs[p] = invs[p] - _dot3(inv_s[p], _split(mid[p]))
        s *= 2
    for t, hh in probs:
        sol = _dot3(_split(invs[t, hh]), rhss[t, hh])
        u_ref[t * st:(t + 1) * st, hh * GDN_DV:(hh + 1) * GDN_DV] = sol[:, :GDN_DV]
        w_ref[t * st:(t + 1) * st, hh * GDN_DV:(hh + 1) * GDN_DV] = sol[:, GDN_DV:]


def _gdn_scan_body(u_ref, w_ref, qk_ref, qg_ref, kd_ref, eg_ref, zs_ref, nw_ref, s0_ref, o_ref, so_ref, st_ref, *, c, sg):
    n = pl.program_id(1)

    @pl.when(n == 0)
    def _():
        st_ref[...] = s0_ref[...]

    probs = [(s, hh) for s in range(sg) for hh in range(GDN_HEADS)]
    hsl = lambda hh: slice(hh * GDN_DV, (hh + 1) * GDN_DV)
    state_b = {(s, hh): _bf(st_ref[s, hh]) for s, hh in probs}
    v_b = {}
    for s, hh in probs:
        v_b[s, hh] = _bf(u_ref[s, :, hsl(hh)] - _dot(_bf(w_ref[s, :, hsl(hh)]), state_b[s, hh]))
    o_inter = {(s, hh): _dot(_bf(qg_ref[s, :, hsl(hh)]), state_b[s, hh]) for s, hh in probs}
    for s, hh in probs:
        col0 = (s * c) % GDN_ST
        qk = qk_ref[s, :, hh * GDN_ST + col0:hh * GDN_ST + col0 + c]
        o = o_inter[s, hh] + _dot(_bf(qk), v_b[s, hh])
        o = o * lax.rsqrt(jnp.mean(o * o, axis=-1, keepdims=True) + RMS_EPS) * nw_ref[...]
        o_ref[s, :, hsl(hh)] = o * zs_ref[s, :, hsl(hh)]
    for s, hh in probs:
        eg_last = eg_ref[s, c - 1:c, SM_GA + hh:SM_GA + hh + 1]
        st_ref[s, hh] = st_ref[s, hh] * eg_last + _dot_tn(_bf(kd_ref[s, :, hsl(hh)]), v_b[s, hh])

    @pl.when(n == pl.num_programs(1) - 1)
    def _():
        so_ref[...] = st_ref[...]


def gdn_mixer(h, row0, conv_w, alog_t, dtb_t, norm_w, conv_state, s0, bsz, seq, seq0=0):
    c = min(GDN_CHUNK, seq)
    sample = seq == c
    rows = bsz * seq
    rt = GDN_RT
    nt = rows // rt
    blk0 = row0 // rt
    tps = max(seq // rt, 1)
    cs_rows = jnp.pad(conv_state, ((0, 0), (8 - (GDN_CONV - 1), 0), (0, 0))).reshape(bsz * 8, GDN_CONV_DIM)
    row = lambda w, col: pl.BlockSpec((rt, w), lambda i: (blk0 + i, col))
    full = lambda shape: pl.BlockSpec(shape, lambda i: (0,) * len(shape))
    if sample:
        aux_specs = [pl.BlockSpec((rt, GDN_CONV_DIM), lambda i: (i, 0))]
        aux = [cs_rows]
        scratch = [pltpu.VMEM((8 + rt, GDN_CONV_DIM), F32), pltpu.VMEM((8 + rt, GDN_CONV_DIM), F32)]
    else:
        aux_specs = [pl.BlockSpec((8, GDN_CONV_DIM), lambda i: (jnp.maximum((row0 + i * rt) // 8 - 1, 0), C_GQKV // GDN_CONV_DIM)),
                     pl.BlockSpec((8, GDN_CONV_DIM), lambda i: (i // tps, 0))]
        aux = [h, cs_rows]
        scratch = [pltpu.VMEM((8 + rt, GDN_CONV_DIM), F32)]
    out_w = [GROUP_WIDTH] * 5 + [LANE, GROUP_WIDTH]
    prep = pl.pallas_call(
        functools.partial(_gdn_prep_body, c=c, sample=sample, tiles_per_seq=tps),
        grid=(nt,),
        in_specs=[row(GDN_CONV_DIM, C_GQKV // GDN_CONV_DIM)] + aux_specs
                 + [row(GROUP_WIDTH, C_GZ // GROUP_WIDTH), row(LANE, C_SM // LANE),
                    full((GDN_CONV, GDN_CONV_DIM)), full((1, LANE)), full((1, LANE))],
        out_specs=[pl.BlockSpec((rt, w), lambda i: (i, 0)) for w in out_w],
        out_shape=[jax.ShapeDtypeStruct((rows, w), F32) for w in out_w],
        scratch_shapes=scratch,
        compiler_params=_cp("arbitrary"),
        name="gdn_prep",
    )(h, *aux, h, h, conv_w, alog_t, dtb_t)
    prep = [a.reshape(bsz, seq, a.shape[-1]) for a in prep]
    sg = GDN_SG
    ng, nc = bsz // sg, seq // c
    seq_blk = lambda w: pl.BlockSpec((sg, c, w), lambda g, n: (g, n, 0))
    st_spec = pl.BlockSpec((sg, GDN_HEADS, GDN_DK, GDN_DV), lambda g, n: (g, 0, 0, 0))
    s0_spec = pl.BlockSpec((sg, GDN_HEADS, GDN_DK, GDN_DV), lambda g, n: (seq0 // sg + g, 0, 0, 0))
    o, s_fin = pl.pallas_call(
        functools.partial(_gdn_scan_body, c=c, sg=sg),
        grid=(ng, nc),
        in_specs=[seq_blk(w) for w in out_w] + [pl.BlockSpec((1, GDN_DV), lambda g, n: (0, 0)), s0_spec],
        out_specs=[seq_blk(GROUP_WIDTH), st_spec],
        out_shape=[jax.ShapeDtypeStruct((bsz, seq, GROUP_WIDTH), F32),
                   jax.ShapeDtypeStruct((bsz, GDN_HEADS, GDN_DK, GDN_DV), F32)],
        scratch_shapes=[pltpu.VMEM((sg, GDN_HEADS, GDN_DK, GDN_DV), F32)],
        compiler_params=_cp("arbitrary", "arbitrary"),
        name="gdn_scan",
    )(*prep, norm_w, s0)
    return o.reshape(rows, GROUP_WIDTH), s_fin


GLA_RT = 256


def _dot_exact(x, w):
    hi = _bf(x)
    r1 = x - hi.astype(F32)
    mid = _bf(r1)
    lo = _bf(r1 - mid.astype(F32))
    return _dot(hi, w) + (_dot(mid, w) + _dot(lo, w))


def _gla_body(qk_ref, vr_ref, sm_ref, gw_ref, gb_ref, nw_ref, s0_ref, o_ref, so_ref,
              kb_ref, bb_ref, vb_ref, st_ref, *, c, seq):
    tr = GLA_RT
    n = pl.program_id(1)
    pad = GLA_CHUNK
    carried = seq >= tr

    @pl.when(n == 0)
    def _():
        kb_ref[0:pad, :] = jnp.zeros((pad, 128), F32)
        bb_ref[0:pad, :] = jnp.zeros((pad, 128), F32)
        vb_ref[0:pad, :] = jnp.zeros((pad, 256), F32)
        if carried:
            st_ref[...] = s0_ref[0]

    q = qk_ref[:, 0:128] * (GLA_DK ** -0.5)
    k = qk_ref[:, 128:256]
    v = vr_ref[:, 0:256]
    lg = sm_ref[:, SM_LG:SM_LG + GLA_RANK]
    log_a = _log_sigmoid(_dot(_bf(lg), _bf(gw_ref[...])) + gb_ref[...]) / GLA_TAU
    ri = _iota((tr, tr), 0)
    ci = _iota((tr, tr), 1)
    tri = _bf(((ci <= ri) & (_shr(ri, c) == _shr(ci, c))).astype(F32))
    bc = _dot_nt_exact_lhs(tri, log_a)
    kb_ref[pad:pad + tr, :] = k
    bb_ref[pad:pad + tr, :] = bc
    vb_ref[pad:pad + tr, :] = v
    ones_kv = _bf((_shr(_iota((128, 256), 0), GLA_DK) == _shr(_iota((128, 256), 1), GLA_DV)).astype(F32))
    pos = _iota((tr, 128), 0) & (c - 1)
    ps = []
    for dl in range(c):
        ks = kb_ref[pad - dl:pad - dl + tr, :]
        bs = bb_ref[pad - dl:pad - dl + tr, :]
        ps.append(jnp.where(pos >= dl, q * ks * jnp.exp(bc - bs), 0.0))
    att = _dot_exact(jnp.concatenate(ps, axis=0), ones_kv)
    o_intra = jnp.zeros((tr, 256), F32)
    for dl in range(c):
        vs = vb_ref[pad - dl:pad - dl + tr, :]
        o_intra = o_intra + _bf(att[dl * tr:(dl + 1) * tr, :]).astype(F32) * _bf(vs).astype(F32)
    bd = (_shr(_iota((256, 128), 0), GLA_DV) == _shr(_iota((256, 128), 1), GLA_DK)).astype(F32)
    nch = tr // c
    sls = [slice(ch * c, (ch + 1) * c) for ch in range(nch)]
    b_last = [bc[sl, :][c - 1:c, :] for sl in sls]
    incs = [bd * _dot_tn(_bf(v[sl, :]), _bf(k[sl, :] * jnp.exp(bl - bc[sl, :]))) for sl, bl in zip(sls, b_last)]
    outs = []
    st = st_ref[...] if carried else None
    for ch, sl in enumerate(sls):
        if not carried:
            st = s0_ref[(ch * c) // seq]
        outs.append(_dot_nt(_bf(q[sl, :] * jnp.exp(bc[sl, :])), _bf(st)) + o_intra[sl, :])
        st = st * jnp.exp(b_last[ch]) + incs[ch]
        if not carried:
            so_ref[(ch * c) // seq] = st
    if carried:
        st_ref[...] = st
        so_ref[0] = st
    o = jnp.concatenate(outs, axis=0)
    ones_vv = _bf((_shr(_iota((256, 256), 0), GLA_DV) == _shr(_iota((256, 256), 1), GLA_DV)).astype(F32))
    ms = _dot_exact(o * o, ones_vv) * (1.0 / GLA_DV)
    o_ref[...] = o * lax.rsqrt(ms + RMS_EPS) * nw_ref[...] * _silu(vr_ref[:, 256:512])


def _dot_nt_exact_lhs(w, x):
    hi = _bf(x)
    r1 = x - hi.astype(F32)
    mid = _bf(r1)
    lo = _bf(r1 - mid.astype(F32))
    return _dot(w, hi) + (_dot(w, mid) + _dot(w, lo))


def gla_mixer(h, row0, gate_w, gate_b, norm_w4, s0t, bsz, seq):
    c = min(GLA_CHUNK, seq)
    tr = GLA_RT
    spt = max(tr // seq, 1)
    nt = max(seq // tr, 1)
    ng = bsz // spt
    blk0 = row0 // tr
    row = lambda w, col: pl.BlockSpec((tr, w), lambda g, n: (blk0 + g * nt + n, col))
    full = lambda shape: pl.BlockSpec(shape, lambda g, n: (0,) * len(shape))
    st_spec = pl.BlockSpec((spt, 256, 128), lambda g, n: (g, 0, 0))
    return pl.pallas_call(
        functools.partial(_gla_body, c=c, seq=seq),
        grid=(ng, nt),
        in_specs=[row(256, C_LQ // 256), row(512, C_LV // 512), row(LANE, C_SM // LANE),
                  full((GLA_RANK, 128)), full((1, 128)), full((1, 256)), st_spec],
        out_specs=[pl.BlockSpec((tr, 256), lambda g, n: (g * nt + n, 0)), st_spec],
        out_shape=[jax.ShapeDtypeStruct((bsz * seq, 256), F32), jax.ShapeDtypeStruct((bsz, 256, 128), F32)],
        scratch_shapes=[pltpu.VMEM((GLA_CHUNK + tr, 128), F32), pltpu.VMEM((GLA_CHUNK + tr, 128), F32),
                        pltpu.VMEM((GLA_CHUNK + tr, 256), F32), pltpu.VMEM((256, 128), F32)],
        compiler_params=_cp("arbitrary", "arbitrary"),
        name="gla",
    )(h, h, h, gate_w, gate_b, norm_w4, s0t)


def gla_state_to_t(s):
    eye = jnp.eye(GLA_HEADS, dtype=s.dtype)
    return jnp.einsum('bhde,hg->bhegd', s, eye).reshape(s.shape[0], 256, 128)


def gla_state_from_t(st):
    b = st.shape[0]
    t5 = st.reshape(b, GLA_HEADS, GLA_DV, GLA_HEADS, GLA_DK)
    diag = jnp.stack([t5[:, hh, :, hh, :] for hh in range(GLA_HEADS)], axis=1)
    return jnp.swapaxes(diag, 2, 3)


TOK_SUB = D_MODEL // LANE


def _to_token_tiles(ref, row0, y):
    n = y.shape[0]
    for s in range(TOK_SUB):
        ref[pl.ds(row0 * TOK_SUB + s, n, stride=TOK_SUB), :] = y[:, s * LANE:(s + 1) * LANE]


def _from_token_tiles(ref, row0, n, s):
    return ref[pl.ds(row0 * TOK_SUB + s, n, stride=TOK_SUB), :]


def _out_proj_body(x_ref, ap, bp, cp, dp, a_s, b_s, c_s, d_s, w_ref, g_ref, b_ref, rw_ref, rb_ref,
                   o_ref, t_ref, e_ref, gt_ref, *, n_prompt_tiles):
    i = pl.program_id(0)

    def run(refs):
        acc = DN_ALPHA * x_ref[...]
        for m, r in enumerate(refs):
            acc = acc + _dot(_bf(r[...]), w_ref[m * GROUP_WIDTH:(m + 1) * GROUP_WIDTH, :])
        y = _layer_norm(acc, g_ref[...], b_ref[...])
        o_ref[...] = y
        _to_token_tiles(t_ref, 0, y)
        e_ref[...], gt_ref[...] = _route(y, rw_ref, rb_ref)

    @pl.when(i < n_prompt_tiles)
    def _():
        run((ap, bp, cp, dp))

    @pl.when(i >= n_prompt_tiles)
    def _():
        run((a_s, b_s, c_s, d_s))


def out_proj_ln(x, mix_p, mix_s, w, g, b, rw, rb):
    t = x.shape[0]
    npt = mix_p[0].shape[0] // ROW_TILE
    nst = mix_s[0].shape[0] // ROW_TILE
    p_spec = pl.BlockSpec((ROW_TILE, GROUP_WIDTH), lambda i: (jnp.minimum(i, npt - 1), 0))
    s_spec = pl.BlockSpec((ROW_TILE, GROUP_WIDTH), lambda i: (jnp.clip(i - npt, 0, nst - 1), 0))
    full = lambda shape: pl.BlockSpec(shape, lambda i: (0,) * len(shape))
    return pl.pallas_call(
        functools.partial(_out_proj_body, n_prompt_tiles=npt),
        grid=(t // ROW_TILE,),
        in_specs=[pl.BlockSpec((ROW_TILE, D_MODEL), lambda i: (i, 0))] + [p_spec] * 4 + [s_spec] * 4
                 + [full((D_MODEL, D_MODEL)), full((1, D_MODEL)), full((1, D_MODEL)), full((D_MODEL, LANE)), full((1, LANE))],
        out_specs=[pl.BlockSpec((ROW_TILE, D_MODEL), lambda i: (i, 0)),
                   pl.BlockSpec((ROW_TILE * TOK_SUB, LANE), lambda i: (i, 0)),
                   pl.BlockSpec((ROW_TILE, LANE), lambda i: (i, 0)), pl.BlockSpec((ROW_TILE, LANE), lambda i: (i, 0))],
        out_shape=[jax.ShapeDtypeStruct((t, D_MODEL), F32), jax.ShapeDtypeStruct((t * TOK_SUB, LANE), F32),
                   jax.ShapeDtypeStruct((t, LANE), I32), jax.ShapeDtypeStruct((t, LANE), F32)],
        compiler_params=_cp("arbitrary"),
        name="out_proj_ln",
    )(x, *mix_p, *mix_s, w, g, b, rw, rb)


def _route(x, w_ref, b_ref):
    logits = _dot(_bf(x), _bf(w_ref[...])) + b_ref[...]
    lane = _iota(logits.shape, 1)
    vals = jnp.where(lane < N_EXPERTS, logits, -jnp.inf)
    eo = jnp.zeros(logits.shape, I32)
    top = []
    for k in range(TOP_K):
        m = jnp.max(vals, axis=-1, keepdims=True)
        idx = jnp.min(jnp.where(vals == m, lane.astype(F32), float(LANE)), axis=-1, keepdims=True).astype(I32)
        eo = jnp.where(lane == k, idx, eo)
        top.append(m)
        vals = jnp.where(lane == idx, -jnp.inf, vals)
    ex = [jnp.exp(v - top[0]) for v in top]
    den = ex[0] + ex[1] + ex[2] + ex[3]
    go = jnp.zeros(logits.shape, F32)
    for k in range(TOP_K):
        go = jnp.where(lane == k, ex[k] / den, go)
    return eo, go


DISPATCH_TILE = 256


def _tok(i):
    return pl.ds(pl.multiple_of(i * TOK_SUB, TOK_SUB), TOK_SUB)


def _dispatch_copy(x_ref, xs_ref, sem, r, d):
    return pltpu.make_async_copy(x_ref.at[_tok(r), :], xs_ref.at[_tok(d), :], sem)


def _dispatch_body(dest_ref, x_ref, xs_in_ref, xs_ref, sem):
    del xs_in_ref

    def start(r, carry):
        for k in range(TOP_K):
            _dispatch_copy(x_ref, xs_ref, sem, r, dest_ref[0, 0, r * TOP_K + k]).start(priority=k % 2)
        return carry

    lax.fori_loop(0, DISPATCH_TILE, start, 0)

    def wait(r, carry):
        for k in range(TOP_K):
            _dispatch_copy(x_ref, xs_ref, sem, 0, 0).wait()
        return carry

    lax.fori_loop(0, DISPATCH_TILE, wait, 0)


def moe_dispatch(xt, dest, n_rows, xs_buf=None):
    t = xt.shape[0] // TOK_SUB
    nt = t // DISPATCH_TILE
    zeros = jnp.zeros((n_rows * TOK_SUB, LANE), F32) if xs_buf is None else xs_buf
    return pl.pallas_call(
        _dispatch_body,
        grid=(nt,),
        in_specs=[pl.BlockSpec((1, 1, DISPATCH_TILE * TOP_K), lambda i: (i, 0, 0), memory_space=pltpu.SMEM),
                  pl.BlockSpec((DISPATCH_TILE * TOK_SUB, LANE), lambda i: (i, 0)),
                  pl.BlockSpec(memory_space=pl.ANY)],
        out_specs=pl.BlockSpec(memory_space=pl.ANY),
        out_shape=jax.ShapeDtypeStruct((n_rows * TOK_SUB, LANE), F32),
        scratch_shapes=[pltpu.SemaphoreType.DMA(())],
        input_output_aliases={2: 0},
        compiler_params=_cp("arbitrary"),
        name="moe_dispatch",
    )(dest.reshape(nt, 1, DISPATCH_TILE * TOP_K), xt, zeros)


EXPERT_COLS = 256


def _expert_body(te_ref, tv_ref, xs_ref, wgu_ref, bgu_ref, wd_ref, bd_ref, ys_ref, wgu_s, wd2_s, tmp_s):
    i = pl.program_id(0)
    valid = tv_ref[i] != 0
    changed = (i == 0) | (te_ref[i] != te_ref[jnp.maximum(i - 1, 0)])

    @pl.when(i == 0)
    def _():
        tmp_s[...] = jnp.zeros(tmp_s.shape, F32)

    @pl.when(valid & changed)
    def _():
        for c0 in range(0, 2 * D_FF, 512):
            wgu_s[:, c0:c0 + 512] = _bf(wgu_ref[0, :, c0:c0 + 512])
        for j in range(D_MODEL // LANE):
            tmp_s[pl.ds(0, D_FF, stride=2), :] = wd_ref[0, :, j * LANE:(j + 1) * LANE]
            wd2_s[:, j * LANE:(j + 1) * LANE] = _bf(tmp_s[...])

    @pl.when(valid)
    def _():
        xb = jnp.concatenate([_bf(_from_token_tiles(xs_ref, 0, MOE_TILE, s)) for s in range(TOK_SUB)], axis=1)
        chunks = range(0, 2 * D_FF, EXPERT_COLS)
        gus = [_dot(xb, wgu_s[:, c0:c0 + EXPERT_COLS]) + bgu_ref[0][:, c0:c0 + EXPERT_COLS] for c0 in chunks]
        even = (_iota((MOE_TILE, EXPERT_COLS), 1) & 1) == 0
        y = jnp.zeros((MOE_TILE, D_MODEL), F32)
        for c0, gu in zip(chunks, gus):
            nxt = pltpu.roll(gu, EXPERT_COLS - 1, 1)
            gate = jnp.minimum(gu, SWIGLU_LIMIT)
            up = jnp.clip(nxt, -SWIGLU_LIMIT, SWIGLU_LIMIT)
            act = gate * _sigmoid(gate * SWIGLU_ALPHA) * (up + 1.0)
            act = jnp.where(even, act, 0.0)
            y = y + _dot(_bf(act), wd2_s[c0:c0 + EXPERT_COLS, :])
        _to_token_tiles(ys_ref, 0, y + bd_ref[0])

    @pl.when(jnp.logical_not(valid))
    def _():
        ys_ref[...] = jnp.zeros(ys_ref.shape, F32)


def moe_experts(xs, tile_e, tile_valid, wgu, bgu, wd, bd):
    n_rows = xs.shape[0] // TOK_SUB
    nt = n_rows // MOE_TILE
    grid_spec = pltpu.PrefetchScalarGridSpec(
        num_scalar_prefetch=2,
        grid=(nt,),
        in_specs=[pl.BlockSpec((MOE_TILE * TOK_SUB, LANE), lambda i, te, tv: (i, 0)),
                  pl.BlockSpec((1, D_MODEL, 2 * D_FF), lambda i, te, tv: (te[i], 0, 0)),
                  pl.BlockSpec((1, 1, 2 * D_FF), lambda i, te, tv: (te[i], 0, 0)),
                  pl.BlockSpec((1, D_FF, D_MODEL), lambda i, te, tv: (te[i], 0, 0)),
                  pl.BlockSpec((1, 1, D_MODEL), lambda i, te, tv: (te[i], 0, 0))],
        out_specs=pl.BlockSpec((MOE_TILE * TOK_SUB, LANE), lambda i, te, tv: (i, 0)),
        scratch_shapes=[pltpu.VMEM((D_MODEL, 2 * D_FF), BF16), pltpu.VMEM((2 * D_FF, D_MODEL), BF16),
                        pltpu.VMEM((2 * D_FF, LANE), F32)],
    )
    return pl.pallas_call(
        _expert_body,
        grid_spec=grid_spec,
        out_shape=jax.ShapeDtypeStruct((n_rows * TOK_SUB, LANE), F32),
        compiler_params=_cp("arbitrary"),
        name="moe_experts",
    )(tile_e, tile_valid, xs, wgu, bgu, wd, bd)


def _combine_copy(ys_ref, buf_ref, sems, slot, d, k, r):
    return pltpu.make_async_copy(ys_ref.at[_tok(d), :], buf_ref.at[slot, k, _tok(r), :], sems.at[slot])


def _combine_body(dest_ref, next_dest_ref, gates_ref, x_ref, g_ref, b_ref, ys_ref, o_ref, buf_ref, sems):
    i = pl.program_id(0)
    slot = i % 2

    def gather(idx_ref, into):
        def start(r, carry):
            for k in range(TOP_K):
                _combine_copy(ys_ref, buf_ref, sems, into, idx_ref[0, 0, r * TOP_K + k], k, r).start(priority=k % 2)
            return carry

        lax.fori_loop(0, DISPATCH_TILE, start, 0)

    @pl.when(i == 0)
    def _():
        gather(dest_ref, 0)

    @pl.when(i + 1 < pl.num_programs(0))
    def _():
        gather(next_dest_ref, 1 - slot)

    def wait(r, carry):
        for k in range(TOP_K):
            _combine_copy(ys_ref, buf_ref, sems, slot, 0, k, 0).wait()
        return carry

    lax.fori_loop(0, DISPATCH_TILE, wait, 0)
    gates = gates_ref[...]
    cols = []
    for s in range(TOK_SUB):
        acc = DN_ALPHA * x_ref[:, s * LANE:(s + 1) * LANE]
        for k in range(TOP_K):
            acc = acc + gates[:, k:k + 1] * buf_ref[slot, k, pl.ds(s, DISPATCH_TILE, stride=TOK_SUB), :]
        cols.append(acc)
    o_ref[...] = _layer_norm(jnp.concatenate(cols, axis=1), g_ref[...], b_ref[...])


def moe_combine_ln(x, ys, dest, gates, g, b):
    t = x.shape[0]
    nt = t // DISPATCH_TILE
    dest3 = dest.reshape(nt, 1, DISPATCH_TILE * TOP_K)
    full = lambda shape: pl.BlockSpec(shape, lambda i: (0,) * len(shape))
    return pl.pallas_call(
        _combine_body,
        grid=(nt,),
        in_specs=[pl.BlockSpec((1, 1, DISPATCH_TILE * TOP_K), lambda i: (i, 0, 0), memory_space=pltpu.SMEM),
                  pl.BlockSpec((1, 1, DISPATCH_TILE * TOP_K), lambda i: (jnp.minimum(i + 1, nt - 1), 0, 0),
                               memory_space=pltpu.SMEM),
                  pl.BlockSpec((DISPATCH_TILE, LANE), lambda i: (i, 0)),
                  pl.BlockSpec((DISPATCH_TILE, D_MODEL), lambda i: (i, 0)),
                  full((1, D_MODEL)), full((1, D_MODEL)),
                  pl.BlockSpec(memory_space=pl.ANY)],
        out_specs=pl.BlockSpec((DISPATCH_TILE, D_MODEL), lambda i: (i, 0)),
        out_shape=jax.ShapeDtypeStruct((t, D_MODEL), F32),
        scratch_shapes=[pltpu.VMEM((2, TOP_K, DISPATCH_TILE * TOK_SUB, LANE), F32), pltpu.SemaphoreType.DMA((2,))],
        compiler_params=_cp("arbitrary"),
        name="moe_combine_ln",
    )(dest3, dest3, gates, x, g, b, ys)


def _rank_body(e_ref, rank_ref, cnt_ref, run_ref):
    i = pl.program_id(0)

    @pl.when(i == 0)
    def _():
        run_ref[...] = jnp.zeros(run_ref.shape, F32)

    e = e_ref[...]
    lane = _iota(e.shape, 1)
    hot = [lane == e[:, k:k + 1] for k in range(TOP_K)]
    m = sum(h.astype(F32) for h in hot)
    ri = _iota((ROW_TILE, ROW_TILE), 0)
    ci = _iota((ROW_TILE, ROW_TILE), 1)
    before = _dot(_bf((ci < ri).astype(F32)), _bf(m)) + run_ref[0:1, :]
    rank = jnp.zeros(e.shape, F32)
    for k in range(TOP_K):
        rank = jnp.where(lane == k, jnp.sum(jnp.where(hot[k], before, 0.0), axis=-1, keepdims=True), rank)
    rank_ref[...] = rank.astype(I32)
    run_ref[0:1, :] = run_ref[0:1, :] + jnp.sum(m, axis=0, keepdims=True)
    cnt_ref[...] = run_ref[...].astype(I32)


def moe_rank(e_pad):
    t = e_pad.shape[0]
    return pl.pallas_call(
        _rank_body,
        grid=(t // ROW_TILE,),
        in_specs=[pl.BlockSpec((ROW_TILE, LANE), lambda i: (i, 0))],
        out_specs=[pl.BlockSpec((ROW_TILE, LANE), lambda i: (i, 0)), pl.BlockSpec((8, LANE), lambda i: (0, 0))],
        out_shape=[jax.ShapeDtypeStruct((t, LANE), I32), jax.ShapeDtypeStruct((8, LANE), I32)],
        scratch_shapes=[pltpu.VMEM((8, LANE), F32)],
        compiler_params=_cp("arbitrary"),
        name="moe_rank",
    )(e_pad)


def moe_plan(e_pad, n_tokens):
    tk = n_tokens * TOP_K
    rank_pad, cnt = moe_rank(e_pad)
    top_e, rank = e_pad[:, :TOP_K], rank_pad[:, :TOP_K]
    counts = cnt[0, :N_EXPERTS]
    ntile = (counts + MOE_TILE - 1) // MOE_TILE
    tile_end = jnp.cumsum(ntile)
    tile_start = tile_end - ntile
    start_of = jnp.sum(jnp.where(top_e[..., None] == jnp.arange(N_EXPERTS, dtype=I32), tile_start, 0), axis=-1)
    dest = (start_of * MOE_TILE + rank).reshape(tk)
    n_tiles = -(-tk // MOE_TILE) + N_EXPERTS
    tiles = jnp.arange(n_tiles, dtype=I32)
    tile_e = jnp.minimum(jnp.sum((tile_end[None, :] <= tiles[:, None]).astype(I32), axis=1), N_EXPERTS - 1)
    tile_valid = (tiles < tile_end[-1]).astype(I32)
    return dest.astype(I32), tile_e, tile_valid, n_tiles * MOE_TILE


def moe_ffn_ln(x1, x1t, e_pad, gates, wgu, bgu, wd, bd, g, b, expert0=0, xs_buf=None):
    t = x1.shape[0]
    dest, tile_e, tile_valid, n_rows = moe_plan(e_pad, t)
    xs = moe_dispatch(x1t, dest, n_rows, xs_buf)
    ys = moe_experts(xs, tile_e + expert0, tile_valid, wgu, bgu, wd, bd)
    return moe_combine_ln(x1, ys, dest, gates, g, b), xs


def _rope_tables(pos):
    half = ROT_DIM // 2
    inv_freq = ROPE_THETA ** (-jnp.arange(half, dtype=F32) / half)
    ang = pos.astype(F32)[:, None] * inv_freq[None, :]
    cos, sin = jnp.cos(ang), jnp.sin(ang)
    n = pos.shape[0]
    ones = jnp.ones((n, HEAD_DIM - ROT_DIM), F32)
    cos_h = jnp.concatenate([cos, cos, ones], axis=1)
    sin_h = jnp.concatenate([-sin, sin, 0.0 * ones], axis=1)
    return jnp.concatenate([cos_h, cos_h], axis=1), jnp.concatenate([sin_h, sin_h], axis=1)


def _conv_tail(h, row0, bsz, seq):
    assert seq >= GDN_CONV - 1
    n = GDN_CONV - 1
    if bsz <= 8:
        return jnp.stack([h[row0 + (b + 1) * seq - n:row0 + (b + 1) * seq, C_GQKV:C_GQKV + GDN_CONV_DIM]
                          for b in range(bsz)], axis=0)
    blk = h[row0:row0 + bsz * seq, C_GQKV:C_GQKV + GDN_CONV_DIM].reshape(bsz, seq, GDN_CONV_DIM)
    return blk[:, seq - n:, :]


def _lane_row(v, offset):
    return jnp.zeros((1, LANE), F32).at[0, offset:offset + v.shape[0]].set(v)


def kernel(x_prompt, x_sample, cache_swa_k, cache_swa_v, state_ssm_re, state_ssm_im, state_gdn_conv, state_gdn, state_gla, w_in, w_out, attn_sinks, ssm_a_re, ssm_a_im, ssm_b_re, ssm_b_im, ssm_c_re, ssm_c_im, ssm_d, ssm_log_dt, ssm_glu_w, ssm_glu_b, gdn_conv_w, gdn_a_log, gdn_dt_bias, gdn_norm_w, gla_gate_w, gla_gate_b, gla_norm_w, ln1_g, ln1_b, ln2_g, ln2_b, router_w, router_b, moe_w_gate_up, moe_b_gate_up, moe_w_down, moe_b_down):
    bp, lp, _ = x_prompt.shape
    bs, ls, _ = x_sample.shape
    n_p, n_s = bp * lp, bs * ls
    depth = w_in.shape[0]

    w_in_r = jnp.concatenate([w_in[..., :1792], w_in[..., 1800:2568], w_in[..., 1792:1800], w_in[..., 2568:N_IN],
                              jnp.zeros(w_in.shape[:2] + (NH - N_IN,), w_in.dtype)], axis=-1).astype(BF16)
    w_out_b = w_out.astype(BF16)
    glu_w_b = ssm_glu_w.astype(BF16)
    rw_pad = jnp.pad(router_w, ((0, 0), (0, 0), (0, LANE - N_EXPERTS)))
    rb_pad = jnp.pad(router_b, ((0, 0), (0, LANE - N_EXPERTS)))
    wgu_all = moe_w_gate_up.reshape(depth * N_EXPERTS, D_MODEL, 2 * D_FF)
    bgu_all = moe_b_gate_up.reshape(depth * N_EXPERTS, 1, 2 * D_FF)
    wd_all = moe_w_down.reshape(depth * N_EXPERTS, D_FF, D_MODEL)
    bd_all = moe_b_down.reshape(depth * N_EXPERTS, 1, D_MODEL)

    cos_p, sin_p = _rope_tables(jnp.arange(lp, dtype=I32))
    cos_s, sin_s = _rope_tables(PAST_LEN + jnp.arange(ls, dtype=I32))

    x = jnp.concatenate([x_prompt.reshape(n_p, D_MODEL), x_sample.reshape(n_s, D_MODEL)], axis=0)
    zeros = lambda *s: jnp.zeros(s, F32)
    new_p = [[] for _ in range(7)]
    new_s = [[] for _ in range(7)]
    xs_buf = None
    kc_all = cache_swa_k.reshape(depth * bs, WINDOW, 128)
    vc_all = cache_swa_v.reshape(depth * bs, WINDOW, 128)
    gdn_s_all = state_gdn.reshape(depth * bs, GDN_HEADS, GDN_DK, GDN_DV)
    for l in range(depth):
        h = in_proj(x, w_in_r[l])
        sinks = attn_sinks[l]
        oa_p, pk, pv = swa_prompt(h, sinks, cos_p, sin_p, bp, lp)
        oa_s, sk, sv = swa_sample(h, n_p, sinks, kc_all, vc_all, cos_s, sin_s, bs, ls, seq0=l * bs)
        bw, a_bar, cw, dsk = s5_params(ssm_a_re[l], ssm_a_im[l], ssm_b_re[l], ssm_b_im[l], ssm_c_re[l], ssm_c_im[l],
                                       ssm_d[l], ssm_log_dt[l])
        glu_b = ssm_glu_b[l].reshape(1, 2 * GROUP_WIDTH)
        ob_p, hl_p = s5_prompt(h, zeros(bp, 2 * SSM_W), bw, a_bar, cw, dsk, glu_w_b[l], glu_b, bp, lp)
        h0_s = jnp.concatenate([state_ssm_re[l].reshape(bs, SSM_W), state_ssm_im[l].reshape(bs, SSM_W)], axis=1)
        ob_s, hl_s = s5_sample(h, n_p, h0_s, bw, a_bar, cw, dsk, glu_w_b[l], glu_b, bs, ls)
        alog_t = _lane_row(gdn_a_log[l], SM_GA)
        dtb_t = _lane_row(gdn_dt_bias[l], SM_GA)
        gnw = gdn_norm_w[l].reshape(1, GDN_DV)
        oc_p, gs_p = gdn_mixer(h, 0, gdn_conv_w[l], alog_t, dtb_t, gnw, zeros(bp, GDN_CONV - 1, GDN_CONV_DIM),
                               zeros(bp, GDN_HEADS, GDN_DK, GDN_DV), bp, lp)
        oc_s, gs_s = gdn_mixer(h, n_p, gdn_conv_w[l], alog_t, dtb_t, gnw, state_gdn_conv[l], gdn_s_all, bs, ls,
                               seq0=l * bs)
        cv_p = _conv_tail(h, 0, bp, lp)
        cv_s = _conv_tail(h, n_p, bs, ls)
        lgb = gla_gate_b[l].reshape(1, 128)
        lnw = jnp.tile(gla_norm_w[l], GLA_HEADS).reshape(1, 256)
        od_p, lt_p = gla_mixer(h, 0, gla_gate_w[l], lgb, lnw, zeros(bp, 256, 128), bp, lp)
        od_s, lt_s = gla_mixer(h, n_p, gla_gate_w[l], lgb, lnw, gla_state_to_t(state_gla[l]), bs, ls)

        x1, x1t, e_pad, gates = out_proj_ln(
            x, (oa_p, ob_p.reshape(n_p, GROUP_WIDTH), oc_p, od_p), (oa_s, ob_s, oc_s, od_s), w_out_b[l],
            ln1_g[l].reshape(1, D_MODEL), ln1_b[l].reshape(1, D_MODEL), rw_pad[l], rb_pad[l].reshape(1, LANE))
        x, xs_buf = moe_ffn_ln(x1, x1t, e_pad, gates, wgu_all, bgu_all, wd_all, bd_all, ln2_g[l].reshape(1, D_MODEL),
                               ln2_b[l].reshape(1, D_MODEL), expert0=l * N_EXPERTS, xs_buf=xs_buf)

        st_p = (pk.reshape(bp, WINDOW, A_KV_HEADS, HEAD_DIM), pv.reshape(bp, WINDOW, A_KV_HEADS, HEAD_DIM),
                hl_p[:, :SSM_W].reshape(bp, SSM_GROUPS, SSM_STATE), hl_p[:, SSM_W:].reshape(bp, SSM_GROUPS, SSM_STATE),
                cv_p, gs_p, gla_state_from_t(lt_p))
        st_s = (sk.reshape(bs, WINDOW, A_KV_HEADS, HEAD_DIM), sv.reshape(bs, WINDOW, A_KV_HEADS, HEAD_DIM),
                hl_s[:, :SSM_W].reshape(bs, SSM_GROUPS, SSM_STATE), hl_s[:, SSM_W:].reshape(bs, SSM_GROUPS, SSM_STATE),
                cv_s, gs_s, gla_state_from_t(lt_s))
        for i in range(7):
            new_p[i].append(st_p[i])
            new_s[i].append(st_s[i])
    y_p = x[:n_p].reshape(bp, lp, D_MODEL)
    y_s = x[n_p:].reshape(bs, ls, D_MODEL)
    return (y_p, y_s) + tuple(jnp.stack(t, axis=0) for t in new_p) + tuple(jnp.stack(t, axis=0) for t in new_s)
```
